```python
import math
import jax, jax.numpy as jnp
from jax import lax
import numpy as np

D_MODEL = 1024
BATCH = 32
SEQ = 256
DEPTH = 2
DEC_BATCH = 4
DEC_SEQ = 2048
PAST_LEN = 512

GRID_W = 64
N_HEADS = 8
HEAD_DIM = 64
V_DIM = 2 * HEAD_DIM
ROPE_THETA = 10000.0
Q_BLOCK = 128
CHUNK = 128
GM_WIDTH = 2 * D_MODEL
GM_GROUPS = 8
N_EXPERT_GROUPS = 4
EXPERTS_PER_GROUP = 8
N_EXPERTS = N_EXPERT_GROUPS * EXPERTS_PER_GROUP
EXPERT_FF = D_MODEL // 2
MOE_TOP_K = 2
MOE_BLOCK = 128
N_ATTN = (DEPTH + 1) // 2
N_CHUNK = DEPTH // 2
EPS = 1e-6

kernel_name = 'hybrid_diffattn_chunkgmlp_hmoe_dit_step'


def rmsnorm(x, g):
    xf = x.astype(jnp.float32)
    y = xf * lax.rsqrt(jnp.mean(xf * xf, axis=-1, keepdims=True) + EPS)
    return (y * g.astype(jnp.float32)).astype(x.dtype)


def lambda_init(layer):
    return 0.8 - 0.6 * math.exp(-0.3 * layer)


def axial_rope(x):
    L = x.shape[1]
    rows = L // GRID_W
    row = jnp.repeat(jnp.arange(rows, dtype=jnp.float32), GRID_W)
    col = jnp.tile(jnp.arange(GRID_W, dtype=jnp.float32), rows)
    half = HEAD_DIM // 2
    inv = 1.0 / (ROPE_THETA ** (jnp.arange(0, half, 2, dtype=jnp.float32) / half))
    ang_r = row[:, None] * inv[None, :]
    ang_c = col[:, None] * inv[None, :]
    ang = jnp.concatenate([ang_r, ang_r, ang_c, ang_c], axis=-1)
    cos = jnp.cos(ang)[None, :, None, None, :].astype(x.dtype)
    sin = jnp.sin(ang)[None, :, None, None, :].astype(x.dtype)
    xs = x.reshape(x.shape[:-1] + (2, 2, half // 2))
    rot = jnp.stack([-xs[..., 1, :], xs[..., 0, :]], axis=-2).reshape(x.shape)
    return x * cos + rot * sin


def ada_mod(cvec, w, b):
    return jnp.split(jax.nn.silu(cvec) @ w + b, 6, axis=-1)


def modulated(x, g, shift, scale):
    return rmsnorm(x, g) * (1.0 + scale) + shift


def qkv_proj(h, wq, wk, wv, qn, kn):
    B, L, _ = h.shape
    q = (h @ wq).reshape(B, L, N_HEADS, 2, HEAD_DIM)
    k = (h @ wk).reshape(B, L, N_HEADS, 2, HEAD_DIM)
    v = (h @ wv).reshape(B, L, N_HEADS, V_DIM)
    return rmsnorm(q, qn), rmsnorm(k, kn), v


def diff_lambda(lam_p, layer):
    lf = lam_p.astype(jnp.float32)
    return jnp.exp(jnp.sum(lf[0] * lf[1])) - jnp.exp(jnp.sum(lf[2] * lf[3])) + lambda_init(layer)


def diff_attend(q, k, v, lam):
    B, Lq = q.shape[:2]
    nb = Lq // Q_BLOCK
    qb = q.reshape(B, nb, Q_BLOCK, N_HEADS, 2, HEAD_DIM).transpose(1, 0, 2, 3, 4, 5)
    scale = HEAD_DIM ** -0.5

    def block(qi):
        s = jnp.einsum('bqhmd,bkhmd->bhmqk', qi, k, preferred_element_type=jnp.float32) * scale
        p = jax.nn.softmax(s, axis=-1)
        a = p[:, :, 0] - lam * p[:, :, 1]
        return jnp.einsum('bhqk,bkhe->bqhe', a.astype(v.dtype), v)

    o = lax.map(block, qb)
    return o.transpose(1, 0, 2, 3, 4).reshape(B, Lq, N_HEADS, V_DIM)


def diff_out(o, subln, wo, layer):
    B, L = o.shape[:2]
    o = rmsnorm(o, subln) * (1.0 - lambda_init(layer))
    return o.reshape(B, L, N_HEADS * V_DIM) @ wo


def chunk_gmlp(h, w_in, b_in, v_g, ws, bs, w_out):
    B, L, _ = h.shape
    z = jax.nn.gelu(h @ w_in + b_in)
    u, v = jnp.split(z, 2, axis=-1)
    v = rmsnorm(v, v_g)
    cg = GM_WIDTH // GM_GROUPS
    vc = v.reshape(B, L // CHUNK, CHUNK, GM_GROUPS, cg)
    vm = jnp.einsum('gpq,bnqgc->bnpgc', ws, vc) + bs.T[None, None, :, :, None]
    return (u * vm.reshape(B, L, GM_WIDTH)) @ w_out


def hier_moe(h, wc, bc, wf, bf, w1, w3, w2):
    B, L, D = h.shape
    T = B * L
    xt = h.reshape(T, D)
    p_c = jax.nn.softmax((xt @ wc).astype(jnp.float32) + bc.astype(jnp.float32), axis=-1)
    pg, gi = lax.top_k(p_c, 1)
    pg, gi = pg[:, 0], gi[:, 0]
    lf = ((xt @ wf).astype(jnp.float32) + bf.astype(jnp.float32)).reshape(T, N_EXPERT_GROUPS, EXPERTS_PER_GROUP)
    sel = lf[jnp.arange(T), gi]
    tv, ti = lax.top_k(sel, MOE_TOP_K)
    wts = jax.nn.softmax(tv, axis=-1) * pg[:, None]
    e = (gi[:, None] * EXPERTS_PER_GROUP + ti).reshape(-1).astype(jnp.int32)
    w = wts.reshape(-1)
    tok = jnp.repeat(jnp.arange(T, dtype=jnp.int32), MOE_TOP_K)
    N = T * MOE_TOP_K
    counts = jax.ops.segment_sum(jnp.ones((N,), jnp.int32), e, num_segments=N_EXPERTS)
    padded = ((counts + MOE_BLOCK - 1) // MOE_BLOCK) * MOE_BLOCK
    pend = jnp.cumsum(padded)
    pstart = pend - padded
    cstart = jnp.cumsum(counts) - counts
    order = jnp.argsort(e)
    se = e[order]
    dest = pstart[se] + jnp.arange(N, dtype=jnp.int32) - cstart[se]
    n_blocks = N // MOE_BLOCK + N_EXPERTS
    npad = n_blocks * MOE_BLOCK
    tok_buf = jnp.full((npad,), T, jnp.int32).at[dest].set(tok[order])
    w_buf = jnp.zeros((npad,), h.dtype).at[dest].set(w[order].astype(h.dtype))
    block_e = jnp.clip(jnp.searchsorted(pend, jnp.arange(n_blocks, dtype=jnp.int32) * MOE_BLOCK, side='right'), 0, N_EXPERTS - 1)
    xpad = jnp.concatenate([xt, jnp.zeros((1, D), xt.dtype)], axis=0)

    def run(args):
        tb, be = args
        xb = xpad[tb]
        hb = jax.nn.silu(xb @ w1[be]) * (xb @ w3[be])
        return hb @ w2[be]

    yb = lax.map(run, (tok_buf.reshape(n_blocks, MOE_BLOCK), block_e)).reshape(npad, D)
    y = jnp.zeros((T + 1, D), h.dtype).at[tok_buf].add(yb * w_buf[:, None])
    return y[:T].reshape(B, L, D)


def setup_inputs(seed: int = 0) -> dict:
    key = jax.random.key(seed)
    ks = jax.random.split(key, 40)
    f32 = jnp.float32
    n = lambda k, s, sc: (jax.random.normal(k, s, f32) * sc).astype(f32)
    D = D_MODEL
    return {
        'x_prompt': n(ks[0], (BATCH, SEQ, D), 1.0),
        'x_sample': n(ks[1], (DEC_BATCH, DEC_SEQ, D), 1.0),
        'cache_k': n(ks[2], (DEC_BATCH, N_ATTN, PAST_LEN, N_HEADS, 2, HEAD_DIM), 1.0),
        'cache_v': n(ks[3], (DEC_BATCH, N_ATTN, PAST_LEN, N_HEADS, V_DIM), 1.0),
        'c': n(ks[4], (DEC_BATCH, D), 1.0),
        'c_ctx': n(ks[5], (D,), 1.0),
        'ada_w': n(ks[6], (DEPTH, D, 6 * D), 0.5 * D ** -0.5),
        'ada_b': n(ks[7], (DEPTH, 6 * D), 0.01),
        'norm1_g': 1.0 + n(ks[8], (DEPTH, D), 0.01),
        'norm2_g': 1.0 + n(ks[9], (DEPTH, D), 0.01),
        'attn_wq': n(ks[10], (N_ATTN, D, N_HEADS * 2 * HEAD_DIM), D ** -0.5),
        'attn_wk': n(ks[11], (N_ATTN, D, N_HEADS * 2 * HEAD_DIM), D ** -0.5),
        'attn_wv': n(ks[12], (N_ATTN, D, N_HEADS * V_DIM), D ** -0.5),
        'attn_wo': n(ks[13], (N_ATTN, N_HEADS * V_DIM, D), (N_HEADS * V_DIM) ** -0.5),
        'attn_qnorm': 1.0 + n(ks[14], (N_ATTN, HEAD_DIM), 0.01),
        'attn_knorm': 1.0 + n(ks[15], (N_ATTN, HEAD_DIM), 0.01),
        'attn_lam': n(ks[16], (N_ATTN, 4, HEAD_DIM), 0.1),
        'attn_subln': 1.0 + n(ks[17], (N_ATTN, V_DIM), 0.01),
        'gm_win': n(ks[18], (N_CHUNK, D, 2 * GM_WIDTH), D ** -0.5),
        'gm_bin': n(ks[19], (N_CHUNK, 2 * GM_WIDTH), 0.01),
        'gm_vnorm': 1.0 + n(ks[20], (N_CHUNK, GM_WIDTH), 0.01),
        'gm_ws': n(ks[21], (N_CHUNK, GM_GROUPS, CHUNK, CHUNK), CHUNK ** -0.5),
        'gm_bs': 1.0 + n(ks[22], (N_CHUNK, GM_GROUPS, CHUNK), 0.01),
        'gm_wout': n(ks[23], (N_CHUNK, GM_WIDTH, D), GM_WIDTH ** -0.5),
        'moe_wc': n(ks[24], (DEPTH, D, N_EXPERT_GROUPS), D ** -0.5),
        'moe_bc': n(ks[25], (DEPTH, N_EXPERT_GROUPS), 0.01),
        'moe_wf': n(ks[26], (DEPTH, D, N_EXPERTS), D ** -0.5),
        'moe_bf': n(ks[27], (DEPTH, N_EXPERTS), 0.01),
        'moe_w1': n(ks[28], (DEPTH, N_EXPERTS, D, EXPERT_FF), D ** -0.5),
        'moe_w3': n(ks[29], (DEPTH, N_EXPERTS, D, EXPERT_FF), D ** -0.5),
        'moe_w2': n(ks[30], (DEPTH, N_EXPERTS, EXPERT_FF, D), EXPERT_FF ** -0.5),
    }


def reference(x_prompt, x_sample, cache_k, cache_v, c, c_ctx, ada_w, ada_b, norm1_g, norm2_g,
              attn_wq, attn_wk, attn_wv, attn_wo, attn_qnorm, attn_knorm, attn_lam, attn_subln,
              gm_win, gm_bin, gm_vnorm, gm_ws, gm_bs, gm_wout,
              moe_wc, moe_bc, moe_wf, moe_bf, moe_w1, moe_w3, moe_w2):
    h = x_prompt
    ks_out, vs_out = [], []
    for i in range(DEPTH):
        sh1, sc1, g1, sh2, sc2, g2 = ada_mod(c_ctx[None, None, :], ada_w[i], ada_b[i])
        a = modulated(h, norm1_g[i], sh1, sc1)
        j = i // 2
        if i % 2 == 0:
            q, k, v = qkv_proj(a, attn_wq[j], attn_wk[j], attn_wv[j], attn_qnorm[j], attn_knorm[j])
            lam = diff_lambda(attn_lam[j], i)
            mix = diff_out(diff_attend(q, k, v, lam), attn_subln[j], attn_wo[j], i)
            ks_out.append(k)
            vs_out.append(v)
        else:
            mix = chunk_gmlp(a, gm_win[j], gm_bin[j], gm_vnorm[j], gm_ws[j], gm_bs[j], gm_wout[j])
        h = h + g1 * mix
        a = modulated(h, norm2_g[i], sh2, sc2)
        h = h + g2 * hier_moe(a, moe_wc[i], moe_bc[i], moe_wf[i], moe_bf[i], moe_w1[i], moe_w3[i], moe_w2[i])
    y_prompt = h
    new_cache_k = jnp.stack(ks_out, axis=1)
    new_cache_v = jnp.stack(vs_out, axis=1)

    h = x_sample
    for i in range(DEPTH):
        sh1, sc1, g1, sh2, sc2, g2 = ada_mod(c[:, None, :], ada_w[i], ada_b[i])
        a = modulated(h, norm1_g[i], sh1, sc1)
        j = i // 2
        if i % 2 == 0:
            q, k, v = qkv_proj(a, attn_wq[j], attn_wk[j], attn_wv[j], attn_qnorm[j], attn_knorm[j])
            q = axial_rope(q)
            k = axial_rope(k)
            k_all = jnp.concatenate([cache_k[:, j], k], axis=1)
            v_all = jnp.concatenate([cache_v[:, j], v], axis=1)
            lam = diff_lambda(attn_lam[j], i)
            mix = diff_out(diff_attend(q, k_all, v_all, lam), attn_subln[j], attn_wo[j], i)
        else:
            mix = chunk_gmlp(a, gm_win[j], gm_bin[j], gm_vnorm[j], gm_ws[j], gm_bs[j], gm_wout[j])
        h = h + g1 * mix
        a = modulated(h, norm2_g[i], sh2, sc2)
        h = h + g2 * hier_moe(a, moe_wc[i], moe_bc[i], moe_wf[i], moe_bf[i], moe_w1[i], moe_w3[i], moe_w2[i])
    y_sample = h
    return (y_prompt, y_sample, new_cache_k, new_cache_v)
```

```python
import functools
import math

import jax
import jax.numpy as jnp
from jax import lax
from jax.experimental import pallas as pl
from jax.experimental.pallas import tpu as pltpu

F32 = jnp.float32
BF16 = jnp.bfloat16

N_HEADS = 8
HEAD_DIM = 64
V_DIM = 2 * HEAD_DIM
GRID_W = 64
ROPE_THETA = 10000.0
CHUNK = 128
GM_GROUPS = 8
N_EXPERT_GROUPS = 4
EXPERTS_PER_GROUP = 8
N_EXPERTS = N_EXPERT_GROUPS * EXPERTS_PER_GROUP
EPS = 1e-6

LANES = 128
ROW_TILE = 256
MOE_BLOCK = 256
GATHER_ROWS = 256
ATTN_Q_TILE = 256
VMEM_LIMIT = 56 * 1024 * 1024

R_E1, R_E2, R_W1, R_W2, R_RANK1, R_RANK2 = 0, 1, 2, 3, 4, 5
L_EXPERT0 = N_EXPERT_GROUPS


def _lambda_init(layer):
    return 0.8 - 0.6 * math.exp(-0.3 * layer)


def _cparams(n_axes):
    return pltpu.CompilerParams(dimension_semantics=("arbitrary",) * n_axes, vmem_limit_bytes=VMEM_LIMIT)


def _seq_of_tile(i, n_ctx_tiles, tiles_per_req):
    return jnp.where(i < n_ctx_tiles, 0, 1 + (i - n_ctx_tiles) // tiles_per_req)


def _mod_row(mod_ref, part, seq):
    return mod_ref[part, pl.ds(seq, 1), :]


def _modulated(x, g, shift, scale):
    y = x * lax.rsqrt(jnp.mean(x * x, axis=-1, keepdims=True) + EPS)
    return (y * g) * (1.0 + scale) + shift


def _split_bf16(x):
    hi = x.astype(BF16)
    lo = (x - hi.astype(F32)).astype(BF16)
    return hi, lo


def _dot(a, b):
    return jnp.dot(a, b, preferred_element_type=F32)


def _ada_kernel(c_ref, w_ref, b_ref, o_ref):
    c = c_ref[...]
    s = c * jax.nn.sigmoid(c)
    o_ref[...] = _dot(s.astype(BF16), w_ref[...].astype(BF16)) + b_ref[...]


def _ada_mod(cvec, ada_w, ada_b):
    depth, d, d6 = ada_w.shape
    parts = d6 // d
    rows = cvec.shape[0]
    return pl.pallas_call(
        _ada_kernel,
        grid=(depth, parts),
        in_specs=[
            pl.BlockSpec((rows, d), lambda l, j: (0, 0)),
            pl.BlockSpec((None, d, d), lambda l, j: (l, 0, j)),
            pl.BlockSpec((None, 1, d), lambda l, j: (l, 0, j)),
        ],
        out_specs=pl.BlockSpec((None, None, rows, d), lambda l, j: (l, j, 0, 0)),
        out_shape=jax.ShapeDtypeStruct((depth, parts, rows, d), F32),
        compiler_params=_cparams(2),
        name="ada_mod",
    )(cvec, ada_w, ada_b.reshape(depth, 1, d6))


def _pre_attn_kernel(h_ref, mod_ref, g_ref, w_ref, qg_ref, kg_ref, gs_ref, gb_ref, cos_ref, sin_ref,
                     q_ref, k_ref, v_ref, kf_ref, vf_ref, *, n_ctx_tiles, tiles_per_req):
    i = pl.program_id(0)
    seq = _seq_of_tile(i, n_ctx_tiles, tiles_per_req)
    d = h_ref.shape[1]
    a = _modulated(h_ref[...], g_ref[...], _mod_row(mod_ref, 0, seq), _mod_row(mod_ref, 1, seq))
    qkv = _dot(a.astype(BF16), w_ref[...])
    q = qkv[:, :d]
    k = qkv[:, d:2 * d]
    v = qkv[:, 2 * d:]
    sq_hi, sq_lo = _split_bf16(jnp.concatenate([q * q, k * k], axis=1))
    gsum = _dot(sq_hi, gs_ref[...]) + _dot(sq_lo, gs_ref[...])
    r_hi, r_lo = _split_bf16(lax.rsqrt(gsum * (1.0 / HEAD_DIM) + EPS))
    rb = _dot(r_hi, gb_ref[...]) + _dot(r_lo, gb_ref[...])
    qn = (q * rb[:, :d]) * qg_ref[...]
    kn = (k * rb[:, d:]) * kg_ref[...]
    v_ref[...] = v.astype(BF16)
    q_scale = HEAD_DIM ** -0.5

    @pl.when(i < n_ctx_tiles)
    def _():
        q_ref[...] = (qn * q_scale).astype(BF16)
        k_ref[...] = kn.astype(BF16)
        kf_ref[...] = kn
        vf_ref[...] = v

    @pl.when(i >= n_ctx_tiles)
    def _():
        cos = cos_ref[...]
        sin = sin_ref[...]
        lane = lax.broadcasted_iota(jnp.int32, cos.shape, 1)
        first = (lane & 31) < 16

        def rope(x):
            outs = []
            for hh in range(d // LANES):
                xs = x[:, hh * LANES:(hh + 1) * LANES]
                rot = jnp.where(first, pltpu.roll(xs, LANES - 16, 1), pltpu.roll(xs, 16, 1))
                outs.append(xs * cos + rot * sin)
            return jnp.concatenate(outs, axis=1)

        q_ref[...] = (rope(qn) * q_scale).astype(BF16)
        k_ref[...] = rope(kn).astype(BF16)


def _rope_tables(n_pos):
    half = HEAD_DIM // 2
    pos = jnp.arange(n_pos, dtype=jnp.int32)
    row = (pos // GRID_W).astype(F32)
    col = (pos % GRID_W).astype(F32)
    inv = 1.0 / (ROPE_THETA ** (jnp.arange(0, half, 2, dtype=F32) / half))
    ang_r = row[:, None] * inv[None, :]
    ang_c = col[:, None] * inv[None, :]
    ang = jnp.concatenate([ang_r, ang_r, ang_c, ang_c], axis=-1)
    quarter = half // 2
    sign = jnp.tile(jnp.concatenate([-jnp.ones((quarter,), F32), jnp.ones((quarter,), F32)]), 2)
    cos = jnp.tile(jnp.cos(ang), (1, 2))
    sin = jnp.tile(jnp.sin(ang) * sign[None, :], (1, 2))
    return cos, sin


def _pre_attn(h, mod, g, wqkv, qg, kg, n_ctx, rows_per_req):
    t, d = h.shape
    tm = ROW_TILE
    n_ctx_tiles = n_ctx // tm
    tiles_per_req = rows_per_req // tm
    groups = d // HEAD_DIM
    gid = jnp.arange(2 * d, dtype=jnp.int32) // HEAD_DIM
    gs = (gid[:, None] == jnp.arange(LANES, dtype=jnp.int32)[None, :]).astype(BF16)
    gb = gs.T
    assert 2 * groups <= LANES
    cos, sin = _rope_tables(rows_per_req)
    pos_map = lambda i: (jnp.where(i < n_ctx_tiles, 0, (i - n_ctx_tiles) % tiles_per_req), 0)
    ctx_map = lambda i: (jnp.minimum(i, n_ctx_tiles - 1), 0)
    full = lambda shape: pl.BlockSpec(shape, lambda i: (0,) * len(shape))
    row = pl.BlockSpec((tm, d), lambda i: (i, 0))
    kern = functools.partial(_pre_attn_kernel, n_ctx_tiles=n_ctx_tiles, tiles_per_req=tiles_per_req)
    return pl.pallas_call(
        kern,
        grid=(t // tm,),
        in_specs=[row, full(mod.shape), full((1, d)), full(wqkv.shape), full((1, d)), full((1, d)),
                  full(gs.shape), full(gb.shape),
                  pl.BlockSpec((tm, LANES), pos_map), pl.BlockSpec((tm, LANES), pos_map)],
        out_specs=[row, row, row, pl.BlockSpec((tm, d), ctx_map), pl.BlockSpec((tm, d), ctx_map)],
        out_shape=[jax.ShapeDtypeStruct((t, d), BF16)] * 3 + [jax.ShapeDtypeStruct((n_ctx, d), F32)] * 2,
        compiler_params=_cparams(1),
        name="pre_attn",
    )(h, mod, g, wqkv, qg, kg, gs, gb, cos, sin)


def _attn_kernel(*refs, n_heads, has_cache, layer):
    if has_cache:
        lam_ref, sub_ref, q_ref, kn_ref, vn_ref, kc_ref, vc_ref, o_ref = refs
    else:
        lam_ref, sub_ref, q_ref, kn_ref, vn_ref, o_ref = refs
    lf = lam_ref[...]
    lam = (jnp.exp(jnp.sum(lf[0:1] * lf[1:2], axis=-1, keepdims=True))
           - jnp.exp(jnp.sum(lf[2:3] * lf[3:4], axis=-1, keepdims=True)) + _lambda_init(layer))
    nt = (((1,), (1,)), ((), ()))
    tq = q_ref.shape[0]
    lane = lax.broadcasted_iota(jnp.int32, (tq, LANES), 1)
    for hh in range(n_heads):
        sl = slice(hh * LANES, (hh + 1) * LANES)
        qh = q_ref[:, sl]
        zero = jnp.zeros_like(qh)
        qs = (jnp.where(lane < HEAD_DIM, qh, zero), jnp.where(lane >= HEAD_DIM, qh, zero))
        keys = [kn_ref[:, sl]]
        vals = [vn_ref[:, sl]]
        if has_cache:
            keys.append(kc_ref[:, sl].astype(BF16))
            vals.append(vc_ref[:, sl].astype(BF16))
        probs = []
        for qm in qs:
            s = [lax.dot_general(qm, kk, nt, preferred_element_type=F32) for kk in keys]
            m = functools.reduce(jnp.maximum, [jnp.max(x, axis=-1, keepdims=True) for x in s])
            e = [jnp.exp(x - m) for x in s]
            den = functools.reduce(lambda u, w: u + w, [jnp.sum(x, axis=-1, keepdims=True) for x in e])
            probs.append((e, 1.0 / den))
        (e0, r0), (e1, r1) = probs
        o = None
        for j, vv in enumerate(vals):
            aj = e0[j] * r0 - lam * (e1[j] * r1)
            oj = _dot(aj.astype(BF16), vv)
            o = oj if o is None else o + oj
        o = o * lax.rsqrt(jnp.mean(o * o, axis=-1, keepdims=True) + EPS)
        o = (o * sub_ref[...]) * (1.0 - _lambda_init(layer))
        o_ref[:, sl] = o.astype(BF16)


def _attention(q, k, v, lam_p, subln, cache_k, cache_v, n_ctx, ctx_len, n_req, req_len, layer):
    t, d = q.shape
    small = lambda shape: pl.BlockSpec(shape, lambda *_: (0,) * len(shape))
    n_ctx_req = n_ctx // ctx_len
    blk = pl.BlockSpec((ctx_len, d), lambda b: (b, 0))
    o_ctx = pl.pallas_call(
        functools.partial(_attn_kernel, n_heads=N_HEADS, has_cache=False, layer=layer),
        grid=(n_ctx_req,),
        in_specs=[small(lam_p.shape), small(subln.shape), blk, blk, blk],
        out_specs=blk,
        out_shape=jax.ShapeDtypeStruct((n_ctx, d), BF16),
        compiler_params=_cparams(1),
        name="attn_ctx",
    )(lam_p, subln, q, k, v)

    tq = ATTN_Q_TILE
    nq = req_len // tq
    q0 = n_ctx // tq
    r0 = n_ctx // req_len
    past = cache_k.shape[1]
    small3 = lambda shape: pl.BlockSpec(shape, lambda b, h, i: (0,) * len(shape))
    qspec = pl.BlockSpec((tq, LANES), lambda b, h, i: (q0 + b * nq + i, h))
    kvspec = pl.BlockSpec((req_len, LANES), lambda b, h, i: (r0 + b, h))
    cspec = pl.BlockSpec((None, past, LANES), lambda b, h, i: (b, 0, h))
    o_den = pl.pallas_call(
        functools.partial(_attn_kernel, n_heads=1, has_cache=True, layer=layer),
        grid=(n_req, N_HEADS, nq),
        in_specs=[small3(lam_p.shape), pl.BlockSpec((1, LANES), lambda b, h, i: (0, 0)),
                  qspec, kvspec, kvspec, cspec, cspec],
        out_specs=pl.BlockSpec((tq, LANES), lambda b, h, i: (b * nq + i, h)),
        out_shape=jax.ShapeDtypeStruct((n_req * req_len, d), BF16),
        compiler_params=_cparams(3),
        name="attn_latent",
    )(lam_p, subln, q, k, v, cache_k, cache_v)
    return jnp.concatenate([o_ctx, o_den], axis=0)


def _route(a2, wr_hi, wr_lo, br, ltri, carry_ref):
    tm = a2.shape[0]
    a_hi, a_lo = _split_bf16(a2)
    logits = _dot(a_hi, wr_hi) + (_dot(a_lo, wr_hi) + _dot(a_hi, wr_lo)) + br
    lane_i = lax.broadcasted_iota(jnp.int32, (tm, LANES), 1)
    lane = lane_i.astype(F32)
    neg = jnp.full((tm, LANES), -jnp.inf, F32)
    big = jnp.full((tm, LANES), float(LANES), F32)
    first_lane = lambda mask: jnp.min(jnp.where(mask, lane, big), axis=-1, keepdims=True)

    lc = jnp.where(lane_i < N_EXPERT_GROUPS, logits, neg)
    mc = jnp.max(lc, axis=-1, keepdims=True)
    pg = 1.0 / jnp.sum(jnp.exp(lc - mc), axis=-1, keepdims=True)
    gi = first_lane(lc == mc)
    assert EXPERTS_PER_GROUP == 8
    grp = lax.shift_right_arithmetic(lane_i - L_EXPERT0, 3).astype(F32)
    in_group = (lane_i >= L_EXPERT0) & (lane_i < L_EXPERT0 + N_EXPERTS) & (grp == gi)
    ls = jnp.where(in_group, logits, neg)
    t1 = jnp.max(ls, axis=-1, keepdims=True)
    i1 = first_lane(ls == t1)
    ls2 = jnp.where(lane == i1, neg, ls)
    t2 = jnp.max(ls2, axis=-1, keepdims=True)
    i2 = first_lane(ls2 == t2)
    ex = jnp.exp(t2 - t1)
    w1 = pg * (1.0 / (1.0 + ex))
    w2 = pg * (ex / (1.0 + ex))
    e1 = i1 - float(L_EXPERT0)
    e2 = i2 - float(L_EXPERT0)
    oh1 = lane == e1
    oh2 = lane == e2
    onehot = oh1.astype(F32) + oh2.astype(F32)
    before = _dot(ltri, onehot.astype(BF16)) + carry_ref[...]
    zero = jnp.zeros_like(before)
    rank1 = jnp.sum(jnp.where(oh1, before, zero), axis=-1, keepdims=True)
    rank2 = jnp.sum(jnp.where(oh2, before, zero), axis=-1, keepdims=True)
    carry_ref[...] = carry_ref[...] + jnp.sum(onehot, axis=0, keepdims=True)
    slab = jnp.zeros((tm, LANES), F32)
    for ln, val in ((R_E1, e1), (R_E2, e2), (R_W1, w1), (R_W2, w2), (R_RANK1, rank1), (R_RANK2, rank2)):
        slab = jnp.where(lane_i == ln, val, slab)
    return slab


def _router_weights(wc, bc, wf, bf_):
    d = wc.shape[0]
    pad = LANES - N_EXPERT_GROUPS - N_EXPERTS
    w = jnp.concatenate([wc, wf, jnp.zeros((d, pad), F32)], axis=1)
    b = jnp.concatenate([bc, bf_, jnp.zeros((pad,), F32)])[None, :]
    hi = w.astype(BF16)
    lo = (w - hi.astype(F32)).astype(BF16)
    return hi, lo, b


def _lower_tri(n):
    r = jnp.arange(n, dtype=jnp.int32)
    return (r[None, :] < r[:, None]).astype(BF16)


def _post_attn_kernel(h_ref, o_ref, mod_ref, wo_ref, g2_ref, wrh_ref, wrl_ref, br_ref, ltri_ref,
                      h1_ref, a2_ref, route_ref, cnt_ref, carry_ref, *, n_ctx_tiles, tiles_per_req):
    i = pl.program_id(0)
    seq = _seq_of_tile(i, n_ctx_tiles, tiles_per_req)

    @pl.when(i == 0)
    def _():
        carry_ref[...] = jnp.zeros_like(carry_ref)

    h1 = h_ref[...] + _mod_row(mod_ref, 2, seq) * _dot(o_ref[...], wo_ref[...])
    h1_ref[...] = h1
    a2 = _modulated(h1, g2_ref[...], _mod_row(mod_ref, 3, seq), _mod_row(mod_ref, 4, seq))
    a2_ref[...] = a2
    route_ref[...] = _route(a2, wrh_ref[...], wrl_ref[...], br_ref[...], ltri_ref[...], carry_ref)
    cnt_ref[...] = carry_ref[...]


def _post_attn(h, o, mod, wo, g2, router, n_ctx, rows_per_req):
    t, d = h.shape
    tm = ROW_TILE
    wrh, wrl, br = router
    ltri = _lower_tri(tm)
    full = lambda shape: pl.BlockSpec(shape, lambda i: (0,) * len(shape))
    row = pl.BlockSpec((tm, d), lambda i: (i, 0))
    kern = functools.partial(_post_attn_kernel, n_ctx_tiles=n_ctx // tm, tiles_per_req=rows_per_req // tm)
    return pl.pallas_call(
        kern,
        grid=(t // tm,),
        in_specs=[row, row, full(mod.shape), full(wo.shape), full((1, d)), full(wrh.shape), full(wrl.shape),
                  full(br.shape), full(ltri.shape)],
        out_specs=[row, row, pl.BlockSpec((tm, LANES), lambda i: (i, 0)), full((1, LANES))],
        out_shape=[jax.ShapeDtypeStruct((t, d), F32), jax.ShapeDtypeStruct((t, d), F32),
                   jax.ShapeDtypeStruct((t, LANES), F32), jax.ShapeDtypeStruct((1, LANES), F32)],
        scratch_shapes=[pltpu.VMEM((1, LANES), F32)],
        compiler_params=_cparams(1),
        name="post_attn_router",
    )(h, o, mod, wo, g2, wrh, wrl, br, ltri)


def _gather_kernel(idx_ref, src_ref, out_ref, sem):
    rows = out_ref.shape[0]

    def issue(j, carry):
        r = idx_ref[0, j]
        pltpu.make_async_copy(src_ref.at[pl.ds(r, 1), :], out_ref.at[pl.ds(j, 1), :], sem).start()
        return carry

    lax.fori_loop(0, rows, issue, 0, unroll=8)
    pltpu.make_async_copy(src_ref.at[pl.ds(0, rows), :], out_ref, sem).wait()


def _gather_rows(src, idx):
    n = idx.shape[0]
    d = src.shape[1]
    r = GATHER_ROWS
    return pl.pallas_call(
        _gather_kernel,
        grid=(n // r,),
        in_specs=[pl.BlockSpec((None, 1, r), lambda i: (i, 0, 0), memory_space=pltpu.SMEM),
                  pl.BlockSpec(memory_space=pl.ANY)],
        out_specs=pl.BlockSpec((r, d), lambda i: (i, 0)),
        out_shape=jax.ShapeDtypeStruct((n, d), src.dtype),
        scratch_shapes=[pltpu.SemaphoreType.DMA(())],
        compiler_params=_cparams(1),
        name="gather_rows",
    )(idx.reshape(n // r, 1, r), src)


def _expert_kernel(be_ref, nb_ref, x_ref, w1_ref, w3_ref, w2_ref, y_ref, w1b, w3b, w2b):
    b = pl.program_id(0)
    prev = be_ref[jnp.maximum(b - 1, 0)]
    used = b < nb_ref[0]

    @pl.when(used & ((b == 0) | (be_ref[b] != prev)))
    def _():
        w1b[...] = w1_ref[...].astype(BF16)
        w3b[...] = w3_ref[...].astype(BF16)
        w2b[...] = w2_ref[...].astype(BF16)

    @pl.when(used)
    def _():
        x = x_ref[...].astype(BF16)
        hb = jax.nn.silu(_dot(x, w1b[...])) * _dot(x, w3b[...])
        y_ref[...] = _dot(hb.astype(BF16), w2b[...])

    @pl.when(jnp.logical_not(used))
    def _():
        y_ref[...] = jnp.zeros_like(y_ref)


def _expert_mlps(xs, block_e, n_used, w1, w3, w2):
    npad, d = xs.shape
    ff = w1.shape[2]
    bm = MOE_BLOCK
    grid_spec = pltpu.PrefetchScalarGridSpec(
        num_scalar_prefetch=2,
        grid=(npad // bm,),
        in_specs=[pl.BlockSpec((bm, d), lambda b, be, nb: (b, 0)),
                  pl.BlockSpec((None, d, ff), lambda b, be, nb: (be[b], 0, 0)),
                  pl.BlockSpec((None, d, ff), lambda b, be, nb: (be[b], 0, 0)),
                  pl.BlockSpec((None, ff, d), lambda b, be, nb: (be[b], 0, 0))],
        out_specs=pl.BlockSpec((bm, d), lambda b, be, nb: (b, 0)),
        scratch_shapes=[pltpu.VMEM((d, ff), BF16), pltpu.VMEM((d, ff), BF16), pltpu.VMEM((ff, d), BF16)],
    )
    return pl.pallas_call(
        _expert_kernel,
        grid_spec=grid_spec,
        out_shape=jax.ShapeDtypeStruct((npad, d), F32),
        compiler_params=_cparams(1),
        name="expert_mlps",
    )(block_e, n_used, xs, w1, w3, w2)


def _moe(a2, route, counts, w1, w3, w2):
    t, d = a2.shape
    bm = MOE_BLOCK
    n = 2 * t
    e = route[:, R_E1:R_E2 + 1].astype(jnp.int32).reshape(n)
    rank = route[:, R_RANK1:R_RANK2 + 1].astype(jnp.int32).reshape(n)
    cnt = counts[0, :N_EXPERTS].astype(jnp.int32)
    padded = ((cnt + bm - 1) // bm) * bm
    pend = jnp.cumsum(padded)
    pstart = pend - padded
    pos = pstart[e] + rank
    n_blocks = n // bm + N_EXPERTS
    npad = n_blocks * bm
    block_e = jnp.clip(jnp.searchsorted(pend, jnp.arange(n_blocks, dtype=jnp.int32) * bm, side="right"),
                       0, N_EXPERTS - 1).astype(jnp.int32)
    n_used = (pend[-1:] // bm).astype(jnp.int32)
    last_e = block_e[jnp.maximum(n_used[0] - 1, 0)]
    block_e = jnp.where(jnp.arange(n_blocks) < n_used[0], block_e, last_e)
    tok_sorted = jnp.zeros((npad,), jnp.int32).at[pos].set(jnp.arange(n, dtype=jnp.int32) // 2)
    xs = _gather_rows(a2, tok_sorted)
    ys = _expert_mlps(xs, block_e, n_used, w1, w3, w2)
    return _gather_rows(ys, pos).reshape(t, 2 * d)


def _moe_combine(h, y2, route, g2):
    d = h.shape[1]
    w1 = route[:, R_W1:R_W1 + 1]
    w2 = route[:, R_W2:R_W2 + 1]
    return h + g2 * (w1 * y2[:, :d] + w2 * y2[:, d:])


def _gmlp_kernel(h_ref, y2_ref, rt_ref, modp_ref, mod_ref, g1_ref, win_ref, bin_ref, vg_ref, ws_ref, bsb_ref,
                 wout_ref, g2_ref, wrh_ref, wrl_ref, br_ref, ltri_ref,
                 h1_ref, a2_ref, route_ref, cnt_ref, carry_ref, *, n_ctx_tiles, tiles_per_req):
    i = pl.program_id(0)
    seq = _seq_of_tile(i, n_ctx_tiles, tiles_per_req)

    @pl.when(i == 0)
    def _():
        carry_ref[...] = jnp.zeros_like(carry_ref)

    h = _moe_combine(h_ref[...], y2_ref[...], rt_ref[...], _mod_row(modp_ref, 5, seq))
    a = _modulated(h, g1_ref[...], _mod_row(mod_ref, 0, seq), _mod_row(mod_ref, 1, seq))
    gw = vg_ref.shape[1]
    z = jax.nn.gelu(_dot(a.astype(BF16), win_ref[...]) + bin_ref[...])
    u = z[:, :gw]
    v = z[:, gw:]
    v = (v * lax.rsqrt(jnp.mean(v * v, axis=-1, keepdims=True) + EPS)) * vg_ref[...]
    vb = v.astype(BF16)
    cg = gw // GM_GROUPS
    tm = h.shape[0]
    rows = []
    for c in range(tm // CHUNK):
        cols = []
        for g in range(GM_GROUPS):
            cols.append(_dot(ws_ref[g], vb[c * CHUNK:(c + 1) * CHUNK, g * cg:(g + 1) * cg]))
        rows.append(jnp.concatenate(cols, axis=1) + bsb_ref[...])
    vm = jnp.concatenate(rows, axis=0)
    mix = _dot((u * vm).astype(BF16), wout_ref[...])
    h1 = h + _mod_row(mod_ref, 2, seq) * mix
    h1_ref[...] = h1
    a2 = _modulated(h1, g2_ref[...], _mod_row(mod_ref, 3, seq), _mod_row(mod_ref, 4, seq))
    a2_ref[...] = a2
    route_ref[...] = _route(a2, wrh_ref[...], wrl_ref[...], br_ref[...], ltri_ref[...], carry_ref)
    cnt_ref[...] = carry_ref[...]


def _gmlp_layer(h, y2, route_prev, mod_prev, mod, g1, win, bin_, vg, ws, bsb, wout, g2, router,
                n_ctx, rows_per_req):
    t, d = h.shape
    tm = ROW_TILE
    wrh, wrl, br = router
    ltri = _lower_tri(tm)
    full = lambda shape: pl.BlockSpec(shape, lambda i: (0,) * len(shape))
    row = lambda w: pl.BlockSpec((tm, w), lambda i: (i, 0))
    kern = functools.partial(_gmlp_kernel, n_ctx_tiles=n_ctx // tm, tiles_per_req=rows_per_req // tm)
    args = (h, y2, route_prev, mod_prev, mod, g1, win, bin_, vg, ws, bsb, wout, g2, wrh, wrl, br, ltri)
    in_specs = [row(d), row(2 * d), row(LANES)] + [full(a.shape) for a in args[3:]]
    return pl.pallas_call(
        kern,
        grid=(t // tm,),
        in_specs=in_specs,
        out_specs=[row(d), row(d), row(LANES), full((1, LANES))],
        out_shape=[jax.ShapeDtypeStruct((t, d), F32), jax.ShapeDtypeStruct((t, d), F32),
                   jax.ShapeDtypeStruct((t, LANES), F32), jax.ShapeDtypeStruct((1, LANES), F32)],
        scratch_shapes=[pltpu.VMEM((1, LANES), F32)],
        compiler_params=_cparams(1),
        name="gmlp_router",
    )(*args)


def _final_kernel(h_ref, y2_ref, rt_ref, mod_ref, o_ref, *, n_ctx_tiles, tiles_per_req):
    seq = _seq_of_tile(pl.program_id(0), n_ctx_tiles, tiles_per_req)
    o_ref[...] = _moe_combine(h_ref[...], y2_ref[...], rt_ref[...], _mod_row(mod_ref, 5, seq))


def _final_combine(h, y2, route, mod, n_ctx, rows_per_req):
    t, d = h.shape
    tm = ROW_TILE
    row = lambda w: pl.BlockSpec((tm, w), lambda i: (i, 0))
    kern = functools.partial(_final_kernel, n_ctx_tiles=n_ctx // tm, tiles_per_req=rows_per_req // tm)
    return pl.pallas_call(
        kern,
        grid=(t // tm,),
        in_specs=[row(d), row(2 * d), row(LANES), pl.BlockSpec(mod.shape, lambda i: (0, 0, 0))],
        out_specs=row(d),
        out_shape=jax.ShapeDtypeStruct((t, d), F32),
        compiler_params=_cparams(1),
        name="final_combine",
    )(h, y2, route, mod)


def kernel(x_prompt, x_sample, cache_k, cache_v, c, c_ctx, ada_w, ada_b, norm1_g, norm2_g, attn_wq, attn_wk,
           attn_wv, attn_wo, attn_qnorm, attn_knorm, attn_lam, attn_subln, gm_win, gm_bin, gm_vnorm, gm_ws,
           gm_bs, gm_wout, moe_wc, moe_bc, moe_wf, moe_bf, moe_w1, moe_w3, moe_w2):
    batch, seq_len, d = x_prompt.shape
    n_req, req_len, _ = x_sample.shape
    depth = ada_w.shape[0]
    assert depth == 2 and attn_wq.shape[0] == 1 and gm_win.shape[0] == 1
    n_ctx = batch * seq_len
    assert n_ctx % req_len == 0 and req_len % ROW_TILE == 0 and seq_len == ROW_TILE
    past = cache_k.shape[2]

    h0 = jnp.concatenate([x_prompt.reshape(n_ctx, d), x_sample.reshape(n_req * req_len, d)], axis=0)
    cvec = jnp.concatenate([c_ctx[None, :], c, jnp.zeros((8 - 1 - n_req, d), F32)], axis=0)
    mod = _ada_mod(cvec, ada_w, ada_b)
    routers = [_router_weights(moe_wc[i], moe_bc[i], moe_wf[i], moe_bf[i]) for i in range(depth)]
    row_vec = lambda v: v.reshape(1, -1)

    wqkv = jnp.concatenate([attn_wq[0], attn_wk[0], attn_wv[0]], axis=1).astype(BF16)
    reps = d // HEAD_DIM
    q, k, v, k_new, v_new = _pre_attn(h0, mod[0], row_vec(norm1_g[0]), wqkv,
                                      row_vec(jnp.tile(attn_qnorm[0], reps)), row_vec(jnp.tile(attn_knorm[0], reps)),
                                      n_ctx, req_len)
    o = _attention(q, k, v, attn_lam[0], row_vec(attn_subln[0]),
                   cache_k[:, 0].reshape(n_req, past, d), cache_v[:, 0].reshape(n_req, past, d),
                   n_ctx, seq_len, n_req, req_len, layer=0)
    h1, a2, route0, cnt0 = _post_attn(h0, o, mod[0], attn_wo[0].astype(BF16), row_vec(norm2_g[0]), routers[0],
                                      n_ctx, req_len)
    y2 = _moe(a2, route0, cnt0, moe_w1[0], moe_w3[0], moe_w2[0])

    gw = gm_vnorm.shape[1]
    bsb = jnp.repeat(gm_bs[0].T, gw // GM_GROUPS, axis=1)
    h2, a2, route1, cnt1 = _gmlp_layer(h1, y2, route0, mod[0], mod[1], row_vec(norm1_g[1]), gm_win[0].astype(BF16),
                                       row_vec(gm_bin[0]), row_vec(gm_vnorm[0]), gm_ws[0].astype(BF16), bsb,
                                       gm_wout[0].astype(BF16), row_vec(norm2_g[1]), routers[1], n_ctx, req_len)
    y2 = _moe(a2, route1, cnt1, moe_w1[1], moe_w3[1], moe_w2[1])
    out = _final_combine(h2, y2, route1, mod[1], n_ctx, req_len)

    y_prompt = out[:n_ctx].reshape(batch, seq_len, d)
    y_sample = out[n_ctx:].reshape(n_req, req_len, d)
    new_cache_k = k_new.reshape(batch, 1, seq_len, N_HEADS, 2, HEAD_DIM)
    new_cache_v = v_new.reshape(batch, 1, seq_len, N_HEADS, V_DIM)
    return (y_prompt, y_sample, new_cache_k, new_cache_v)
```

```python
import functools
import math

import jax
import jax.numpy as jnp
from jax import lax
from jax.experimental import pallas as pl
from jax.experimental.pallas import tpu as pltpu

F32 = jnp.float32
BF16 = jnp.bfloat16

N_HEADS = 8
HEAD_DIM = 64
V_DIM = 2 * HEAD_DIM
GRID_W = 64
ROPE_THETA = 10000.0
CHUNK = 128
GM_GROUPS = 8
N_EXPERT_GROUPS = 4
EXPERTS_PER_GROUP = 8
N_EXPERTS = N_EXPERT_GROUPS * EXPERTS_PER_GROUP
EPS = 1e-6

LANES = 128
ROW_TILE = 256
MOE_BLOCK = 256
GATHER_ROWS = 256
ATTN_Q_TILE = 256
VMEM_LIMIT = 56 * 1024 * 1024

R_E1, R_E2, R_W1, R_W2, R_RANK1, R_RANK2 = 0, 1, 2, 3, 4, 5
L_EXPERT0 = N_EXPERT_GROUPS


def _lambda_init(layer):
    return 0.8 - 0.6 * math.exp(-0.3 * layer)


def _cparams(n_axes):
    return pltpu.CompilerParams(dimension_semantics=("arbitrary",) * n_axes, vmem_limit_bytes=VMEM_LIMIT)


def _seq_of_tile(i, n_ctx_tiles, tiles_per_req):
    return jnp.where(i < n_ctx_tiles, 0, 1 + (i - n_ctx_tiles) // tiles_per_req)


def _pick_rows(i, n_ctx_tiles, ctx_ref, lat_ref):
    return jnp.where(i < n_ctx_tiles, ctx_ref[...], lat_ref[...])


def _split_specs(tm, d, n_ctx_tiles):
    return [pl.BlockSpec((tm, d), lambda i: (jnp.minimum(i, n_ctx_tiles - 1), 0)),
            pl.BlockSpec((tm, d), lambda i: (jnp.maximum(i - n_ctx_tiles, 0), 0))]


def _mod_row(mod_ref, part, seq):
    return mod_ref[part, pl.ds(seq, 1), :]


def _modulated(x, g, shift, scale):
    y = x * lax.rsqrt(jnp.mean(x * x, axis=-1, keepdims=True) + EPS)
    return (y * g) * (1.0 + scale) + shift


def _split_bf16(x):
    hi = x.astype(BF16)
    lo = (x - hi.astype(F32)).astype(BF16)
    return hi, lo


def _dot(a, b):
    return jnp.dot(a, b, preferred_element_type=F32)


def _ada_kernel(c_ref, w_ref, b_ref, o_ref):
    c = c_ref[...]
    s = c * jax.nn.sigmoid(c)
    o_ref[...] = _dot(s.astype(BF16), w_ref[...].astype(BF16)) + b_ref[...]


def _ada_mod(cvec, ada_w, ada_b):
    depth, d, d6 = ada_w.shape
    parts = d6 // d
    rows = cvec.shape[0]
    return pl.pallas_call(
        _ada_kernel,
        grid=(depth, parts),
        in_specs=[
            pl.BlockSpec((rows, d), lambda l, j: (0, 0)),
            pl.BlockSpec((None, d, d), lambda l, j: (l, 0, j)),
            pl.BlockSpec((None, 1, d), lambda l, j: (l, 0, j)),
        ],
        out_specs=pl.BlockSpec((None, None, rows, d), lambda l, j: (l, j, 0, 0)),
        out_shape=jax.ShapeDtypeStruct((depth, parts, rows, d), F32),
        compiler_params=_cparams(2),
        name="ada_mod",
    )(cvec, ada_w, ada_b.reshape(depth, 1, d6))


def _pre_attn_kernel(hc_ref, hl_ref, mod_ref, g_ref, w_ref, qg_ref, kg_ref, gs_ref, gb_ref, cos_ref, sin_ref,
                     q_ref, k_ref, v_ref, kf_ref, vf_ref, *, n_ctx_tiles, tiles_per_req):
    i = pl.program_id(0)
    seq = _seq_of_tile(i, n_ctx_tiles, tiles_per_req)
    d = hc_ref.shape[1]
    x = _pick_rows(i, n_ctx_tiles, hc_ref, hl_ref)
    a = _modulated(x, g_ref[...], _mod_row(mod_ref, 0, seq), _mod_row(mod_ref, 1, seq))
    qkv = _dot(a.astype(BF16), w_ref[...])
    q = qkv[:, :d]
    k = qkv[:, d:2 * d]
    v = qkv[:, 2 * d:]
    sq_hi, sq_lo = _split_bf16(jnp.concatenate([q * q, k * k], axis=1))
    gsum = _dot(sq_hi, gs_ref[...]) + _dot(sq_lo, gs_ref[...])
    r_hi, r_lo = _split_bf16(lax.rsqrt(gsum * (1.0 / HEAD_DIM) + EPS))
    rb = _dot(r_hi, gb_ref[...]) + _dot(r_lo, gb_ref[...])
    qn = (q * rb[:, :d]) * qg_ref[...]
    kn = (k * rb[:, d:]) * kg_ref[...]
    v_ref[...] = v.astype(BF16)
    q_scale = HEAD_DIM ** -0.5

    @pl.when(i < n_ctx_tiles)
    def _():
        q_ref[...] = (qn * q_scale).astype(BF16)
        k_ref[...] = kn.astype(BF16)
        kf_ref[...] = kn
        vf_ref[...] = v

    @pl.when(i >= n_ctx_tiles)
    def _():
        cos = cos_ref[...]
        sin = sin_ref[...]
        lane = lax.broadcasted_iota(jnp.int32, cos.shape, 1)
        first = (lane & 31) < 16

        def rope(x):
            outs = []
            for hh in range(d // LANES):
                xs = x[:, hh * LANES:(hh + 1) * LANES]
                rot = jnp.where(first, pltpu.roll(xs, LANES - 16, 1), pltpu.roll(xs, 16, 1))
                outs.append(xs * cos + rot * sin)
            return jnp.concatenate(outs, axis=1)

        q_ref[...] = (rope(qn) * q_scale).astype(BF16)
        k_ref[...] = rope(kn).astype(BF16)


def _rope_tables(n_pos):
    half = HEAD_DIM // 2
    pos = jnp.arange(n_pos, dtype=jnp.int32)
    row = (pos // GRID_W).astype(F32)
    col = (pos % GRID_W).astype(F32)
    inv = 1.0 / (ROPE_THETA ** (jnp.arange(0, half, 2, dtype=F32) / half))
    ang_r = row[:, None] * inv[None, :]
    ang_c = col[:, None] * inv[None, :]
    ang = jnp.concatenate([ang_r, ang_r, ang_c, ang_c], axis=-1)
    quarter = half // 2
    sign = jnp.tile(jnp.concatenate([-jnp.ones((quarter,), F32), jnp.ones((quarter,), F32)]), 2)
    cos = jnp.tile(jnp.cos(ang), (1, 2))
    sin = jnp.tile(jnp.sin(ang) * sign[None, :], (1, 2))
    return cos, sin


def _pre_attn(h_ctx, h_lat, mod, g, wqkv, qg, kg, rows_per_req):
    n_ctx, d = h_ctx.shape
    t = n_ctx + h_lat.shape[0]
    tm = ROW_TILE
    n_ctx_tiles = n_ctx // tm
    tiles_per_req = rows_per_req // tm
    groups = d // HEAD_DIM
    gid = jnp.arange(2 * d, dtype=jnp.int32) // HEAD_DIM
    gs = (gid[:, None] == jnp.arange(LANES, dtype=jnp.int32)[None, :]).astype(BF16)
    gb = gs.T
    assert 2 * groups <= LANES
    cos, sin = _rope_tables(rows_per_req)
    pos_map = lambda i: (jnp.where(i < n_ctx_tiles, 0, (i - n_ctx_tiles) % tiles_per_req), 0)
    ctx_map = lambda i: (jnp.minimum(i, n_ctx_tiles - 1), 0)
    full = lambda shape: pl.BlockSpec(shape, lambda i: (0,) * len(shape))
    row = pl.BlockSpec((tm, d), lambda i: (i, 0))
    kern = functools.partial(_pre_attn_kernel, n_ctx_tiles=n_ctx_tiles, tiles_per_req=tiles_per_req)
    return pl.pallas_call(
        kern,
        grid=(t // tm,),
        in_specs=_split_specs(tm, d, n_ctx_tiles) + [
                  full(mod.shape), full((1, d)), full(wqkv.shape), full((1, d)), full((1, d)),
                  full(gs.shape), full(gb.shape),
                  pl.BlockSpec((tm, LANES), pos_map), pl.BlockSpec((tm, LANES), pos_map)],
        out_specs=[row, row, row, pl.BlockSpec((tm, d), ctx_map), pl.BlockSpec((tm, d), ctx_map)],
        out_shape=[jax.ShapeDtypeStruct((t, d), BF16)] * 3 + [jax.ShapeDtypeStruct((n_ctx, d), F32)] * 2,
        compiler_params=_cparams(1),
        name="pre_attn",
    )(h_ctx, h_lat, mod, g, wqkv, qg, kg, gs, gb, cos, sin)


def _attn_kernel(*refs, n_heads, has_cache, layer):
    if has_cache:
        lam_ref, sub_ref, q_ref, kn_ref, vn_ref, kc_ref, vc_ref, o_ref = refs
    else:
        lam_ref, sub_ref, q_ref, kn_ref, vn_ref, o_ref = refs
    lf = lam_ref[...]
    lam = (jnp.exp(jnp.sum(lf[0:1] * lf[1:2], axis=-1, keepdims=True))
           - jnp.exp(jnp.sum(lf[2:3] * lf[3:4], axis=-1, keepdims=True)) + _lambda_init(layer))
    nt = (((1,), (1,)), ((), ()))
    tq = q_ref.shape[0]
    lane = lax.broadcasted_iota(jnp.int32, (tq, LANES), 1)
    for hh in range(n_heads):
        sl = slice(hh * LANES, (hh + 1) * LANES)
        qh = q_ref[:, sl]
        zero = jnp.zeros_like(qh)
        qs = (jnp.where(lane < HEAD_DIM, qh, zero), jnp.where(lane >= HEAD_DIM, qh, zero))
        keys = [kn_ref[:, sl]]
        vals = [vn_ref[:, sl]]
        if has_cache:
            keys.append(kc_ref[:, sl].astype(BF16))
            vals.append(vc_ref[:, sl].astype(BF16))
        probs = []
        for qm in qs:
            s = [lax.dot_general(qm, kk, nt, preferred_element_type=F32) for kk in keys]
            m = functools.reduce(jnp.maximum, [jnp.max(x, axis=-1, keepdims=True) for x in s])
            e = [jnp.exp(x - m) for x in s]
            den = functools.reduce(lambda u, w: u + w, [jnp.sum(x, axis=-1, keepdims=True) for x in e])
            probs.append((e, 1.0 / den))
        (e0, r0), (e1, r1) = probs
        o = None
        for j, vv in enumerate(vals):
            aj = e0[j] * r0 - lam * (e1[j] * r1)
            oj = _dot(aj.astype(BF16), vv)
            o = oj if o is None else o + oj
        o = o * lax.rsqrt(jnp.mean(o * o, axis=-1, keepdims=True) + EPS)
        o = (o * sub_ref[...]) * (1.0 - _lambda_init(layer))
        o_ref[:, sl] = o.astype(BF16)


def _attention(q, k, v, lam_p, subln, cache_k, cache_v, n_ctx, ctx_len, n_req, req_len, layer):
    t, d = q.shape
    small = lambda shape: pl.BlockSpec(shape, lambda *_: (0,) * len(shape))
    n_ctx_req = n_ctx // ctx_len
    blk = pl.BlockSpec((ctx_len, d), lambda b: (b, 0))
    o_ctx = pl.pallas_call(
        functools.partial(_attn_kernel, n_heads=N_HEADS, has_cache=False, layer=layer),
        grid=(n_ctx_req,),
        in_specs=[small(lam_p.shape), small(subln.shape), blk, blk, blk],
        out_specs=blk,
        out_shape=jax.ShapeDtypeStruct((n_ctx, d), BF16),
        compiler_params=_cparams(1),
        name="attn_ctx",
    )(lam_p, subln, q, k, v)

    tq = ATTN_Q_TILE
    nq = req_len // tq
    q0 = n_ctx // tq
    r0 = n_ctx // req_len
    past = cache_k.shape[1]
    small3 = lambda shape: pl.BlockSpec(shape, lambda b, h, i: (0,) * len(shape))
    qspec = pl.BlockSpec((tq, LANES), lambda b, h, i: (q0 + b * nq + i, h))
    kvspec = pl.BlockSpec((req_len, LANES), lambda b, h, i: (r0 + b, h))
    cspec = pl.BlockSpec((None, past, LANES), lambda b, h, i: (b, 0, h))
    o_lat = pl.pallas_call(
        functools.partial(_attn_kernel, n_heads=1, has_cache=True, layer=layer),
        grid=(n_req, N_HEADS, nq),
        in_specs=[small3(lam_p.shape), pl.BlockSpec((1, LANES), lambda b, h, i: (0, 0)),
                  qspec, kvspec, kvspec, cspec, cspec],
        out_specs=pl.BlockSpec((tq, LANES), lambda b, h, i: (b * nq + i, h)),
        out_shape=jax.ShapeDtypeStruct((n_req * req_len, d), BF16),
        compiler_params=_cparams(3),
        name="attn_latent",
    )(lam_p, subln, q, k, v, cache_k, cache_v)
    return o_ctx, o_lat


def _route(a2, wr_hi, wr_lo, br, ltri, carry_ref):
    tm = a2.shape[0]
    a_hi, a_lo = _split_bf16(a2)
    logits = _dot(a_hi, wr_hi) + (_dot(a_lo, wr_hi) + _dot(a_hi, wr_lo)) + br
    lane_i = lax.broadcasted_iota(jnp.int32, (tm, LANES), 1)
    lane = lane_i.astype(F32)
    neg = jnp.full((tm, LANES), -jnp.inf, F32)
    big = jnp.full((tm, LANES), float(LANES), F32)
    first_lane = lambda mask: jnp.min(jnp.where(mask, lane, big), axis=-1, keepdims=True)

    lc = jnp.where(lane_i < N_EXPERT_GROUPS, logits, neg)
    mc = jnp.max(lc, axis=-1, keepdims=True)
    pg = 1.0 / jnp.sum(jnp.exp(lc - mc), axis=-1, keepdims=True)
    gi = first_lane(lc == mc)
    assert EXPERTS_PER_GROUP == 8
    grp = lax.shift_right_arithmetic(lane_i - L_EXPERT0, 3).astype(F32)
    in_group = (lane_i >= L_EXPERT0) & (lane_i < L_EXPERT0 + N_EXPERTS) & (grp == gi)
    ls = jnp.where(in_group, logits, neg)
    t1 = jnp.max(ls, axis=-1, keepdims=True)
    i1 = first_lane(ls == t1)
    ls2 = jnp.where(lane == i1, neg, ls)
    t2 = jnp.max(ls2, axis=-1, keepdims=True)
    i2 = first_lane(ls2 == t2)
    ex = jnp.exp(t2 - t1)
    w1 = pg * (1.0 / (1.0 + ex))
    w2 = pg * (ex / (1.0 + ex))
    e1 = i1 - float(L_EXPERT0)
    e2 = i2 - float(L_EXPERT0)
    oh1 = lane == e1
    oh2 = lane == e2
    onehot = oh1.astype(F32) + oh2.astype(F32)
    before = _dot(ltri, onehot.astype(BF16)) + carry_ref[...]
    zero = jnp.zeros_like(before)
    rank1 = jnp.sum(jnp.where(oh1, before, zero), axis=-1, keepdims=True)
    rank2 = jnp.sum(jnp.where(oh2, before, zero), axis=-1, keepdims=True)
    carry_ref[...] = carry_ref[...] + jnp.sum(onehot, axis=0, keepdims=True)
    slab = jnp.zeros((tm, LANES), F32)
    for ln, val in ((R_E1, e1), (R_E2, e2), (R_W1, w1), (R_W2, w2), (R_RANK1, rank1), (R_RANK2, rank2)):
        slab = jnp.where(lane_i == ln, val, slab)
    return slab


def _router_weights(wc, bc, wf, bf_):
    d = wc.shape[0]
    pad = LANES - N_EXPERT_GROUPS - N_EXPERTS
    w = jnp.concatenate([wc, wf, jnp.zeros((d, pad), F32)], axis=1)
    b = jnp.concatenate([bc, bf_, jnp.zeros((pad,), F32)])[None, :]
    hi = w.astype(BF16)
    lo = (w - hi.astype(F32)).astype(BF16)
    return hi, lo, b


def _lower_tri(n):
    r = jnp.arange(n, dtype=jnp.int32)
    return (r[None, :] < r[:, None]).astype(BF16)


def _post_attn_kernel(hc_ref, hl_ref, oc_ref, ol_ref, mod_ref, wo_ref, g2_ref, wrh_ref, wrl_ref, br_ref, ltri_ref,
                      h1_ref, a2_ref, route_ref, cnt_ref, carry_ref, *, n_ctx_tiles, tiles_per_req):
    i = pl.program_id(0)
    seq = _seq_of_tile(i, n_ctx_tiles, tiles_per_req)

    @pl.when(i == 0)
    def _():
        carry_ref[...] = jnp.zeros_like(carry_ref)

    o = _pick_rows(i, n_ctx_tiles, oc_ref, ol_ref)
    h1 = _pick_rows(i, n_ctx_tiles, hc_ref, hl_ref) + _mod_row(mod_ref, 2, seq) * _dot(o, wo_ref[...])
    h1_ref[...] = h1
    a2 = _modulated(h1, g2_ref[...], _mod_row(mod_ref, 3, seq), _mod_row(mod_ref, 4, seq))
    a2_ref[...] = a2
    route_ref[...] = _route(a2, wrh_ref[...], wrl_ref[...], br_ref[...], ltri_ref[...], carry_ref)
    cnt_ref[...] = carry_ref[...]


def _post_attn(h_ctx, h_lat, o_ctx, o_lat, mod, wo, g2, router, rows_per_req):
    n_ctx, d = h_ctx.shape
    t = n_ctx + h_lat.shape[0]
    tm = ROW_TILE
    wrh, wrl, br = router
    ltri = _lower_tri(tm)
    full = lambda shape: pl.BlockSpec(shape, lambda i: (0,) * len(shape))
    row = pl.BlockSpec((tm, d), lambda i: (i, 0))
    kern = functools.partial(_post_attn_kernel, n_ctx_tiles=n_ctx // tm, tiles_per_req=rows_per_req // tm)
    return pl.pallas_call(
        kern,
        grid=(t // tm,),
        in_specs=_split_specs(tm, d, n_ctx // tm) + _split_specs(tm, d, n_ctx // tm) + [
                  full(mod.shape), full(wo.shape), full((1, d)), full(wrh.shape), full(wrl.shape),
                  full(br.shape), full(ltri.shape)],
        out_specs=[row, row, pl.BlockSpec((tm, LANES), lambda i: (i, 0)), full((1, LANES))],
        out_shape=[jax.ShapeDtypeStruct((t, d), F32), jax.ShapeDtypeStruct((t, d), F32),
                   jax.ShapeDtypeStruct((t, LANES), F32), jax.ShapeDtypeStruct((1, LANES), F32)],
        scratch_shapes=[pltpu.VMEM((1, LANES), F32)],
        compiler_params=_cparams(1),
        name="post_attn_router",
    )(h_ctx, h_lat, o_ctx, o_lat, mod, wo, g2, wrh, wrl, br, ltri)


def _gather_kernel(idx_ref, src_ref, out_ref, sem, *, per_row):
    rows = out_ref.shape[0]
    d = src_ref.shape[1]
    unroll = 8

    def issue(g, carry):
        for u in range(unroll):
            j = g * unroll + u
            for s in range(per_row):
                r = idx_ref[0, j * per_row + s]
                pltpu.make_async_copy(src_ref.at[pl.ds(r, 1), :], out_ref.at[pl.ds(j, 1), pl.ds(s * d, d)],
                                      sem).start(priority=(u * per_row + s) % 2)
        return carry

    lax.fori_loop(0, rows // unroll, issue, 0)
    pltpu.make_async_copy(out_ref, out_ref, sem).wait()


def _gather_rows(src, idx, per_row=1):
    n = idx.shape[0]
    d = src.shape[1]
    r = GATHER_ROWS
    rows = r // per_row
    return pl.pallas_call(
        functools.partial(_gather_kernel, per_row=per_row),
        grid=(n // r,),
        in_specs=[pl.BlockSpec((None, 1, r), lambda i: (i, 0, 0), memory_space=pltpu.SMEM),
                  pl.BlockSpec(memory_space=pl.ANY)],
        out_specs=pl.BlockSpec((rows, per_row * d), lambda i: (i, 0)),
        out_shape=jax.ShapeDtypeStruct((n // per_row, per_row * d), src.dtype),
        scratch_shapes=[pltpu.SemaphoreType.DMA(())],
        compiler_params=_cparams(1),
        name="gather_rows",
    )(idx.reshape(n // r, 1, r), src)


def _expert_kernel(be_ref, nb_ref, x_ref, w1_ref, w3_ref, w2_ref, y_ref, w1b, w3b, w2b):
    b = pl.program_id(0)
    prev = be_ref[jnp.maximum(b - 1, 0)]
    used = b < nb_ref[0]

    @pl.when(used & ((b == 0) | (be_ref[b] != prev)))
    def _():
        w1b[...] = w1_ref[...].astype(BF16)
        w3b[...] = w3_ref[...].astype(BF16)
        w2b[...] = w2_ref[...].astype(BF16)

    @pl.when(used)
    def _():
        x = x_ref[...].astype(BF16)
        hb = jax.nn.silu(_dot(x, w1b[...])) * _dot(x, w3b[...])
        y_ref[...] = _dot(hb.astype(BF16), w2b[...])

    @pl.when(jnp.logical_not(used))
    def _():
        y_ref[...] = jnp.zeros_like(y_ref)


def _expert_mlps(xs, block_e, n_used, w1, w3, w2, layer):
    npad, d = xs.shape
    ff = w1.shape[3]
    bm = MOE_BLOCK
    grid_spec = pltpu.PrefetchScalarGridSpec(
        num_scalar_prefetch=2,
        grid=(npad // bm,),
        in_specs=[pl.BlockSpec((bm, d), lambda b, be, nb: (b, 0)),
                  pl.BlockSpec((None, None, d, ff), lambda b, be, nb: (layer, be[b], 0, 0)),
                  pl.BlockSpec((None, None, d, ff), lambda b, be, nb: (layer, be[b], 0, 0)),
                  pl.BlockSpec((None, None, ff, d), lambda b, be, nb: (layer, be[b], 0, 0))],
        out_specs=pl.BlockSpec((bm, d), lambda b, be, nb: (b, 0)),
        scratch_shapes=[pltpu.VMEM((d, ff), BF16), pltpu.VMEM((d, ff), BF16), pltpu.VMEM((ff, d), BF16)],
    )
    return pl.pallas_call(
        _expert_kernel,
        grid_spec=grid_spec,
        out_shape=jax.ShapeDtypeStruct((npad, d), F32),
        compiler_params=_cparams(1),
        name="expert_mlps",
    )(block_e, n_used, xs, w1, w3, w2)


def _moe(a2, route, counts, w1, w3, w2, layer):
    t, d = a2.shape
    bm = MOE_BLOCK
    n = 2 * t
    e = route[:, R_E1:R_E2 + 1].astype(jnp.int32).reshape(n)
    rank = route[:, R_RANK1:R_RANK2 + 1].astype(jnp.int32).reshape(n)
    cnt = counts[0, :N_EXPERTS].astype(jnp.int32)
    padded = ((cnt + bm - 1) // bm) * bm
    pend = jnp.cumsum(padded)
    pstart = pend - padded
    pos = jnp.sum(jnp.where(e[:, None] == jnp.arange(N_EXPERTS, dtype=jnp.int32)[None, :], pstart[None, :], 0),
                  axis=1) + rank
    n_blocks = n // bm + N_EXPERTS
    npad = n_blocks * bm
    starts = jnp.arange(n_blocks, dtype=jnp.int32) * bm
    n_used = pend[-1:] // bm
    last_start = (n_used[0] - 1) * bm
    block_e = jnp.sum((pend[None, :] <= jnp.minimum(starts, last_start)[:, None]).astype(jnp.int32), axis=1)
    tok_sorted = (jnp.arange(npad, dtype=jnp.int32) % t).at[pos].set(jnp.arange(n, dtype=jnp.int32) // 2)
    xs = _gather_rows(a2, tok_sorted)
    ys = _expert_mlps(xs, block_e, n_used, w1, w3, w2, layer)
    return _gather_rows(ys, pos, per_row=2)


def _moe_combine(h, y2, route, g2):
    d = h.shape[1]
    w1 = route[:, R_W1:R_W1 + 1]
    w2 = route[:, R_W2:R_W2 + 1]
    return h + g2 * (w1 * y2[:, :d] + w2 * y2[:, d:])


def _gmlp_kernel(h_ref, y2_ref, rt_ref, modp_ref, mod_ref, g1_ref, win_ref, bin_ref, vg_ref, ws_ref, bsb_ref,
                 wout_ref, g2_ref, wrh_ref, wrl_ref, br_ref, ltri_ref,
                 h1_ref, a2_ref, route_ref, cnt_ref, carry_ref, *, n_ctx_tiles, tiles_per_req):
    i = pl.program_id(0)
    seq = _seq_of_tile(i, n_ctx_tiles, tiles_per_req)

    @pl.when(i == 0)
    def _():
        carry_ref[...] = jnp.zeros_like(carry_ref)

    h = _moe_combine(h_ref[...], y2_ref[...], rt_ref[...], _mod_row(modp_ref, 5, seq))
    a = _modulated(h, g1_ref[...], _mod_row(mod_ref, 0, seq), _mod_row(mod_ref, 1, seq))
    gw = vg_ref.shape[1]
    z = jax.nn.gelu(_dot(a.astype(BF16), win_ref[...]) + bin_ref[...])
    u = z[:, :gw]
    v = z[:, gw:]
    v = (v * lax.rsqrt(jnp.mean(v * v, axis=-1, keepdims=True) + EPS)) * vg_ref[...]
    vb = v.astype(BF16)
    cg = gw // GM_GROUPS
    tm = h.shape[0]
    rows = []
    for c in range(tm // CHUNK):
        cols = []
        for g in range(GM_GROUPS):
            cols.append(_dot(ws_ref[g], vb[c * CHUNK:(c + 1) * CHUNK, g * cg:(g + 1) * cg]))
        rows.append(jnp.concatenate(cols, axis=1) + bsb_ref[...])
    vm = jnp.concatenate(rows, axis=0)
    mix = _dot((u * vm).astype(BF16), wout_ref[...])
    h1 = h + _mod_row(mod_ref, 2, seq) * mix
    h1_ref[...] = h1
    a2 = _modulated(h1, g2_ref[...], _mod_row(mod_ref, 3, seq), _mod_row(mod_ref, 4, seq))
    a2_ref[...] = a2
    route_ref[...] = _route(a2, wrh_ref[...], wrl_ref[...], br_ref[...], ltri_ref[...], carry_ref)
    cnt_ref[...] = carry_ref[...]


def _gmlp_layer(h, y2, route_prev, mod_prev, mod, g1, win, bin_, vg, ws, bsb, wout, g2, router,
                n_ctx, rows_per_req):
    t, d = h.shape
    tm = ROW_TILE
    wrh, wrl, br = router
    ltri = _lower_tri(tm)
    full = lambda shape: pl.BlockSpec(shape, lambda i: (0,) * len(shape))
    row = lambda w: pl.BlockSpec((tm, w), lambda i: (i, 0))
    kern = functools.partial(_gmlp_kernel, n_ctx_tiles=n_ctx // tm, tiles_per_req=rows_per_req // tm)
    args = (h, y2, route_prev, mod_prev, mod, g1, win, bin_, vg, ws, bsb, wout, g2, wrh, wrl, br, ltri)
    in_specs = [row(d), row(2 * d), row(LANES)] + [full(a.shape) for a in args[3:]]
    return pl.pallas_call(
        kern,
        grid=(t // tm,),
        in_specs=in_specs,
        out_specs=[row(d), row(d), row(LANES), full((1, LANES))],
        out_shape=[jax.ShapeDtypeStruct((t, d), F32), jax.ShapeDtypeStruct((t, d), F32),
                   jax.ShapeDtypeStruct((t, LANES), F32), jax.ShapeDtypeStruct((1, LANES), F32)],
        scratch_shapes=[pltpu.VMEM((1, LANES), F32)],
        compiler_params=_cparams(1),
        name="gmlp_router",
    )(*args)


def _final_kernel(h_ref, y2_ref, rt_ref, mod_ref, oc_ref, ol_ref, *, n_ctx_tiles, tiles_per_req):
    i = pl.program_id(0)
    seq = _seq_of_tile(i, n_ctx_tiles, tiles_per_req)
    out = _moe_combine(h_ref[...], y2_ref[...], rt_ref[...], _mod_row(mod_ref, 5, seq))

    @pl.when(i < n_ctx_tiles)
    def _():
        oc_ref[...] = out

    @pl.when(i >= n_ctx_tiles)
    def _():
        ol_ref[...] = out


def _final_combine(h, y2, route, mod, n_ctx, rows_per_req):
    t, d = h.shape
    tm = ROW_TILE
    row = lambda w: pl.BlockSpec((tm, w), lambda i: (i, 0))
    kern = functools.partial(_final_kernel, n_ctx_tiles=n_ctx // tm, tiles_per_req=rows_per_req // tm)
    return pl.pallas_call(
        kern,
        grid=(t // tm,),
        in_specs=[row(d), row(2 * d), row(LANES), pl.BlockSpec(mod.shape, lambda i: (0, 0, 0))],
        out_specs=_split_specs(tm, d, n_ctx // tm),
        out_shape=[jax.ShapeDtypeStruct((n_ctx, d), F32), jax.ShapeDtypeStruct((t - n_ctx, d), F32)],
        compiler_params=_cparams(1),
        name="final_combine",
    )(h, y2, route, mod)


def kernel(x_prompt, x_sample, cache_k, cache_v, c, c_ctx, ada_w, ada_b, norm1_g, norm2_g, attn_wq, attn_wk,
           attn_wv, attn_wo, attn_qnorm, attn_knorm, attn_lam, attn_subln, gm_win, gm_bin, gm_vnorm, gm_ws,
           gm_bs, gm_wout, moe_wc, moe_bc, moe_wf, moe_bf, moe_w1, moe_w3, moe_w2):
    batch, seq_len, d = x_prompt.shape
    n_req, req_len, _ = x_sample.shape
    depth = ada_w.shape[0]
    assert depth == 2 and attn_wq.shape[0] == 1 and gm_win.shape[0] == 1
    n_ctx = batch * seq_len
    assert n_ctx % req_len == 0 and req_len % ROW_TILE == 0 and seq_len == ROW_TILE
    past = cache_k.shape[2]

    h_ctx = x_prompt.reshape(n_ctx, d)
    h_lat = x_sample.reshape(n_req * req_len, d)
    cvec = jnp.concatenate([c_ctx[None, :], c, jnp.zeros((8 - 1 - n_req, d), F32)], axis=0)
    mod = _ada_mod(cvec, ada_w, ada_b)
    routers = [_router_weights(moe_wc[i], moe_bc[i], moe_wf[i], moe_bf[i]) for i in range(depth)]
    row_vec = lambda v: v.reshape(1, -1)

    wqkv = jnp.concatenate([attn_wq[0], attn_wk[0], attn_wv[0]], axis=1).astype(BF16)
    reps = d // HEAD_DIM
    q, k, v, k_new, v_new = _pre_attn(h_ctx, h_lat, mod[0], row_vec(norm1_g[0]), wqkv,
                                      row_vec(jnp.tile(attn_qnorm[0], reps)), row_vec(jnp.tile(attn_knorm[0], reps)),
                                      req_len)
    o_ctx, o_lat = _attention(q, k, v, attn_lam[0], row_vec(attn_subln[0]),
                              cache_k[:, 0].reshape(n_req, past, d), cache_v[:, 0].reshape(n_req, past, d),
                              n_ctx, seq_len, n_req, req_len, layer=0)
    h1, a2, route0, cnt0 = _post_attn(h_ctx, h_lat, o_ctx, o_lat, mod[0], attn_wo[0].astype(BF16),
                                      row_vec(norm2_g[0]), routers[0], req_len)
    y2 = _moe(a2, route0, cnt0, moe_w1, moe_w3, moe_w2, layer=0)

    gw = gm_vnorm.shape[1]
    bsb = jnp.repeat(gm_bs[0].T, gw // GM_GROUPS, axis=1)
    h2, a2, route1, cnt1 = _gmlp_layer(h1, y2, route0, mod[0], mod[1], row_vec(norm1_g[1]), gm_win[0].astype(BF16),
                                       row_vec(gm_bin[0]), row_vec(gm_vnorm[0]), gm_ws[0].astype(BF16), bsb,
                                       gm_wout[0].astype(BF16), row_vec(norm2_g[1]), routers[1], n_ctx, req_len)
    y2 = _moe(a2, route1, cnt1, moe_w1, moe_w3, moe_w2, layer=1)
    y_ctx, y_lat = _final_combine(h2, y2, route1, mod[1], n_ctx, req_len)

    y_prompt = y_ctx.reshape(batch, seq_len, d)
    y_sample = y_lat.reshape(n_req, req_len, d)
    new_cache_k = k_new.reshape(batch, 1, seq_len, N_HEADS, 2, HEAD_DIM)
    new_cache_v = v_new.reshape(batch, 1, seq_len, N_HEADS, V_DIM)
    return (y_prompt, y_sample, new_cache_k, new_cache_v)
```

```python
import functools
import math

import jax
import jax.numpy as jnp
from jax import lax
from jax.experimental import pallas as pl
from jax.experimental.pallas import tpu as pltpu
from jax.experimental.pallas import tpu_sc as plsc

F32 = jnp.float32
BF16 = jnp.bfloat16

N_HEADS = 8
HEAD_DIM = 64
V_DIM = 2 * HEAD_DIM
GRID_W = 64
ROPE_THETA = 10000.0
CHUNK = 128
GM_GROUPS = 8
N_EXPERT_GROUPS = 4
EXPERTS_PER_GROUP = 8
N_EXPERTS = N_EXPERT_GROUPS * EXPERTS_PER_GROUP
EPS = 1e-6

LANES = 128
ROW_TILE = 256
MOE_BLOCK = 256
GATHER_ROWS = 32
ATTN_Q_TILE = 256
VMEM_LIMIT = 56 * 1024 * 1024

R_E1, R_E2, R_W1, R_W2, R_RANK1, R_RANK2 = 0, 1, 2, 3, 4, 5
L_EXPERT0 = N_EXPERT_GROUPS


def _lambda_init(layer):
    return 0.8 - 0.6 * math.exp(-0.3 * layer)


def _cparams(n_axes):
    return pltpu.CompilerParams(dimension_semantics=("arbitrary",) * n_axes, vmem_limit_bytes=VMEM_LIMIT)


def _seq_of_tile(i, n_ctx_tiles, tiles_per_req):
    return jnp.where(i < n_ctx_tiles, 0, 1 + (i - n_ctx_tiles) // tiles_per_req)


def _pick_rows(i, n_ctx_tiles, ctx_ref, lat_ref):
    return jnp.where(i < n_ctx_tiles, ctx_ref[...], lat_ref[...])


def _split_specs(tm, d, n_ctx_tiles):
    return [pl.BlockSpec((tm, d), lambda i: (jnp.minimum(i, n_ctx_tiles - 1), 0)),
            pl.BlockSpec((tm, d), lambda i: (jnp.maximum(i - n_ctx_tiles, 0), 0))]


def _mod_row(mod_ref, part, seq):
    return mod_ref[part, pl.ds(seq, 1), :]


def _modulated(x, g, shift, scale):
    y = x * lax.rsqrt(jnp.mean(x * x, axis=-1, keepdims=True) + EPS)
    return (y * g) * (1.0 + scale) + shift


def _split_bf16(x):
    hi = x.astype(BF16)
    lo = (x - hi.astype(F32)).astype(BF16)
    return hi, lo


def _dot(a, b):
    return jnp.dot(a, b, preferred_element_type=F32)


def _ada_kernel(c_ref, w_ref, b_ref, o_ref):
    c = c_ref[...]
    s = c * jax.nn.sigmoid(c)
    o_ref[...] = _dot(s.astype(BF16), w_ref[...].astype(BF16)) + b_ref[...]


def _ada_mod(cvec, ada_w, ada_b):
    depth, d, d6 = ada_w.shape
    parts = d6 // d
    rows = cvec.shape[0]
    return pl.pallas_call(
        _ada_kernel,
        grid=(depth, parts),
        in_specs=[
            pl.BlockSpec((rows, d), lambda l, j: (0, 0)),
            pl.BlockSpec((None, d, d), lambda l, j: (l, 0, j)),
            pl.BlockSpec((None, 1, d), lambda l, j: (l, 0, j)),
        ],
        out_specs=pl.BlockSpec((None, None, rows, d), lambda l, j: (l, j, 0, 0)),
        out_shape=jax.ShapeDtypeStruct((depth, parts, rows, d), F32),
        compiler_params=_cparams(2),
        name="ada_mod",
    )(cvec, ada_w, ada_b.reshape(depth, 1, d6))


def _pre_attn_kernel(hc_ref, hl_ref, mod_ref, g_ref, w_ref, qg_ref, kg_ref, gs_ref, gb_ref, cos_ref, sin_ref,
                     q_ref, k_ref, v_ref, kf_ref, vf_ref, *, n_ctx_tiles, tiles_per_req):
    i = pl.program_id(0)
    seq = _seq_of_tile(i, n_ctx_tiles, tiles_per_req)
    d = hc_ref.shape[1]
    x = _pick_rows(i, n_ctx_tiles, hc_ref, hl_ref)
    a = _modulated(x, g_ref[...], _mod_row(mod_ref, 0, seq), _mod_row(mod_ref, 1, seq))
    qkv = _dot(a.astype(BF16), w_ref[...])
    q = qkv[:, :d]
    k = qkv[:, d:2 * d]
    v = qkv[:, 2 * d:]
    sq_hi, sq_lo = _split_bf16(jnp.concatenate([q * q, k * k], axis=1))
    gsum = _dot(sq_hi, gs_ref[...]) + _dot(sq_lo, gs_ref[...])
    r_hi, r_lo = _split_bf16(lax.rsqrt(gsum * (1.0 / HEAD_DIM) + EPS))
    rb = _dot(r_hi, gb_ref[...]) + _dot(r_lo, gb_ref[...])
    qn = (q * rb[:, :d]) * qg_ref[...]
    kn = (k * rb[:, d:]) * kg_ref[...]
    v_ref[...] = v.astype(BF16)
    q_scale = HEAD_DIM ** -0.5

    @pl.when(i < n_ctx_tiles)
    def _():
        q_ref[...] = (qn * q_scale).astype(BF16)
        k_ref[...] = kn.astype(BF16)
        kf_ref[...] = kn
        vf_ref[...] = v

    @pl.when(i >= n_ctx_tiles)
    def _():
        cos = cos_ref[...]
        sin = sin_ref[...]
        lane = lax.broadcasted_iota(jnp.int32, cos.shape, 1)
        first = (lane & 31) < 16

        def rope(x):
            outs = []
            for hh in range(d // LANES):
                xs = x[:, hh * LANES:(hh + 1) * LANES]
                rot = jnp.where(first, pltpu.roll(xs, LANES - 16, 1), pltpu.roll(xs, 16, 1))
                outs.append(xs * cos + rot * sin)
            return jnp.concatenate(outs, axis=1)

        q_ref[...] = (rope(qn) * q_scale).astype(BF16)
        k_ref[...] = rope(kn).astype(BF16)


def _rope_tables(n_pos):
    half = HEAD_DIM // 2
    pos = jnp.arange(n_pos, dtype=jnp.int32)
    row = (pos // GRID_W).astype(F32)
    col = (pos % GRID_W).astype(F32)
    inv = 1.0 / (ROPE_THETA ** (jnp.arange(0, half, 2, dtype=F32) / half))
    ang_r = row[:, None] * inv[None, :]
    ang_c = col[:, None] * inv[None, :]
    ang = jnp.concatenate([ang_r, ang_r, ang_c, ang_c], axis=-1)
    quarter = half // 2
    sign = jnp.tile(jnp.concatenate([-jnp.ones((quarter,), F32), jnp.ones((quarter,), F32)]), 2)
    cos = jnp.tile(jnp.cos(ang), (1, 2))
    sin = jnp.tile(jnp.sin(ang) * sign[None, :], (1, 2))
    return cos, sin


def _pre_attn(h_ctx, h_lat, mod, g, wqkv, qg, kg, rows_per_req):
    n_ctx, d = h_ctx.shape
    t = n_ctx + h_lat.shape[0]
    tm = ROW_TILE
    n_ctx_tiles = n_ctx // tm
    tiles_per_req = rows_per_req // tm
    groups = d // HEAD_DIM
    gid = jnp.arange(2 * d, dtype=jnp.int32) // HEAD_DIM
    gs = (gid[:, None] == jnp.arange(LANES, dtype=jnp.int32)[None, :]).astype(BF16)
    gb = gs.T
    assert 2 * groups <= LANES
    cos, sin = _rope_tables(rows_per_req)
    pos_map = lambda i: (jnp.where(i < n_ctx_tiles, 0, (i - n_ctx_tiles) % tiles_per_req), 0)
    ctx_map = lambda i: (jnp.minimum(i, n_ctx_tiles - 1), 0)
    full = lambda shape: pl.BlockSpec(shape, lambda i: (0,) * len(shape))
    row = pl.BlockSpec((tm, d), lambda i: (i, 0))
    kern = functools.partial(_pre_attn_kernel, n_ctx_tiles=n_ctx_tiles, tiles_per_req=tiles_per_req)
    return pl.pallas_call(
        kern,
        grid=(t // tm,),
        in_specs=_split_specs(tm, d, n_ctx_tiles) + [
                  full(mod.shape), full((1, d)), full(wqkv.shape), full((1, d)), full((1, d)),
                  full(gs.shape), full(gb.shape),
                  pl.BlockSpec((tm, LANES), pos_map), pl.BlockSpec((tm, LANES), pos_map)],
        out_specs=[row, row, row, pl.BlockSpec((tm, d), ctx_map), pl.BlockSpec((tm, d), ctx_map)],
        out_shape=[jax.ShapeDtypeStruct((t, d), BF16)] * 3 + [jax.ShapeDtypeStruct((n_ctx, d), F32)] * 2,
        compiler_params=_cparams(1),
        name="pre_attn",
    )(h_ctx, h_lat, mod, g, wqkv, qg, kg, gs, gb, cos, sin)


def _attn_kernel(*refs, n_heads, has_cache, layer):
    if has_cache:
        lam_ref, sub_ref, q_ref, kn_ref, vn_ref, kc_ref, vc_ref, o_ref = refs
    else:
        lam_ref, sub_ref, q_ref, kn_ref, vn_ref, o_ref = refs
    lf = lam_ref[...]
    lam = (jnp.exp(jnp.sum(lf[0:1] * lf[1:2], axis=-1, keepdims=True))
           - jnp.exp(jnp.sum(lf[2:3] * lf[3:4], axis=-1, keepdims=True)) + _lambda_init(layer))
    nt = (((1,), (1,)), ((), ()))
    tq = q_ref.shape[0]
    lane = lax.broadcasted_iota(jnp.int32, (tq, LANES), 1)
    for hh in range(n_heads):
        sl = slice(hh * LANES, (hh + 1) * LANES)
        qh = q_ref[:, sl]
        zero = jnp.zeros_like(qh)
        qs = (jnp.where(lane < HEAD_DIM, qh, zero), jnp.where(lane >= HEAD_DIM, qh, zero))
        keys = [kn_ref[:, sl]]
        vals = [vn_ref[:, sl]]
        if has_cache:
            keys.append(kc_ref[:, sl].astype(BF16))
            vals.append(vc_ref[:, sl].astype(BF16))
        probs = []
        for qm in qs:
            s = [lax.dot_general(qm, kk, nt, preferred_element_type=F32) for kk in keys]
            m = functools.reduce(jnp.maximum, [jnp.max(x, axis=-1, keepdims=True) for x in s])
            e = [jnp.exp(x - m) for x in s]
            den = functools.reduce(lambda u, w: u + w, [jnp.sum(x, axis=-1, keepdims=True) for x in e])
            probs.append((e, 1.0 / den))
        (e0, r0), (e1, r1) = probs
        o = None
        for j, vv in enumerate(vals):
            aj = e0[j] * r0 - lam * (e1[j] * r1)
            oj = _dot(aj.astype(BF16), vv)
            o = oj if o is None else o + oj
        o = o * lax.rsqrt(jnp.mean(o * o, axis=-1, keepdims=True) + EPS)
        o = (o * sub_ref[...]) * (1.0 - _lambda_init(layer))
        o_ref[:, sl] = o.astype(BF16)


def _attention(q, k, v, lam_p, subln, cache_k, cache_v, n_ctx, ctx_len, n_req, req_len, layer):
    t, d = q.shape
    small = lambda shape: pl.BlockSpec(shape, lambda *_: (0,) * len(shape))
    n_ctx_req = n_ctx // ctx_len
    blk = pl.BlockSpec((ctx_len, d), lambda b: (b, 0))
    o_ctx = pl.pallas_call(
        functools.partial(_attn_kernel, n_heads=N_HEADS, has_cache=False, layer=layer),
        grid=(n_ctx_req,),
        in_specs=[small(lam_p.shape), small(subln.shape), blk, blk, blk],
        out_specs=blk,
        out_shape=jax.ShapeDtypeStruct((n_ctx, d), BF16),
        compiler_params=_cparams(1),
        name="attn_ctx",
    )(lam_p, subln, q, k, v)

    tq = ATTN_Q_TILE
    nq = req_len // tq
    q0 = n_ctx // tq
    r0 = n_ctx // req_len
    past = cache_k.shape[1]
    small3 = lambda shape: pl.BlockSpec(shape, lambda b, h, i: (0,) * len(shape))
    qspec = pl.BlockSpec((tq, LANES), lambda b, h, i: (q0 + b * nq + i, h))
    kvspec = pl.BlockSpec((req_len, LANES), lambda b, h, i: (r0 + b, h))
    cspec = pl.BlockSpec((None, past, LANES), lambda b, h, i: (b, 0, h))
    o_lat = pl.pallas_call(
        functools.partial(_attn_kernel, n_heads=1, has_cache=True, layer=layer),
        grid=(n_req, N_HEADS, nq),
        in_specs=[small3(lam_p.shape), pl.BlockSpec((1, LANES), lambda b, h, i: (0, 0)),
                  qspec, kvspec, kvspec, cspec, cspec],
        out_specs=pl.BlockSpec((tq, LANES), lambda b, h, i: (b * nq + i, h)),
        out_shape=jax.ShapeDtypeStruct((n_req * req_len, d), BF16),
        compiler_params=_cparams(3),
        name="attn_latent",
    )(lam_p, subln, q, k, v, cache_k, cache_v)
    return o_ctx, o_lat


def _route(a2, wr_hi, wr_lo, br, ltri, carry_ref):
    tm = a2.shape[0]
    a_hi, a_lo = _split_bf16(a2)
    logits = _dot(a_hi, wr_hi) + (_dot(a_lo, wr_hi) + _dot(a_hi, wr_lo)) + br
    lane_i = lax.broadcasted_iota(jnp.int32, (tm, LANES), 1)
    lane = lane_i.astype(F32)
    neg = jnp.full((tm, LANES), -jnp.inf, F32)
    big = jnp.full((tm, LANES), float(LANES), F32)
    first_lane = lambda mask: jnp.min(jnp.where(mask, lane, big), axis=-1, keepdims=True)

    lc = jnp.where(lane_i < N_EXPERT_GROUPS, logits, neg)
    mc = jnp.max(lc, axis=-1, keepdims=True)
    pg = 1.0 / jnp.sum(jnp.exp(lc - mc), axis=-1, keepdims=True)
    gi = first_lane(lc == mc)
    assert EXPERTS_PER_GROUP == 8
    grp = lax.shift_right_arithmetic(lane_i - L_EXPERT0, 3).astype(F32)
    in_group = (lane_i >= L_EXPERT0) & (lane_i < L_EXPERT0 + N_EXPERTS) & (grp == gi)
    ls = jnp.where(in_group, logits, neg)
    t1 = jnp.max(ls, axis=-1, keepdims=True)
    i1 = first_lane(ls == t1)
    ls2 = jnp.where(lane == i1, neg, ls)
    t2 = jnp.max(ls2, axis=-1, keepdims=True)
    i2 = first_lane(ls2 == t2)
    ex = jnp.exp(t2 - t1)
    w1 = pg * (1.0 / (1.0 + ex))
    w2 = pg * (ex / (1.0 + ex))
    e1 = i1 - float(L_EXPERT0)
    e2 = i2 - float(L_EXPERT0)
    oh1 = lane == e1
    oh2 = lane == e2
    onehot = oh1.astype(F32) + oh2.astype(F32)
    before = _dot(ltri, onehot.astype(BF16)) + carry_ref[...]
    zero = jnp.zeros_like(before)
    rank1 = jnp.sum(jnp.where(oh1, before, zero), axis=-1, keepdims=True)
    rank2 = jnp.sum(jnp.where(oh2, before, zero), axis=-1, keepdims=True)
    carry_ref[...] = carry_ref[...] + jnp.sum(onehot, axis=0, keepdims=True)
    slab = jnp.zeros((tm, LANES), F32)
    for ln, val in ((R_E1, e1), (R_E2, e2), (R_W1, w1), (R_W2, w2), (R_RANK1, rank1), (R_RANK2, rank2)):
        slab = jnp.where(lane_i == ln, val, slab)
    return slab


def _router_weights(wc, bc, wf, bf_):
    d = wc.shape[0]
    pad = LANES - N_EXPERT_GROUPS - N_EXPERTS
    w = jnp.concatenate([wc, wf, jnp.zeros((d, pad), F32)], axis=1)
    b = jnp.concatenate([bc, bf_, jnp.zeros((pad,), F32)])[None, :]
    hi = w.astype(BF16)
    lo = (w - hi.astype(F32)).astype(BF16)
    return hi, lo, b


def _lower_tri(n):
    r = jnp.arange(n, dtype=jnp.int32)
    return (r[None, :] < r[:, None]).astype(BF16)


def _post_attn_kernel(hc_ref, hl_ref, oc_ref, ol_ref, mod_ref, wo_ref, g2_ref, wrh_ref, wrl_ref, br_ref, ltri_ref,
                      h1_ref, a2_ref, route_ref, cnt_ref, carry_ref, *, n_ctx_tiles, tiles_per_req):
    i = pl.program_id(0)
    seq = _seq_of_tile(i, n_ctx_tiles, tiles_per_req)

    @pl.when(i == 0)
    def _():
        carry_ref[...] = jnp.zeros_like(carry_ref)

    o = _pick_rows(i, n_ctx_tiles, oc_ref, ol_ref)
    h1 = _pick_rows(i, n_ctx_tiles, hc_ref, hl_ref) + _mod_row(mod_ref, 2, seq) * _dot(o, wo_ref[...])
    h1_ref[...] = h1
    a2 = _modulated(h1, g2_ref[...], _mod_row(mod_ref, 3, seq), _mod_row(mod_ref, 4, seq))
    a2_ref[...] = a2
    route_ref[...] = _route(a2, wrh_ref[...], wrl_ref[...], br_ref[...], ltri_ref[...], carry_ref)
    cnt_ref[...] = carry_ref[...]


def _post_attn(h_ctx, h_lat, o_ctx, o_lat, mod, wo, g2, router, rows_per_req):
    n_ctx, d = h_ctx.shape
    t = n_ctx + h_lat.shape[0]
    tm = ROW_TILE
    wrh, wrl, br = router
    ltri = _lower_tri(tm)
    full = lambda shape: pl.BlockSpec(shape, lambda i: (0,) * len(shape))
    row = pl.BlockSpec((tm, d), lambda i: (i, 0))
    kern = functools.partial(_post_attn_kernel, n_ctx_tiles=n_ctx // tm, tiles_per_req=rows_per_req // tm)
    return pl.pallas_call(
        kern,
        grid=(t // tm,),
        in_specs=_split_specs(tm, d, n_ctx // tm) + _split_specs(tm, d, n_ctx // tm) + [
                  full(mod.shape), full(wo.shape), full((1, d)), full(wrh.shape), full(wrl.shape),
                  full(br.shape), full(ltri.shape)],
        out_specs=[row, row, pl.BlockSpec((tm, LANES), lambda i: (i, 0)), full((1, LANES))],
        out_shape=[jax.ShapeDtypeStruct((t, d), F32), jax.ShapeDtypeStruct((t, d), F32),
                   jax.ShapeDtypeStruct((t, LANES), F32), jax.ShapeDtypeStruct((1, LANES), F32)],
        scratch_shapes=[pltpu.VMEM((1, LANES), F32)],
        compiler_params=_cparams(1),
        name="post_attn_router",
    )(h_ctx, h_lat, o_ctx, o_lat, mod, wo, g2, wrh, wrl, br, ltri)


def _gather_rows(src, idx):
    n = idx.shape[0]
    d = src.shape[1]
    w = GATHER_ROWS
    mesh = plsc.VectorSubcoreMesh(core_axis_name="core", subcore_axis_name="subcore")

    @pl.kernel(out_type=jax.ShapeDtypeStruct((n, d), src.dtype), mesh=mesh)
    def gather(src_hbm, idx_hbm, out_hbm):
        def body(idx_vmem, out_vmem):
            pltpu.sync_copy(src_hbm.at[idx_vmem.at[0, pl.ds(0, w)]], out_vmem)

        pltpu.emit_pipeline(
            body,
            grid=(n // w,),
            in_specs=[pl.BlockSpec((1, LANES), lambda i: (i, 0))],
            out_specs=[pl.BlockSpec((w, d), lambda i: (i, 0))],
            core_axis_name=("core", "subcore"),
            dimension_semantics=(pltpu.PARALLEL,),
        )(idx_hbm, out_hbm)

    idx_rows = jnp.pad(idx.reshape(n // w, w), ((0, 0), (0, LANES - w)))
    return gather(src, idx_rows)


def _expert_kernel(be_ref, nb_ref, x_ref, w1_ref, w3_ref, w2_ref, y_ref, w1b, w3b, w2b):
    b = pl.program_id(0)
    prev = be_ref[jnp.maximum(b - 1, 0)]
    used = b < nb_ref[0]

    @pl.when(used & ((b == 0) | (be_ref[b] != prev)))
    def _():
        w1b[...] = w1_ref[...].astype(BF16)
        w3b[...] = w3_ref[...].astype(BF16)
        w2b[...] = w2_ref[...].astype(BF16)

    @pl.when(used)
    def _():
        x = x_ref[...].astype(BF16)
        hb = jax.nn.silu(_dot(x, w1b[...])) * _dot(x, w3b[...])
        y_ref[...] = _dot(hb.astype(BF16), w2b[...])

    @pl.when(jnp.logical_not(used))
    def _():
        y_ref[...] = jnp.zeros_like(y_ref)


def _expert_mlps(xs, block_e, n_used, w1, w3, w2, layer):
    npad, d = xs.shape
    ff = w1.shape[3]
    bm = MOE_BLOCK
    grid_spec = pltpu.PrefetchScalarGridSpec(
        num_scalar_prefetch=2,
        grid=(npad // bm,),
        in_specs=[pl.BlockSpec((bm, d), lambda b, be, nb: (b, 0)),
                  pl.BlockSpec((None, None, d, ff), lambda b, be, nb: (layer, be[b], 0, 0)),
                  pl.BlockSpec((None, None, d, ff), lambda b, be, nb: (layer, be[b], 0, 0)),
                  pl.BlockSpec((None, None, ff, d), lambda b, be, nb: (layer, be[b], 0, 0))],
        out_specs=pl.BlockSpec((bm, d), lambda b, be, nb: (b, 0)),
        scratch_shapes=[pltpu.VMEM((d, ff), BF16), pltpu.VMEM((d, ff), BF16), pltpu.VMEM((ff, d), BF16)],
    )
    return pl.pallas_call(
        _expert_kernel,
        grid_spec=grid_spec,
        out_shape=jax.ShapeDtypeStruct((npad, d), F32),
        compiler_params=_cparams(1),
        name="expert_mlps",
    )(block_e, n_used, xs, w1, w3, w2)


def _moe(a2, route, counts, w1, w3, w2, layer):
    t, d = a2.shape
    bm = MOE_BLOCK
    n = 2 * t
    e = route[:, R_E1:R_E2 + 1].astype(jnp.int32).reshape(n)
    rank = route[:, R_RANK1:R_RANK2 + 1].astype(jnp.int32).reshape(n)
    cnt = counts[0, :N_EXPERTS].astype(jnp.int32)
    padded = ((cnt + bm - 1) // bm) * bm
    pend = jnp.cumsum(padded)
    pstart = pend - padded
    pos = jnp.sum(jnp.where(e[:, None] == jnp.arange(N_EXPERTS, dtype=jnp.int32)[None, :], pstart[None, :], 0),
                  axis=1) + rank
    n_blocks = n // bm + N_EXPERTS
    npad = n_blocks * bm
    starts = jnp.arange(n_blocks, dtype=jnp.int32) * bm
    n_used = pend[-1:] // bm
    last_start = (n_used[0] - 1) * bm
    block_e = jnp.sum((pend[None, :] <= jnp.minimum(starts, last_start)[:, None]).astype(jnp.int32), axis=1)
    tok_sorted = (jnp.arange(npad, dtype=jnp.int32) % t).at[pos].set(jnp.arange(n, dtype=jnp.int32) // 2)
    xs = _gather_rows(a2, tok_sorted)
    ys = _expert_mlps(xs, block_e, n_used, w1, w3, w2, layer)
    return _gather_rows(ys, jnp.concatenate([pos[0::2], pos[1::2]]))


def _moe_combine(h, y_first, y_second, route, g2):
    w1 = route[:, R_W1:R_W1 + 1]
    w2 = route[:, R_W2:R_W2 + 1]
    return h + g2 * (w1 * y_first + w2 * y_second)


def _gmlp_kernel(h_ref, ya_ref, yb_ref, rt_ref, modp_ref, mod_ref, g1_ref, win_ref, bin_ref, vg_ref, ws_ref, bsb_ref,
                 wout_ref, g2_ref, wrh_ref, wrl_ref, br_ref, ltri_ref,
                 h1_ref, a2_ref, route_ref, cnt_ref, carry_ref, *, n_ctx_tiles, tiles_per_req):
    i = pl.program_id(0)
    seq = _seq_of_tile(i, n_ctx_tiles, tiles_per_req)

    @pl.when(i == 0)
    def _():
        carry_ref[...] = jnp.zeros_like(carry_ref)

    h = _moe_combine(h_ref[...], ya_ref[...], yb_ref[...], rt_ref[...], _mod_row(modp_ref, 5, seq))
    a = _modulated(h, g1_ref[...], _mod_row(mod_ref, 0, seq), _mod_row(mod_ref, 1, seq))
    gw = vg_ref.shape[1]
    z = jax.nn.gelu(_dot(a.astype(BF16), win_ref[...]) + bin_ref[...])
    u = z[:, :gw]
    v = z[:, gw:]
    v = (v * lax.rsqrt(jnp.mean(v * v, axis=-1, keepdims=True) + EPS)) * vg_ref[...]
    vb = v.astype(BF16)
    cg = gw // GM_GROUPS
    tm = h.shape[0]
    rows = []
    for c in range(tm // CHUNK):
        cols = []
        for g in range(GM_GROUPS):
            cols.append(_dot(ws_ref[g], vb[c * CHUNK:(c + 1) * CHUNK, g * cg:(g + 1) * cg]))
        rows.append(jnp.concatenate(cols, axis=1) + bsb_ref[...])
    vm = jnp.concatenate(rows, axis=0)
    mix = _dot((u * vm).astype(BF16), wout_ref[...])
    h1 = h + _mod_row(mod_ref, 2, seq) * mix
    h1_ref[...] = h1
    a2 = _modulated(h1, g2_ref[...], _mod_row(mod_ref, 3, seq), _mod_row(mod_ref, 4, seq))
    a2_ref[...] = a2
    route_ref[...] = _route(a2, wrh_ref[...], wrl_ref[...], br_ref[...], ltri_ref[...], carry_ref)
    cnt_ref[...] = carry_ref[...]


def _gmlp_layer(h, y2, route_prev, mod_prev, mod, g1, win, bin_, vg, ws, bsb, wout, g2, router,
                n_ctx, rows_per_req):
    t, d = h.shape
    tm = ROW_TILE
    wrh, wrl, br = router
    ltri = _lower_tri(tm)
    full = lambda shape: pl.BlockSpec(shape, lambda i: (0,) * len(shape))
    row = lambda w: pl.BlockSpec((tm, w), lambda i: (i, 0))
    kern = functools.partial(_gmlp_kernel, n_ctx_tiles=n_ctx // tm, tiles_per_req=rows_per_req // tm)
    args = (h, y2, y2, route_prev, mod_prev, mod, g1, win, bin_, vg, ws, bsb, wout, g2, wrh, wrl, br, ltri)
    second = pl.BlockSpec((tm, d), lambda i: (i + t // tm, 0))
    in_specs = [row(d), row(d), second, row(LANES)] + [full(a.shape) for a in args[4:]]
    return pl.pallas_call(
        kern,
        grid=(t // tm,),
        in_specs=in_specs,
        out_specs=[row(d), row(d), row(LANES), full((1, LANES))],
        out_shape=[jax.ShapeDtypeStruct((t, d), F32), jax.ShapeDtypeStruct((t, d), F32),
                   jax.ShapeDtypeStruct((t, LANES), F32), jax.ShapeDtypeStruct((1, LANES), F32)],
        scratch_shapes=[pltpu.VMEM((1, LANES), F32)],
        compiler_params=_cparams(1),
        name="gmlp_router",
    )(*args)


def _final_kernel(h_ref, ya_ref, yb_ref, rt_ref, mod_ref, oc_ref, ol_ref, *, n_ctx_tiles, tiles_per_req):
    i = pl.program_id(0)
    seq = _seq_of_tile(i, n_ctx_tiles, tiles_per_req)
    out = _moe_combine(h_ref[...], ya_ref[...], yb_ref[...], rt_ref[...], _mod_row(mod_ref, 5, seq))

    @pl.when(i < n_ctx_tiles)
    def _():
        oc_ref[...] = out

    @pl.when(i >= n_ctx_tiles)
    def _():
        ol_ref[...] = out


def _final_combine(h, y2, route, mod, n_ctx, rows_per_req):
    t, d = h.shape
    tm = ROW_TILE
    row = lambda w: pl.BlockSpec((tm, w), lambda i: (i, 0))
    kern = functools.partial(_final_kernel, n_ctx_tiles=n_ctx // tm, tiles_per_req=rows_per_req // tm)
    return pl.pallas_call(
        kern,
        grid=(t // tm,),
        in_specs=[row(d), row(d), pl.BlockSpec((tm, d), lambda i: (i + t // tm, 0)), row(LANES),
                  pl.BlockSpec(mod.shape, lambda i: (0, 0, 0))],
        out_specs=_split_specs(tm, d, n_ctx // tm),
        out_shape=[jax.ShapeDtypeStruct((n_ctx, d), F32), jax.ShapeDtypeStruct((t - n_ctx, d), F32)],
        compiler_params=_cparams(1),
        name="final_combine",
    )(h, y2, y2, route, mod)


def kernel(x_prompt, x_sample, cache_k, cache_v, c, c_ctx, ada_w, ada_b, norm1_g, norm2_g, attn_wq, attn_wk,
           attn_wv, attn_wo, attn_qnorm, attn_knorm, attn_lam, attn_subln, gm_win, gm_bin, gm_vnorm, gm_ws,
           gm_bs, gm_wout, moe_wc, moe_bc, moe_wf, moe_bf, moe_w1, moe_w3, moe_w2):
    batch, seq_len, d = x_prompt.shape
    n_req, req_len, _ = x_sample.shape
    depth = ada_w.shape[0]
    assert depth == 2 and attn_wq.shape[0] == 1 and gm_win.shape[0] == 1
    n_ctx = batch * seq_len
    assert n_ctx % req_len == 0 and req_len % ROW_TILE == 0 and seq_len == ROW_TILE
    past = cache_k.shape[2]

    h_ctx = x_prompt.reshape(n_ctx, d)
    h_lat = x_sample.reshape(n_req * req_len, d)
    cvec = jnp.concatenate([c_ctx[None, :], c, jnp.zeros((8 - 1 - n_req, d), F32)], axis=0)
    mod = _ada_mod(cvec, ada_w, ada_b)
    routers = [_router_weights(moe_wc[i], moe_bc[i], moe_wf[i], moe_bf[i]) for i in range(depth)]
    row_vec = lambda v: v.reshape(1, -1)

    wqkv = jnp.concatenate([attn_wq[0], attn_wk[0], attn_wv[0]], axis=1).astype(BF16)
    reps = d // HEAD_DIM
    q, k, v, k_new, v_new = _pre_attn(h_ctx, h_lat, mod[0], row_vec(norm1_g[0]), wqkv,
                                      row_vec(jnp.tile(attn_qnorm[0], reps)), row_vec(jnp.tile(attn_knorm[0], reps)),
                                      req_len)
    o_ctx, o_lat = _attention(q, k, v, attn_lam[0], row_vec(attn_subln[0]),
                              cache_k[:, 0].reshape(n_req, past, d), cache_v[:, 0].reshape(n_req, past, d),
                              n_ctx, seq_len, n_req, req_len, layer=0)
    h1, a2, route0, cnt0 = _post_attn(h_ctx, h_lat, o_ctx, o_lat, mod[0], attn_wo[0].astype(BF16),
                                      row_vec(norm2_g[0]), routers[0], req_len)
    y2 = _moe(a2, route0, cnt0, moe_w1, moe_w3, moe_w2, layer=0)

    gw = gm_vnorm.shape[1]
    bsb = jnp.repeat(gm_bs[0].T, gw // GM_GROUPS, axis=1)
    h2, a2, route1, cnt1 = _gmlp_layer(h1, y2, route0, mod[0], mod[1], row_vec(norm1_g[1]), gm_win[0].astype(BF16),
                                       row_vec(gm_bin[0]), row_vec(gm_vnorm[0]), gm_ws[0].astype(BF16), bsb,
                                       gm_wout[0].astype(BF16), row_vec(norm2_g[1]), routers[1], n_ctx, req_len)
    y2 = _moe(a2, route1, cnt1, moe_w1, moe_w3, moe_w2, layer=1)
    y_ctx, y_lat = _final_combine(h2, y2, route1, mod[1], n_ctx, req_len)

    y_prompt = y_ctx.reshape(batch, seq_len, d)
    y_sample = y_lat.reshape(n_req, req_len, d)
    new_cache_k = k_new.reshape(batch, 1, seq_len, N_HEADS, 2, HEAD_DIM)
    new_cache_v = v_new.reshape(batch, 1, seq_len, N_HEADS, V_DIM)
    return (y_prompt, y_sample, new_cache_k, new_cache_v)
```

```python
import functools
import math

import jax
import jax.numpy as jnp
from jax import lax
from jax.experimental import pallas as pl
from jax.experimental.pallas import tpu as pltpu
from jax.experimental.pallas import tpu_sc as plsc

F32 = jnp.float32
BF16 = jnp.bfloat16

N_HEADS = 8
HEAD_DIM = 64
V_DIM = 2 * HEAD_DIM
GRID_W = 64
ROPE_THETA = 10000.0
CHUNK = 128
GM_GROUPS = 8
N_EXPERT_GROUPS = 4
EXPERTS_PER_GROUP = 8
N_EXPERTS = N_EXPERT_GROUPS * EXPERTS_PER_GROUP
EPS = 1e-6

LANES = 128
ROW_TILE = 256
MOE_BLOCK = 256
GATHER_ROWS = 32
ATTN_Q_TILE = 256
VMEM_LIMIT = 56 * 1024 * 1024

R_E1, R_E2, R_W1, R_W2, R_RANK1, R_RANK2 = 0, 1, 2, 3, 4, 5
L_EXPERT0 = N_EXPERT_GROUPS


def _lambda_init(layer):
    return 0.8 - 0.6 * math.exp(-0.3 * layer)


def _cparams(n_axes):
    return pltpu.CompilerParams(dimension_semantics=("arbitrary",) * n_axes, vmem_limit_bytes=VMEM_LIMIT)


def _seq_of_tile(i, n_ctx_tiles, tiles_per_req):
    return jnp.where(i < n_ctx_tiles, 0, 1 + (i - n_ctx_tiles) // tiles_per_req)


def _pick_rows(i, n_ctx_tiles, ctx_ref, lat_ref):
    return jnp.where(i < n_ctx_tiles, ctx_ref[...], lat_ref[...])


def _split_specs(tm, d, n_ctx_tiles):
    return [pl.BlockSpec((tm, d), lambda i: (jnp.minimum(i, n_ctx_tiles - 1), 0)),
            pl.BlockSpec((tm, d), lambda i: (jnp.maximum(i - n_ctx_tiles, 0), 0))]


def _mod_row(mod_ref, part, seq):
    return mod_ref[part, pl.ds(seq, 1), :]


def _modulated(x, g, shift, scale):
    y = x * lax.rsqrt(jnp.mean(x * x, axis=-1, keepdims=True) + EPS)
    return (y * g) * (1.0 + scale) + shift


def _split_bf16(x):
    hi = x.astype(BF16)
    lo = (x - hi.astype(F32)).astype(BF16)
    return hi, lo


def _dot(a, b):
    return jnp.dot(a, b, preferred_element_type=F32)


def _ada_kernel(c_ref, w_ref, b_ref, o_ref):
    c = c_ref[...]
    s = c * jax.nn.sigmoid(c)
    o_ref[...] = _dot(s.astype(BF16), w_ref[...].astype(BF16)) + b_ref[...]


def _ada_mod(cvec, ada_w, ada_b):
    depth, d, d6 = ada_w.shape
    parts = d6 // d
    rows = cvec.shape[0]
    return pl.pallas_call(
        _ada_kernel,
        grid=(depth, parts),
        in_specs=[
            pl.BlockSpec((rows, d), lambda l, j: (0, 0)),
            pl.BlockSpec((None, d, d), lambda l, j: (l, 0, j)),
            pl.BlockSpec((None, 1, d), lambda l, j: (l, 0, j)),
        ],
        out_specs=pl.BlockSpec((None, None, rows, d), lambda l, j: (l, j, 0, 0)),
        out_shape=jax.ShapeDtypeStruct((depth, parts, rows, d), F32),
        compiler_params=_cparams(2),
        name="ada_mod",
    )(cvec, ada_w, ada_b.reshape(depth, 1, d6))


def _pre_attn_kernel(hc_ref, hl_ref, mod_ref, g_ref, w_ref, qg_ref, kg_ref, gs_ref, gb_ref, cos_ref, sin_ref,
                     q_ref, k_ref, v_ref, kf_ref, vf_ref, *, n_ctx_tiles, tiles_per_req):
    i = pl.program_id(0)
    seq = _seq_of_tile(i, n_ctx_tiles, tiles_per_req)
    d = hc_ref.shape[1]
    x = _pick_rows(i, n_ctx_tiles, hc_ref, hl_ref)
    a = _modulated(x, g_ref[...], _mod_row(mod_ref, 0, seq), _mod_row(mod_ref, 1, seq))
    qkv = _dot(a.astype(BF16), w_ref[...])
    q = qkv[:, :d]
    k = qkv[:, d:2 * d]
    v = qkv[:, 2 * d:]
    sq_hi, sq_lo = _split_bf16(jnp.concatenate([q * q, k * k], axis=1))
    gsum = _dot(sq_hi, gs_ref[...]) + _dot(sq_lo, gs_ref[...])
    r_hi, r_lo = _split_bf16(lax.rsqrt(gsum * (1.0 / HEAD_DIM) + EPS))
    rb = _dot(r_hi, gb_ref[...]) + _dot(r_lo, gb_ref[...])
    qn = (q * rb[:, :d]) * qg_ref[...]
    kn = (k * rb[:, d:]) * kg_ref[...]
    v_ref[...] = v.astype(BF16)
    q_scale = HEAD_DIM ** -0.5

    @pl.when(i < n_ctx_tiles)
    def _():
        q_ref[...] = (qn * q_scale).astype(BF16)
        k_ref[...] = kn.astype(BF16)
        kf_ref[...] = kn
        vf_ref[...] = v

    @pl.when(i >= n_ctx_tiles)
    def _():
        cos = cos_ref[...]
        sin = sin_ref[...]
        lane = lax.broadcasted_iota(jnp.int32, cos.shape, 1)
        first = (lane & 31) < 16

        def rope(x):
            outs = []
            for hh in range(d // LANES):
                xs = x[:, hh * LANES:(hh + 1) * LANES]
                rot = jnp.where(first, pltpu.roll(xs, LANES - 16, 1), pltpu.roll(xs, 16, 1))
                outs.append(xs * cos + rot * sin)
            return jnp.concatenate(outs, axis=1)

        q_ref[...] = (rope(qn) * q_scale).astype(BF16)
        k_ref[...] = rope(kn).astype(BF16)


def _rope_tables(n_pos):
    half = HEAD_DIM // 2
    pos = jnp.arange(n_pos, dtype=jnp.int32)
    row = (pos // GRID_W).astype(F32)
    col = (pos % GRID_W).astype(F32)
    inv = 1.0 / (ROPE_THETA ** (jnp.arange(0, half, 2, dtype=F32) / half))
    ang_r = row[:, None] * inv[None, :]
    ang_c = col[:, None] * inv[None, :]
    ang = jnp.concatenate([ang_r, ang_r, ang_c, ang_c], axis=-1)
    quarter = half // 2
    sign = jnp.tile(jnp.concatenate([-jnp.ones((quarter,), F32), jnp.ones((quarter,), F32)]), 2)
    cos = jnp.tile(jnp.cos(ang), (1, 2))
    sin = jnp.tile(jnp.sin(ang) * sign[None, :], (1, 2))
    return cos, sin


def _pre_attn(h_ctx, h_lat, mod, g, wqkv, qg, kg, rows_per_req):
    n_ctx, d = h_ctx.shape
    t = n_ctx + h_lat.shape[0]
    tm = ROW_TILE
    n_ctx_tiles = n_ctx // tm
    tiles_per_req = rows_per_req // tm
    groups = d // HEAD_DIM
    gid = jnp.arange(2 * d, dtype=jnp.int32) // HEAD_DIM
    gs = (gid[:, None] == jnp.arange(LANES, dtype=jnp.int32)[None, :]).astype(BF16)
    gb = gs.T
    assert 2 * groups <= LANES
    cos, sin = _rope_tables(rows_per_req)
    pos_map = lambda i: (jnp.where(i < n_ctx_tiles, 0, (i - n_ctx_tiles) % tiles_per_req), 0)
    ctx_map = lambda i: (jnp.minimum(i, n_ctx_tiles - 1), 0)
    full = lambda shape: pl.BlockSpec(shape, lambda i: (0,) * len(shape))
    row = pl.BlockSpec((tm, d), lambda i: (i, 0))
    kern = functools.partial(_pre_attn_kernel, n_ctx_tiles=n_ctx_tiles, tiles_per_req=tiles_per_req)
    return pl.pallas_call(
        kern,
        grid=(t // tm,),
        in_specs=_split_specs(tm, d, n_ctx_tiles) + [
                  full(mod.shape), full((1, d)), full(wqkv.shape), full((1, d)), full((1, d)),
                  full(gs.shape), full(gb.shape),
                  pl.BlockSpec((tm, LANES), pos_map), pl.BlockSpec((tm, LANES), pos_map)],
        out_specs=[row, row, row, pl.BlockSpec((tm, d), ctx_map), pl.BlockSpec((tm, d), ctx_map)],
        out_shape=[jax.ShapeDtypeStruct((t, d), BF16)] * 3 + [jax.ShapeDtypeStruct((n_ctx, d), F32)] * 2,
        compiler_params=_cparams(1),
        name="pre_attn",
    )(h_ctx, h_lat, mod, g, wqkv, qg, kg, gs, gb, cos, sin)


def _attn_kernel(*refs, n_heads, has_cache, layer):
    if has_cache:
        lam_ref, sub_ref, q_ref, kn_ref, vn_ref, kc_ref, vc_ref, o_ref = refs
    else:
        lam_ref, sub_ref, q_ref, kn_ref, vn_ref, o_ref = refs
    lf = lam_ref[...]
    lam = (jnp.exp(jnp.sum(lf[0:1] * lf[1:2], axis=-1, keepdims=True))
           - jnp.exp(jnp.sum(lf[2:3] * lf[3:4], axis=-1, keepdims=True)) + _lambda_init(layer))
    nt = (((1,), (1,)), ((), ()))
    tq = q_ref.shape[0]
    lane = lax.broadcasted_iota(jnp.int32, (tq, LANES), 1)
    for hh in range(n_heads):
        sl = slice(hh * LANES, (hh + 1) * LANES)
        qh = q_ref[:, sl]
        zero = jnp.zeros_like(qh)
        qs = (jnp.where(lane < HEAD_DIM, qh, zero), jnp.where(lane >= HEAD_DIM, qh, zero))
        keys = [kn_ref[:, sl]]
        vals = [vn_ref[:, sl]]
        if has_cache:
            keys.append(kc_ref[:, sl].astype(BF16))
            vals.append(vc_ref[:, sl].astype(BF16))
        probs = []
        for qm in qs:
            s = [lax.dot_general(qm, kk, nt, preferred_element_type=F32) for kk in keys]
            m = functools.reduce(jnp.maximum, [jnp.max(x, axis=-1, keepdims=True) for x in s])
            e = [jnp.exp(x - m) for x in s]
            den = functools.reduce(lambda u, w: u + w, [jnp.sum(x, axis=-1, keepdims=True) for x in e])
            probs.append((e, 1.0 / den))
        (e0, r0), (e1, r1) = probs
        o = None
        for j, vv in enumerate(vals):
            aj = e0[j] * r0 - lam * (e1[j] * r1)
            oj = _dot(aj.astype(BF16), vv)
            o = oj if o is None else o + oj
        o = o * lax.rsqrt(jnp.mean(o * o, axis=-1, keepdims=True) + EPS)
        o = (o * sub_ref[...]) * (1.0 - _lambda_init(layer))
        o_ref[:, sl] = o.astype(BF16)


def _attention(q, k, v, lam_p, subln, cache_k, cache_v, n_ctx, ctx_len, n_req, req_len, layer):
    t, d = q.shape
    small = lambda shape: pl.BlockSpec(shape, lambda *_: (0,) * len(shape))
    n_ctx_req = n_ctx // ctx_len
    blk = pl.BlockSpec((ctx_len, d), lambda b: (b, 0))
    o_ctx = pl.pallas_call(
        functools.partial(_attn_kernel, n_heads=N_HEADS, has_cache=False, layer=layer),
        grid=(n_ctx_req,),
        in_specs=[small(lam_p.shape), small(subln.shape), blk, blk, blk],
        out_specs=blk,
        out_shape=jax.ShapeDtypeStruct((n_ctx, d), BF16),
        compiler_params=_cparams(1),
        name="attn_ctx",
    )(lam_p, subln, q, k, v)

    tq = ATTN_Q_TILE
    nq = req_len // tq
    q0 = n_ctx // tq
    r0 = n_ctx // req_len
    past = cache_k.shape[1]
    small3 = lambda shape: pl.BlockSpec(shape, lambda b, h, i: (0,) * len(shape))
    qspec = pl.BlockSpec((tq, LANES), lambda b, h, i: (q0 + b * nq + i, h))
    kvspec = pl.BlockSpec((req_len, LANES), lambda b, h, i: (r0 + b, h))
    cspec = pl.BlockSpec((None, past, LANES), lambda b, h, i: (b, 0, h))
    o_lat = pl.pallas_call(
        functools.partial(_attn_kernel, n_heads=1, has_cache=True, layer=layer),
        grid=(n_req, N_HEADS, nq),
        in_specs=[small3(lam_p.shape), pl.BlockSpec((1, LANES), lambda b, h, i: (0, 0)),
                  qspec, kvspec, kvspec, cspec, cspec],
        out_specs=pl.BlockSpec((tq, LANES), lambda b, h, i: (b * nq + i, h)),
        out_shape=jax.ShapeDtypeStruct((n_req * req_len, d), BF16),
        compiler_params=_cparams(3),
        name="attn_latent",
    )(lam_p, subln, q, k, v, cache_k, cache_v)
    return o_ctx, o_lat


def _route(a2, wr_hi, wr_lo, br, ltri, carry_ref):
    tm = a2.shape[0]
    a_hi, a_lo = _split_bf16(a2)
    logits = _dot(a_hi, wr_hi) + (_dot(a_lo, wr_hi) + _dot(a_hi, wr_lo)) + br
    lane_i = lax.broadcasted_iota(jnp.int32, (tm, LANES), 1)
    lane = lane_i.astype(F32)
    neg = jnp.full((tm, LANES), -jnp.inf, F32)
    big = jnp.full((tm, LANES), float(LANES), F32)
    first_lane = lambda mask: jnp.min(jnp.where(mask, lane, big), axis=-1, keepdims=True)

    lc = jnp.where(lane_i < N_EXPERT_GROUPS, logits, neg)
    mc = jnp.max(lc, axis=-1, keepdims=True)
    pg = 1.0 / jnp.sum(jnp.exp(lc - mc), axis=-1, keepdims=True)
    gi = first_lane(lc == mc)
    assert EXPERTS_PER_GROUP == 8
    grp = lax.shift_right_arithmetic(lane_i - L_EXPERT0, 3).astype(F32)
    in_group = (lane_i >= L_EXPERT0) & (lane_i < L_EXPERT0 + N_EXPERTS) & (grp == gi)
    ls = jnp.where(in_group, logits, neg)
    t1 = jnp.max(ls, axis=-1, keepdims=True)
    i1 = first_lane(ls == t1)
    ls2 = jnp.where(lane == i1, neg, ls)
    t2 = jnp.max(ls2, axis=-1, keepdims=True)
    i2 = first_lane(ls2 == t2)
    ex = jnp.exp(t2 - t1)
    w1 = pg * (1.0 / (1.0 + ex))
    w2 = pg * (ex / (1.0 + ex))
    e1 = i1 - float(L_EXPERT0)
    e2 = i2 - float(L_EXPERT0)
    oh1 = lane == e1
    oh2 = lane == e2
    onehot = oh1.astype(F32) + oh2.astype(F32)
    before = _dot(ltri, onehot.astype(BF16)) + carry_ref[...]
    zero = jnp.zeros_like(before)
    rank1 = jnp.sum(jnp.where(oh1, before, zero), axis=-1, keepdims=True)
    rank2 = jnp.sum(jnp.where(oh2, before, zero), axis=-1, keepdims=True)
    carry_ref[...] = carry_ref[...] + jnp.sum(onehot, axis=0, keepdims=True)
    slab = jnp.zeros((tm, LANES), F32)
    for ln, val in ((R_E1, e1), (R_E2, e2), (R_W1, w1), (R_W2, w2), (R_RANK1, rank1), (R_RANK2, rank2)):
        slab = jnp.where(lane_i == ln, val, slab)
    return slab


def _router_weights(wc, bc, wf, bf_):
    d = wc.shape[0]
    pad = LANES - N_EXPERT_GROUPS - N_EXPERTS
    w = jnp.concatenate([wc, wf, jnp.zeros((d, pad), F32)], axis=1)
    b = jnp.concatenate([bc, bf_, jnp.zeros((pad,), F32)])[None, :]
    hi = w.astype(BF16)
    lo = (w - hi.astype(F32)).astype(BF16)
    return hi, lo, b


def _lower_tri(n):
    r = jnp.arange(n, dtype=jnp.int32)
    return (r[None, :] < r[:, None]).astype(BF16)


def _post_attn_kernel(hc_ref, hl_ref, oc_ref, ol_ref, mod_ref, wo_ref, g2_ref, wrh_ref, wrl_ref, br_ref, ltri_ref,
                      h1_ref, a2_ref, route_ref, cnt_ref, carry_ref, *, n_ctx_tiles, tiles_per_req):
    i = pl.program_id(0)
    seq = _seq_of_tile(i, n_ctx_tiles, tiles_per_req)

    @pl.when(i == 0)
    def _():
        carry_ref[...] = jnp.zeros_like(carry_ref)

    o = _pick_rows(i, n_ctx_tiles, oc_ref, ol_ref)
    h1 = _pick_rows(i, n_ctx_tiles, hc_ref, hl_ref) + _mod_row(mod_ref, 2, seq) * _dot(o, wo_ref[...])
    h1_ref[...] = h1
    a2 = _modulated(h1, g2_ref[...], _mod_row(mod_ref, 3, seq), _mod_row(mod_ref, 4, seq))
    a2_ref[...] = a2
    route_ref[...] = _route(a2, wrh_ref[...], wrl_ref[...], br_ref[...], ltri_ref[...], carry_ref)
    cnt_ref[...] = carry_ref[...]


def _post_attn(h_ctx, h_lat, o_ctx, o_lat, mod, wo, g2, router, rows_per_req):
    n_ctx, d = h_ctx.shape
    t = n_ctx + h_lat.shape[0]
    tm = ROW_TILE
    wrh, wrl, br = router
    ltri = _lower_tri(tm)
    full = lambda shape: pl.BlockSpec(shape, lambda i: (0,) * len(shape))
    row = pl.BlockSpec((tm, d), lambda i: (i, 0))
    kern = functools.partial(_post_attn_kernel, n_ctx_tiles=n_ctx // tm, tiles_per_req=rows_per_req // tm)
    return pl.pallas_call(
        kern,
        grid=(t // tm,),
        in_specs=_split_specs(tm, d, n_ctx // tm) + _split_specs(tm, d, n_ctx // tm) + [
                  full(mod.shape), full(wo.shape), full((1, d)), full(wrh.shape), full(wrl.shape),
                  full(br.shape), full(ltri.shape)],
        out_specs=[row, row, pl.BlockSpec((tm, LANES), lambda i: (i, 0)), full((1, LANES))],
        out_shape=[jax.ShapeDtypeStruct((t, d), F32), jax.ShapeDtypeStruct((t, d), F32),
                   jax.ShapeDtypeStruct((t, LANES), F32), jax.ShapeDtypeStruct((1, LANES), F32)],
        scratch_shapes=[pltpu.VMEM((1, LANES), F32)],
        compiler_params=_cparams(1),
        name="post_attn_router",
    )(h_ctx, h_lat, o_ctx, o_lat, mod, wo, g2, wrh, wrl, br, ltri)


def _gather_rows(src, idx):
    n = idx.shape[0]
    d = src.shape[1]
    w = GATHER_ROWS
    mesh = plsc.VectorSubcoreMesh(core_axis_name="core", subcore_axis_name="subcore")

    @pl.kernel(out_type=jax.ShapeDtypeStruct((n, d), src.dtype), mesh=mesh)
    def gather(src_hbm, idx_hbm, out_hbm):
        def body(idx_vmem, out_vmem):
            pltpu.sync_copy(src_hbm.at[idx_vmem.at[0, pl.ds(0, w)]], out_vmem)

        pltpu.emit_pipeline(
            body,
            grid=(n // w,),
            in_specs=[pl.BlockSpec((1, LANES), lambda i: (i, 0))],
            out_specs=[pl.BlockSpec((w, d), lambda i: (i, 0))],
            core_axis_name=("core", "subcore"),
            dimension_semantics=(pltpu.PARALLEL,),
        )(idx_hbm, out_hbm)

    return gather(src, _index_windows(idx, w))


def _index_windows(idx, w):
    return jnp.pad(idx.reshape(idx.shape[0] // w, w), ((0, 0), (0, LANES - w)))


def _scatter_rows_twice(src, idx_a, idx_b, n_out):
    t, d = src.shape
    w = GATHER_ROWS
    mesh = plsc.VectorSubcoreMesh(core_axis_name="core", subcore_axis_name="subcore")

    @pl.kernel(out_type=jax.ShapeDtypeStruct((n_out, d), src.dtype), mesh=mesh)
    def scatter(src_hbm, ia_hbm, ib_hbm, out_hbm):
        def body(src_vmem, ia_vmem, ib_vmem):
            pltpu.sync_copy(src_vmem, out_hbm.at[ia_vmem.at[0, pl.ds(0, w)]])
            pltpu.sync_copy(src_vmem, out_hbm.at[ib_vmem.at[0, pl.ds(0, w)]])

        pltpu.emit_pipeline(
            body,
            grid=(t // w,),
            in_specs=[pl.BlockSpec((w, d), lambda i: (i, 0)),
                      pl.BlockSpec((1, LANES), lambda i: (i, 0)),
                      pl.BlockSpec((1, LANES), lambda i: (i, 0))],
            out_specs=[],
            core_axis_name=("core", "subcore"),
            dimension_semantics=(pltpu.PARALLEL,),
        )(src_hbm, ia_hbm, ib_hbm)

    return scatter(src, _index_windows(idx_a, w), _index_windows(idx_b, w))


def _expert_kernel(be_ref, nb_ref, x_ref, w1_ref, w3_ref, w2_ref, y_ref, w1b, w3b, w2b):
    b = pl.program_id(0)
    prev = be_ref[jnp.maximum(b - 1, 0)]
    used = b < nb_ref[0]

    @pl.when(used & ((b == 0) | (be_ref[b] != prev)))
    def _():
        w1b[...] = w1_ref[...].astype(BF16)
        w3b[...] = w3_ref[...].astype(BF16)
        w2b[...] = w2_ref[...].astype(BF16)

    @pl.when(used)
    def _():
        x = x_ref[...].astype(BF16)
        hb = jax.nn.silu(_dot(x, w1b[...])) * _dot(x, w3b[...])
        y_ref[...] = _dot(hb.astype(BF16), w2b[...])

    @pl.when(jnp.logical_not(used))
    def _():
        y_ref[...] = jnp.zeros_like(y_ref)


def _expert_mlps(xs, block_e, n_used, w1, w3, w2, layer):
    npad, d = xs.shape
    ff = w1.shape[3]
    bm = MOE_BLOCK
    grid_spec = pltpu.PrefetchScalarGridSpec(
        num_scalar_prefetch=2,
        grid=(npad // bm,),
        in_specs=[pl.BlockSpec((bm, d), lambda b, be, nb: (b, 0)),
                  pl.BlockSpec((None, None, d, ff), lambda b, be, nb: (layer, be[b], 0, 0)),
                  pl.BlockSpec((None, None, d, ff), lambda b, be, nb: (layer, be[b], 0, 0)),
                  pl.BlockSpec((None, None, ff, d), lambda b, be, nb: (layer, be[b], 0, 0))],
        out_specs=pl.BlockSpec((bm, d), lambda b, be, nb: (b, 0)),
        scratch_shapes=[pltpu.VMEM((d, ff), BF16), pltpu.VMEM((d, ff), BF16), pltpu.VMEM((ff, d), BF16)],
    )
    return pl.pallas_call(
        _expert_kernel,
        grid_spec=grid_spec,
        out_shape=jax.ShapeDtypeStruct((npad, d), F32),
        compiler_params=_cparams(1),
        name="expert_mlps",
    )(block_e, n_used, xs, w1, w3, w2)


def _moe(a2, route, counts, w1, w3, w2, layer):
    t, d = a2.shape
    bm = MOE_BLOCK
    n = 2 * t
    e = route[:, R_E1:R_E2 + 1].astype(jnp.int32).reshape(n)
    rank = route[:, R_RANK1:R_RANK2 + 1].astype(jnp.int32).reshape(n)
    cnt = counts[0, :N_EXPERTS].astype(jnp.int32)
    padded = ((cnt + bm - 1) // bm) * bm
    pend = jnp.cumsum(padded)
    pstart = pend - padded
    pos = jnp.sum(jnp.where(e[:, None] == jnp.arange(N_EXPERTS, dtype=jnp.int32)[None, :], pstart[None, :], 0),
                  axis=1) + rank
    n_blocks = n // bm + N_EXPERTS
    npad = n_blocks * bm
    starts = jnp.arange(n_blocks, dtype=jnp.int32) * bm
    n_used = pend[-1:] // bm
    last_start = (n_used[0] - 1) * bm
    block_e = jnp.sum((pend[None, :] <= jnp.minimum(starts, last_start)[:, None]).astype(jnp.int32), axis=1)
    pos_first, pos_second = pos[0::2], pos[1::2]
    xs = _scatter_rows_twice(a2, pos_first, pos_second, npad)
    ys = _expert_mlps(xs, block_e, n_used, w1, w3, w2, layer)
    return _gather_rows(ys, jnp.concatenate([pos_first, pos_second]))


def _moe_combine(h, y_first, y_second, route, g2):
    w1 = route[:, R_W1:R_W1 + 1]
    w2 = route[:, R_W2:R_W2 + 1]
    return h + g2 * (w1 * y_first + w2 * y_second)


def _gmlp_kernel(h_ref, ya_ref, yb_ref, rt_ref, modp_ref, mod_ref, g1_ref, win_ref, bin_ref, vg_ref, ws_ref, bsb_ref,
                 wout_ref, g2_ref, wrh_ref, wrl_ref, br_ref, ltri_ref,
                 h1_ref, a2_ref, route_ref, cnt_ref, carry_ref, *, n_ctx_tiles, tiles_per_req):
    i = pl.program_id(0)
    seq = _seq_of_tile(i, n_ctx_tiles, tiles_per_req)

    @pl.when(i == 0)
    def _():
        carry_ref[...] = jnp.zeros_like(carry_ref)

    h = _moe_combine(h_ref[...], ya_ref[...], yb_ref[...], rt_ref[...], _mod_row(modp_ref, 5, seq))
    a = _modulated(h, g1_ref[...], _mod_row(mod_ref, 0, seq), _mod_row(mod_ref, 1, seq))
    gw = vg_ref.shape[1]
    z = jax.nn.gelu(_dot(a.astype(BF16), win_ref[...]) + bin_ref[...])
    u = z[:, :gw]
    v = z[:, gw:]
    v = (v * lax.rsqrt(jnp.mean(v * v, axis=-1, keepdims=True) + EPS)) * vg_ref[...]
    vb = v.astype(BF16)
    cg = gw // GM_GROUPS
    tm = h.shape[0]
    rows = []
    for c in range(tm // CHUNK):
        cols = []
        for g in range(GM_GROUPS):
            cols.append(_dot(ws_ref[g], vb[c * CHUNK:(c + 1) * CHUNK, g * cg:(g + 1) * cg]))
        rows.append(jnp.concatenate(cols, axis=1) + bsb_ref[...])
    vm = jnp.concatenate(rows, axis=0)
    mix = _dot((u * vm).astype(BF16), wout_ref[...])
    h1 = h + _mod_row(mod_ref, 2, seq) * mix
    h1_ref[...] = h1
    a2 = _modulated(h1, g2_ref[...], _mod_row(mod_ref, 3, seq), _mod_row(mod_ref, 4, seq))
    a2_ref[...] = a2
    route_ref[...] = _route(a2, wrh_ref[...], wrl_ref[...], br_ref[...], ltri_ref[...], carry_ref)
    cnt_ref[...] = carry_ref[...]


def _gmlp_layer(h, y2, route_prev, mod_prev, mod, g1, win, bin_, vg, ws, bsb, wout, g2, router,
                n_ctx, rows_per_req):
    t, d = h.shape
    tm = ROW_TILE
    wrh, wrl, br = router
    ltri = _lower_tri(tm)
    full = lambda shape: pl.BlockSpec(shape, lambda i: (0,) * len(shape))
    row = lambda w: pl.BlockSpec((tm, w), lambda i: (i, 0))
    kern = functools.partial(_gmlp_kernel, n_ctx_tiles=n_ctx // tm, tiles_per_req=rows_per_req // tm)
    args = (h, y2, y2, route_prev, mod_prev, mod, g1, win, bin_, vg, ws, bsb, wout, g2, wrh, wrl, br, ltri)
    second = pl.BlockSpec((tm, d), lambda i: (i + t // tm, 0))
    in_specs = [row(d), row(d), second, row(LANES)] + [full(a.shape) for a in args[4:]]
    return pl.pallas_call(
        kern,
        grid=(t // tm,),
        in_specs=in_specs,
        out_specs=[row(d), row(d), row(LANES), full((1, LANES))],
        out_shape=[jax.ShapeDtypeStruct((t, d), F32), jax.ShapeDtypeStruct((t, d), F32),
                   jax.ShapeDtypeStruct((t, LANES), F32), jax.ShapeDtypeStruct((1, LANES), F32)],
        scratch_shapes=[pltpu.VMEM((1, LANES), F32)],
        compiler_params=_cparams(1),
        name="gmlp_router",
    )(*args)


def _final_kernel(h_ref, ya_ref, yb_ref, rt_ref, mod_ref, oc_ref, ol_ref, *, n_ctx_tiles, tiles_per_req):
    i = pl.program_id(0)
    seq = _seq_of_tile(i, n_ctx_tiles, tiles_per_req)
    out = _moe_combine(h_ref[...], ya_ref[...], yb_ref[...], rt_ref[...], _mod_row(mod_ref, 5, seq))

    @pl.when(i < n_ctx_tiles)
    def _():
        oc_ref[...] = out

    @pl.when(i >= n_ctx_tiles)
    def _():
        ol_ref[...] = out


def _final_combine(h, y2, route, mod, n_ctx, rows_per_req):
    t, d = h.shape
    tm = ROW_TILE
    row = lambda w: pl.BlockSpec((tm, w), lambda i: (i, 0))
    kern = functools.partial(_final_kernel, n_ctx_tiles=n_ctx // tm, tiles_per_req=rows_per_req // tm)
    return pl.pallas_call(
        kern,
        grid=(t // tm,),
        in_specs=[row(d), row(d), pl.BlockSpec((tm, d), lambda i: (i + t // tm, 0)), row(LANES),
                  pl.BlockSpec(mod.shape, lambda i: (0, 0, 0))],
        out_specs=_split_specs(tm, d, n_ctx // tm),
        out_shape=[jax.ShapeDtypeStruct((n_ctx, d), F32), jax.ShapeDtypeStruct((t - n_ctx, d), F32)],
        compiler_params=_cparams(1),
        name="final_combine",
    )(h, y2, y2, route, mod)


def kernel(x_prompt, x_sample, cache_k, cache_v, c, c_ctx, ada_w, ada_b, norm1_g, norm2_g, attn_wq, attn_wk,
           attn_wv, attn_wo, attn_qnorm, attn_knorm, attn_lam, attn_subln, gm_win, gm_bin, gm_vnorm, gm_ws,
           gm_bs, gm_wout, moe_wc, moe_bc, moe_wf, moe_bf, moe_w1, moe_w3, moe_w2):
    batch, seq_len, d = x_prompt.shape
    n_req, req_len, _ = x_sample.shape
    depth = ada_w.shape[0]
    assert depth == 2 and attn_wq.shape[0] == 1 and gm_win.shape[0] == 1
    n_ctx = batch * seq_len
    assert n_ctx % req_len == 0 and req_len % ROW_TILE == 0 and seq_len == ROW_TILE
    past = cache_k.shape[2]

    h_ctx = x_prompt.reshape(n_ctx, d)
    h_lat = x_sample.reshape(n_req * req_len, d)
    cvec = jnp.concatenate([c_ctx[None, :], c, jnp.zeros((8 - 1 - n_req, d), F32)], axis=0)
    mod = _ada_mod(cvec, ada_w, ada_b)
    routers = [_router_weights(moe_wc[i], moe_bc[i], moe_wf[i], moe_bf[i]) for i in range(depth)]
    row_vec = lambda v: v.reshape(1, -1)

    wqkv = jnp.concatenate([attn_wq[0], attn_wk[0], attn_wv[0]], axis=1).astype(BF16)
    reps = d // HEAD_DIM
    q, k, v, k_new, v_new = _pre_attn(h_ctx, h_lat, mod[0], row_vec(norm1_g[0]), wqkv,
                                      row_vec(jnp.tile(attn_qnorm[0], reps)), row_vec(jnp.tile(attn_knorm[0], reps)),
                                      req_len)
    o_ctx, o_lat = _attention(q, k, v, attn_lam[0], row_vec(attn_subln[0]),
                              cache_k[:, 0].reshape(n_req, past, d), cache_v[:, 0].reshape(n_req, past, d),
                              n_ctx, seq_len, n_req, req_len, layer=0)
    h1, a2, route0, cnt0 = _post_attn(h_ctx, h_lat, o_ctx, o_lat, mod[0], attn_wo[0].astype(BF16),
                                      row_vec(norm2_g[0]), routers[0], req_len)
    y2 = _moe(a2, route0, cnt0, moe_w1, moe_w3, moe_w2, layer=0)

    gw = gm_vnorm.shape[1]
    bsb = jnp.repeat(gm_bs[0].T, gw // GM_GROUPS, axis=1)
    h2, a2, route1, cnt1 = _gmlp_layer(h1, y2, route0, mod[0], mod[1], row_vec(norm1_g[1]), gm_win[0].astype(BF16),
                                       row_vec(gm_bin[0]), row_vec(gm_vnorm[0]), gm_ws[0].astype(BF16), bsb,
                                       gm_wout[0].astype(BF16), row_vec(norm2_g[1]), routers[1], n_ctx, req_len)
    y2 = _moe(a2, route1, cnt1, moe_w1, moe_w3, moe_w2, layer=1)
    y_ctx, y_lat = _final_combine(h2, y2, route1, mod[1], n_ctx, req_len)

    y_prompt = y_ctx.reshape(batch, seq_len, d)
    y_sample = y_lat.reshape(n_req, req_len, d)
    new_cache_k = k_new.reshape(batch, 1, seq_len, N_HEADS, 2, HEAD_DIM)
    new_cache_v = v_new.reshape(batch, 1, seq_len, N_HEADS, V_DIM)
    return (y_prompt, y_sample, new_cache_k, new_cache_v)
```

```python
import functools
import math

import jax
import jax.numpy as jnp
from jax import lax
from jax.experimental import pallas as pl
from jax.experimental.pallas import tpu as pltpu
from jax.experimental.pallas import tpu_sc as plsc

F32 = jnp.float32
BF16 = jnp.bfloat16

N_HEADS = 8
HEAD_DIM = 64
V_DIM = 2 * HEAD_DIM
GRID_W = 64
ROPE_THETA = 10000.0
CHUNK = 128
GM_GROUPS = 8
N_EXPERT_GROUPS = 4
EXPERTS_PER_GROUP = 8
N_EXPERTS = N_EXPERT_GROUPS * EXPERTS_PER_GROUP
EPS = 1e-6

LANES = 128
ROW_TILE = 256
MOE_BLOCK = 512
GATHER_ROWS = 32
ATTN_Q_TILE = 256
VMEM_LIMIT = 56 * 1024 * 1024

R_E1, R_E2, R_W1, R_W2, R_RANK1, R_RANK2 = 0, 1, 2, 3, 4, 5
L_EXPERT0 = N_EXPERT_GROUPS


def _lambda_init(layer):
    return 0.8 - 0.6 * math.exp(-0.3 * layer)


def _cparams(n_axes):
    return pltpu.CompilerParams(dimension_semantics=("arbitrary",) * n_axes, vmem_limit_bytes=VMEM_LIMIT)


def _seq_of_tile(i, n_ctx_tiles, tiles_per_req):
    return jnp.where(i < n_ctx_tiles, 0, 1 + (i - n_ctx_tiles) // tiles_per_req)


def _pick_rows(i, n_ctx_tiles, ctx_ref, lat_ref):
    return jnp.where(i < n_ctx_tiles, ctx_ref[...], lat_ref[...])


def _split_specs(tm, d, n_ctx_tiles):
    return [pl.BlockSpec((tm, d), lambda i: (jnp.minimum(i, n_ctx_tiles - 1), 0)),
            pl.BlockSpec((tm, d), lambda i: (jnp.maximum(i - n_ctx_tiles, 0), 0))]


def _mod_row(mod_ref, part, seq):
    return mod_ref[part, pl.ds(seq, 1), :]


def _modulated(x, g, shift, scale):
    y = x * lax.rsqrt(jnp.mean(x * x, axis=-1, keepdims=True) + EPS)
    return (y * g) * (1.0 + scale) + shift


def _split_bf16(x):
    hi = x.astype(BF16)
    lo = (x - hi.astype(F32)).astype(BF16)
    return hi, lo


def _dot(a, b):
    return jnp.dot(a, b, preferred_element_type=F32)


def _ada_kernel(c_ref, w_ref, b_ref, o_ref):
    c = c_ref[...]
    s = c * jax.nn.sigmoid(c)
    o_ref[...] = _dot(s.astype(BF16), w_ref[...].astype(BF16)) + b_ref[...]


def _ada_mod(cvec, ada_w, ada_b):
    depth, d, d6 = ada_w.shape
    parts = d6 // d
    rows = cvec.shape[0]
    return pl.pallas_call(
        _ada_kernel,
        grid=(depth, parts),
        in_specs=[
            pl.BlockSpec((rows, d), lambda l, j: (0, 0)),
            pl.BlockSpec((None, d, d), lambda l, j: (l, 0, j)),
            pl.BlockSpec((None, 1, d), lambda l, j: (l, 0, j)),
        ],
        out_specs=pl.BlockSpec((None, None, rows, d), lambda l, j: (l, j, 0, 0)),
        out_shape=jax.ShapeDtypeStruct((depth, parts, rows, d), F32),
        compiler_params=_cparams(2),
        name="ada_mod",
    )(cvec, ada_w, ada_b.reshape(depth, 1, d6))


def _pre_attn_kernel(hc_ref, hl_ref, mod_ref, g_ref, w_ref, qg_ref, kg_ref, gs_ref, gb_ref, cos_ref, sin_ref,
                     q_ref, k_ref, v_ref, kf_ref, vf_ref, *, n_ctx_tiles, tiles_per_req):
    i = pl.program_id(0)
    seq = _seq_of_tile(i, n_ctx_tiles, tiles_per_req)
    d = hc_ref.shape[1]
    x = _pick_rows(i, n_ctx_tiles, hc_ref, hl_ref)
    a = _modulated(x, g_ref[...], _mod_row(mod_ref, 0, seq), _mod_row(mod_ref, 1, seq))
    qkv = _dot(a.astype(BF16), w_ref[...])
    q = qkv[:, :d]
    k = qkv[:, d:2 * d]
    v = qkv[:, 2 * d:]
    sq_hi, sq_lo = _split_bf16(jnp.concatenate([q * q, k * k], axis=1))
    gsum = _dot(sq_hi, gs_ref[...]) + _dot(sq_lo, gs_ref[...])
    r_hi, r_lo = _split_bf16(lax.rsqrt(gsum * (1.0 / HEAD_DIM) + EPS))
    rb = _dot(r_hi, gb_ref[...]) + _dot(r_lo, gb_ref[...])
    qn = (q * rb[:, :d]) * qg_ref[...]
    kn = (k * rb[:, d:]) * kg_ref[...]
    v_ref[...] = v.astype(BF16)
    q_scale = HEAD_DIM ** -0.5

    @pl.when(i < n_ctx_tiles)
    def _():
        q_ref[...] = (qn * q_scale).astype(BF16)
        k_ref[...] = kn.astype(BF16)
        kf_ref[...] = kn
        vf_ref[...] = v

    @pl.when(i >= n_ctx_tiles)
    def _():
        cos = cos_ref[...]
        sin = sin_ref[...]
        lane = lax.broadcasted_iota(jnp.int32, cos.shape, 1)
        first = (lane & 31) < 16

        def rope(x):
            outs = []
            for hh in range(d // LANES):
                xs = x[:, hh * LANES:(hh + 1) * LANES]
                rot = jnp.where(first, pltpu.roll(xs, LANES - 16, 1), pltpu.roll(xs, 16, 1))
                outs.append(xs * cos + rot * sin)
            return jnp.concatenate(outs, axis=1)

        q_ref[...] = (rope(qn) * q_scale).astype(BF16)
        k_ref[...] = rope(kn).astype(BF16)


def _rope_tables(n_pos):
    half = HEAD_DIM // 2
    pos = jnp.arange(n_pos, dtype=jnp.int32)
    row = (pos // GRID_W).astype(F32)
    col = (pos % GRID_W).astype(F32)
    inv = 1.0 / (ROPE_THETA ** (jnp.arange(0, half, 2, dtype=F32) / half))
    ang_r = row[:, None] * inv[None, :]
    ang_c = col[:, None] * inv[None, :]
    ang = jnp.concatenate([ang_r, ang_r, ang_c, ang_c], axis=-1)
    quarter = half // 2
    sign = jnp.tile(jnp.concatenate([-jnp.ones((quarter,), F32), jnp.ones((quarter,), F32)]), 2)
    cos = jnp.tile(jnp.cos(ang), (1, 2))
    sin = jnp.tile(jnp.sin(ang) * sign[None, :], (1, 2))
    return cos, sin


def _pre_attn(h_ctx, h_lat, mod, g, wqkv, qg, kg, rows_per_req):
    n_ctx, d = h_ctx.shape
    t = n_ctx + h_lat.shape[0]
    tm = ROW_TILE
    n_ctx_tiles = n_ctx // tm
    tiles_per_req = rows_per_req // tm
    groups = d // HEAD_DIM
    gid = jnp.arange(2 * d, dtype=jnp.int32) // HEAD_DIM
    gs = (gid[:, None] == jnp.arange(LANES, dtype=jnp.int32)[None, :]).astype(BF16)
    gb = gs.T
    assert 2 * groups <= LANES
    cos, sin = _rope_tables(rows_per_req)
    pos_map = lambda i: (jnp.where(i < n_ctx_tiles, 0, (i - n_ctx_tiles) % tiles_per_req), 0)
    ctx_map = lambda i: (jnp.minimum(i, n_ctx_tiles - 1), 0)
    full = lambda shape: pl.BlockSpec(shape, lambda i: (0,) * len(shape))
    row = pl.BlockSpec((tm, d), lambda i: (i, 0))
    kern = functools.partial(_pre_attn_kernel, n_ctx_tiles=n_ctx_tiles, tiles_per_req=tiles_per_req)
    return pl.pallas_call(
        kern,
        grid=(t // tm,),
        in_specs=_split_specs(tm, d, n_ctx_tiles) + [
                  full(mod.shape), full((1, d)), full(wqkv.shape), full((1, d)), full((1, d)),
                  full(gs.shape), full(gb.shape),
                  pl.BlockSpec((tm, LANES), pos_map), pl.BlockSpec((tm, LANES), pos_map)],
        out_specs=[row, row, row, pl.BlockSpec((tm, d), ctx_map), pl.BlockSpec((tm, d), ctx_map)],
        out_shape=[jax.ShapeDtypeStruct((t, d), BF16)] * 3 + [jax.ShapeDtypeStruct((n_ctx, d), F32)] * 2,
        compiler_params=_cparams(1),
        name="pre_attn",
    )(h_ctx, h_lat, mod, g, wqkv, qg, kg, gs, gb, cos, sin)


def _attn_kernel(*refs, n_heads, has_cache, layer):
    if has_cache:
        lam_ref, sub_ref, q_ref, kn_ref, vn_ref, kc_ref, vc_ref, o_ref = refs
    else:
        lam_ref, sub_ref, q_ref, kn_ref, vn_ref, o_ref = refs
    lf = lam_ref[...]
    lam = (jnp.exp(jnp.sum(lf[0:1] * lf[1:2], axis=-1, keepdims=True))
           - jnp.exp(jnp.sum(lf[2:3] * lf[3:4], axis=-1, keepdims=True)) + _lambda_init(layer))
    nt = (((1,), (1,)), ((), ()))
    tq = q_ref.shape[0]
    lane = lax.broadcasted_iota(jnp.int32, (tq, LANES), 1)
    for hh in range(n_heads):
        sl = slice(hh * LANES, (hh + 1) * LANES)
        qh = q_ref[:, sl]
        zero = jnp.zeros_like(qh)
        qs = (jnp.where(lane < HEAD_DIM, qh, zero), jnp.where(lane >= HEAD_DIM, qh, zero))
        keys = [kn_ref[:, sl]]
        vals = [vn_ref[:, sl]]
        if has_cache:
            keys.append(kc_ref[:, sl].astype(BF16))
            vals.append(vc_ref[:, sl].astype(BF16))
        probs = []
        for qm in qs:
            s = [lax.dot_general(qm, kk, nt, preferred_element_type=F32) for kk in keys]
            m = functools.reduce(jnp.maximum, [jnp.max(x, axis=-1, keepdims=True) for x in s])
            e = [jnp.exp(x - m) for x in s]
            den = functools.reduce(lambda u, w: u + w, [jnp.sum(x, axis=-1, keepdims=True) for x in e])
            probs.append((e, 1.0 / den))
        (e0, r0), (e1, r1) = probs
        o = None
        for j, vv in enumerate(vals):
            aj = e0[j] * r0 - lam * (e1[j] * r1)
            oj = _dot(aj.astype(BF16), vv)
            o = oj if o is None else o + oj
        o = o * lax.rsqrt(jnp.mean(o * o, axis=-1, keepdims=True) + EPS)
        o = (o * sub_ref[...]) * (1.0 - _lambda_init(layer))
        o_ref[:, sl] = o.astype(BF16)


def _attention(q, k, v, lam_p, subln, cache_k, cache_v, n_ctx, ctx_len, n_req, req_len, layer):
    t, d = q.shape
    small = lambda shape: pl.BlockSpec(shape, lambda *_: (0,) * len(shape))
    n_ctx_req = n_ctx // ctx_len
    blk = pl.BlockSpec((ctx_len, d), lambda b: (b, 0))
    o_ctx = pl.pallas_call(
        functools.partial(_attn_kernel, n_heads=N_HEADS, has_cache=False, layer=layer),
        grid=(n_ctx_req,),
        in_specs=[small(lam_p.shape), small(subln.shape), blk, blk, blk],
        out_specs=blk,
        out_shape=jax.ShapeDtypeStruct((n_ctx, d), BF16),
        compiler_params=_cparams(1),
        name="attn_ctx",
    )(lam_p, subln, q, k, v)

    tq = ATTN_Q_TILE
    nq = req_len // tq
    q0 = n_ctx // tq
    r0 = n_ctx // req_len
    past = cache_k.shape[1]
    small3 = lambda shape: pl.BlockSpec(shape, lambda b, h, i: (0,) * len(shape))
    qspec = pl.BlockSpec((tq, LANES), lambda b, h, i: (q0 + b * nq + i, h))
    kvspec = pl.BlockSpec((req_len, LANES), lambda b, h, i: (r0 + b, h))
    cspec = pl.BlockSpec((None, past, LANES), lambda b, h, i: (b, 0, h))
    o_lat = pl.pallas_call(
        functools.partial(_attn_kernel, n_heads=1, has_cache=True, layer=layer),
        grid=(n_req, N_HEADS, nq),
        in_specs=[small3(lam_p.shape), pl.BlockSpec((1, LANES), lambda b, h, i: (0, 0)),
                  qspec, kvspec, kvspec, cspec, cspec],
        out_specs=pl.BlockSpec((tq, LANES), lambda b, h, i: (b * nq + i, h)),
        out_shape=jax.ShapeDtypeStruct((n_req * req_len, d), BF16),
        compiler_params=_cparams(3),
        name="attn_latent",
    )(lam_p, subln, q, k, v, cache_k, cache_v)
    return o_ctx, o_lat


def _route(a2, wr_hi, wr_lo, br, ltri, carry_ref):
    tm = a2.shape[0]
    a_hi, a_lo = _split_bf16(a2)
    logits = _dot(a_hi, wr_hi) + (_dot(a_lo, wr_hi) + _dot(a_hi, wr_lo)) + br
    lane_i = lax.broadcasted_iota(jnp.int32, (tm, LANES), 1)
    lane = lane_i.astype(F32)
    neg = jnp.full((tm, LANES), -jnp.inf, F32)
    big = jnp.full((tm, LANES), float(LANES), F32)
    first_lane = lambda mask: jnp.min(jnp.where(mask, lane, big), axis=-1, keepdims=True)

    lc = jnp.where(lane_i < N_EXPERT_GROUPS, logits, neg)
    mc = jnp.max(lc, axis=-1, keepdims=True)
    pg = 1.0 / jnp.sum(jnp.exp(lc - mc), axis=-1, keepdims=True)
    gi = first_lane(lc == mc)
    assert EXPERTS_PER_GROUP == 8
    grp = lax.shift_right_arithmetic(lane_i - L_EXPERT0, 3).astype(F32)
    in_group = (lane_i >= L_EXPERT0) & (lane_i < L_EXPERT0 + N_EXPERTS) & (grp == gi)
    ls = jnp.where(in_group, logits, neg)
    t1 = jnp.max(ls, axis=-1, keepdims=True)
    i1 = first_lane(ls == t1)
    ls2 = jnp.where(lane == i1, neg, ls)
    t2 = jnp.max(ls2, axis=-1, keepdims=True)
    i2 = first_lane(ls2 == t2)
    ex = jnp.exp(t2 - t1)
    w1 = pg * (1.0 / (1.0 + ex))
    w2 = pg * (ex / (1.0 + ex))
    e1 = i1 - float(L_EXPERT0)
    e2 = i2 - float(L_EXPERT0)
    oh1 = lane == e1
    oh2 = lane == e2
    onehot = oh1.astype(F32) + oh2.astype(F32)
    before = _dot(ltri, onehot.astype(BF16)) + carry_ref[...]
    zero = jnp.zeros_like(before)
    rank1 = jnp.sum(jnp.where(oh1, before, zero), axis=-1, keepdims=True)
    rank2 = jnp.sum(jnp.where(oh2, before, zero), axis=-1, keepdims=True)
    carry_ref[...] = carry_ref[...] + jnp.sum(onehot, axis=0, keepdims=True)
    slab = jnp.zeros((tm, LANES), F32)
    for ln, val in ((R_E1, e1), (R_E2, e2), (R_W1, w1), (R_W2, w2), (R_RANK1, rank1), (R_RANK2, rank2)):
        slab = jnp.where(lane_i == ln, val, slab)
    return slab


def _router_weights(wc, bc, wf, bf_):
    d = wc.shape[0]
    pad = LANES - N_EXPERT_GROUPS - N_EXPERTS
    w = jnp.concatenate([wc, wf, jnp.zeros((d, pad), F32)], axis=1)
    b = jnp.concatenate([bc, bf_, jnp.zeros((pad,), F32)])[None, :]
    hi = w.astype(BF16)
    lo = (w - hi.astype(F32)).astype(BF16)
    return hi, lo, b


def _lower_tri(n):
    r = jnp.arange(n, dtype=jnp.int32)
    return (r[None, :] < r[:, None]).astype(BF16)


def _post_attn_kernel(hc_ref, hl_ref, oc_ref, ol_ref, mod_ref, wo_ref, g2_ref, wrh_ref, wrl_ref, br_ref, ltri_ref,
                      h1_ref, a2_ref, route_ref, cnt_ref, carry_ref, *, n_ctx_tiles, tiles_per_req):
    i = pl.program_id(0)
    seq = _seq_of_tile(i, n_ctx_tiles, tiles_per_req)

    @pl.when(i == 0)
    def _():
        carry_ref[...] = jnp.zeros_like(carry_ref)

    o = _pick_rows(i, n_ctx_tiles, oc_ref, ol_ref)
    h1 = _pick_rows(i, n_ctx_tiles, hc_ref, hl_ref) + _mod_row(mod_ref, 2, seq) * _dot(o, wo_ref[...])
    h1_ref[...] = h1
    a2 = _modulated(h1, g2_ref[...], _mod_row(mod_ref, 3, seq), _mod_row(mod_ref, 4, seq))
    a2_ref[...] = a2
    route_ref[...] = _route(a2, wrh_ref[...], wrl_ref[...], br_ref[...], ltri_ref[...], carry_ref)
    cnt_ref[...] = carry_ref[...]


def _post_attn(h_ctx, h_lat, o_ctx, o_lat, mod, wo, g2, router, rows_per_req):
    n_ctx, d = h_ctx.shape
    t = n_ctx + h_lat.shape[0]
    tm = ROW_TILE
    wrh, wrl, br = router
    ltri = _lower_tri(tm)
    full = lambda shape: pl.BlockSpec(shape, lambda i: (0,) * len(shape))
    row = pl.BlockSpec((tm, d), lambda i: (i, 0))
    kern = functools.partial(_post_attn_kernel, n_ctx_tiles=n_ctx // tm, tiles_per_req=rows_per_req // tm)
    return pl.pallas_call(
        kern,
        grid=(t // tm,),
        in_specs=_split_specs(tm, d, n_ctx // tm) + _split_specs(tm, d, n_ctx // tm) + [
                  full(mod.shape), full(wo.shape), full((1, d)), full(wrh.shape), full(wrl.shape),
                  full(br.shape), full(ltri.shape)],
        out_specs=[row, row, pl.BlockSpec((tm, LANES), lambda i: (i, 0)), full((1, LANES))],
        out_shape=[jax.ShapeDtypeStruct((t, d), F32), jax.ShapeDtypeStruct((t, d), F32),
                   jax.ShapeDtypeStruct((t, LANES), F32), jax.ShapeDtypeStruct((1, LANES), F32)],
        scratch_shapes=[pltpu.VMEM((1, LANES), F32)],
        compiler_params=_cparams(1),
        name="post_attn_router",
    )(h_ctx, h_lat, o_ctx, o_lat, mod, wo, g2, wrh, wrl, br, ltri)


def _gather_rows(src, idx):
    n = idx.shape[0]
    d = src.shape[1]
    w = GATHER_ROWS
    mesh = plsc.VectorSubcoreMesh(core_axis_name="core", subcore_axis_name="subcore")

    @pl.kernel(out_type=jax.ShapeDtypeStruct((n, d), src.dtype), mesh=mesh)
    def gather(src_hbm, idx_hbm, out_hbm):
        def body(idx_vmem, out_vmem):
            pltpu.sync_copy(src_hbm.at[idx_vmem.at[0, pl.ds(0, w)]], out_vmem)

        pltpu.emit_pipeline(
            body,
            grid=(n // w,),
            in_specs=[pl.BlockSpec((1, LANES), lambda i: (i, 0))],
            out_specs=[pl.BlockSpec((w, d), lambda i: (i, 0))],
            core_axis_name=("core", "subcore"),
            dimension_semantics=(pltpu.PARALLEL,),
        )(idx_hbm, out_hbm)

    return gather(src, _index_windows(idx, w))


def _index_windows(idx, w):
    return jnp.pad(idx.reshape(idx.shape[0] // w, w), ((0, 0), (0, LANES - w)))


def _scatter_rows_twice(src, idx_a, idx_b, n_out):
    t, d = src.shape
    w = GATHER_ROWS
    mesh = plsc.VectorSubcoreMesh(core_axis_name="core", subcore_axis_name="subcore")

    @pl.kernel(out_type=jax.ShapeDtypeStruct((n_out, d), src.dtype), mesh=mesh)
    def scatter(src_hbm, ia_hbm, ib_hbm, out_hbm):
        def body(src_vmem, ia_vmem, ib_vmem):
            pltpu.sync_copy(src_vmem, out_hbm.at[ia_vmem.at[0, pl.ds(0, w)]])
            pltpu.sync_copy(src_vmem, out_hbm.at[ib_vmem.at[0, pl.ds(0, w)]])

        pltpu.emit_pipeline(
            body,
            grid=(t // w,),
            in_specs=[pl.BlockSpec((w, d), lambda i: (i, 0)),
                      pl.BlockSpec((1, LANES), lambda i: (i, 0)),
                      pl.BlockSpec((1, LANES), lambda i: (i, 0))],
            out_specs=[],
            core_axis_name=("core", "subcore"),
            dimension_semantics=(pltpu.PARALLEL,),
        )(src_hbm, ia_hbm, ib_hbm)

    return scatter(src, _index_windows(idx_a, w), _index_windows(idx_b, w))


def _expert_kernel(be_ref, nb_ref, x_ref, w1_ref, w3_ref, w2_ref, y_ref, w1b, w3b, w2b):
    b = pl.program_id(0)
    prev = be_ref[jnp.maximum(b - 1, 0)]
    used = b < nb_ref[0]

    @pl.when(used & ((b == 0) | (be_ref[b] != prev)))
    def _():
        w1b[...] = w1_ref[...].astype(BF16)
        w3b[...] = w3_ref[...].astype(BF16)
        w2b[...] = w2_ref[...].astype(BF16)

    @pl.when(used)
    def _():
        x = x_ref[...].astype(BF16)
        hb = jax.nn.silu(_dot(x, w1b[...])) * _dot(x, w3b[...])
        y_ref[...] = _dot(hb.astype(BF16), w2b[...])

    @pl.when(jnp.logical_not(used))
    def _():
        y_ref[...] = jnp.zeros_like(y_ref)


def _expert_mlps(xs, block_e, n_used, w1, w3, w2, layer):
    npad, d = xs.shape
    ff = w1.shape[3]
    bm = MOE_BLOCK
    grid_spec = pltpu.PrefetchScalarGridSpec(
        num_scalar_prefetch=2,
        grid=(npad // bm,),
        in_specs=[pl.BlockSpec((bm, d), lambda b, be, nb: (b, 0)),
                  pl.BlockSpec((None, None, d, ff), lambda b, be, nb: (layer, be[b], 0, 0)),
                  pl.BlockSpec((None, None, d, ff), lambda b, be, nb: (layer, be[b], 0, 0)),
                  pl.BlockSpec((None, None, ff, d), lambda b, be, nb: (layer, be[b], 0, 0))],
        out_specs=pl.BlockSpec((bm, d), lambda b, be, nb: (b, 0)),
        scratch_shapes=[pltpu.VMEM((d, ff), BF16), pltpu.VMEM((d, ff), BF16), pltpu.VMEM((ff, d), BF16)],
    )
    return pl.pallas_call(
        _expert_kernel,
        grid_spec=grid_spec,
        out_shape=jax.ShapeDtypeStruct((npad, d), F32),
        compiler_params=_cparams(1),
        name="expert_mlps",
    )(block_e, n_used, xs, w1, w3, w2)


def _moe(a2, route, counts, w1, w3, w2, layer):
    t, d = a2.shape
    bm = MOE_BLOCK
    n = 2 * t
    e = route[:, R_E1:R_E2 + 1].astype(jnp.int32).reshape(n)
    rank = route[:, R_RANK1:R_RANK2 + 1].astype(jnp.int32).reshape(n)
    cnt = counts[0, :N_EXPERTS].astype(jnp.int32)
    padded = ((cnt + bm - 1) // bm) * bm
    pend = jnp.cumsum(padded)
    pstart = pend - padded
    pos = jnp.sum(jnp.where(e[:, None] == jnp.arange(N_EXPERTS, dtype=jnp.int32)[None, :], pstart[None, :], 0),
                  axis=1) + rank
    n_blocks = n // bm + N_EXPERTS
    npad = n_blocks * bm
    starts = jnp.arange(n_blocks, dtype=jnp.int32) * bm
    n_used = pend[-1:] // bm
    last_start = (n_used[0] - 1) * bm
    block_e = jnp.sum((pend[None, :] <= jnp.minimum(starts, last_start)[:, None]).astype(jnp.int32), axis=1)
    pos_first, pos_second = pos[0::2], pos[1::2]
    xs = _scatter_rows_twice(a2, pos_first, pos_second, npad)
    ys = _expert_mlps(xs, block_e, n_used, w1, w3, w2, layer)
    return _gather_rows(ys, jnp.concatenate([pos_first, pos_second]))


def _moe_combine(h, y_first, y_second, route, g2):
    w1 = route[:, R_W1:R_W1 + 1]
    w2 = route[:, R_W2:R_W2 + 1]
    return h + g2 * (w1 * y_first + w2 * y_second)


def _gmlp_kernel(h_ref, ya_ref, yb_ref, rt_ref, modp_ref, mod_ref, g1_ref, win_ref, bin_ref, vg_ref, ws_ref, bsb_ref,
                 wout_ref, g2_ref, wrh_ref, wrl_ref, br_ref, ltri_ref,
                 h1_ref, a2_ref, route_ref, cnt_ref, carry_ref, *, n_ctx_tiles, tiles_per_req):
    i = pl.program_id(0)
    seq = _seq_of_tile(i, n_ctx_tiles, tiles_per_req)

    @pl.when(i == 0)
    def _():
        carry_ref[...] = jnp.zeros_like(carry_ref)

    h = _moe_combine(h_ref[...], ya_ref[...], yb_ref[...], rt_ref[...], _mod_row(modp_ref, 5, seq))
    a = _modulated(h, g1_ref[...], _mod_row(mod_ref, 0, seq), _mod_row(mod_ref, 1, seq))
    gw = vg_ref.shape[1]
    z = jax.nn.gelu(_dot(a.astype(BF16), win_ref[...]) + bin_ref[...])
    u = z[:, :gw]
    v = z[:, gw:]
    v = (v * lax.rsqrt(jnp.mean(v * v, axis=-1, keepdims=True) + EPS)) * vg_ref[...]
    vb = v.astype(BF16)
    cg = gw // GM_GROUPS
    tm = h.shape[0]
    rows = []
    for c in range(tm // CHUNK):
        cols = []
        for g in range(GM_GROUPS):
            cols.append(_dot(ws_ref[g], vb[c * CHUNK:(c + 1) * CHUNK, g * cg:(g + 1) * cg]))
        rows.append(jnp.concatenate(cols, axis=1) + bsb_ref[...])
    vm = jnp.concatenate(rows, axis=0)
    mix = _dot((u * vm).astype(BF16), wout_ref[...])
    h1 = h + _mod_row(mod_ref, 2, seq) * mix
    h1_ref[...] = h1
    a2 = _modulated(h1, g2_ref[...], _mod_row(mod_ref, 3, seq), _mod_row(mod_ref, 4, seq))
    a2_ref[...] = a2
    route_ref[...] = _route(a2, wrh_ref[...], wrl_ref[...], br_ref[...], ltri_ref[...], carry_ref)
    cnt_ref[...] = carry_ref[...]


def _gmlp_layer(h, y2, route_prev, mod_prev, mod, g1, win, bin_, vg, ws, bsb, wout, g2, router,
                n_ctx, rows_per_req):
    t, d = h.shape
    tm = ROW_TILE
    wrh, wrl, br = router
    ltri = _lower_tri(tm)
    full = lambda shape: pl.BlockSpec(shape, lambda i: (0,) * len(shape))
    row = lambda w: pl.BlockSpec((tm, w), lambda i: (i, 0))
    kern = functools.partial(_gmlp_kernel, n_ctx_tiles=n_ctx // tm, tiles_per_req=rows_per_req // tm)
    args = (h, y2, y2, route_prev, mod_prev, mod, g1, win, bin_, vg, ws, bsb, wout, g2, wrh, wrl, br, ltri)
    second = pl.BlockSpec((tm, d), lambda i: (i + t // tm, 0))
    in_specs = [row(d), row(d), second, row(LANES)] + [full(a.shape) for a in args[4:]]
    return pl.pallas_call(
        kern,
        grid=(t // tm,),
        in_specs=in_specs,
        out_specs=[row(d), row(d), row(LANES), full((1, LANES))],
        out_shape=[jax.ShapeDtypeStruct((t, d), F32), jax.ShapeDtypeStruct((t, d), F32),
                   jax.ShapeDtypeStruct((t, LANES), F32), jax.ShapeDtypeStruct((1, LANES), F32)],
        scratch_shapes=[pltpu.VMEM((1, LANES), F32)],
        compiler_params=_cparams(1),
        name="gmlp_router",
    )(*args)


def _final_kernel(h_ref, ya_ref, yb_ref, rt_ref, mod_ref, oc_ref, ol_ref, *, n_ctx_tiles, tiles_per_req):
    i = pl.program_id(0)
    seq = _seq_of_tile(i, n_ctx_tiles, tiles_per_req)
    out = _moe_combine(h_ref[...], ya_ref[...], yb_ref[...], rt_ref[...], _mod_row(mod_ref, 5, seq))

    @pl.when(i < n_ctx_tiles)
    def _():
        oc_ref[...] = out

    @pl.when(i >= n_ctx_tiles)
    def _():
        ol_ref[...] = out


def _final_combine(h, y2, route, mod, n_ctx, rows_per_req):
    t, d = h.shape
    tm = ROW_TILE
    row = lambda w: pl.BlockSpec((tm, w), lambda i: (i, 0))
    kern = functools.partial(_final_kernel, n_ctx_tiles=n_ctx // tm, tiles_per_req=rows_per_req // tm)
    return pl.pallas_call(
        kern,
        grid=(t // tm,),
        in_specs=[row(d), row(d), pl.BlockSpec((tm, d), lambda i: (i + t // tm, 0)), row(LANES),
                  pl.BlockSpec(mod.shape, lambda i: (0, 0, 0))],
        out_specs=_split_specs(tm, d, n_ctx // tm),
        out_shape=[jax.ShapeDtypeStruct((n_ctx, d), F32), jax.ShapeDtypeStruct((t - n_ctx, d), F32)],
        compiler_params=_cparams(1),
        name="final_combine",
    )(h, y2, y2, route, mod)


def kernel(x_prompt, x_sample, cache_k, cache_v, c, c_ctx, ada_w, ada_b, norm1_g, norm2_g, attn_wq, attn_wk,
           attn_wv, attn_wo, attn_qnorm, attn_knorm, attn_lam, attn_subln, gm_win, gm_bin, gm_vnorm, gm_ws,
           gm_bs, gm_wout, moe_wc, moe_bc, moe_wf, moe_bf, moe_w1, moe_w3, moe_w2):
    batch, seq_len, d = x_prompt.shape
    n_req, req_len, _ = x_sample.shape
    depth = ada_w.shape[0]
    assert depth == 2 and attn_wq.shape[0] == 1 and gm_win.shape[0] == 1
    n_ctx = batch * seq_len
    assert n_ctx % req_len == 0 and req_len % ROW_TILE == 0 and seq_len == ROW_TILE
    past = cache_k.shape[2]

    h_ctx = x_prompt.reshape(n_ctx, d)
    h_lat = x_sample.reshape(n_req * req_len, d)
    cvec = jnp.concatenate([c_ctx[None, :], c, jnp.zeros((8 - 1 - n_req, d), F32)], axis=0)
    mod = _ada_mod(cvec, ada_w, ada_b)
    routers = [_router_weights(moe_wc[i], moe_bc[i], moe_wf[i], moe_bf[i]) for i in range(depth)]
    row_vec = lambda v: v.reshape(1, -1)

    wqkv = jnp.concatenate([attn_wq[0], attn_wk[0], attn_wv[0]], axis=1).astype(BF16)
    reps = d // HEAD_DIM
    q, k, v, k_new, v_new = _pre_attn(h_ctx, h_lat, mod[0], row_vec(norm1_g[0]), wqkv,
                                      row_vec(jnp.tile(attn_qnorm[0], reps)), row_vec(jnp.tile(attn_knorm[0], reps)),
                                      req_len)
    o_ctx, o_lat = _attention(q, k, v, attn_lam[0], row_vec(attn_subln[0]),
                              cache_k[:, 0].reshape(n_req, past, d), cache_v[:, 0].reshape(n_req, past, d),
                              n_ctx, seq_len, n_req, req_len, layer=0)
    h1, a2, route0, cnt0 = _post_attn(h_ctx, h_lat, o_ctx, o_lat, mod[0], attn_wo[0].astype(BF16),
                                      row_vec(norm2_g[0]), routers[0], req_len)
    y2 = _moe(a2, route0, cnt0, moe_w1, moe_w3, moe_w2, layer=0)

    gw = gm_vnorm.shape[1]
    bsb = jnp.repeat(gm_bs[0].T, gw // GM_GROUPS, axis=1)
    h2, a2, route1, cnt1 = _gmlp_layer(h1, y2, route0, mod[0], mod[1], row_vec(norm1_g[1]), gm_win[0].astype(BF16),
                                       row_vec(gm_bin[0]), row_vec(gm_vnorm[0]), gm_ws[0].astype(BF16), bsb,
                                       gm_wout[0].astype(BF16), row_vec(norm2_g[1]), routers[1], n_ctx, req_len)
    y2 = _moe(a2, route1, cnt1, moe_w1, moe_w3, moe_w2, layer=1)
    y_ctx, y_lat = _final_combine(h2, y2, route1, mod[1], n_ctx, req_len)

    y_prompt = y_ctx.reshape(batch, seq_len, d)
    y_sample = y_lat.reshape(n_req, req_len, d)
    new_cache_k = k_new.reshape(batch, 1, seq_len, N_HEADS, 2, HEAD_DIM)
    new_cache_v = v_new.reshape(batch, 1, seq_len, N_HEADS, V_DIM)
    return (y_prompt, y_sample, new_cache_k, new_cache_v)
```

```python
import functools
import math

import jax
import jax.numpy as jnp
from jax import lax
from jax.experimental import pallas as pl
from jax.experimental.pallas import tpu as pltpu
from jax.experimental.pallas import tpu_sc as plsc

F32 = jnp.float32
BF16 = jnp.bfloat16

N_HEADS = 8
HEAD_DIM = 64
V_DIM = 2 * HEAD_DIM
GRID_W = 64
ROPE_THETA = 10000.0
CHUNK = 128
GM_GROUPS = 8
N_EXPERT_GROUPS = 4
EXPERTS_PER_GROUP = 8
N_EXPERTS = N_EXPERT_GROUPS * EXPERTS_PER_GROUP
EPS = 1e-6

LANES = 128
ROW_TILE = 256
MOE_BLOCK = 256
SC_WINDOW_BYTES = 128 * 1024
ATTN_Q_TILE = 256
VMEM_LIMIT = 56 * 1024 * 1024

R_E1, R_E2, R_W1, R_W2, R_RANK1, R_RANK2 = 0, 1, 2, 3, 4, 5
L_EXPERT0 = N_EXPERT_GROUPS


def _lambda_init(layer):
    return 0.8 - 0.6 * math.exp(-0.3 * layer)


def _cparams(n_axes):
    return pltpu.CompilerParams(dimension_semantics=("arbitrary",) * n_axes, vmem_limit_bytes=VMEM_LIMIT)


def _seq_of_tile(i, n_ctx_tiles, tiles_per_req):
    return jnp.where(i < n_ctx_tiles, 0, 1 + (i - n_ctx_tiles) // tiles_per_req)


def _pick_rows(i, n_ctx_tiles, ctx_ref, lat_ref):
    return jnp.where(i < n_ctx_tiles, ctx_ref[...], lat_ref[...])


def _split_specs(tm, d, n_ctx_tiles):
    return [pl.BlockSpec((tm, d), lambda i: (jnp.minimum(i, n_ctx_tiles - 1), 0)),
            pl.BlockSpec((tm, d), lambda i: (jnp.maximum(i - n_ctx_tiles, 0), 0))]


def _mod_row(mod_ref, part, seq):
    return mod_ref[part, pl.ds(seq, 1), :]


def _modulated(x, g, shift, scale):
    y = x * lax.rsqrt(jnp.mean(x * x, axis=-1, keepdims=True) + EPS)
    return (y * g) * (1.0 + scale) + shift


def _split_bf16(x):
    hi = x.astype(BF16)
    lo = (x - hi.astype(F32)).astype(BF16)
    return hi, lo


def _dot(a, b):
    return jnp.dot(a, b, preferred_element_type=F32)


def _pack_bf16_pairs(x):
    half = x.shape[1] // 2
    bits = lax.bitcast_convert_type(x.astype(BF16).astype(F32), jnp.uint32)
    packed = (bits[:, :half] >> 16) | bits[:, half:]
    return lax.bitcast_convert_type(packed, jnp.int32)


def _unpack_bf16_pairs(p):
    u = lax.bitcast_convert_type(p, jnp.uint32)
    lo = lax.bitcast_convert_type(u << 16, F32).astype(BF16)
    hi = lax.bitcast_convert_type(u & jnp.uint32(0xFFFF0000), F32).astype(BF16)
    return lo, hi


def _ada_kernel(c_ref, w_ref, b_ref, o_ref):
    c = c_ref[...]
    s = c * jax.nn.sigmoid(c)
    o_ref[...] = _dot(s.astype(BF16), w_ref[...].astype(BF16)) + b_ref[...]


def _ada_mod(cvec, ada_w, ada_b):
    depth, d, d6 = ada_w.shape
    parts = d6 // d
    rows = cvec.shape[0]
    return pl.pallas_call(
        _ada_kernel,
        grid=(depth, parts),
        in_specs=[
            pl.BlockSpec((rows, d), lambda l, j: (0, 0)),
            pl.BlockSpec((None, d, d), lambda l, j: (l, 0, j)),
            pl.BlockSpec((None, 1, d), lambda l, j: (l, 0, j)),
        ],
        out_specs=pl.BlockSpec((None, None, rows, d), lambda l, j: (l, j, 0, 0)),
        out_shape=jax.ShapeDtypeStruct((depth, parts, rows, d), F32),
        compiler_params=_cparams(2),
        name="ada_mod",
    )(cvec, ada_w, ada_b.reshape(depth, 1, d6))


def _pre_attn_kernel(hc_ref, hl_ref, mod_ref, g_ref, w_ref, qg_ref, kg_ref, gs_ref, gb_ref, cos_ref, sin_ref,
                     q_ref, k_ref, v_ref, kf_ref, vf_ref, *, n_ctx_tiles, tiles_per_req):
    i = pl.program_id(0)
    seq = _seq_of_tile(i, n_ctx_tiles, tiles_per_req)
    d = hc_ref.shape[1]
    x = _pick_rows(i, n_ctx_tiles, hc_ref, hl_ref)
    a = _modulated(x, g_ref[...], _mod_row(mod_ref, 0, seq), _mod_row(mod_ref, 1, seq))
    qkv = _dot(a.astype(BF16), w_ref[...])
    q = qkv[:, :d]
    k = qkv[:, d:2 * d]
    v = qkv[:, 2 * d:]
    sq_hi, sq_lo = _split_bf16(jnp.concatenate([q * q, k * k], axis=1))
    gsum = _dot(sq_hi, gs_ref[...]) + _dot(sq_lo, gs_ref[...])
    r_hi, r_lo = _split_bf16(lax.rsqrt(gsum * (1.0 / HEAD_DIM) + EPS))
    rb = _dot(r_hi, gb_ref[...]) + _dot(r_lo, gb_ref[...])
    qn = (q * rb[:, :d]) * qg_ref[...]
    kn = (k * rb[:, d:]) * kg_ref[...]
    v_ref[...] = v.astype(BF16)
    q_scale = HEAD_DIM ** -0.5

    @pl.when(i < n_ctx_tiles)
    def _():
        q_ref[...] = (qn * q_scale).astype(BF16)
        k_ref[...] = kn.astype(BF16)
        kf_ref[...] = kn
        vf_ref[...] = v

    @pl.when(i >= n_ctx_tiles)
    def _():
        cos = cos_ref[...]
        sin = sin_ref[...]
        lane = lax.broadcasted_iota(jnp.int32, cos.shape, 1)
        first = (lane & 31) < 16

        def rope(x):
            outs = []
            for hh in range(d // LANES):
                xs = x[:, hh * LANES:(hh + 1) * LANES]
                rot = jnp.where(first, pltpu.roll(xs, LANES - 16, 1), pltpu.roll(xs, 16, 1))
                outs.append(xs * cos + rot * sin)
            return jnp.concatenate(outs, axis=1)

        q_ref[...] = (rope(qn) * q_scale).astype(BF16)
        k_ref[...] = rope(kn).astype(BF16)


def _rope_tables(n_pos):
    half = HEAD_DIM // 2
    pos = jnp.arange(n_pos, dtype=jnp.int32)
    row = (pos // GRID_W).astype(F32)
    col = (pos % GRID_W).astype(F32)
    inv = 1.0 / (ROPE_THETA ** (jnp.arange(0, half, 2, dtype=F32) / half))
    ang_r = row[:, None] * inv[None, :]
    ang_c = col[:, None] * inv[None, :]
    ang = jnp.concatenate([ang_r, ang_r, ang_c, ang_c], axis=-1)
    quarter = half // 2
    sign = jnp.tile(jnp.concatenate([-jnp.ones((quarter,), F32), jnp.ones((quarter,), F32)]), 2)
    cos = jnp.tile(jnp.cos(ang), (1, 2))
    sin = jnp.tile(jnp.sin(ang) * sign[None, :], (1, 2))
    return cos, sin


def _pre_attn(h_ctx, h_lat, mod, g, wqkv, qg, kg, rows_per_req):
    n_ctx, d = h_ctx.shape
    t = n_ctx + h_lat.shape[0]
    tm = ROW_TILE
    n_ctx_tiles = n_ctx // tm
    tiles_per_req = rows_per_req // tm
    groups = d // HEAD_DIM
    gid = jnp.arange(2 * d, dtype=jnp.int32) // HEAD_DIM
    gs = (gid[:, None] == jnp.arange(LANES, dtype=jnp.int32)[None, :]).astype(BF16)
    gb = gs.T
    assert 2 * groups <= LANES
    cos, sin = _rope_tables(rows_per_req)
    pos_map = lambda i: (jnp.where(i < n_ctx_tiles, 0, (i - n_ctx_tiles) % tiles_per_req), 0)
    ctx_map = lambda i: (jnp.minimum(i, n_ctx_tiles - 1), 0)
    full = lambda shape: pl.BlockSpec(shape, lambda i: (0,) * len(shape))
    row = pl.BlockSpec((tm, d), lambda i: (i, 0))
    kern = functools.partial(_pre_attn_kernel, n_ctx_tiles=n_ctx_tiles, tiles_per_req=tiles_per_req)
    return pl.pallas_call(
        kern,
        grid=(t // tm,),
        in_specs=_split_specs(tm, d, n_ctx_tiles) + [
                  full(mod.shape), full((1, d)), full(wqkv.shape), full((1, d)), full((1, d)),
                  full(gs.shape), full(gb.shape),
                  pl.BlockSpec((tm, LANES), pos_map), pl.BlockSpec((tm, LANES), pos_map)],
        out_specs=[row, row, row, pl.BlockSpec((tm, d), ctx_map), pl.BlockSpec((tm, d), ctx_map)],
        out_shape=[jax.ShapeDtypeStruct((t, d), BF16)] * 3 + [jax.ShapeDtypeStruct((n_ctx, d), F32)] * 2,
        compiler_params=_cparams(1),
        name="pre_attn",
    )(h_ctx, h_lat, mod, g, wqkv, qg, kg, gs, gb, cos, sin)


def _attn_kernel(*refs, n_heads, has_cache, layer):
    if has_cache:
        lam_ref, sub_ref, q_ref, kn_ref, vn_ref, kc_ref, vc_ref, o_ref = refs
    else:
        lam_ref, sub_ref, q_ref, kn_ref, vn_ref, o_ref = refs
    lf = lam_ref[...]
    lam = (jnp.exp(jnp.sum(lf[0:1] * lf[1:2], axis=-1, keepdims=True))
           - jnp.exp(jnp.sum(lf[2:3] * lf[3:4], axis=-1, keepdims=True)) + _lambda_init(layer))
    nt = (((1,), (1,)), ((), ()))
    tq = q_ref.shape[0]
    lane = lax.broadcasted_iota(jnp.int32, (tq, LANES), 1)
    for hh in range(n_heads):
        sl = slice(hh * LANES, (hh + 1) * LANES)
        qh = q_ref[:, sl]
        zero = jnp.zeros_like(qh)
        qs = (jnp.where(lane < HEAD_DIM, qh, zero), jnp.where(lane >= HEAD_DIM, qh, zero))
        keys = [kn_ref[:, sl]]
        vals = [vn_ref[:, sl]]
        if has_cache:
            keys.append(kc_ref[:, sl].astype(BF16))
            vals.append(vc_ref[:, sl].astype(BF16))
        probs = []
        for qm in qs:
            s = [lax.dot_general(qm, kk, nt, preferred_element_type=F32) for kk in keys]
            m = functools.reduce(jnp.maximum, [jnp.max(x, axis=-1, keepdims=True) for x in s])
            e = [jnp.exp(x - m) for x in s]
            den = functools.reduce(lambda u, w: u + w, [jnp.sum(x, axis=-1, keepdims=True) for x in e])
            probs.append((e, 1.0 / den))
        (e0, r0), (e1, r1) = probs
        o = None
        for j, vv in enumerate(vals):
            aj = e0[j] * r0 - lam * (e1[j] * r1)
            oj = _dot(aj.astype(BF16), vv)
            o = oj if o is None else o + oj
        o = o * lax.rsqrt(jnp.mean(o * o, axis=-1, keepdims=True) + EPS)
        o = (o * sub_ref[...]) * (1.0 - _lambda_init(layer))
        o_ref[:, sl] = o.astype(BF16)


def _attention(q, k, v, lam_p, subln, cache_k, cache_v, n_ctx, ctx_len, n_req, req_len, layer):
    t, d = q.shape
    small = lambda shape: pl.BlockSpec(shape, lambda *_: (0,) * len(shape))
    n_ctx_req = n_ctx // ctx_len
    blk = pl.BlockSpec((ctx_len, d), lambda b: (b, 0))
    o_ctx = pl.pallas_call(
        functools.partial(_attn_kernel, n_heads=N_HEADS, has_cache=False, layer=layer),
        grid=(n_ctx_req,),
        in_specs=[small(lam_p.shape), small(subln.shape), blk, blk, blk],
        out_specs=blk,
        out_shape=jax.ShapeDtypeStruct((n_ctx, d), BF16),
        compiler_params=_cparams(1),
        name="attn_ctx",
    )(lam_p, subln, q, k, v)

    tq = ATTN_Q_TILE
    nq = req_len // tq
    q0 = n_ctx // tq
    r0 = n_ctx // req_len
    past = cache_k.shape[1]
    small3 = lambda shape: pl.BlockSpec(shape, lambda b, h, i: (0,) * len(shape))
    qspec = pl.BlockSpec((tq, LANES), lambda b, h, i: (q0 + b * nq + i, h))
    kvspec = pl.BlockSpec((req_len, LANES), lambda b, h, i: (r0 + b, h))
    cspec = pl.BlockSpec((None, past, LANES), lambda b, h, i: (b, 0, h))
    o_lat = pl.pallas_call(
        functools.partial(_attn_kernel, n_heads=1, has_cache=True, layer=layer),
        grid=(n_req, N_HEADS, nq),
        in_specs=[small3(lam_p.shape), pl.BlockSpec((1, LANES), lambda b, h, i: (0, 0)),
                  qspec, kvspec, kvspec, cspec, cspec],
        out_specs=pl.BlockSpec((tq, LANES), lambda b, h, i: (b * nq + i, h)),
        out_shape=jax.ShapeDtypeStruct((n_req * req_len, d), BF16),
        compiler_params=_cparams(3),
        name="attn_latent",
    )(lam_p, subln, q, k, v, cache_k, cache_v)
    return o_ctx, o_lat


def _route(a2, wr_hi, wr_lo, br, ltri, carry_ref):
    tm = a2.shape[0]
    a_hi, a_lo = _split_bf16(a2)
    logits = _dot(a_hi, wr_hi) + (_dot(a_lo, wr_hi) + _dot(a_hi, wr_lo)) + br
    lane_i = lax.broadcasted_iota(jnp.int32, (tm, LANES), 1)
    lane = lane_i.astype(F32)
    neg = jnp.full((tm, LANES), -jnp.inf, F32)
    big = jnp.full((tm, LANES), float(LANES), F32)
    first_lane = lambda mask: jnp.min(jnp.where(mask, lane, big), axis=-1, keepdims=True)

    lc = jnp.where(lane_i < N_EXPERT_GROUPS, logits, neg)
    mc = jnp.max(lc, axis=-1, keepdims=True)
    pg = 1.0 / jnp.sum(jnp.exp(lc - mc), axis=-1, keepdims=True)
    gi = first_lane(lc == mc)
    assert EXPERTS_PER_GROUP == 8
    grp = lax.shift_right_arithmetic(lane_i - L_EXPERT0, 3).astype(F32)
    in_group = (lane_i >= L_EXPERT0) & (lane_i < L_EXPERT0 + N_EXPERTS) & (grp == gi)
    ls = jnp.where(in_group, logits, neg)
    t1 = jnp.max(ls, axis=-1, keepdims=True)
    i1 = first_lane(ls == t1)
    ls2 = jnp.where(lane == i1, neg, ls)
    t2 = jnp.max(ls2, axis=-1, keepdims=True)
    i2 = first_lane(ls2 == t2)
    ex = jnp.exp(t2 - t1)
    w1 = pg * (1.0 / (1.0 + ex))
    w2 = pg * (ex / (1.0 + ex))
    e1 = i1 - float(L_EXPERT0)
    e2 = i2 - float(L_EXPERT0)
    oh1 = lane == e1
    oh2 = lane == e2
    onehot = oh1.astype(F32) + oh2.astype(F32)
    before = _dot(ltri, onehot.astype(BF16)) + carry_ref[...]
    zero = jnp.zeros_like(before)
    rank1 = jnp.sum(jnp.where(oh1, before, zero), axis=-1, keepdims=True)
    rank2 = jnp.sum(jnp.where(oh2, before, zero), axis=-1, keepdims=True)
    carry_ref[...] = carry_ref[...] + jnp.sum(onehot, axis=0, keepdims=True)
    slab = jnp.zeros((tm, LANES), F32)
    for ln, val in ((R_E1, e1), (R_E2, e2), (R_W1, w1), (R_W2, w2), (R_RANK1, rank1), (R_RANK2, rank2)):
        slab = jnp.where(lane_i == ln, val, slab)
    return slab


def _router_weights(wc, bc, wf, bf_):
    d = wc.shape[0]
    pad = LANES - N_EXPERT_GROUPS - N_EXPERTS
    w = jnp.concatenate([wc, wf, jnp.zeros((d, pad), F32)], axis=1)
    b = jnp.concatenate([bc, bf_, jnp.zeros((pad,), F32)])[None, :]
    hi = w.astype(BF16)
    lo = (w - hi.astype(F32)).astype(BF16)
    return hi, lo, b


def _lower_tri(n):
    r = jnp.arange(n, dtype=jnp.int32)
    return (r[None, :] < r[:, None]).astype(BF16)


def _post_attn_kernel(hc_ref, hl_ref, oc_ref, ol_ref, mod_ref, wo_ref, g2_ref, wrh_ref, wrl_ref, br_ref, ltri_ref,
                      h1_ref, a2_ref, route_ref, cnt_ref, carry_ref, *, n_ctx_tiles, tiles_per_req):
    i = pl.program_id(0)
    seq = _seq_of_tile(i, n_ctx_tiles, tiles_per_req)

    @pl.when(i == 0)
    def _():
        carry_ref[...] = jnp.zeros_like(carry_ref)

    o = _pick_rows(i, n_ctx_tiles, oc_ref, ol_ref)
    h1 = _pick_rows(i, n_ctx_tiles, hc_ref, hl_ref) + _mod_row(mod_ref, 2, seq) * _dot(o, wo_ref[...])
    h1_ref[...] = h1
    a2 = _modulated(h1, g2_ref[...], _mod_row(mod_ref, 3, seq), _mod_row(mod_ref, 4, seq))
    a2_ref[...] = _pack_bf16_pairs(a2)
    route_ref[...] = _route(a2, wrh_ref[...], wrl_ref[...], br_ref[...], ltri_ref[...], carry_ref)
    cnt_ref[...] = carry_ref[...]


def _post_attn(h_ctx, h_lat, o_ctx, o_lat, mod, wo, g2, router, rows_per_req):
    n_ctx, d = h_ctx.shape
    t = n_ctx + h_lat.shape[0]
    tm = ROW_TILE
    wrh, wrl, br = router
    ltri = _lower_tri(tm)
    full = lambda shape: pl.BlockSpec(shape, lambda i: (0,) * len(shape))
    row = pl.BlockSpec((tm, d), lambda i: (i, 0))
    kern = functools.partial(_post_attn_kernel, n_ctx_tiles=n_ctx // tm, tiles_per_req=rows_per_req // tm)
    return pl.pallas_call(
        kern,
        grid=(t // tm,),
        in_specs=_split_specs(tm, d, n_ctx // tm) + _split_specs(tm, d, n_ctx // tm) + [
                  full(mod.shape), full(wo.shape), full((1, d)), full(wrh.shape), full(wrl.shape),
                  full(br.shape), full(ltri.shape)],
        out_specs=[row, pl.BlockSpec((tm, d // 2), lambda i: (i, 0)), pl.BlockSpec((tm, LANES), lambda i: (i, 0)),
                   full((1, LANES))],
        out_shape=[jax.ShapeDtypeStruct((t, d), F32), jax.ShapeDtypeStruct((t, d // 2), jnp.int32),
                   jax.ShapeDtypeStruct((t, LANES), F32), jax.ShapeDtypeStruct((1, LANES), F32)],
        scratch_shapes=[pltpu.VMEM((1, LANES), F32)],
        compiler_params=_cparams(1),
        name="post_attn_router",
    )(h_ctx, h_lat, o_ctx, o_lat, mod, wo, g2, wrh, wrl, br, ltri)


def _gather_rows(src, idx):
    n = idx.shape[0]
    d = src.shape[1]
    w = _sc_window_rows(src)
    mesh = plsc.VectorSubcoreMesh(core_axis_name="core", subcore_axis_name="subcore")

    @pl.kernel(out_type=jax.ShapeDtypeStruct((n, d), src.dtype), mesh=mesh)
    def gather(src_hbm, idx_hbm, out_hbm):
        def body(idx_vmem, out_vmem):
            pltpu.sync_copy(src_hbm.at[idx_vmem.at[0, pl.ds(0, w)]], out_vmem)

        pltpu.emit_pipeline(
            body,
            grid=(n // w,),
            in_specs=[pl.BlockSpec((1, LANES), lambda i: (i, 0))],
            out_specs=[pl.BlockSpec((w, d), lambda i: (i, 0))],
            core_axis_name=("core", "subcore"),
            dimension_semantics=(pltpu.PARALLEL,),
        )(idx_hbm, out_hbm)

    return gather(src, _index_windows(idx, w))


def _sc_window_rows(src):
    return min(LANES, SC_WINDOW_BYTES // (src.shape[1] * src.dtype.itemsize))


def _index_windows(idx, w):
    return jnp.pad(idx.reshape(idx.shape[0] // w, w), ((0, 0), (0, LANES - w)))


def _scatter_rows_twice(src, idx_a, idx_b, n_out):
    t, d = src.shape
    w = _sc_window_rows(src)
    mesh = plsc.VectorSubcoreMesh(core_axis_name="core", subcore_axis_name="subcore")

    @pl.kernel(out_type=jax.ShapeDtypeStruct((n_out, d), src.dtype), mesh=mesh)
    def scatter(src_hbm, ia_hbm, ib_hbm, out_hbm):
        def body(src_vmem, ia_vmem, ib_vmem):
            pltpu.sync_copy(src_vmem, out_hbm.at[ia_vmem.at[0, pl.ds(0, w)]])
            pltpu.sync_copy(src_vmem, out_hbm.at[ib_vmem.at[0, pl.ds(0, w)]])

        pltpu.emit_pipeline(
            body,
            grid=(t // w,),
            in_specs=[pl.BlockSpec((w, d), lambda i: (i, 0)),
                      pl.BlockSpec((1, LANES), lambda i: (i, 0)),
                      pl.BlockSpec((1, LANES), lambda i: (i, 0))],
            out_specs=[],
            core_axis_name=("core", "subcore"),
            dimension_semantics=(pltpu.PARALLEL,),
        )(src_hbm, ia_hbm, ib_hbm)

    return scatter(src, _index_windows(idx_a, w), _index_windows(idx_b, w))


def _expert_kernel(be_ref, nb_ref, x_ref, w1_ref, w3_ref, w2_ref, y_ref, w1b, w3b, w2b):
    b = pl.program_id(0)
    prev = be_ref[jnp.maximum(b - 1, 0)]
    used = b < nb_ref[0]

    @pl.when(used & ((b == 0) | (be_ref[b] != prev)))
    def _():
        w1b[...] = w1_ref[...].astype(BF16)
        w3b[...] = w3_ref[...].astype(BF16)
        w2b[...] = w2_ref[...].astype(BF16)

    @pl.when(used)
    def _():
        x_lo, x_hi = _unpack_bf16_pairs(x_ref[...])
        half = x_lo.shape[1]
        h1 = _dot(x_lo, w1b[:half, :]) + _dot(x_hi, w1b[half:, :])
        h3 = _dot(x_lo, w3b[:half, :]) + _dot(x_hi, w3b[half:, :])
        hb = jax.nn.silu(h1) * h3
        y_ref[...] = _dot(hb.astype(BF16), w2b[...])


def _expert_mlps(xs, block_e, n_used, w1, w3, w2, layer):
    npad, half = xs.shape
    d, ff = w1.shape[2], w1.shape[3]
    assert d == 2 * half
    bm = MOE_BLOCK
    rows = lambda b, be, nb: (jnp.minimum(b, nb[0] - 1), 0)
    grid_spec = pltpu.PrefetchScalarGridSpec(
        num_scalar_prefetch=2,
        grid=(npad // bm,),
        in_specs=[pl.BlockSpec((bm, half), rows),
                  pl.BlockSpec((None, None, d, ff), lambda b, be, nb: (layer, be[b], 0, 0)),
                  pl.BlockSpec((None, None, d, ff), lambda b, be, nb: (layer, be[b], 0, 0)),
                  pl.BlockSpec((None, None, ff, d), lambda b, be, nb: (layer, be[b], 0, 0))],
        out_specs=pl.BlockSpec((bm, d), rows),
        scratch_shapes=[pltpu.VMEM((d, ff), BF16), pltpu.VMEM((d, ff), BF16), pltpu.VMEM((ff, d), BF16)],
    )
    return pl.pallas_call(
        _expert_kernel,
        grid_spec=grid_spec,
        out_shape=jax.ShapeDtypeStruct((npad, d), F32),
        compiler_params=_cparams(1),
        name="expert_mlps",
    )(block_e, n_used, xs, w1, w3, w2)


def _moe(a2p, route, counts, w1, w3, w2, layer):
    t = a2p.shape[0]
    bm = MOE_BLOCK
    n = 2 * t
    e = route[:, R_E1:R_E2 + 1].astype(jnp.int32).reshape(n)
    rank = route[:, R_RANK1:R_RANK2 + 1].astype(jnp.int32).reshape(n)
    cnt = counts[0, :N_EXPERTS].astype(jnp.int32)
    padded = ((cnt + bm - 1) // bm) * bm
    pend = jnp.cumsum(padded)
    pstart = pend - padded
    pos = jnp.sum(jnp.where(e[:, None] == jnp.arange(N_EXPERTS, dtype=jnp.int32)[None, :], pstart[None, :], 0),
                  axis=1) + rank
    n_blocks = n // bm + N_EXPERTS
    npad = n_blocks * bm
    starts = jnp.arange(n_blocks, dtype=jnp.int32) * bm
    n_used = pend[-1:] // bm
    last_start = (n_used[0] - 1) * bm
    block_e = jnp.sum((pend[None, :] <= jnp.minimum(starts, last_start)[:, None]).astype(jnp.int32), axis=1)
    pos_first, pos_second = pos[0::2], pos[1::2]
    xs = _scatter_rows_twice(a2p, pos_first, pos_second, npad)
    ys = _expert_mlps(xs, block_e, n_used, w1, w3, w2, layer)
    return _gather_rows(ys, jnp.concatenate([pos_first, pos_second]))


def _moe_combine(h, y_first, y_second, route, g2):
    w1 = route[:, R_W1:R_W1 + 1]
    w2 = route[:, R_W2:R_W2 + 1]
    return h + g2 * (w1 * y_first + w2 * y_second)


def _gmlp_kernel(h_ref, ya_ref, yb_ref, rt_ref, modp_ref, mod_ref, g1_ref, win_ref, bin_ref, vg_ref, ws_ref, bsb_ref,
                 wout_ref, g2_ref, wrh_ref, wrl_ref, br_ref, ltri_ref,
                 h1_ref, a2_ref, route_ref, cnt_ref, carry_ref, *, n_ctx_tiles, tiles_per_req):
    i = pl.program_id(0)
    seq = _seq_of_tile(i, n_ctx_tiles, tiles_per_req)

    @pl.when(i == 0)
    def _():
        carry_ref[...] = jnp.zeros_like(carry_ref)

    h = _moe_combine(h_ref[...], ya_ref[...], yb_ref[...], rt_ref[...], _mod_row(modp_ref, 5, seq))
    a = _modulated(h, g1_ref[...], _mod_row(mod_ref, 0, seq), _mod_row(mod_ref, 1, seq))
    gw = vg_ref.shape[1]
    z = jax.nn.gelu(_dot(a.astype(BF16), win_ref[...]) + bin_ref[...])
    u = z[:, :gw]
    v = z[:, gw:]
    v = (v * lax.rsqrt(jnp.mean(v * v, axis=-1, keepdims=True) + EPS)) * vg_ref[...]
    vb = v.astype(BF16)
    cg = gw // GM_GROUPS
    tm = h.shape[0]
    rows = []
    for c in range(tm // CHUNK):
        cols = []
        for g in range(GM_GROUPS):
            cols.append(_dot(ws_ref[g], vb[c * CHUNK:(c + 1) * CHUNK, g * cg:(g + 1) * cg]))
        rows.append(jnp.concatenate(cols, axis=1) + bsb_ref[...])
    vm = jnp.concatenate(rows, axis=0)
    mix = _dot((u * vm).astype(BF16), wout_ref[...])
    h1 = h + _mod_row(mod_ref, 2, seq) * mix
    h1_ref[...] = h1
    a2 = _modulated(h1, g2_ref[...], _mod_row(mod_ref, 3, seq), _mod_row(mod_ref, 4, seq))
    a2_ref[...] = _pack_bf16_pairs(a2)
    route_ref[...] = _route(a2, wrh_ref[...], wrl_ref[...], br_ref[...], ltri_ref[...], carry_ref)
    cnt_ref[...] = carry_ref[...]


def _gmlp_layer(h, y2, route_prev, mod_prev, mod, g1, win, bin_, vg, ws, bsb, wout, g2, router,
                n_ctx, rows_per_req):
    t, d = h.shape
    tm = ROW_TILE
    wrh, wrl, br = router
    ltri = _lower_tri(tm)
    full = lambda shape: pl.BlockSpec(shape, lambda i: (0,) * len(shape))
    row = lambda w: pl.BlockSpec((tm, w), lambda i: (i, 0))
    kern = functools.partial(_gmlp_kernel, n_ctx_tiles=n_ctx // tm, tiles_per_req=rows_per_req // tm)
    args = (h, y2, y2, route_prev, mod_prev, mod, g1, win, bin_, vg, ws, bsb, wout, g2, wrh, wrl, br, ltri)
    second = pl.BlockSpec((tm, d), lambda i: (i + t // tm, 0))
    in_specs = [row(d), row(d), second, row(LANES)] + [full(a.shape) for a in args[4:]]
    return pl.pallas_call(
        kern,
        grid=(t // tm,),
        in_specs=in_specs,
        out_specs=[row(d), row(d // 2), row(LANES), full((1, LANES))],
        out_shape=[jax.ShapeDtypeStruct((t, d), F32), jax.ShapeDtypeStruct((t, d // 2), jnp.int32),
                   jax.ShapeDtypeStruct((t, LANES), F32), jax.ShapeDtypeStruct((1, LANES), F32)],
        scratch_shapes=[pltpu.VMEM((1, LANES), F32)],
        compiler_params=_cparams(1),
        name="gmlp_router",
    )(*args)


def _final_kernel(h_ref, ya_ref, yb_ref, rt_ref, mod_ref, oc_ref, ol_ref, *, n_ctx_tiles, tiles_per_req):
    i = pl.program_id(0)
    seq = _seq_of_tile(i, n_ctx_tiles, tiles_per_req)
    out = _moe_combine(h_ref[...], ya_ref[...], yb_ref[...], rt_ref[...], _mod_row(mod_ref, 5, seq))

    @pl.when(i < n_ctx_tiles)
    def _():
        oc_ref[...] = out

    @pl.when(i >= n_ctx_tiles)
    def _():
        ol_ref[...] = out


def _final_combine(h, y2, route, mod, n_ctx, rows_per_req):
    t, d = h.shape
    tm = ROW_TILE
    row = lambda w: pl.BlockSpec((tm, w), lambda i: (i, 0))
    kern = functools.partial(_final_kernel, n_ctx_tiles=n_ctx // tm, tiles_per_req=rows_per_req // tm)
    return pl.pallas_call(
        kern,
        grid=(t // tm,),
        in_specs=[row(d), row(d), pl.BlockSpec((tm, d), lambda i: (i + t // tm, 0)), row(LANES),
                  pl.BlockSpec(mod.shape, lambda i: (0, 0, 0))],
        out_specs=_split_specs(tm, d, n_ctx // tm),
        out_shape=[jax.ShapeDtypeStruct((n_ctx, d), F32), jax.ShapeDtypeStruct((t - n_ctx, d), F32)],
        compiler_params=_cparams(1),
        name="final_combine",
    )(h, y2, y2, route, mod)


def kernel(x_prompt, x_sample, cache_k, cache_v, c, c_ctx, ada_w, ada_b, norm1_g, norm2_g, attn_wq, attn_wk,
           attn_wv, attn_wo, attn_qnorm, attn_knorm, attn_lam, attn_subln, gm_win, gm_bin, gm_vnorm, gm_ws,
           gm_bs, gm_wout, moe_wc, moe_bc, moe_wf, moe_bf, moe_w1, moe_w3, moe_w2):
    batch, seq_len, d = x_prompt.shape
    n_req, req_len, _ = x_sample.shape
    depth = ada_w.shape[0]
    assert depth == 2 and attn_wq.shape[0] == 1 and gm_win.shape[0] == 1
    n_ctx = batch * seq_len
    assert n_ctx % req_len == 0 and req_len % ROW_TILE == 0 and seq_len == ROW_TILE
    past = cache_k.shape[2]

    h_ctx = x_prompt.reshape(n_ctx, d)
    h_lat = x_sample.reshape(n_req * req_len, d)
    cvec = jnp.concatenate([c_ctx[None, :], c, jnp.zeros((8 - 1 - n_req, d), F32)], axis=0)
    mod = _ada_mod(cvec, ada_w, ada_b)
    routers = [_router_weights(moe_wc[i], moe_bc[i], moe_wf[i], moe_bf[i]) for i in range(depth)]
    row_vec = lambda v: v.reshape(1, -1)

    wqkv = jnp.concatenate([attn_wq[0], attn_wk[0], attn_wv[0]], axis=1).astype(BF16)
    reps = d // HEAD_DIM
    q, k, v, k_new, v_new = _pre_attn(h_ctx, h_lat, mod[0], row_vec(norm1_g[0]), wqkv,
                                      row_vec(jnp.tile(attn_qnorm[0], reps)), row_vec(jnp.tile(attn_knorm[0], reps)),
                                      req_len)
    o_ctx, o_lat = _attention(q, k, v, attn_lam[0], row_vec(attn_subln[0]),
                              cache_k[:, 0].reshape(n_req, past, d), cache_v[:, 0].reshape(n_req, past, d),
                              n_ctx, seq_len, n_req, req_len, layer=0)
    h1, a2, route0, cnt0 = _post_attn(h_ctx, h_lat, o_ctx, o_lat, mod[0], attn_wo[0].astype(BF16),
                                      row_vec(norm2_g[0]), routers[0], req_len)
    y2 = _moe(a2, route0, cnt0, moe_w1, moe_w3, moe_w2, layer=0)

    gw = gm_vnorm.shape[1]
    bsb = jnp.repeat(gm_bs[0].T, gw // GM_GROUPS, axis=1)
    h2, a2, route1, cnt1 = _gmlp_layer(h1, y2, route0, mod[0], mod[1], row_vec(norm1_g[1]), gm_win[0].astype(BF16),
                                       row_vec(gm_bin[0]), row_vec(gm_vnorm[0]), gm_ws[0].astype(BF16), bsb,
                                       gm_wout[0].astype(BF16), row_vec(norm2_g[1]), routers[1], n_ctx, req_len)
    y2 = _moe(a2, route1, cnt1, moe_w1, moe_w3, moe_w2, layer=1)
    y_ctx, y_lat = _final_combine(h2, y2, route1, mod[1], n_ctx, req_len)

    y_prompt = y_ctx.reshape(batch, seq_len, d)
    y_sample = y_lat.reshape(n_req, req_len, d)
    new_cache_k = k_new.reshape(batch, 1, seq_len, N_HEADS, 2, HEAD_DIM)
    new_cache_v = v_new.reshape(batch, 1, seq_len, N_HEADS, V_DIM)
    return (y_prompt, y_sample, new_cache_k, new_cache_v)
```

```python
import functools
import math

import jax
import jax.numpy as jnp
from jax import lax
from jax.experimental import pallas as pl
from jax.experimental.pallas import tpu as pltpu
from jax.experimental.pallas import tpu_sc as plsc

F32 = jnp.float32
BF16 = jnp.bfloat16

N_HEADS = 8
HEAD_DIM = 64
V_DIM = 2 * HEAD_DIM
GRID_W = 64
ROPE_THETA = 10000.0
CHUNK = 128
GM_GROUPS = 8
N_EXPERT_GROUPS = 4
EXPERTS_PER_GROUP = 8
N_EXPERTS = N_EXPERT_GROUPS * EXPERTS_PER_GROUP
EPS = 1e-6

LANES = 128
ROW_TILE = 256
SUBTILES = 2
MOE_BLOCK = 256
SC_WINDOW_BYTES = 128 * 1024
ATTN_Q_TILE = 256
VMEM_LIMIT = 56 * 1024 * 1024

R_E1, R_E2, R_W1, R_W2, R_RANK1, R_RANK2 = 0, 1, 2, 3, 4, 5
L_EXPERT0 = N_EXPERT_GROUPS


def _lambda_init(layer):
    return 0.8 - 0.6 * math.exp(-0.3 * layer)


def _cparams(n_axes):
    return pltpu.CompilerParams(dimension_semantics=("arbitrary",) * n_axes, vmem_limit_bytes=VMEM_LIMIT)


def _seq_of_tile(i, n_ctx_tiles, tiles_per_req):
    return jnp.where(i < n_ctx_tiles, 0, 1 + (i - n_ctx_tiles) // tiles_per_req)


def _pick_rows(i, n_ctx_tiles, ctx_ref, lat_ref):
    return jnp.where(i < n_ctx_tiles, ctx_ref[...], lat_ref[...])


def _split_specs(tm, d, n_ctx_tiles):
    return [pl.BlockSpec((tm, d), lambda i: (jnp.minimum(i, n_ctx_tiles - 1), 0)),
            pl.BlockSpec((tm, d), lambda i: (jnp.maximum(i - n_ctx_tiles, 0), 0))]


def _mod_row(mod_ref, part, seq):
    return mod_ref[part, pl.ds(seq, 1), :]


def _modulated(x, g, shift, scale):
    y = x * lax.rsqrt(jnp.mean(x * x, axis=-1, keepdims=True) + EPS)
    return (y * g) * (1.0 + scale) + shift


def _split_bf16(x):
    hi = x.astype(BF16)
    lo = (x - hi.astype(F32)).astype(BF16)
    return hi, lo


def _dot(a, b):
    return jnp.dot(a, b, preferred_element_type=F32)


def _pack_bf16_pairs(x):
    half = x.shape[1] // 2
    bits = lax.bitcast_convert_type(x.astype(BF16).astype(F32), jnp.uint32)
    packed = (bits[:, :half] >> 16) | bits[:, half:]
    return lax.bitcast_convert_type(packed, jnp.int32)


def _unpack_bf16_pairs(p):
    u = lax.bitcast_convert_type(p, jnp.uint32)
    lo = lax.bitcast_convert_type(u << 16, F32).astype(BF16)
    hi = lax.bitcast_convert_type(u & jnp.uint32(0xFFFF0000), F32).astype(BF16)
    return lo, hi


def _ada_kernel(c_ref, w_ref, b_ref, o_ref):
    c = c_ref[...]
    s = c * jax.nn.sigmoid(c)
    o_ref[...] = _dot(s.astype(BF16), w_ref[...].astype(BF16)) + b_ref[...]


def _ada_mod(cvec, ada_w, ada_b):
    depth, d, d6 = ada_w.shape
    parts = d6 // d
    rows = cvec.shape[0]
    return pl.pallas_call(
        _ada_kernel,
        grid=(depth, parts),
        in_specs=[
            pl.BlockSpec((rows, d), lambda l, j: (0, 0)),
            pl.BlockSpec((None, d, d), lambda l, j: (l, 0, j)),
            pl.BlockSpec((None, 1, d), lambda l, j: (l, 0, j)),
        ],
        out_specs=pl.BlockSpec((None, None, rows, d), lambda l, j: (l, j, 0, 0)),
        out_shape=jax.ShapeDtypeStruct((depth, parts, rows, d), F32),
        compiler_params=_cparams(2),
        name="ada_mod",
    )(cvec, ada_w, ada_b.reshape(depth, 1, d6))


def _pre_attn_kernel(hc_ref, hl_ref, mod_ref, g_ref, w_ref, qg_ref, kg_ref, gs_ref, gb_ref, cos_ref, sin_ref,
                     q_ref, k_ref, v_ref, kf_ref, vf_ref, *, n_ctx_tiles, tiles_per_req):
    i = pl.program_id(0)
    seq = _seq_of_tile(i, n_ctx_tiles, tiles_per_req)
    d = hc_ref.shape[1]
    x = _pick_rows(i, n_ctx_tiles, hc_ref, hl_ref)
    a = _modulated(x, g_ref[...], _mod_row(mod_ref, 0, seq), _mod_row(mod_ref, 1, seq))
    qkv = _dot(a.astype(BF16), w_ref[...])
    q = qkv[:, :d]
    k = qkv[:, d:2 * d]
    v = qkv[:, 2 * d:]
    sq_hi, sq_lo = _split_bf16(jnp.concatenate([q * q, k * k], axis=1))
    gsum = _dot(sq_hi, gs_ref[...]) + _dot(sq_lo, gs_ref[...])
    r_hi, r_lo = _split_bf16(lax.rsqrt(gsum * (1.0 / HEAD_DIM) + EPS))
    rb = _dot(r_hi, gb_ref[...]) + _dot(r_lo, gb_ref[...])
    qn = (q * rb[:, :d]) * qg_ref[...]
    kn = (k * rb[:, d:]) * kg_ref[...]
    v_ref[...] = v.astype(BF16)
    q_scale = HEAD_DIM ** -0.5

    @pl.when(i < n_ctx_tiles)
    def _():
        q_ref[...] = (qn * q_scale).astype(BF16)
        k_ref[...] = kn.astype(BF16)
        kf_ref[...] = kn.T.reshape(kf_ref.shape)
        vf_ref[...] = v

    @pl.when(i >= n_ctx_tiles)
    def _():
        cos = cos_ref[...]
        sin = sin_ref[...]
        lane = lax.broadcasted_iota(jnp.int32, cos.shape, 1)
        first = (lane & 31) < 16

        def rope(x):
            outs = []
            for hh in range(d // LANES):
                xs = x[:, hh * LANES:(hh + 1) * LANES]
                rot = jnp.where(first, pltpu.roll(xs, LANES - 16, 1), pltpu.roll(xs, 16, 1))
                outs.append(xs * cos + rot * sin)
            return jnp.concatenate(outs, axis=1)

        q_ref[...] = (rope(qn) * q_scale).astype(BF16)
        k_ref[...] = rope(kn).astype(BF16)


def _rope_tables(n_pos):
    half = HEAD_DIM // 2
    pos = jnp.arange(n_pos, dtype=jnp.int32)
    row = (pos // GRID_W).astype(F32)
    col = (pos % GRID_W).astype(F32)
    inv = 1.0 / (ROPE_THETA ** (jnp.arange(0, half, 2, dtype=F32) / half))
    ang_r = row[:, None] * inv[None, :]
    ang_c = col[:, None] * inv[None, :]
    ang = jnp.concatenate([ang_r, ang_r, ang_c, ang_c], axis=-1)
    quarter = half // 2
    sign = jnp.tile(jnp.concatenate([-jnp.ones((quarter,), F32), jnp.ones((quarter,), F32)]), 2)
    cos = jnp.tile(jnp.cos(ang), (1, 2))
    sin = jnp.tile(jnp.sin(ang) * sign[None, :], (1, 2))
    return cos, sin


def _pre_attn(h_ctx, h_lat, mod, g, wqkv, qg, kg, rows_per_req):
    n_ctx, d = h_ctx.shape
    t = n_ctx + h_lat.shape[0]
    tm = ROW_TILE
    n_ctx_tiles = n_ctx // tm
    tiles_per_req = rows_per_req // tm
    groups = d // HEAD_DIM
    gid = jnp.arange(2 * d, dtype=jnp.int32) // HEAD_DIM
    gs = (gid[:, None] == jnp.arange(LANES, dtype=jnp.int32)[None, :]).astype(BF16)
    gb = gs.T
    assert 2 * groups <= LANES
    cos, sin = _rope_tables(rows_per_req)
    pos_map = lambda i: (jnp.where(i < n_ctx_tiles, 0, (i - n_ctx_tiles) % tiles_per_req), 0)
    ctx_map = lambda i: (jnp.minimum(i, n_ctx_tiles - 1), 0)
    full = lambda shape: pl.BlockSpec(shape, lambda i: (0,) * len(shape))
    row = pl.BlockSpec((tm, d), lambda i: (i, 0))
    kern = functools.partial(_pre_attn_kernel, n_ctx_tiles=n_ctx_tiles, tiles_per_req=tiles_per_req)
    return pl.pallas_call(
        kern,
        grid=(t // tm,),
        in_specs=_split_specs(tm, d, n_ctx_tiles) + [
                  full(mod.shape), full((1, d)), full(wqkv.shape), full((1, d)), full((1, d)),
                  full(gs.shape), full(gb.shape),
                  pl.BlockSpec((tm, LANES), pos_map), pl.BlockSpec((tm, LANES), pos_map)],
        out_specs=[row, row, row,
                   pl.BlockSpec((None, d // HEAD_DIM, HEAD_DIM, tm), lambda i: (ctx_map(i)[0], 0, 0, 0)),
                   pl.BlockSpec((tm, d), ctx_map)],
        out_shape=[jax.ShapeDtypeStruct((t, d), BF16)] * 3 + [
            jax.ShapeDtypeStruct((n_ctx_tiles, d // HEAD_DIM, HEAD_DIM, tm), F32), jax.ShapeDtypeStruct((n_ctx, d), F32)],
        compiler_params=_cparams(1),
        name="pre_attn",
    )(h_ctx, h_lat, mod, g, wqkv, qg, kg, gs, gb, cos, sin)


def _attn_kernel(*refs, n_heads, has_cache, layer):
    if has_cache:
        lam_ref, sub_ref, q_ref, kn_ref, vn_ref, kc_ref, vc_ref, o_ref = refs
    else:
        lam_ref, sub_ref, q_ref, kn_ref, vn_ref, o_ref = refs
    lf = lam_ref[...]
    lam = (jnp.exp(jnp.sum(lf[0:1] * lf[1:2], axis=-1, keepdims=True))
           - jnp.exp(jnp.sum(lf[2:3] * lf[3:4], axis=-1, keepdims=True)) + _lambda_init(layer))
    nt = (((1,), (1,)), ((), ()))
    tq = q_ref.shape[0]
    lane = lax.broadcasted_iota(jnp.int32, (tq, LANES), 1)
    for hh in range(n_heads):
        sl = slice(hh * LANES, (hh + 1) * LANES)
        qh = q_ref[:, sl]
        zero = jnp.zeros_like(qh)
        qs = (jnp.where(lane < HEAD_DIM, qh, zero), jnp.where(lane >= HEAD_DIM, qh, zero))
        keys = [kn_ref[:, sl]]
        vals = [vn_ref[:, sl]]
        if has_cache:
            keys.append(kc_ref[:, sl].astype(BF16))
            vals.append(vc_ref[:, sl].astype(BF16))
        probs = []
        for qm in qs:
            s = [lax.dot_general(qm, kk, nt, preferred_element_type=F32) for kk in keys]
            m = functools.reduce(jnp.maximum, [jnp.max(x, axis=-1, keepdims=True) for x in s])
            e = [jnp.exp(x - m) for x in s]
            den = functools.reduce(lambda u, w: u + w, [jnp.sum(x, axis=-1, keepdims=True) for x in e])
            probs.append((e, 1.0 / den))
        (e0, r0), (e1, r1) = probs
        lr1 = lam * r1
        o = None
        for j, vv in enumerate(vals):
            aj = e0[j] * r0 - e1[j] * lr1
            oj = _dot(aj.astype(BF16), vv)
            o = oj if o is None else o + oj
        o = o * lax.rsqrt(jnp.mean(o * o, axis=-1, keepdims=True) + EPS)
        o = (o * sub_ref[...]) * (1.0 - _lambda_init(layer))
        o_ref[:, sl] = o.astype(BF16)


def _attention(q, k, v, lam_p, subln, cache_k, cache_v, n_ctx, ctx_len, n_req, req_len, layer):
    t, d = q.shape
    small = lambda shape: pl.BlockSpec(shape, lambda *_: (0,) * len(shape))
    n_ctx_req = n_ctx // ctx_len
    blk = pl.BlockSpec((ctx_len, d), lambda b: (b, 0))
    o_ctx = pl.pallas_call(
        functools.partial(_attn_kernel, n_heads=N_HEADS, has_cache=False, layer=layer),
        grid=(n_ctx_req,),
        in_specs=[small(lam_p.shape), small(subln.shape), blk, blk, blk],
        out_specs=blk,
        out_shape=jax.ShapeDtypeStruct((n_ctx, d), BF16),
        compiler_params=_cparams(1),
        name="attn_ctx",
    )(lam_p, subln, q, k, v)

    tq = ATTN_Q_TILE
    nq = req_len // tq
    q0 = n_ctx // tq
    r0 = n_ctx // req_len
    past = cache_k.shape[1]
    small3 = lambda shape: pl.BlockSpec(shape, lambda b, h, i: (0,) * len(shape))
    qspec = pl.BlockSpec((tq, LANES), lambda b, h, i: (q0 + b * nq + i, h))
    kvspec = pl.BlockSpec((req_len, LANES), lambda b, h, i: (r0 + b, h))
    cspec = pl.BlockSpec((None, past, LANES), lambda b, h, i: (b, 0, h))
    o_lat = pl.pallas_call(
        functools.partial(_attn_kernel, n_heads=1, has_cache=True, layer=layer),
        grid=(n_req, N_HEADS, nq),
        in_specs=[small3(lam_p.shape), pl.BlockSpec((1, LANES), lambda b, h, i: (0, 0)),
                  qspec, kvspec, kvspec, cspec, cspec],
        out_specs=pl.BlockSpec((tq, LANES), lambda b, h, i: (b * nq + i, h)),
        out_shape=jax.ShapeDtypeStruct((n_req * req_len, d), BF16),
        compiler_params=_cparams(3),
        name="attn_latent",
    )(lam_p, subln, q, k, v, cache_k, cache_v)
    return o_ctx, o_lat


def _route(a2, wr_hi, wr_lo, br, ltri, carry_ref):
    tm = a2.shape[0]
    a_hi, a_lo = _split_bf16(a2)
    logits = _dot(a_hi, wr_hi) + (_dot(a_lo, wr_hi) + _dot(a_hi, wr_lo)) + br
    lane_i = lax.broadcasted_iota(jnp.int32, (tm, LANES), 1)
    lane = lane_i.astype(F32)
    neg = jnp.full((tm, LANES), -jnp.inf, F32)
    big = jnp.full((tm, LANES), float(LANES), F32)
    first_lane = lambda mask: jnp.min(jnp.where(mask, lane, big), axis=-1, keepdims=True)

    lc = jnp.where(lane_i < N_EXPERT_GROUPS, logits, neg)
    mc = jnp.max(lc, axis=-1, keepdims=True)
    pg = 1.0 / jnp.sum(jnp.exp(lc - mc), axis=-1, keepdims=True)
    gi = first_lane(lc == mc)
    assert EXPERTS_PER_GROUP == 8
    grp = lax.shift_right_arithmetic(lane_i - L_EXPERT0, 3).astype(F32)
    in_group = (lane_i >= L_EXPERT0) & (lane_i < L_EXPERT0 + N_EXPERTS) & (grp == gi)
    ls = jnp.where(in_group, logits, neg)
    t1 = jnp.max(ls, axis=-1, keepdims=True)
    i1 = first_lane(ls == t1)
    ls2 = jnp.where(lane == i1, neg, ls)
    t2 = jnp.max(ls2, axis=-1, keepdims=True)
    i2 = first_lane(ls2 == t2)
    ex = jnp.exp(t2 - t1)
    w1 = pg * (1.0 / (1.0 + ex))
    w2 = pg * (ex / (1.0 + ex))
    e1 = i1 - float(L_EXPERT0)
    e2 = i2 - float(L_EXPERT0)
    oh1 = lane == e1
    oh2 = lane == e2
    onehot = oh1.astype(F32) + oh2.astype(F32)
    before = _dot(ltri, onehot.astype(BF16)) + carry_ref[...]
    zero = jnp.zeros_like(before)
    rank1 = jnp.sum(jnp.where(oh1, before, zero), axis=-1, keepdims=True)
    rank2 = jnp.sum(jnp.where(oh2, before, zero), axis=-1, keepdims=True)
    carry_ref[...] = carry_ref[...] + jnp.sum(onehot, axis=0, keepdims=True)
    slab = jnp.zeros((tm, LANES), F32)
    for ln, val in ((R_E1, e1), (R_E2, e2), (R_W1, w1), (R_W2, w2), (R_RANK1, rank1), (R_RANK2, rank2)):
        slab = jnp.where(lane_i == ln, val, slab)
    return slab


def _router_weights(wc, bc, wf, bf_):
    d = wc.shape[0]
    pad = LANES - N_EXPERT_GROUPS - N_EXPERTS
    w = jnp.concatenate([wc, wf, jnp.zeros((d, pad), F32)], axis=1)
    b = jnp.concatenate([bc, bf_, jnp.zeros((pad,), F32)])[None, :]
    hi = w.astype(BF16)
    lo = (w - hi.astype(F32)).astype(BF16)
    return hi, lo, b


def _lower_tri(n):
    r = jnp.arange(n, dtype=jnp.int32)
    return (r[None, :] < r[:, None]).astype(BF16)


def _post_attn_kernel(hc_ref, hl_ref, oc_ref, ol_ref, mod_ref, wo_ref, g2_ref, wrh_ref, wrl_ref, br_ref, ltri_ref,
                      h1_ref, a2_ref, route_ref, cnt_ref, carry_ref, *, n_ctx_tiles, tiles_per_req):
    i = pl.program_id(0)
    seq = _seq_of_tile(i, n_ctx_tiles, tiles_per_req)

    @pl.when(i == 0)
    def _():
        carry_ref[...] = jnp.zeros_like(carry_ref)

    for s in range(h1_ref.shape[0] // ROW_TILE):
        rows = pl.ds(s * ROW_TILE, ROW_TILE)
        o = jnp.where(i < n_ctx_tiles, oc_ref[rows, :], ol_ref[rows, :])
        h = jnp.where(i < n_ctx_tiles, hc_ref[rows, :], hl_ref[rows, :])
        h1 = h + _mod_row(mod_ref, 2, seq) * _dot(o, wo_ref[...])
        h1_ref[rows, :] = h1
        a2 = _modulated(h1, g2_ref[...], _mod_row(mod_ref, 3, seq), _mod_row(mod_ref, 4, seq))
        a2_ref[rows, :] = _pack_bf16_pairs(a2)
        route_ref[rows, :] = _route(a2, wrh_ref[...], wrl_ref[...], br_ref[...], ltri_ref[...], carry_ref)
    cnt_ref[...] = carry_ref[...]


def _post_attn(h_ctx, h_lat, o_ctx, o_lat, mod, wo, g2, router, rows_per_req):
    n_ctx, d = h_ctx.shape
    t = n_ctx + h_lat.shape[0]
    tm = ROW_TILE * SUBTILES
    wrh, wrl, br = router
    ltri = _lower_tri(ROW_TILE)
    full = lambda shape: pl.BlockSpec(shape, lambda i: (0,) * len(shape))
    row = pl.BlockSpec((tm, d), lambda i: (i, 0))
    kern = functools.partial(_post_attn_kernel, n_ctx_tiles=n_ctx // tm, tiles_per_req=rows_per_req // tm)
    return pl.pallas_call(
        kern,
        grid=(t // tm,),
        in_specs=_split_specs(tm, d, n_ctx // tm) + _split_specs(tm, d, n_ctx // tm) + [
                  full(mod.shape), full(wo.shape), full((1, d)), full(wrh.shape), full(wrl.shape),
                  full(br.shape), full(ltri.shape)],
        out_specs=[row, pl.BlockSpec((tm, d // 2), lambda i: (i, 0)), pl.BlockSpec((tm, LANES), lambda i: (i, 0)),
                   full((1, LANES))],
        out_shape=[jax.ShapeDtypeStruct((t, d), F32), jax.ShapeDtypeStruct((t, d // 2), jnp.int32),
                   jax.ShapeDtypeStruct((t, LANES), F32), jax.ShapeDtypeStruct((1, LANES), F32)],
        scratch_shapes=[pltpu.VMEM((1, LANES), F32)],
        compiler_params=_cparams(1),
        name="post_attn_router",
    )(h_ctx, h_lat, o_ctx, o_lat, mod, wo, g2, wrh, wrl, br, ltri)


def _gather_rows(src, idx):
    n = idx.shape[0]
    d = src.shape[1]
    w = _sc_window_rows(src)
    mesh = plsc.VectorSubcoreMesh(core_axis_name="core", subcore_axis_name="subcore")

    @pl.kernel(out_type=jax.ShapeDtypeStruct((n, d), src.dtype), mesh=mesh)
    def gather(src_hbm, idx_hbm, out_hbm):
        def body(idx_vmem, out_vmem):
            pltpu.sync_copy(src_hbm.at[idx_vmem.at[0, pl.ds(0, w)]], out_vmem)

        pltpu.emit_pipeline(
            body,
            grid=(n // w,),
            in_specs=[pl.BlockSpec((1, LANES), lambda i: (i, 0))],
            out_specs=[pl.BlockSpec((w, d), lambda i: (i, 0))],
            core_axis_name=("core", "subcore"),
            dimension_semantics=(pltpu.PARALLEL,),
        )(idx_hbm, out_hbm)

    return gather(src, _index_windows(idx, w))


def _sc_window_rows(src):
    return min(LANES, SC_WINDOW_BYTES // (src.shape[1] * src.dtype.itemsize))


def _index_windows(idx, w):
    return jnp.pad(idx.reshape(idx.shape[0] // w, w), ((0, 0), (0, LANES - w)))


def _scatter_rows_twice(src, idx_a, idx_b, n_out):
    t, d = src.shape
    w = _sc_window_rows(src)
    mesh = plsc.VectorSubcoreMesh(core_axis_name="core", subcore_axis_name="subcore")

    @pl.kernel(out_type=jax.ShapeDtypeStruct((n_out, d), src.dtype), mesh=mesh)
    def scatter(src_hbm, ia_hbm, ib_hbm, out_hbm):
        def body(src_vmem, ia_vmem, ib_vmem):
            pltpu.sync_copy(src_vmem, out_hbm.at[ia_vmem.at[0, pl.ds(0, w)]])
            pltpu.sync_copy(src_vmem, out_hbm.at[ib_vmem.at[0, pl.ds(0, w)]])

        pltpu.emit_pipeline(
            body,
            grid=(t // w,),
            in_specs=[pl.BlockSpec((w, d), lambda i: (i, 0)),
                      pl.BlockSpec((1, LANES), lambda i: (i, 0)),
                      pl.BlockSpec((1, LANES), lambda i: (i, 0))],
            out_specs=[],
            core_axis_name=("core", "subcore"),
            dimension_semantics=(pltpu.PARALLEL,),
        )(src_hbm, ia_hbm, ib_hbm)

    return scatter(src, _index_windows(idx_a, w), _index_windows(idx_b, w))


def _expert_kernel(be_ref, nb_ref, x_ref, w1_ref, w3_ref, w2_ref, y_ref, w1b, w3b, w2b):
    b = pl.program_id(0)
    prev = be_ref[jnp.maximum(b - 1, 0)]
    used = b < nb_ref[0]

    @pl.when(used & ((b == 0) | (be_ref[b] != prev)))
    def _():
        w1b[...] = w1_ref[...].astype(BF16)
        w3b[...] = w3_ref[...].astype(BF16)
        w2b[...] = w2_ref[...].astype(BF16)

    @pl.when(used)
    def _():
        x_lo, x_hi = _unpack_bf16_pairs(x_ref[...])
        half = x_lo.shape[1]
        h1 = _dot(x_lo, w1b[:half, :]) + _dot(x_hi, w1b[half:, :])
        h3 = _dot(x_lo, w3b[:half, :]) + _dot(x_hi, w3b[half:, :])
        hb = jax.nn.silu(h1) * h3
        y_ref[...] = _dot(hb.astype(BF16), w2b[...])


def _expert_mlps(xs, block_e, n_used, w1, w3, w2, layer):
    npad, half = xs.shape
    d, ff = w1.shape[2], w1.shape[3]
    assert d == 2 * half
    bm = MOE_BLOCK
    rows = lambda b, be, nb: (jnp.minimum(b, nb[0] - 1), 0)
    grid_spec = pltpu.PrefetchScalarGridSpec(
        num_scalar_prefetch=2,
        grid=(npad // bm,),
        in_specs=[pl.BlockSpec((bm, half), rows),
                  pl.BlockSpec((None, None, d, ff), lambda b, be, nb: (layer, be[b], 0, 0)),
                  pl.BlockSpec((None, None, d, ff), lambda b, be, nb: (layer, be[b], 0, 0)),
                  pl.BlockSpec((None, None, ff, d), lambda b, be, nb: (layer, be[b], 0, 0))],
        out_specs=pl.BlockSpec((bm, d), rows),
        scratch_shapes=[pltpu.VMEM((d, ff), BF16), pltpu.VMEM((d, ff), BF16), pltpu.VMEM((ff, d), BF16)],
    )
    return pl.pallas_call(
        _expert_kernel,
        grid_spec=grid_spec,
        out_shape=jax.ShapeDtypeStruct((npad, d), F32),
        compiler_params=_cparams(1),
        name="expert_mlps",
    )(block_e, n_used, xs, w1, w3, w2)


def _moe(a2p, route, counts, w1, w3, w2, layer):
    t = a2p.shape[0]
    bm = MOE_BLOCK
    n = 2 * t
    e = route[:, R_E1:R_E2 + 1].astype(jnp.int32).reshape(n)
    rank = route[:, R_RANK1:R_RANK2 + 1].astype(jnp.int32).reshape(n)
    cnt = counts[0, :N_EXPERTS].astype(jnp.int32)
    padded = ((cnt + bm - 1) // bm) * bm
    pend = jnp.cumsum(padded)
    pstart = pend - padded
    pos = jnp.sum(jnp.where(e[:, None] == jnp.arange(N_EXPERTS, dtype=jnp.int32)[None, :], pstart[None, :], 0),
                  axis=1) + rank
    n_blocks = n // bm + N_EXPERTS
    npad = n_blocks * bm
    starts = jnp.arange(n_blocks, dtype=jnp.int32) * bm
    n_used = pend[-1:] // bm
    last_start = (n_used[0] - 1) * bm
    block_e = jnp.sum((pend[None, :] <= jnp.minimum(starts, last_start)[:, None]).astype(jnp.int32), axis=1)
    pos_first, pos_second = pos[0::2], pos[1::2]
    xs = _scatter_rows_twice(a2p, pos_first, pos_second, npad)
    ys = _expert_mlps(xs, block_e, n_used, w1, w3, w2, layer)
    return _gather_rows(ys, jnp.concatenate([pos_first, pos_second]))


def _moe_combine(h, y_first, y_second, route, g2):
    w1 = route[:, R_W1:R_W1 + 1]
    w2 = route[:, R_W2:R_W2 + 1]
    return h + g2 * (w1 * y_first + w2 * y_second)


def _gmlp_kernel(h_ref, ya_ref, yb_ref, rt_ref, modp_ref, mod_ref, g1_ref, win_ref, bin_ref, vg_ref, ws_ref, bsb_ref,
                 wout_ref, g2_ref, wrh_ref, wrl_ref, br_ref, ltri_ref,
                 h1_ref, a2_ref, route_ref, cnt_ref, carry_ref, *, n_ctx_tiles, tiles_per_req):
    i = pl.program_id(0)
    seq = _seq_of_tile(i, n_ctx_tiles, tiles_per_req)

    @pl.when(i == 0)
    def _():
        carry_ref[...] = jnp.zeros_like(carry_ref)

    h = _moe_combine(h_ref[...], ya_ref[...], yb_ref[...], rt_ref[...], _mod_row(modp_ref, 5, seq))
    a = _modulated(h, g1_ref[...], _mod_row(mod_ref, 0, seq), _mod_row(mod_ref, 1, seq))
    gw = vg_ref.shape[1]
    z = jax.nn.gelu(_dot(a.astype(BF16), win_ref[...]) + bin_ref[...])
    u = z[:, :gw]
    v = z[:, gw:]
    v = (v * lax.rsqrt(jnp.mean(v * v, axis=-1, keepdims=True) + EPS)) * vg_ref[...]
    vb = v.astype(BF16)
    cg = gw // GM_GROUPS
    tm = h.shape[0]
    rows = []
    for c in range(tm // CHUNK):
        cols = []
        for g in range(GM_GROUPS):
            cols.append(_dot(ws_ref[g], vb[c * CHUNK:(c + 1) * CHUNK, g * cg:(g + 1) * cg]))
        rows.append(jnp.concatenate(cols, axis=1) + bsb_ref[...])
    vm = jnp.concatenate(rows, axis=0)
    mix = _dot((u * vm).astype(BF16), wout_ref[...])
    h1 = h + _mod_row(mod_ref, 2, seq) * mix
    h1_ref[...] = h1
    a2 = _modulated(h1, g2_ref[...], _mod_row(mod_ref, 3, seq), _mod_row(mod_ref, 4, seq))
    a2_ref[...] = _pack_bf16_pairs(a2)
    route_ref[...] = _route(a2, wrh_ref[...], wrl_ref[...], br_ref[...], ltri_ref[...], carry_ref)
    cnt_ref[...] = carry_ref[...]


def _gmlp_layer(h, y2, route_prev, mod_prev, mod, g1, win, bin_, vg, ws, bsb, wout, g2, router,
                n_ctx, rows_per_req):
    t, d = h.shape
    tm = ROW_TILE
    wrh, wrl, br = router
    ltri = _lower_tri(tm)
    full = lambda shape: pl.BlockSpec(shape, lambda i: (0,) * len(shape))
    row = lambda w: pl.BlockSpec((tm, w), lambda i: (i, 0))
    kern = functools.partial(_gmlp_kernel, n_ctx_tiles=n_ctx // tm, tiles_per_req=rows_per_req // tm)
    args = (h, y2, y2, route_prev, mod_prev, mod, g1, win, bin_, vg, ws, bsb, wout, g2, wrh, wrl, br, ltri)
    second = pl.BlockSpec((tm, d), lambda i: (i + t // tm, 0))
    in_specs = [row(d), row(d), second, row(LANES)] + [full(a.shape) for a in args[4:]]
    return pl.pallas_call(
        kern,
        grid=(t // tm,),
        in_specs=in_specs,
        out_specs=[row(d), row(d // 2), row(LANES), full((1, LANES))],
        out_shape=[jax.ShapeDtypeStruct((t, d), F32), jax.ShapeDtypeStruct((t, d // 2), jnp.int32),
                   jax.ShapeDtypeStruct((t, LANES), F32), jax.ShapeDtypeStruct((1, LANES), F32)],
        scratch_shapes=[pltpu.VMEM((1, LANES), F32)],
        compiler_params=_cparams(1),
        name="gmlp_router",
    )(*args)


def _final_kernel(h_ref, ya_ref, yb_ref, rt_ref, mod_ref, oc_ref, ol_ref, *, n_ctx_tiles, tiles_per_req):
    i = pl.program_id(0)
    seq = _seq_of_tile(i, n_ctx_tiles, tiles_per_req)
    out = _moe_combine(h_ref[...], ya_ref[...], yb_ref[...], rt_ref[...], _mod_row(mod_ref, 5, seq))

    @pl.when(i < n_ctx_tiles)
    def _():
        oc_ref[...] = out

    @pl.when(i >= n_ctx_tiles)
    def _():
        ol_ref[...] = out


def _final_combine(h, y2, route, mod, n_ctx, rows_per_req):
    t, d = h.shape
    tm = ROW_TILE
    row = lambda w: pl.BlockSpec((tm, w), lambda i: (i, 0))
    kern = functools.partial(_final_kernel, n_ctx_tiles=n_ctx // tm, tiles_per_req=rows_per_req // tm)
    return pl.pallas_call(
        kern,
        grid=(t // tm,),
        in_specs=[row(d), row(d), pl.BlockSpec((tm, d), lambda i: (i + t // tm, 0)), row(LANES),
                  pl.BlockSpec(mod.shape, lambda i: (0, 0, 0))],
        out_specs=_split_specs(tm, d, n_ctx // tm),
        out_shape=[jax.ShapeDtypeStruct((n_ctx, d), F32), jax.ShapeDtypeStruct((t - n_ctx, d), F32)],
        compiler_params=_cparams(1),
        name="final_combine",
    )(h, y2, y2, route, mod)


def kernel(x_prompt, x_sample, cache_k, cache_v, c, c_ctx, ada_w, ada_b, norm1_g, norm2_g, attn_wq, attn_wk,
           attn_wv, attn_wo, attn_qnorm, attn_knorm, attn_lam, attn_subln, gm_win, gm_bin, gm_vnorm, gm_ws,
           gm_bs, gm_wout, moe_wc, moe_bc, moe_wf, moe_bf, moe_w1, moe_w3, moe_w2):
    batch, seq_len, d = x_prompt.shape
    n_req, req_len, _ = x_sample.shape
    depth = ada_w.shape[0]
    assert depth == 2 and attn_wq.shape[0] == 1 and gm_win.shape[0] == 1
    n_ctx = batch * seq_len
    assert n_ctx % req_len == 0 and req_len % ROW_TILE == 0 and seq_len == ROW_TILE
    past = cache_k.shape[2]

    h_ctx = x_prompt.reshape(n_ctx, d)
    h_lat = x_sample.reshape(n_req * req_len, d)
    cvec = jnp.concatenate([c_ctx[None, :], c, jnp.zeros((8 - 1 - n_req, d), F32)], axis=0)
    mod = _ada_mod(cvec, ada_w, ada_b)
    routers = [_router_weights(moe_wc[i], moe_bc[i], moe_wf[i], moe_bf[i]) for i in range(depth)]
    row_vec = lambda v: v.reshape(1, -1)

    wqkv = jnp.concatenate([attn_wq[0], attn_wk[0], attn_wv[0]], axis=1).astype(BF16)
    reps = d // HEAD_DIM
    q, k, v, k_new, v_new = _pre_attn(h_ctx, h_lat, mod[0], row_vec(norm1_g[0]), wqkv,
                                      row_vec(jnp.tile(attn_qnorm[0], reps)), row_vec(jnp.tile(attn_knorm[0], reps)),
                                      req_len)
    o_ctx, o_lat = _attention(q, k, v, attn_lam[0], row_vec(attn_subln[0]),
                              cache_k[:, 0].reshape(n_req, past, d), cache_v[:, 0].reshape(n_req, past, d),
                              n_ctx, seq_len, n_req, req_len, layer=0)
    h1, a2, route0, cnt0 = _post_attn(h_ctx, h_lat, o_ctx, o_lat, mod[0], attn_wo[0].astype(BF16),
                                      row_vec(norm2_g[0]), routers[0], req_len)
    y2 = _moe(a2, route0, cnt0, moe_w1, moe_w3, moe_w2, layer=0)

    gw = gm_vnorm.shape[1]
    bsb = jnp.repeat(gm_bs[0].T, gw // GM_GROUPS, axis=1)
    h2, a2, route1, cnt1 = _gmlp_layer(h1, y2, route0, mod[0], mod[1], row_vec(norm1_g[1]), gm_win[0].astype(BF16),
                                       row_vec(gm_bin[0]), row_vec(gm_vnorm[0]), gm_ws[0].astype(BF16), bsb,
                                       gm_wout[0].astype(BF16), row_vec(norm2_g[1]), routers[1], n_ctx, req_len)
    y2 = _moe(a2, route1, cnt1, moe_w1, moe_w3, moe_w2, layer=1)
    y_ctx, y_lat = _final_combine(h2, y2, route1, mod[1], n_ctx, req_len)

    y_prompt = y_ctx.reshape(batch, seq_len, d)
    y_sample = y_lat.reshape(n_req, req_len, d)
    new_cache_k = jnp.transpose(k_new.reshape(batch, 1, N_HEADS, 2, HEAD_DIM, seq_len), (0, 1, 5, 2, 3, 4))
    new_cache_v = v_new.reshape(batch, 1, seq_len, N_HEADS, V_DIM)
    return (y_prompt, y_sample, new_cache_k, new_cache_v)
```

```python
import functools
import math

import jax
import jax.numpy as jnp
from jax import lax
from jax.experimental import pallas as pl
from jax.experimental.pallas import tpu as pltpu
from jax.experimental.pallas import tpu_sc as plsc

F32 = jnp.float32
BF16 = jnp.bfloat16

N_HEADS = 8
HEAD_DIM = 64
V_DIM = 2 * HEAD_DIM
GRID_W = 64
ROPE_THETA = 10000.0
CHUNK = 128
GM_GROUPS = 8
N_EXPERT_GROUPS = 4
EXPERTS_PER_GROUP = 8
N_EXPERTS = N_EXPERT_GROUPS * EXPERTS_PER_GROUP
EPS = 1e-6

LANES = 128
ROW_TILE = 256
SUBTILES = 2
GMLP_SUB = 512
MOE_BLOCK = 256
SC_WINDOW_BYTES = 128 * 1024
ATTN_Q_TILE = 256
ATTN_HEADS_PER_STEP = 4
VMEM_LIMIT = 56 * 1024 * 1024

R_E1, R_E2, R_W1, R_W2, R_RANK1, R_RANK2 = 0, 1, 2, 3, 4, 5
L_EXPERT0 = N_EXPERT_GROUPS


def _lambda_init(layer):
    return 0.8 - 0.6 * math.exp(-0.3 * layer)


def _cparams(n_axes):
    return pltpu.CompilerParams(dimension_semantics=("arbitrary",) * n_axes, vmem_limit_bytes=VMEM_LIMIT)


def _seq_of_tile(i, n_ctx_tiles, tiles_per_req):
    return jnp.where(i < n_ctx_tiles, 0, 1 + (i - n_ctx_tiles) // tiles_per_req)


def _pick_rows(i, n_ctx_tiles, ctx_ref, lat_ref):
    return jnp.where(i < n_ctx_tiles, ctx_ref[...], lat_ref[...])


def _split_specs(tm, d, n_ctx_tiles):
    return [pl.BlockSpec((tm, d), lambda i: (jnp.minimum(i, n_ctx_tiles - 1), 0)),
            pl.BlockSpec((tm, d), lambda i: (jnp.maximum(i - n_ctx_tiles, 0), 0))]


def _mod_row(mod_ref, part, seq):
    return mod_ref[part, pl.ds(seq, 1), :]


def _modulated(x, g, shift, scale):
    y = x * lax.rsqrt(jnp.mean(x * x, axis=-1, keepdims=True) + EPS)
    return (y * g) * (1.0 + scale) + shift


def _split_bf16(x):
    hi = x.astype(BF16)
    lo = (x - hi.astype(F32)).astype(BF16)
    return hi, lo


def _gelu_tanh(x):
    c = math.sqrt(2.0 / math.pi)
    hx = 0.5 * x
    return hx + hx * jnp.tanh(x * (c + (c * 0.044715) * (x * x)))


def _dot(a, b):
    return jnp.dot(a, b, preferred_element_type=F32)


def _pack_bf16_pairs(x):
    half = x.shape[1] // 2
    bits = lax.bitcast_convert_type(x.astype(BF16).astype(F32), jnp.uint32)
    packed = (bits[:, :half] >> 16) | bits[:, half:]
    return lax.bitcast_convert_type(packed, jnp.int32)


def _unpack_bf16_pairs(p):
    u = lax.bitcast_convert_type(p, jnp.uint32)
    lo = lax.bitcast_convert_type(u << 16, F32).astype(BF16)
    hi = lax.bitcast_convert_type(u & jnp.uint32(0xFFFF0000), F32).astype(BF16)
    return lo, hi


def _ada_kernel(c_ref, w_ref, b_ref, o_ref):
    c = c_ref[...]
    s = c * jax.nn.sigmoid(c)
    o_ref[...] = _dot(s.astype(BF16), w_ref[...].astype(BF16)) + b_ref[...]


def _ada_mod(cvec, ada_w, ada_b):
    depth, d, d6 = ada_w.shape
    parts = d6 // d
    rows = cvec.shape[0]
    return pl.pallas_call(
        _ada_kernel,
        grid=(depth, parts),
        in_specs=[
            pl.BlockSpec((rows, d), lambda l, j: (0, 0)),
            pl.BlockSpec((None, d, d), lambda l, j: (l, 0, j)),
            pl.BlockSpec((None, 1, d), lambda l, j: (l, 0, j)),
        ],
        out_specs=pl.BlockSpec((None, None, rows, d), lambda l, j: (l, j, 0, 0)),
        out_shape=jax.ShapeDtypeStruct((depth, parts, rows, d), F32),
        compiler_params=_cparams(2),
        name="ada_mod",
    )(cvec, ada_w, ada_b.reshape(depth, 1, d6))


def _pre_attn_kernel(hc_ref, hl_ref, mod_ref, g_ref, w_ref, qg_ref, kg_ref, gs_ref, gb_ref, cos_ref, sin_ref,
                     q_ref, k_ref, v_ref, kf_ref, vf_ref, *, n_ctx_tiles, tiles_per_req):
    i = pl.program_id(0)
    seq = _seq_of_tile(i, n_ctx_tiles, tiles_per_req)
    d = hc_ref.shape[1]
    x = _pick_rows(i, n_ctx_tiles, hc_ref, hl_ref)
    a = _modulated(x, g_ref[...], _mod_row(mod_ref, 0, seq), _mod_row(mod_ref, 1, seq))
    qkv = _dot(a.astype(BF16), w_ref[...])
    q = qkv[:, :d]
    k = qkv[:, d:2 * d]
    v = qkv[:, 2 * d:]
    sq_hi, sq_lo = _split_bf16(jnp.concatenate([q * q, k * k], axis=1))
    gsum = _dot(sq_hi, gs_ref[...]) + _dot(sq_lo, gs_ref[...])
    r_hi, r_lo = _split_bf16(lax.rsqrt(gsum * (1.0 / HEAD_DIM) + EPS))
    rb = _dot(r_hi, gb_ref[...]) + _dot(r_lo, gb_ref[...])
    qn = (q * rb[:, :d]) * qg_ref[...]
    kn = (k * rb[:, d:]) * kg_ref[...]
    v_ref[...] = v.astype(BF16)
    q_scale = HEAD_DIM ** -0.5

    @pl.when(i < n_ctx_tiles)
    def _():
        q_ref[...] = (qn * q_scale).astype(BF16)
        k_ref[...] = kn.astype(BF16)
        kf_ref[...] = kn.T.reshape(kf_ref.shape)
        vf_ref[...] = v

    @pl.when(i >= n_ctx_tiles)
    def _():
        cos = cos_ref[...]
        sin = sin_ref[...]
        lane = lax.broadcasted_iota(jnp.int32, cos.shape, 1)
        first = (lane & 31) < 16

        def rope(x):
            outs = []
            for hh in range(d // LANES):
                xs = x[:, hh * LANES:(hh + 1) * LANES]
                rot = jnp.where(first, pltpu.roll(xs, LANES - 16, 1), pltpu.roll(xs, 16, 1))
                outs.append(xs * cos + rot * sin)
            return jnp.concatenate(outs, axis=1)

        q_ref[...] = (rope(qn) * q_scale).astype(BF16)
        k_ref[...] = rope(kn).astype(BF16)


def _rope_tables(n_pos):
    half = HEAD_DIM // 2
    pos = jnp.arange(n_pos, dtype=jnp.int32)
    row = (pos // GRID_W).astype(F32)
    col = (pos % GRID_W).astype(F32)
    inv = 1.0 / (ROPE_THETA ** (jnp.arange(0, half, 2, dtype=F32) / half))
    ang_r = row[:, None] * inv[None, :]
    ang_c = col[:, None] * inv[None, :]
    ang = jnp.concatenate([ang_r, ang_r, ang_c, ang_c], axis=-1)
    quarter = half // 2
    sign = jnp.tile(jnp.concatenate([-jnp.ones((quarter,), F32), jnp.ones((quarter,), F32)]), 2)
    cos = jnp.tile(jnp.cos(ang), (1, 2))
    sin = jnp.tile(jnp.sin(ang) * sign[None, :], (1, 2))
    return cos, sin


def _pre_attn(h_ctx, h_lat, mod, g, wqkv, qg, kg, rows_per_req):
    n_ctx, d = h_ctx.shape
    t = n_ctx + h_lat.shape[0]
    tm = ROW_TILE
    n_ctx_tiles = n_ctx // tm
    tiles_per_req = rows_per_req // tm
    groups = d // HEAD_DIM
    gid = jnp.arange(2 * d, dtype=jnp.int32) // HEAD_DIM
    gs = (gid[:, None] == jnp.arange(LANES, dtype=jnp.int32)[None, :]).astype(BF16)
    gb = gs.T
    assert 2 * groups <= LANES
    cos, sin = _rope_tables(rows_per_req)
    pos_map = lambda i: (jnp.where(i < n_ctx_tiles, 0, (i - n_ctx_tiles) % tiles_per_req), 0)
    ctx_map = lambda i: (jnp.minimum(i, n_ctx_tiles - 1), 0)
    full = lambda shape: pl.BlockSpec(shape, lambda i: (0,) * len(shape))
    row = pl.BlockSpec((tm, d), lambda i: (i, 0))
    kern = functools.partial(_pre_attn_kernel, n_ctx_tiles=n_ctx_tiles, tiles_per_req=tiles_per_req)
    return pl.pallas_call(
        kern,
        grid=(t // tm,),
        in_specs=_split_specs(tm, d, n_ctx_tiles) + [
                  full(mod.shape), full((1, d)), full(wqkv.shape), full((1, d)), full((1, d)),
                  full(gs.shape), full(gb.shape),
                  pl.BlockSpec((tm, LANES), pos_map), pl.BlockSpec((tm, LANES), pos_map)],
        out_specs=[row, row, row,
                   pl.BlockSpec((None, d // HEAD_DIM, HEAD_DIM, tm), lambda i: (ctx_map(i)[0], 0, 0, 0)),
                   pl.BlockSpec((tm, d), ctx_map)],
        out_shape=[jax.ShapeDtypeStruct((t, d), BF16)] * 3 + [
            jax.ShapeDtypeStruct((n_ctx_tiles, d // HEAD_DIM, HEAD_DIM, tm), F32), jax.ShapeDtypeStruct((n_ctx, d), F32)],
        compiler_params=_cparams(1),
        name="pre_attn",
    )(h_ctx, h_lat, mod, g, wqkv, qg, kg, gs, gb, cos, sin)


def _attn_kernel(*refs, n_heads, has_cache, layer):
    if has_cache:
        lam_ref, sub_ref, q_ref, kn_ref, vn_ref, kc_ref, vc_ref, o_ref = refs
    else:
        lam_ref, sub_ref, q_ref, kn_ref, vn_ref, o_ref = refs
    lf = lam_ref[...]
    lam = (jnp.exp(jnp.sum(lf[0:1] * lf[1:2], axis=-1, keepdims=True))
           - jnp.exp(jnp.sum(lf[2:3] * lf[3:4], axis=-1, keepdims=True)) + _lambda_init(layer))
    nt = (((1,), (1,)), ((), ()))
    tq = q_ref.shape[0]
    lane = lax.broadcasted_iota(jnp.int32, (tq, LANES), 1)
    for hh in range(n_heads):
        sl = slice(hh * LANES, (hh + 1) * LANES)
        qh = q_ref[:, sl]
        zero = jnp.zeros_like(qh)
        qs = (jnp.where(lane < HEAD_DIM, qh, zero), jnp.where(lane >= HEAD_DIM, qh, zero))
        keys = [kn_ref[:, sl]]
        vals = [vn_ref[:, sl]]
        if has_cache:
            keys.append(kc_ref[:, sl].astype(BF16))
            vals.append(vc_ref[:, sl].astype(BF16))
        probs = []
        for qm in qs:
            s = [lax.dot_general(qm, kk, nt, preferred_element_type=F32) for kk in keys]
            m = functools.reduce(jnp.maximum, [jnp.max(x, axis=-1, keepdims=True) for x in s])
            e = [jnp.exp(x - m) for x in s]
            den = functools.reduce(lambda u, w: u + w, [jnp.sum(x, axis=-1, keepdims=True) for x in e])
            probs.append((e, 1.0 / den))
        (e0, r0), (e1, r1) = probs
        o = None
        for j, vv in enumerate(vals):
            aj = e0[j] * r0 - lam * (e1[j] * r1)
            oj = _dot(aj.astype(BF16), vv)
            o = oj if o is None else o + oj
        o = o * lax.rsqrt(jnp.mean(o * o, axis=-1, keepdims=True) + EPS)
        o = (o * sub_ref[...]) * (1.0 - _lambda_init(layer))
        o_ref[:, sl] = o.astype(BF16)


def _attention(q, k, v, lam_p, subln, cache_k, cache_v, n_ctx, ctx_len, n_req, req_len, layer):
    t, d = q.shape
    small = lambda shape: pl.BlockSpec(shape, lambda *_: (0,) * len(shape))
    n_ctx_req = n_ctx // ctx_len
    blk = pl.BlockSpec((ctx_len, d), lambda b: (b, 0))
    o_ctx = pl.pallas_call(
        functools.partial(_attn_kernel, n_heads=N_HEADS, has_cache=False, layer=layer),
        grid=(n_ctx_req,),
        in_specs=[small(lam_p.shape), small(subln.shape), blk, blk, blk],
        out_specs=blk,
        out_shape=jax.ShapeDtypeStruct((n_ctx, d), BF16),
        compiler_params=_cparams(1),
        name="attn_ctx",
    )(lam_p, subln, q, k, v)

    tq = ATTN_Q_TILE
    nq = req_len // tq
    q0 = n_ctx // tq
    r0 = n_ctx // req_len
    past = cache_k.shape[1]
    hp = ATTN_HEADS_PER_STEP
    small3 = lambda shape: pl.BlockSpec(shape, lambda b, h, i: (0,) * len(shape))
    qspec = pl.BlockSpec((tq, hp * LANES), lambda b, h, i: (q0 + b * nq + i, h))
    kvspec = pl.BlockSpec((req_len, hp * LANES), lambda b, h, i: (r0 + b, h))
    cspec = pl.BlockSpec((None, past, hp * LANES), lambda b, h, i: (b, 0, h))
    o_lat = pl.pallas_call(
        functools.partial(_attn_kernel, n_heads=hp, has_cache=True, layer=layer),
        grid=(n_req, N_HEADS // hp, nq),
        in_specs=[small3(lam_p.shape), pl.BlockSpec((1, LANES), lambda b, h, i: (0, 0)),
                  qspec, kvspec, kvspec, cspec, cspec],
        out_specs=pl.BlockSpec((tq, hp * LANES), lambda b, h, i: (b * nq + i, h)),
        out_shape=jax.ShapeDtypeStruct((n_req * req_len, d), BF16),
        compiler_params=_cparams(3),
        name="attn_latent",
    )(lam_p, subln, q, k, v, cache_k, cache_v)
    return o_ctx, o_lat


def _route(a2, wr, br, ltri, carry_ref):
    tm = a2.shape[0]
    logits = _dot(a2.astype(BF16), wr) + br
    lane_i = lax.broadcasted_iota(jnp.int32, (tm, LANES), 1)
    lane = lane_i.astype(F32)
    neg = jnp.full((tm, LANES), -jnp.inf, F32)
    big = jnp.full((tm, LANES), float(LANES), F32)
    first_lane = lambda mask: jnp.min(jnp.where(mask, lane, big), axis=-1, keepdims=True)

    lc = jnp.where(lane_i < N_EXPERT_GROUPS, logits, neg)
    mc = jnp.max(lc, axis=-1, keepdims=True)
    pg = 1.0 / jnp.sum(jnp.exp(lc - mc), axis=-1, keepdims=True)
    gi = first_lane(lc == mc)
    assert EXPERTS_PER_GROUP == 8
    grp = lax.shift_right_arithmetic(lane_i - L_EXPERT0, 3).astype(F32)
    in_group = (lane_i >= L_EXPERT0) & (lane_i < L_EXPERT0 + N_EXPERTS) & (grp == gi)
    ls = jnp.where(in_group, logits, neg)
    t1 = jnp.max(ls, axis=-1, keepdims=True)
    i1 = first_lane(ls == t1)
    ls2 = jnp.where(lane == i1, neg, ls)
    t2 = jnp.max(ls2, axis=-1, keepdims=True)
    i2 = first_lane(ls2 == t2)
    ex = jnp.exp(t2 - t1)
    w1 = pg * (1.0 / (1.0 + ex))
    w2 = pg * (ex / (1.0 + ex))
    e1 = i1 - float(L_EXPERT0)
    e2 = i2 - float(L_EXPERT0)
    oh1 = lane == e1
    oh2 = lane == e2
    onehot = oh1.astype(F32) + oh2.astype(F32)
    before = _dot(ltri, onehot.astype(BF16)) + carry_ref[...]
    zero = jnp.zeros_like(before)
    rank1 = jnp.sum(jnp.where(oh1, before, zero), axis=-1, keepdims=True)
    rank2 = jnp.sum(jnp.where(oh2, before, zero), axis=-1, keepdims=True)
    carry_ref[...] = carry_ref[...] + jnp.sum(onehot, axis=0, keepdims=True)
    slab = jnp.zeros((tm, LANES), F32)
    for ln, val in ((R_E1, e1), (R_E2, e2), (R_W1, w1), (R_W2, w2), (R_RANK1, rank1), (R_RANK2, rank2)):
        slab = jnp.where(lane_i == ln, val, slab)
    return slab


def _router_weights(wc, bc, wf, bf_):
    d = wc.shape[0]
    pad = LANES - N_EXPERT_GROUPS - N_EXPERTS
    w = jnp.concatenate([wc, wf, jnp.zeros((d, pad), F32)], axis=1)
    b = jnp.concatenate([bc, bf_, jnp.zeros((pad,), F32)])[None, :]
    return w.astype(BF16), b


def _lower_tri(n):
    r = jnp.arange(n, dtype=jnp.int32)
    return (r[None, :] < r[:, None]).astype(BF16)


def _post_attn_kernel(hc_ref, hl_ref, oc_ref, ol_ref, mod_ref, wo_ref, g2_ref, wr_ref, br_ref, ltri_ref,
                      h1_ref, a2_ref, route_ref, cnt_ref, carry_ref, *, n_ctx_tiles, tiles_per_req):
    i = pl.program_id(0)
    seq = _seq_of_tile(i, n_ctx_tiles, tiles_per_req)

    @pl.when(i == 0)
    def _():
        carry_ref[...] = jnp.zeros_like(carry_ref)

    for s in range(h1_ref.shape[0] // ROW_TILE):
        rows = pl.ds(s * ROW_TILE, ROW_TILE)
        o = jnp.where(i < n_ctx_tiles, oc_ref[rows, :], ol_ref[rows, :])
        h = jnp.where(i < n_ctx_tiles, hc_ref[rows, :], hl_ref[rows, :])
        h1 = h + _mod_row(mod_ref, 2, seq) * _dot(o, wo_ref[...])
        h1_ref[rows, :] = h1
        a2 = _modulated(h1, g2_ref[...], _mod_row(mod_ref, 3, seq), _mod_row(mod_ref, 4, seq))
        a2_ref[rows, :] = _pack_bf16_pairs(a2)
        route_ref[rows, :] = _route(a2, wr_ref[...], br_ref[...], ltri_ref[...], carry_ref)
    cnt_ref[...] = carry_ref[...]


def _post_attn(h_ctx, h_lat, o_ctx, o_lat, mod, wo, g2, router, rows_per_req):
    n_ctx, d = h_ctx.shape
    t = n_ctx + h_lat.shape[0]
    tm = ROW_TILE * SUBTILES
    wr, br = router
    ltri = _lower_tri(ROW_TILE)
    full = lambda shape: pl.BlockSpec(shape, lambda i: (0,) * len(shape))
    row = pl.BlockSpec((tm, d), lambda i: (i, 0))
    kern = functools.partial(_post_attn_kernel, n_ctx_tiles=n_ctx // tm, tiles_per_req=rows_per_req // tm)
    return pl.pallas_call(
        kern,
        grid=(t // tm,),
        in_specs=_split_specs(tm, d, n_ctx // tm) + _split_specs(tm, d, n_ctx // tm) + [
                  full(mod.shape), full(wo.shape), full((1, d)), full(wr.shape),
                  full(br.shape), full(ltri.shape)],
        out_specs=[row, pl.BlockSpec((tm, d // 2), lambda i: (i, 0)), pl.BlockSpec((tm, LANES), lambda i: (i, 0)),
                   full((1, LANES))],
        out_shape=[jax.ShapeDtypeStruct((t, d), F32), jax.ShapeDtypeStruct((t, d // 2), jnp.int32),
                   jax.ShapeDtypeStruct((t, LANES), F32), jax.ShapeDtypeStruct((1, LANES), F32)],
        scratch_shapes=[pltpu.VMEM((1, LANES), F32)],
        compiler_params=_cparams(1),
        name="post_attn_router",
    )(h_ctx, h_lat, o_ctx, o_lat, mod, wo, g2, wr, br, ltri)


def _gather_rows(src, idx):
    n = idx.shape[0]
    d = src.shape[1]
    w = _sc_window_rows(src)
    mesh = plsc.VectorSubcoreMesh(core_axis_name="core", subcore_axis_name="subcore")

    @pl.kernel(out_type=jax.ShapeDtypeStruct((n, d), src.dtype), mesh=mesh)
    def gather(src_hbm, idx_hbm, out_hbm):
        def body(idx_vmem, out_vmem):
            pltpu.sync_copy(src_hbm.at[idx_vmem.at[0, pl.ds(0, w)]], out_vmem)

        pltpu.emit_pipeline(
            body,
            grid=(n // w,),
            in_specs=[pl.BlockSpec((1, LANES), lambda i: (i, 0))],
            out_specs=[pl.BlockSpec((w, d), lambda i: (i, 0))],
            core_axis_name=("core", "subcore"),
            dimension_semantics=(pltpu.PARALLEL,),
        )(idx_hbm, out_hbm)

    return gather(src, _index_windows(idx, w))


def _sc_window_rows(src):
    return min(LANES, SC_WINDOW_BYTES // (src.shape[1] * src.dtype.itemsize))


def _index_windows(idx, w):
    return jnp.pad(idx.reshape(idx.shape[0] // w, w), ((0, 0), (0, LANES - w)))


def _scatter_rows_twice(src, idx_a, idx_b, n_out):
    t, d = src.shape
    w = _sc_window_rows(src)
    mesh = plsc.VectorSubcoreMesh(core_axis_name="core", subcore_axis_name="subcore")

    @pl.kernel(out_type=jax.ShapeDtypeStruct((n_out, d), src.dtype), mesh=mesh)
    def scatter(src_hbm, ia_hbm, ib_hbm, out_hbm):
        def body(src_vmem, ia_vmem, ib_vmem):
            pltpu.sync_copy(src_vmem, out_hbm.at[ia_vmem.at[0, pl.ds(0, w)]])
            pltpu.sync_copy(src_vmem, out_hbm.at[ib_vmem.at[0, pl.ds(0, w)]])

        pltpu.emit_pipeline(
            body,
            grid=(t // w,),
            in_specs=[pl.BlockSpec((w, d), lambda i: (i, 0)),
                      pl.BlockSpec((1, LANES), lambda i: (i, 0)),
                      pl.BlockSpec((1, LANES), lambda i: (i, 0))],
            out_specs=[],
            core_axis_name=("core", "subcore"),
            dimension_semantics=(pltpu.PARALLEL,),
        )(src_hbm, ia_hbm, ib_hbm)

    return scatter(src, _index_windows(idx_a, w), _index_windows(idx_b, w))


def _expert_kernel(be_ref, nb_ref, x_ref, w1_ref, w3_ref, w2_ref, y_ref, w1b, w3b, w2b):
    b = pl.program_id(0)
    prev = be_ref[jnp.maximum(b - 1, 0)]
    used = b < nb_ref[0]

    @pl.when(used & ((b == 0) | (be_ref[b] != prev)))
    def _():
        w1b[...] = w1_ref[...].astype(BF16)
        w3b[...] = w3_ref[...].astype(BF16)
        w2b[...] = w2_ref[...].astype(BF16)

    @pl.when(used)
    def _():
        x_lo, x_hi = _unpack_bf16_pairs(x_ref[...])
        half = x_lo.shape[1]
        h1 = _dot(x_lo, w1b[:half, :]) + _dot(x_hi, w1b[half:, :])
        h3 = _dot(x_lo, w3b[:half, :]) + _dot(x_hi, w3b[half:, :])
        hb = jax.nn.silu(h1) * h3
        y_ref[...] = _dot(hb.astype(BF16), w2b[...])


def _expert_mlps(xs, block_e, n_used, w1, w3, w2, layer):
    npad, half = xs.shape
    d, ff = w1.shape[2], w1.shape[3]
    assert d == 2 * half
    bm = MOE_BLOCK
    rows = lambda b, be, nb: (jnp.minimum(b, nb[0] - 1), 0)
    grid_spec = pltpu.PrefetchScalarGridSpec(
        num_scalar_prefetch=2,
        grid=(npad // bm,),
        in_specs=[pl.BlockSpec((bm, half), rows),
                  pl.BlockSpec((None, None, d, ff), lambda b, be, nb: (layer, be[b], 0, 0)),
                  pl.BlockSpec((None, None, d, ff), lambda b, be, nb: (layer, be[b], 0, 0)),
                  pl.BlockSpec((None, None, ff, d), lambda b, be, nb: (layer, be[b], 0, 0))],
        out_specs=pl.BlockSpec((bm, d), rows),
        scratch_shapes=[pltpu.VMEM((d, ff), BF16), pltpu.VMEM((d, ff), BF16), pltpu.VMEM((ff, d), BF16)],
    )
    return pl.pallas_call(
        _expert_kernel,
        grid_spec=grid_spec,
        out_shape=jax.ShapeDtypeStruct((npad, d), F32),
        compiler_params=_cparams(1),
        name="expert_mlps",
    )(block_e, n_used, xs, w1, w3, w2)


def _moe(a2p, route, counts, w1, w3, w2, layer):
    t = a2p.shape[0]
    bm = MOE_BLOCK
    n = 2 * t
    e = route[:, R_E1:R_E2 + 1].astype(jnp.int32).reshape(n)
    rank = route[:, R_RANK1:R_RANK2 + 1].astype(jnp.int32).reshape(n)
    cnt = counts[0, :N_EXPERTS].astype(jnp.int32)
    padded = ((cnt + bm - 1) // bm) * bm
    pend = jnp.cumsum(padded)
    pstart = pend - padded
    pos = jnp.sum(jnp.where(e[:, None] == jnp.arange(N_EXPERTS, dtype=jnp.int32)[None, :], pstart[None, :], 0),
                  axis=1) + rank
    n_blocks = n // bm + N_EXPERTS
    npad = n_blocks * bm
    starts = jnp.arange(n_blocks, dtype=jnp.int32) * bm
    n_used = pend[-1:] // bm
    last_start = (n_used[0] - 1) * bm
    block_e = jnp.sum((pend[None, :] <= jnp.minimum(starts, last_start)[:, None]).astype(jnp.int32), axis=1)
    pos_first, pos_second = pos[0::2], pos[1::2]
    xs = _scatter_rows_twice(a2p, pos_first, pos_second, npad)
    ys = _expert_mlps(xs, block_e, n_used, w1, w3, w2, layer)
    return _gather_rows(ys, jnp.concatenate([pos_first, pos_second]))


def _moe_combine(h, y_first, y_second, route, g2):
    w1 = route[:, R_W1:R_W1 + 1]
    w2 = route[:, R_W2:R_W2 + 1]
    return h + g2 * (w1 * y_first + w2 * y_second)


def _gmlp_kernel(h_ref, ya_ref, yb_ref, rt_ref, modp_ref, mod_ref, g1_ref, win_ref, bin_ref, vg_ref, ws_ref, bsb_ref,
                 wout_ref, g2_ref, wr_ref, br_ref, ltri_ref,
                 h1_ref, a2_ref, route_ref, cnt_ref, carry_ref, *, n_ctx_tiles, tiles_per_req):
    i = pl.program_id(0)
    seq = _seq_of_tile(i, n_ctx_tiles, tiles_per_req)

    @pl.when(i == 0)
    def _():
        carry_ref[...] = jnp.zeros_like(carry_ref)

    gw = vg_ref.shape[1]
    cg = gw // GM_GROUPS
    for s in range(h_ref.shape[0] // GMLP_SUB):
        rs = pl.ds(s * GMLP_SUB, GMLP_SUB)
        h = _moe_combine(h_ref[rs, :], ya_ref[rs, :], yb_ref[rs, :], rt_ref[rs, :], _mod_row(modp_ref, 5, seq))
        a = _modulated(h, g1_ref[...], _mod_row(mod_ref, 0, seq), _mod_row(mod_ref, 1, seq))
        z = _gelu_tanh(_dot(a.astype(BF16), win_ref[...]) + bin_ref[...])
        u = z[:, :gw]
        v = z[:, gw:]
        v = (v * lax.rsqrt(jnp.mean(v * v, axis=-1, keepdims=True) + EPS)) * vg_ref[...]
        vb = v.astype(BF16)
        rows = []
        for c in range(GMLP_SUB // CHUNK):
            cols = []
            for g in range(GM_GROUPS):
                cols.append(_dot(ws_ref[g], vb[c * CHUNK:(c + 1) * CHUNK, g * cg:(g + 1) * cg]))
            rows.append(jnp.concatenate(cols, axis=1) + bsb_ref[...])
        vm = jnp.concatenate(rows, axis=0)
        mix = _dot((u * vm).astype(BF16), wout_ref[...])
        h1 = h + _mod_row(mod_ref, 2, seq) * mix
        h1_ref[rs, :] = h1
        a2 = _modulated(h1, g2_ref[...], _mod_row(mod_ref, 3, seq), _mod_row(mod_ref, 4, seq))
        a2_ref[rs, :] = _pack_bf16_pairs(a2)
        route_ref[rs, :] = _route(a2, wr_ref[...], br_ref[...], ltri_ref[...], carry_ref)
    cnt_ref[...] = carry_ref[...]


def _gmlp_layer(h, y2, route_prev, mod_prev, mod, g1, win, bin_, vg, ws, bsb, wout, g2, router,
                n_ctx, rows_per_req):
    t, d = h.shape
    tm = ROW_TILE * SUBTILES
    wr, br = router
    ltri = _lower_tri(GMLP_SUB)
    full = lambda shape: pl.BlockSpec(shape, lambda i: (0,) * len(shape), pipeline_mode=pl.Buffered(1))
    row = lambda w: pl.BlockSpec((tm, w), lambda i: (i, 0))
    kern = functools.partial(_gmlp_kernel, n_ctx_tiles=n_ctx // tm, tiles_per_req=rows_per_req // tm)
    args = (h, y2, y2, route_prev, mod_prev, mod, g1, win, bin_, vg, ws, bsb, wout, g2, wr, br, ltri)
    second = pl.BlockSpec((tm, d), lambda i: (i + t // tm, 0))
    in_specs = [row(d), row(d), second, row(LANES)] + [full(a.shape) for a in args[4:]]
    return pl.pallas_call(
        kern,
        grid=(t // tm,),
        in_specs=in_specs,
        out_specs=[row(d), row(d // 2), row(LANES), pl.BlockSpec((1, LANES), lambda i: (0, 0))],
        out_shape=[jax.ShapeDtypeStruct((t, d), F32), jax.ShapeDtypeStruct((t, d // 2), jnp.int32),
                   jax.ShapeDtypeStruct((t, LANES), F32), jax.ShapeDtypeStruct((1, LANES), F32)],
        scratch_shapes=[pltpu.VMEM((1, LANES), F32)],
        compiler_params=_cparams(1),
        name="gmlp_router",
    )(*args)


def _final_kernel(h_ref, ya_ref, yb_ref, rt_ref, mod_ref, oc_ref, ol_ref, *, n_ctx_tiles, tiles_per_req):
    i = pl.program_id(0)
    seq = _seq_of_tile(i, n_ctx_tiles, tiles_per_req)
    out = _moe_combine(h_ref[...], ya_ref[...], yb_ref[...], rt_ref[...], _mod_row(mod_ref, 5, seq))

    @pl.when(i < n_ctx_tiles)
    def _():
        oc_ref[...] = out

    @pl.when(i >= n_ctx_tiles)
    def _():
        ol_ref[...] = out


def _final_combine(h, y2, route, mod, n_ctx, rows_per_req):
    t, d = h.shape
    tm = ROW_TILE
    row = lambda w: pl.BlockSpec((tm, w), lambda i: (i, 0))
    kern = functools.partial(_final_kernel, n_ctx_tiles=n_ctx // tm, tiles_per_req=rows_per_req // tm)
    return pl.pallas_call(
        kern,
        grid=(t // tm,),
        in_specs=[row(d), row(d), pl.BlockSpec((tm, d), lambda i: (i + t // tm, 0)), row(LANES),
                  pl.BlockSpec(mod.shape, lambda i: (0, 0, 0))],
        out_specs=_split_specs(tm, d, n_ctx // tm),
        out_shape=[jax.ShapeDtypeStruct((n_ctx, d), F32), jax.ShapeDtypeStruct((t - n_ctx, d), F32)],
        compiler_params=_cparams(1),
        name="final_combine",
    )(h, y2, y2, route, mod)


def kernel(x_prompt, x_sample, cache_k, cache_v, c, c_ctx, ada_w, ada_b, norm1_g, norm2_g, attn_wq, attn_wk,
           attn_wv, attn_wo, attn_qnorm, attn_knorm, attn_lam, attn_subln, gm_win, gm_bin, gm_vnorm, gm_ws,
           gm_bs, gm_wout, moe_wc, moe_bc, moe_wf, moe_bf, moe_w1, moe_w3, moe_w2):
    batch, seq_len, d = x_prompt.shape
    n_req, req_len, _ = x_sample.shape
    depth = ada_w.shape[0]
    assert depth == 2 and attn_wq.shape[0] == 1 and gm_win.shape[0] == 1
    n_ctx = batch * seq_len
    assert n_ctx % req_len == 0 and req_len % ROW_TILE == 0 and seq_len == ROW_TILE
    past = cache_k.shape[2]

    h_ctx = x_prompt.reshape(n_ctx, d)
    h_lat = x_sample.reshape(n_req * req_len, d)
    cvec = jnp.concatenate([c_ctx[None, :], c, jnp.zeros((8 - 1 - n_req, d), F32)], axis=0)
    mod = _ada_mod(cvec, ada_w, ada_b)
    routers = [_router_weights(moe_wc[i], moe_bc[i], moe_wf[i], moe_bf[i]) for i in range(depth)]
    row_vec = lambda v: v.reshape(1, -1)

    wqkv = jnp.concatenate([attn_wq[0], attn_wk[0], attn_wv[0]], axis=1).astype(BF16)
    reps = d // HEAD_DIM
    q, k, v, k_new, v_new = _pre_attn(h_ctx, h_lat, mod[0], row_vec(norm1_g[0]), wqkv,
                                      row_vec(jnp.tile(attn_qnorm[0], reps)), row_vec(jnp.tile(attn_knorm[0], reps)),
                                      req_len)
    o_ctx, o_lat = _attention(q, k, v, attn_lam[0], row_vec(attn_subln[0]),
                              cache_k[:, 0].reshape(n_req, past, d), cache_v[:, 0].reshape(n_req, past, d),
                              n_ctx, seq_len, n_req, req_len, layer=0)
    h1, a2, route0, cnt0 = _post_attn(h_ctx, h_lat, o_ctx, o_lat, mod[0], attn_wo[0].astype(BF16),
                                      row_vec(norm2_g[0]), routers[0], req_len)
    y2 = _moe(a2, route0, cnt0, moe_w1, moe_w3, moe_w2, layer=0)

    gw = gm_vnorm.shape[1]
    bsb = jnp.repeat(gm_bs[0].T, gw // GM_GROUPS, axis=1)
    h2, a2, route1, cnt1 = _gmlp_layer(h1, y2, route0, mod[0], mod[1], row_vec(norm1_g[1]), gm_win[0].astype(BF16),
                                       row_vec(gm_bin[0]), row_vec(gm_vnorm[0]), gm_ws[0].astype(BF16), bsb,
                                       gm_wout[0].astype(BF16), row_vec(norm2_g[1]), routers[1], n_ctx, req_len)
    y2 = _moe(a2, route1, cnt1, moe_w1, moe_w3, moe_w2, layer=1)
    y_ctx, y_lat = _final_combine(h2, y2, route1, mod[1], n_ctx, req_len)

    y_prompt = y_ctx.reshape(batch, seq_len, d)
    y_sample = y_lat.reshape(n_req, req_len, d)
    new_cache_k = jnp.transpose(k_new.reshape(batch, 1, N_HEADS, 2, HEAD_DIM, seq_len), (0, 1, 5, 2, 3, 4))
    new_cache_v = v_new.reshape(batch, 1, seq_len, N_HEADS, V_DIM)
    return (y_prompt, y_sample, new_cache_k, new_cache_v)
```

```python
import functools
import math

import jax
import jax.numpy as jnp
from jax import lax
from jax.experimental import pallas as pl
from jax.experimental.pallas import tpu as pltpu
from jax.experimental.pallas import tpu_sc as plsc

F32 = jnp.float32
BF16 = jnp.bfloat16

N_HEADS = 8
HEAD_DIM = 64
V_DIM = 2 * HEAD_DIM
GRID_W = 64
ROPE_THETA = 10000.0
CHUNK = 128
GM_GROUPS = 8
N_EXPERT_GROUPS = 4
EXPERTS_PER_GROUP = 8
N_EXPERTS = N_EXPERT_GROUPS * EXPERTS_PER_GROUP
EPS = 1e-6

LANES = 128
ROW_TILE = 256
SUBTILES = 2
GMLP_SUB = 512
MOE_BLOCK = 256
SC_WINDOW_BYTES = 128 * 1024
ATTN_Q_TILE = 256
ATTN_HEADS_PER_STEP = 4
VMEM_LIMIT = 56 * 1024 * 1024

R_E1, R_E2, R_W1, R_W2, R_RANK1, R_RANK2 = 0, 1, 2, 3, 4, 5
L_EXPERT0 = N_EXPERT_GROUPS


def _lambda_init(layer):
    return 0.8 - 0.6 * math.exp(-0.3 * layer)


def _cparams(n_axes):
    return pltpu.CompilerParams(dimension_semantics=("arbitrary",) * n_axes, vmem_limit_bytes=VMEM_LIMIT)


def _seq_of_tile(i, n_ctx_tiles, tiles_per_req):
    return jnp.where(i < n_ctx_tiles, 0, 1 + (i - n_ctx_tiles) // tiles_per_req)


def _pick_rows(i, n_ctx_tiles, ctx_ref, lat_ref):
    return jnp.where(i < n_ctx_tiles, ctx_ref[...], lat_ref[...])


def _split_specs(tm, d, n_ctx_tiles):
    return [pl.BlockSpec((tm, d), lambda i: (jnp.minimum(i, n_ctx_tiles - 1), 0)),
            pl.BlockSpec((tm, d), lambda i: (jnp.maximum(i - n_ctx_tiles, 0), 0))]


def _mod_row(mod_ref, part, seq):
    return mod_ref[part, pl.ds(seq, 1), :]


def _modulated(x, g, shift, scale):
    y = x * lax.rsqrt(jnp.mean(x * x, axis=-1, keepdims=True) + EPS)
    return (y * g) * (1.0 + scale) + shift


def _split_bf16(x):
    hi = x.astype(BF16)
    lo = (x - hi.astype(F32)).astype(BF16)
    return hi, lo


def _gelu_tanh(x):
    c = math.sqrt(2.0 / math.pi)
    hx = 0.5 * x
    return hx + hx * jnp.tanh(x * (c + (c * 0.044715) * (x * x)))


def _dot(a, b):
    return jnp.dot(a, b, preferred_element_type=F32)


def _pack_bf16_pairs(x):
    half = x.shape[1] // 2
    bits = lax.bitcast_convert_type(x.astype(BF16).astype(F32), jnp.uint32)
    packed = (bits[:, :half] >> 16) | bits[:, half:]
    return lax.bitcast_convert_type(packed, jnp.int32)


def _unpack_bf16_pairs(p):
    u = lax.bitcast_convert_type(p, jnp.uint32)
    lo = lax.bitcast_convert_type(u << 16, F32).astype(BF16)
    hi = lax.bitcast_convert_type(u & jnp.uint32(0xFFFF0000), F32).astype(BF16)
    return lo, hi


def _ada_kernel(c_ref, w_ref, b_ref, o_ref):
    c = c_ref[...]
    s = c * jax.nn.sigmoid(c)
    o_ref[...] = _dot(s.astype(BF16), w_ref[...].astype(BF16)) + b_ref[...]


def _ada_mod(cvec, ada_w, ada_b):
    depth, d, d6 = ada_w.shape
    parts = d6 // d
    rows = cvec.shape[0]
    return pl.pallas_call(
        _ada_kernel,
        grid=(depth, parts),
        in_specs=[
            pl.BlockSpec((rows, d), lambda l, j: (0, 0)),
            pl.BlockSpec((None, d, d), lambda l, j: (l, 0, j)),
            pl.BlockSpec((None, 1, d), lambda l, j: (l, 0, j)),
        ],
        out_specs=pl.BlockSpec((None, None, rows, d), lambda l, j: (l, j, 0, 0)),
        out_shape=jax.ShapeDtypeStruct((depth, parts, rows, d), F32),
        compiler_params=_cparams(2),
        name="ada_mod",
    )(cvec, ada_w, ada_b.reshape(depth, 1, d6))


def _pre_attn_kernel(hc_ref, hl_ref, mod_ref, g_ref, w_ref, qg_ref, kg_ref, gs_ref, gb_ref, cos_ref, sin_ref,
                     q_ref, k_ref, v_ref, kf_ref, vf_ref, *, n_ctx_tiles, tiles_per_req):
    i = pl.program_id(0)
    seq = _seq_of_tile(i, n_ctx_tiles, tiles_per_req)
    d = hc_ref.shape[1]
    x = _pick_rows(i, n_ctx_tiles, hc_ref, hl_ref)
    a = _modulated(x, g_ref[...], _mod_row(mod_ref, 0, seq), _mod_row(mod_ref, 1, seq))
    qkv = _dot(a.astype(BF16), w_ref[...])
    q = qkv[:, :d]
    k = qkv[:, d:2 * d]
    v = qkv[:, 2 * d:]
    sq_hi, sq_lo = _split_bf16(jnp.concatenate([q * q, k * k], axis=1))
    gsum = _dot(sq_hi, gs_ref[...]) + _dot(sq_lo, gs_ref[...])
    r_hi, r_lo = _split_bf16(lax.rsqrt(gsum * (1.0 / HEAD_DIM) + EPS))
    rb = _dot(r_hi, gb_ref[...]) + _dot(r_lo, gb_ref[...])
    qn = (q * rb[:, :d]) * qg_ref[...]
    kn = (k * rb[:, d:]) * kg_ref[...]
    v_ref[...] = v.astype(BF16)
    q_scale = HEAD_DIM ** -0.5

    @pl.when(i < n_ctx_tiles)
    def _():
        q_ref[...] = (qn * q_scale).astype(BF16)
        k_ref[...] = kn.astype(BF16)
        kf_ref[...] = kn.T.reshape(kf_ref.shape)
        vf_ref[...] = v

    @pl.when(i >= n_ctx_tiles)
    def _():
        cos = cos_ref[...]
        sin = sin_ref[...]
        lane = lax.broadcasted_iota(jnp.int32, cos.shape, 1)
        first = (lane & 31) < 16

        def rope(x):
            outs = []
            for hh in range(d // LANES):
                xs = x[:, hh * LANES:(hh + 1) * LANES]
                rot = jnp.where(first, pltpu.roll(xs, LANES - 16, 1), pltpu.roll(xs, 16, 1))
                outs.append(xs * cos + rot * sin)
            return jnp.concatenate(outs, axis=1)

        q_ref[...] = (rope(qn) * q_scale).astype(BF16)
        k_ref[...] = rope(kn).astype(BF16)


def _rope_tables(n_pos):
    half = HEAD_DIM // 2
    pos = jnp.arange(n_pos, dtype=jnp.int32)
    row = (pos // GRID_W).astype(F32)
    col = (pos % GRID_W).astype(F32)
    inv = 1.0 / (ROPE_THETA ** (jnp.arange(0, half, 2, dtype=F32) / half))
    ang_r = row[:, None] * inv[None, :]
    ang_c = col[:, None] * inv[None, :]
    ang = jnp.concatenate([ang_r, ang_r, ang_c, ang_c], axis=-1)
    quarter = half // 2
    sign = jnp.tile(jnp.concatenate([-jnp.ones((quarter,), F32), jnp.ones((quarter,), F32)]), 2)
    cos = jnp.tile(jnp.cos(ang), (1, 2))
    sin = jnp.tile(jnp.sin(ang) * sign[None, :], (1, 2))
    return cos, sin


def _pre_attn(h_ctx, h_lat, mod, g, wqkv, qg, kg, rows_per_req):
    n_ctx, d = h_ctx.shape
    t = n_ctx + h_lat.shape[0]
    tm = ROW_TILE
    n_ctx_tiles = n_ctx // tm
    tiles_per_req = rows_per_req // tm
    groups = d // HEAD_DIM
    gid = jnp.arange(2 * d, dtype=jnp.int32) // HEAD_DIM
    gs = (gid[:, None] == jnp.arange(LANES, dtype=jnp.int32)[None, :]).astype(BF16)
    gb = gs.T
    assert 2 * groups <= LANES
    cos, sin = _rope_tables(rows_per_req)
    pos_map = lambda i: (jnp.where(i < n_ctx_tiles, 0, (i - n_ctx_tiles) % tiles_per_req), 0)
    ctx_map = lambda i: (jnp.minimum(i, n_ctx_tiles - 1), 0)
    full = lambda shape: pl.BlockSpec(shape, lambda i: (0,) * len(shape))
    row = pl.BlockSpec((tm, d), lambda i: (i, 0))
    kern = functools.partial(_pre_attn_kernel, n_ctx_tiles=n_ctx_tiles, tiles_per_req=tiles_per_req)
    return pl.pallas_call(
        kern,
        grid=(t // tm,),
        in_specs=_split_specs(tm, d, n_ctx_tiles) + [
                  full(mod.shape), full((1, d)), full(wqkv.shape), full((1, d)), full((1, d)),
                  full(gs.shape), full(gb.shape),
                  pl.BlockSpec((tm, LANES), pos_map), pl.BlockSpec((tm, LANES), pos_map)],
        out_specs=[row, row, row,
                   pl.BlockSpec((None, d // HEAD_DIM, HEAD_DIM, tm), lambda i: (ctx_map(i)[0], 0, 0, 0)),
                   pl.BlockSpec((tm, d), ctx_map)],
        out_shape=[jax.ShapeDtypeStruct((t, d), BF16)] * 3 + [
            jax.ShapeDtypeStruct((n_ctx_tiles, d // HEAD_DIM, HEAD_DIM, tm), F32), jax.ShapeDtypeStruct((n_ctx, d), F32)],
        compiler_params=_cparams(1),
        name="pre_attn",
    )(h_ctx, h_lat, mod, g, wqkv, qg, kg, gs, gb, cos, sin)


def _attn_kernel(*refs, n_heads, has_cache, layer):
    if has_cache:
        lam_ref, sub_ref, q_ref, kn_ref, vn_ref, kc_ref, vc_ref, o_ref = refs
    else:
        lam_ref, sub_ref, q_ref, kn_ref, vn_ref, o_ref = refs
    lf = lam_ref[...]
    lam = (jnp.exp(jnp.sum(lf[0:1] * lf[1:2], axis=-1, keepdims=True))
           - jnp.exp(jnp.sum(lf[2:3] * lf[3:4], axis=-1, keepdims=True)) + _lambda_init(layer))
    nt = (((1,), (1,)), ((), ()))
    tq = q_ref.shape[0]
    lane = lax.broadcasted_iota(jnp.int32, (tq, LANES), 1)
    for hh in range(n_heads):
        sl = slice(hh * LANES, (hh + 1) * LANES)
        qh = q_ref[:, sl]
        zero = jnp.zeros_like(qh)
        qs = (jnp.where(lane < HEAD_DIM, qh, zero), jnp.where(lane >= HEAD_DIM, qh, zero))
        keys = [kn_ref[:, sl]]
        vals = [vn_ref[:, sl]]
        if has_cache:
            keys.append(kc_ref[:, sl].astype(BF16))
            vals.append(vc_ref[:, sl].astype(BF16))
        probs = []
        for qm in qs:
            s = [lax.dot_general(qm, kk, nt, preferred_element_type=F32) for kk in keys]
            m = functools.reduce(jnp.maximum, [jnp.max(x, axis=-1, keepdims=True) for x in s])
            e = [jnp.exp(x - m) for x in s]
            den = functools.reduce(lambda u, w: u + w, [jnp.sum(x, axis=-1, keepdims=True) for x in e])
            probs.append((e, 1.0 / den))
        (e0, r0), (e1, r1) = probs
        o = None
        for j, vv in enumerate(vals):
            aj = e0[j] * r0 - lam * (e1[j] * r1)
            oj = _dot(aj.astype(BF16), vv)
            o = oj if o is None else o + oj
        o = o * lax.rsqrt(jnp.mean(o * o, axis=-1, keepdims=True) + EPS)
        o = (o * sub_ref[...]) * (1.0 - _lambda_init(layer))
        o_ref[:, sl] = o.astype(BF16)


def _attention(q, k, v, lam_p, subln, cache_k, cache_v, n_ctx, ctx_len, n_req, req_len, layer):
    t, d = q.shape
    small = lambda shape: pl.BlockSpec(shape, lambda *_: (0,) * len(shape))
    n_ctx_req = n_ctx // ctx_len
    blk = pl.BlockSpec((ctx_len, d), lambda b: (b, 0))
    o_ctx = pl.pallas_call(
        functools.partial(_attn_kernel, n_heads=N_HEADS, has_cache=False, layer=layer),
        grid=(n_ctx_req,),
        in_specs=[small(lam_p.shape), small(subln.shape), blk, blk, blk],
        out_specs=blk,
        out_shape=jax.ShapeDtypeStruct((n_ctx, d), BF16),
        compiler_params=_cparams(1),
        name="attn_ctx",
    )(lam_p, subln, q, k, v)

    tq = ATTN_Q_TILE
    nq = req_len // tq
    q0 = n_ctx // tq
    r0 = n_ctx // req_len
    past = cache_k.shape[1]
    hp = ATTN_HEADS_PER_STEP
    small3 = lambda shape: pl.BlockSpec(shape, lambda b, h, i: (0,) * len(shape))
    qspec = pl.BlockSpec((tq, hp * LANES), lambda b, h, i: (q0 + b * nq + i, h))
    kvspec = pl.BlockSpec((req_len, hp * LANES), lambda b, h, i: (r0 + b, h))
    cspec = pl.BlockSpec((None, past, hp * LANES), lambda b, h, i: (b, 0, h))
    o_lat = pl.pallas_call(
        functools.partial(_attn_kernel, n_heads=hp, has_cache=True, layer=layer),
        grid=(n_req, N_HEADS // hp, nq),
        in_specs=[small3(lam_p.shape), pl.BlockSpec((1, LANES), lambda b, h, i: (0, 0)),
                  qspec, kvspec, kvspec, cspec, cspec],
        out_specs=pl.BlockSpec((tq, hp * LANES), lambda b, h, i: (b * nq + i, h)),
        out_shape=jax.ShapeDtypeStruct((n_req * req_len, d), BF16),
        compiler_params=_cparams(3),
        name="attn_latent",
    )(lam_p, subln, q, k, v, cache_k, cache_v)
    return o_ctx, o_lat


def _route(a2, wr, br, ltri, carry_ref):
    tm = a2.shape[0]
    logits = _dot(a2.astype(BF16), wr) + br
    lane_i = lax.broadcasted_iota(jnp.int32, (tm, LANES), 1)
    lane = lane_i.astype(F32)
    neg = jnp.full((tm, LANES), -jnp.inf, F32)
    big = jnp.full((tm, LANES), float(LANES), F32)
    first_lane = lambda mask: jnp.min(jnp.where(mask, lane, big), axis=-1, keepdims=True)

    lc = jnp.where(lane_i < N_EXPERT_GROUPS, logits, neg)
    mc = jnp.max(lc, axis=-1, keepdims=True)
    pg = 1.0 / jnp.sum(jnp.exp(lc - mc), axis=-1, keepdims=True)
    gi = first_lane(lc == mc)
    assert EXPERTS_PER_GROUP == 8
    grp = lax.shift_right_arithmetic(lane_i - L_EXPERT0, 3).astype(F32)
    in_group = (lane_i >= L_EXPERT0) & (lane_i < L_EXPERT0 + N_EXPERTS) & (grp == gi)
    ls = jnp.where(in_group, logits, neg)
    t1 = jnp.max(ls, axis=-1, keepdims=True)
    i1 = first_lane(ls == t1)
    ls2 = jnp.where(lane == i1, neg, ls)
    t2 = jnp.max(ls2, axis=-1, keepdims=True)
    i2 = first_lane(ls2 == t2)
    ex = jnp.exp(t2 - t1)
    w1 = pg * (1.0 / (1.0 + ex))
    w2 = pg * (ex / (1.0 + ex))
    e1 = i1 - float(L_EXPERT0)
    e2 = i2 - float(L_EXPERT0)
    oh1 = lane == e1
    oh2 = lane == e2
    onehot = oh1.astype(F32) + oh2.astype(F32)
    before = _dot(ltri, onehot.astype(BF16)) + carry_ref[...]
    zero = jnp.zeros_like(before)
    rank1 = jnp.sum(jnp.where(oh1, before, zero), axis=-1, keepdims=True)
    rank2 = jnp.sum(jnp.where(oh2, before, zero), axis=-1, keepdims=True)
    carry_ref[...] = carry_ref[...] + jnp.sum(onehot, axis=0, keepdims=True)
    slab = jnp.zeros((tm, LANES), F32)
    for ln, val in ((R_E1, e1), (R_E2, e2), (R_W1, w1), (R_W2, w2), (R_RANK1, rank1), (R_RANK2, rank2)):
        slab = jnp.where(lane_i == ln, val, slab)
    return slab


def _router_weights(wc, bc, wf, bf_):
    d = wc.shape[0]
    pad = LANES - N_EXPERT_GROUPS - N_EXPERTS
    w = jnp.concatenate([wc, wf, jnp.zeros((d, pad), F32)], axis=1)
    b = jnp.concatenate([bc, bf_, jnp.zeros((pad,), F32)])[None, :]
    return w.astype(BF16), b


def _lower_tri(n):
    r = jnp.arange(n, dtype=jnp.int32)
    return (r[None, :] < r[:, None]).astype(BF16)


def _post_attn_kernel(hc_ref, hl_ref, oc_ref, ol_ref, mod_ref, wo_ref, g2_ref, wr_ref, br_ref, ltri_ref,
                      h1_ref, a2_ref, route_ref, cnt_ref, carry_ref, *, n_ctx_tiles, tiles_per_req):
    i = pl.program_id(0)
    seq = _seq_of_tile(i, n_ctx_tiles, tiles_per_req)

    @pl.when(i == 0)
    def _():
        carry_ref[...] = jnp.zeros_like(carry_ref)

    for s in range(h1_ref.shape[0] // ROW_TILE):
        rows = pl.ds(s * ROW_TILE, ROW_TILE)
        o = jnp.where(i < n_ctx_tiles, oc_ref[rows, :], ol_ref[rows, :])
        h = jnp.where(i < n_ctx_tiles, hc_ref[rows, :], hl_ref[rows, :])
        h1 = h + _mod_row(mod_ref, 2, seq) * _dot(o, wo_ref[...])
        h1_ref[rows, :] = h1
        a2 = _modulated(h1, g2_ref[...], _mod_row(mod_ref, 3, seq), _mod_row(mod_ref, 4, seq))
        a2_ref[rows, :] = _pack_bf16_pairs(a2)
        route_ref[rows, :] = _route(a2, wr_ref[...], br_ref[...], ltri_ref[...], carry_ref)
    cnt_ref[...] = carry_ref[...]


def _post_attn(h_ctx, h_lat, o_ctx, o_lat, mod, wo, g2, router, rows_per_req):
    n_ctx, d = h_ctx.shape
    t = n_ctx + h_lat.shape[0]
    tm = ROW_TILE * SUBTILES
    wr, br = router
    ltri = _lower_tri(ROW_TILE)
    full = lambda shape: pl.BlockSpec(shape, lambda i: (0,) * len(shape))
    row = pl.BlockSpec((tm, d), lambda i: (i, 0))
    kern = functools.partial(_post_attn_kernel, n_ctx_tiles=n_ctx // tm, tiles_per_req=rows_per_req // tm)
    return pl.pallas_call(
        kern,
        grid=(t // tm,),
        in_specs=_split_specs(tm, d, n_ctx // tm) + _split_specs(tm, d, n_ctx // tm) + [
                  full(mod.shape), full(wo.shape), full((1, d)), full(wr.shape),
                  full(br.shape), full(ltri.shape)],
        out_specs=[row, pl.BlockSpec((tm, d // 2), lambda i: (i, 0)), pl.BlockSpec((tm, LANES), lambda i: (i, 0)),
                   full((1, LANES))],
        out_shape=[jax.ShapeDtypeStruct((t, d), F32), jax.ShapeDtypeStruct((t, d // 2), jnp.int32),
                   jax.ShapeDtypeStruct((t, LANES), F32), jax.ShapeDtypeStruct((1, LANES), F32)],
        scratch_shapes=[pltpu.VMEM((1, LANES), F32)],
        compiler_params=_cparams(1),
        name="post_attn_router",
    )(h_ctx, h_lat, o_ctx, o_lat, mod, wo, g2, wr, br, ltri)


def _gather_rows(src, idx):
    n = idx.shape[0]
    d = src.shape[1]
    w = _sc_window_rows(src)
    mesh = plsc.VectorSubcoreMesh(core_axis_name="core", subcore_axis_name="subcore")

    @pl.kernel(out_type=jax.ShapeDtypeStruct((n, d), src.dtype), mesh=mesh)
    def gather(src_hbm, idx_hbm, out_hbm):
        def body(idx_vmem, out_vmem):
            pltpu.sync_copy(src_hbm.at[idx_vmem.at[0, pl.ds(0, w)]], out_vmem)

        pltpu.emit_pipeline(
            body,
            grid=(n // w,),
            in_specs=[pl.BlockSpec((1, LANES), lambda i: (i, 0))],
            out_specs=[pl.BlockSpec((w, d), lambda i: (i, 0))],
            core_axis_name=("core", "subcore"),
            dimension_semantics=(pltpu.PARALLEL,),
        )(idx_hbm, out_hbm)

    return gather(src, _index_windows(idx, w))


def _sc_window_rows(src):
    return min(LANES, SC_WINDOW_BYTES // (src.shape[1] * src.dtype.itemsize))


def _index_windows(idx, w):
    return jnp.pad(idx.reshape(idx.shape[0] // w, w), ((0, 0), (0, LANES - w)))


def _scatter_rows_twice(src, idx_a, idx_b, n_out):
    t, d = src.shape
    w = _sc_window_rows(src)
    mesh = plsc.VectorSubcoreMesh(core_axis_name="core", subcore_axis_name="subcore")

    @pl.kernel(out_type=jax.ShapeDtypeStruct((n_out, d), src.dtype), mesh=mesh)
    def scatter(src_hbm, ia_hbm, ib_hbm, out_hbm):
        def body(src_vmem, ia_vmem, ib_vmem):
            pltpu.sync_copy(src_vmem, out_hbm.at[ia_vmem.at[0, pl.ds(0, w)]])
            pltpu.sync_copy(src_vmem, out_hbm.at[ib_vmem.at[0, pl.ds(0, w)]])

        pltpu.emit_pipeline(
            body,
            grid=(t // w,),
            in_specs=[pl.BlockSpec((w, d), lambda i: (i, 0)),
                      pl.BlockSpec((1, LANES), lambda i: (i, 0)),
                      pl.BlockSpec((1, LANES), lambda i: (i, 0))],
            out_specs=[],
            core_axis_name=("core", "subcore"),
            dimension_semantics=(pltpu.PARALLEL,),
        )(src_hbm, ia_hbm, ib_hbm)

    return scatter(src, _index_windows(idx_a, w), _index_windows(idx_b, w))


def _expert_kernel(be_ref, nb_ref, ne_ref, x_ref, w1_hbm, w3_hbm, w2_hbm, y_ref,
                   w1s, w3s, w2s, w1b, w3b, w2b, sems, *, layer):
    b = pl.program_id(0)
    prev = be_ref[jnp.maximum(b - 1, 0)]
    used = b < nb_ref[0]

    def stage(e):
        return [pltpu.make_async_copy(src.at[layer, e], dst, sems.at[j])
                for j, (src, dst) in enumerate(((w1_hbm, w1s), (w3_hbm, w3s), (w2_hbm, w2s)))]

    @pl.when(used & ((b == 0) | (be_ref[b] != prev)))
    def _():
        @pl.when(b == 0)
        def _():
            for c in stage(be_ref[0]):
                c.start()

        for c in stage(be_ref[b]):
            c.wait()
        w1b[...] = w1s[...].astype(BF16)
        w3b[...] = w3s[...].astype(BF16)
        w2b[...] = w2s[...].astype(BF16)

        @pl.when(ne_ref[b] >= 0)
        def _():
            for c in stage(ne_ref[b]):
                c.start()

    @pl.when(used)
    def _():
        x_lo, x_hi = _unpack_bf16_pairs(x_ref[...])
        half = x_lo.shape[1]
        h1 = _dot(x_lo, w1b[:half, :]) + _dot(x_hi, w1b[half:, :])
        h3 = _dot(x_lo, w3b[:half, :]) + _dot(x_hi, w3b[half:, :])
        hb = jax.nn.silu(h1) * h3
        y_ref[...] = _dot(hb.astype(BF16), w2b[...])


def _expert_mlps(xs, block_e, n_used, next_e, w1, w3, w2, layer):
    npad, half = xs.shape
    d, ff = w1.shape[2], w1.shape[3]
    assert d == 2 * half
    bm = MOE_BLOCK
    rows = lambda b, be, nb, ne: (jnp.minimum(b, nb[0] - 1), 0)
    hbm = pl.BlockSpec(memory_space=pl.ANY)
    grid_spec = pltpu.PrefetchScalarGridSpec(
        num_scalar_prefetch=3,
        grid=(npad // bm,),
        in_specs=[pl.BlockSpec((bm, half), rows), hbm, hbm, hbm],
        out_specs=pl.BlockSpec((bm, d), rows),
        scratch_shapes=[pltpu.VMEM((d, ff), F32), pltpu.VMEM((d, ff), F32), pltpu.VMEM((ff, d), F32),
                        pltpu.VMEM((d, ff), BF16), pltpu.VMEM((d, ff), BF16), pltpu.VMEM((ff, d), BF16),
                        pltpu.SemaphoreType.DMA((3,))],
    )
    return pl.pallas_call(
        functools.partial(_expert_kernel, layer=layer),
        grid_spec=grid_spec,
        out_shape=jax.ShapeDtypeStruct((npad, d), F32),
        compiler_params=_cparams(1),
        name="expert_mlps",
    )(block_e, n_used, next_e, xs, w1, w3, w2)


def _moe(a2p, route, counts, w1, w3, w2, layer):
    t = a2p.shape[0]
    bm = MOE_BLOCK
    n = 2 * t
    e = route[:, R_E1:R_E2 + 1].astype(jnp.int32).reshape(n)
    rank = route[:, R_RANK1:R_RANK2 + 1].astype(jnp.int32).reshape(n)
    cnt = counts[0, :N_EXPERTS].astype(jnp.int32)
    padded = ((cnt + bm - 1) // bm) * bm
    pend = jnp.cumsum(padded)
    pstart = pend - padded
    pos = jnp.sum(jnp.where(e[:, None] == jnp.arange(N_EXPERTS, dtype=jnp.int32)[None, :], pstart[None, :], 0),
                  axis=1) + rank
    n_blocks = n // bm + N_EXPERTS
    npad = n_blocks * bm
    starts = jnp.arange(n_blocks, dtype=jnp.int32) * bm
    n_used = pend[-1:] // bm
    last_start = (n_used[0] - 1) * bm
    block_e = jnp.sum((pend[None, :] <= jnp.minimum(starts, last_start)[:, None]).astype(jnp.int32), axis=1)
    blk = jnp.arange(n_blocks, dtype=jnp.int32)
    later_other = (blk[None, :] > blk[:, None]) & (block_e[None, :] != block_e[:, None]) & (blk[None, :] < n_used[0])
    first_later = jnp.min(jnp.where(later_other, blk[None, :], n_blocks), axis=1)
    next_e = jnp.where(first_later < n_blocks, block_e[jnp.minimum(first_later, n_blocks - 1)], -1).astype(jnp.int32)
    pos_first, pos_second = pos[0::2], pos[1::2]
    xs = _scatter_rows_twice(a2p, pos_first, pos_second, npad)
    ys = _expert_mlps(xs, block_e, n_used, next_e, w1, w3, w2, layer)
    return _gather_rows(ys, jnp.concatenate([pos_first, pos_second]))


def _moe_combine(h, y_first, y_second, route, g2):
    w1 = route[:, R_W1:R_W1 + 1]
    w2 = route[:, R_W2:R_W2 + 1]
    return h + g2 * (w1 * y_first + w2 * y_second)


def _gmlp_kernel(h_ref, ya_ref, yb_ref, rt_ref, modp_ref, mod_ref, g1_ref, win_ref, bin_ref, vg_ref, ws_ref, bsb_ref,
                 wout_ref, g2_ref, wr_ref, br_ref, ltri_ref,
                 h1_ref, a2_ref, route_ref, cnt_ref, carry_ref, *, n_ctx_tiles, tiles_per_req):
    i = pl.program_id(0)
    seq = _seq_of_tile(i, n_ctx_tiles, tiles_per_req)

    @pl.when(i == 0)
    def _():
        carry_ref[...] = jnp.zeros_like(carry_ref)

    gw = vg_ref.shape[1]
    cg = gw // GM_GROUPS
    for s in range(h_ref.shape[0] // GMLP_SUB):
        rs = pl.ds(s * GMLP_SUB, GMLP_SUB)
        h = _moe_combine(h_ref[rs, :], ya_ref[rs, :], yb_ref[rs, :], rt_ref[rs, :], _mod_row(modp_ref, 5, seq))
        a = _modulated(h, g1_ref[...], _mod_row(mod_ref, 0, seq), _mod_row(mod_ref, 1, seq))
        z = _gelu_tanh(_dot(a.astype(BF16), win_ref[...]) + bin_ref[...])
        u = z[:, :gw]
        v = z[:, gw:]
        v = (v * lax.rsqrt(jnp.mean(v * v, axis=-1, keepdims=True) + EPS)) * vg_ref[...]
        vb = v.astype(BF16)
        rows = []
        for c in range(GMLP_SUB // CHUNK):
            cols = []
            for g in range(GM_GROUPS):
                cols.append(_dot(ws_ref[g], vb[c * CHUNK:(c + 1) * CHUNK, g * cg:(g + 1) * cg]))
            rows.append(jnp.concatenate(cols, axis=1) + bsb_ref[...])
        vm = jnp.concatenate(rows, axis=0)
        mix = _dot((u * vm).astype(BF16), wout_ref[...])
        h1 = h + _mod_row(mod_ref, 2, seq) * mix
        h1_ref[rs, :] = h1
        a2 = _modulated(h1, g2_ref[...], _mod_row(mod_ref, 3, seq), _mod_row(mod_ref, 4, seq))
        a2_ref[rs, :] = _pack_bf16_pairs(a2)
        route_ref[rs, :] = _route(a2, wr_ref[...], br_ref[...], ltri_ref[...], carry_ref)
    cnt_ref[...] = carry_ref[...]


def _gmlp_layer(h, y2, route_prev, mod_prev, mod, g1, win, bin_, vg, ws, bsb, wout, g2, router,
                n_ctx, rows_per_req):
    t, d = h.shape
    tm = ROW_TILE * SUBTILES
    wr, br = router
    ltri = _lower_tri(GMLP_SUB)
    full = lambda shape: pl.BlockSpec(shape, lambda i: (0,) * len(shape), pipeline_mode=pl.Buffered(1))
    row = lambda w: pl.BlockSpec((tm, w), lambda i: (i, 0))
    kern = functools.partial(_gmlp_kernel, n_ctx_tiles=n_ctx // tm, tiles_per_req=rows_per_req // tm)
    args = (h, y2, y2, route_prev, mod_prev, mod, g1, win, bin_, vg, ws, bsb, wout, g2, wr, br, ltri)
    second = pl.BlockSpec((tm, d), lambda i: (i + t // tm, 0))
    in_specs = [row(d), row(d), second, row(LANES)] + [full(a.shape) for a in args[4:]]
    return pl.pallas_call(
        kern,
        grid=(t // tm,),
        in_specs=in_specs,
        out_specs=[row(d), row(d // 2), row(LANES), pl.BlockSpec((1, LANES), lambda i: (0, 0))],
        out_shape=[jax.ShapeDtypeStruct((t, d), F32), jax.ShapeDtypeStruct((t, d // 2), jnp.int32),
                   jax.ShapeDtypeStruct((t, LANES), F32), jax.ShapeDtypeStruct((1, LANES), F32)],
        scratch_shapes=[pltpu.VMEM((1, LANES), F32)],
        compiler_params=_cparams(1),
        name="gmlp_router",
    )(*args)


def _final_kernel(h_ref, ya_ref, yb_ref, rt_ref, mod_ref, oc_ref, ol_ref, *, n_ctx_tiles, tiles_per_req):
    i = pl.program_id(0)
    seq = _seq_of_tile(i, n_ctx_tiles, tiles_per_req)
    out = _moe_combine(h_ref[...], ya_ref[...], yb_ref[...], rt_ref[...], _mod_row(mod_ref, 5, seq))

    @pl.when(i < n_ctx_tiles)
    def _():
        oc_ref[...] = out

    @pl.when(i >= n_ctx_tiles)
    def _():
        ol_ref[...] = out


def _final_combine(h, y2, route, mod, n_ctx, rows_per_req):
    t, d = h.shape
    tm = ROW_TILE
    row = lambda w: pl.BlockSpec((tm, w), lambda i: (i, 0))
    kern = functools.partial(_final_kernel, n_ctx_tiles=n_ctx // tm, tiles_per_req=rows_per_req // tm)
    return pl.pallas_call(
        kern,
        grid=(t // tm,),
        in_specs=[row(d), row(d), pl.BlockSpec((tm, d), lambda i: (i + t // tm, 0)), row(LANES),
                  pl.BlockSpec(mod.shape, lambda i: (0, 0, 0))],
        out_specs=_split_specs(tm, d, n_ctx // tm),
        out_shape=[jax.ShapeDtypeStruct((n_ctx, d), F32), jax.ShapeDtypeStruct((t - n_ctx, d), F32)],
        compiler_params=_cparams(1),
        name="final_combine",
    )(h, y2, y2, route, mod)


def kernel(x_prompt, x_sample, cache_k, cache_v, c, c_ctx, ada_w, ada_b, norm1_g, norm2_g, attn_wq, attn_wk,
           attn_wv, attn_wo, attn_qnorm, attn_knorm, attn_lam, attn_subln, gm_win, gm_bin, gm_vnorm, gm_ws,
           gm_bs, gm_wout, moe_wc, moe_bc, moe_wf, moe_bf, moe_w1, moe_w3, moe_w2):
    batch, seq_len, d = x_prompt.shape
    n_req, req_len, _ = x_sample.shape
    depth = ada_w.shape[0]
    assert depth == 2 and attn_wq.shape[0] == 1 and gm_win.shape[0] == 1
    n_ctx = batch * seq_len
    assert n_ctx % req_len == 0 and req_len % ROW_TILE == 0 and seq_len == ROW_TILE
    past = cache_k.shape[2]

    h_ctx = x_prompt.reshape(n_ctx, d)
    h_lat = x_sample.reshape(n_req * req_len, d)
    cvec = jnp.concatenate([c_ctx[None, :], c, jnp.zeros((8 - 1 - n_req, d), F32)], axis=0)
    mod = _ada_mod(cvec, ada_w, ada_b)
    routers = [_router_weights(moe_wc[i], moe_bc[i], moe_wf[i], moe_bf[i]) for i in range(depth)]
    row_vec = lambda v: v.reshape(1, -1)

    wqkv = jnp.concatenate([attn_wq[0], attn_wk[0], attn_wv[0]], axis=1).astype(BF16)
    reps = d // HEAD_DIM
    q, k, v, k_new, v_new = _pre_attn(h_ctx, h_lat, mod[0], row_vec(norm1_g[0]), wqkv,
                                      row_vec(jnp.tile(attn_qnorm[0], reps)), row_vec(jnp.tile(attn_knorm[0], reps)),
                                      req_len)
    o_ctx, o_lat = _attention(q, k, v, attn_lam[0], row_vec(attn_subln[0]),
                              cache_k[:, 0].reshape(n_req, past, d), cache_v[:, 0].reshape(n_req, past, d),
                              n_ctx, seq_len, n_req, req_len, layer=0)
    h1, a2, route0, cnt0 = _post_attn(h_ctx, h_lat, o_ctx, o_lat, mod[0], attn_wo[0].astype(BF16),
                                      row_vec(norm2_g[0]), routers[0], req_len)
    y2 = _moe(a2, route0, cnt0, moe_w1, moe_w3, moe_w2, layer=0)

    gw = gm_vnorm.shape[1]
    bsb = jnp.repeat(gm_bs[0].T, gw // GM_GROUPS, axis=1)
    h2, a2, route1, cnt1 = _gmlp_layer(h1, y2, route0, mod[0], mod[1], row_vec(norm1_g[1]), gm_win[0].astype(BF16),
                                       row_vec(gm_bin[0]), row_vec(gm_vnorm[0]), gm_ws[0].astype(BF16), bsb,
                                       gm_wout[0].astype(BF16), row_vec(norm2_g[1]), routers[1], n_ctx, req_len)
    y2 = _moe(a2, route1, cnt1, moe_w1, moe_w3, moe_w2, layer=1)
    y_ctx, y_lat = _final_combine(h2, y2, route1, mod[1], n_ctx, req_len)

    y_prompt = y_ctx.reshape(batch, seq_len, d)
    y_sample = y_lat.reshape(n_req, req_len, d)
    new_cache_k = jnp.transpose(k_new.reshape(batch, 1, N_HEADS, 2, HEAD_DIM, seq_len), (0, 1, 5, 2, 3, 4))
    new_cache_v = v_new.reshape(batch, 1, seq_len, N_HEADS, V_DIM)
    return (y_prompt, y_sample, new_cache_k, new_cache_v)
```

```python
import functools
import math

import jax
import jax.numpy as jnp
from jax import lax
from jax.experimental import pallas as pl
from jax.experimental.pallas import tpu as pltpu
from jax.experimental.pallas import tpu_sc as plsc

F32 = jnp.float32
BF16 = jnp.bfloat16

N_HEADS = 8
HEAD_DIM = 64
V_DIM = 2 * HEAD_DIM
GRID_W = 64
ROPE_THETA = 10000.0
CHUNK = 128
GM_GROUPS = 8
N_EXPERT_GROUPS = 4
EXPERTS_PER_GROUP = 8
N_EXPERTS = N_EXPERT_GROUPS * EXPERTS_PER_GROUP
EPS = 1e-6

LANES = 128
ROW_TILE = 256
SUBTILES = 2
GMLP_SUB = 512
MOE_BLOCK = 256
SC_WINDOW_BYTES = 128 * 1024
ATTN_Q_TILE = 256
ATTN_HEADS_PER_STEP = 4
VMEM_LIMIT = 56 * 1024 * 1024

R_E1, R_E2, R_W1, R_W2, R_RANK1, R_RANK2 = 0, 1, 2, 3, 4, 5
L_EXPERT0 = N_EXPERT_GROUPS


def _lambda_init(layer):
    return 0.8 - 0.6 * math.exp(-0.3 * layer)


def _cparams(n_axes):
    return pltpu.CompilerParams(dimension_semantics=("arbitrary",) * n_axes, vmem_limit_bytes=VMEM_LIMIT)


def _seq_of_tile(i, n_ctx_tiles, tiles_per_req):
    return jnp.where(i < n_ctx_tiles, 0, 1 + (i - n_ctx_tiles) // tiles_per_req)


def _pick_rows(i, n_ctx_tiles, ctx_ref, lat_ref):
    return jnp.where(i < n_ctx_tiles, ctx_ref[...], lat_ref[...])


def _split_specs(tm, d, n_ctx_tiles):
    return [pl.BlockSpec((tm, d), lambda i: (jnp.minimum(i, n_ctx_tiles - 1), 0)),
            pl.BlockSpec((tm, d), lambda i: (jnp.maximum(i - n_ctx_tiles, 0), 0))]


def _mod_row(mod_ref, part, seq):
    return mod_ref[part, pl.ds(seq, 1), :]


def _modulated(x, g, shift, scale):
    y = x * lax.rsqrt(jnp.mean(x * x, axis=-1, keepdims=True) + EPS)
    return (y * g) * (1.0 + scale) + shift


def _split_bf16(x):
    hi = x.astype(BF16)
    lo = (x - hi.astype(F32)).astype(BF16)
    return hi, lo


def _gelu_tanh(x):
    c = math.sqrt(2.0 / math.pi)
    hx = 0.5 * x
    return hx + hx * jnp.tanh(x * (c + (c * 0.044715) * (x * x)))


def _dot(a, b):
    return jnp.dot(a, b, preferred_element_type=F32)


def _pack_bf16_pairs(x):
    half = x.shape[1] // 2
    bits = lax.bitcast_convert_type(x.astype(BF16).astype(F32), jnp.uint32)
    packed = (bits[:, :half] >> 16) | bits[:, half:]
    return lax.bitcast_convert_type(packed, jnp.int32)


def _unpack_bf16_pairs(p):
    u = lax.bitcast_convert_type(p, jnp.uint32)
    lo = lax.bitcast_convert_type(u << 16, F32).astype(BF16)
    hi = lax.bitcast_convert_type(u & jnp.uint32(0xFFFF0000), F32).astype(BF16)
    return lo, hi


def _ada_kernel(c_ref, w_ref, b_ref, o_ref):
    c = c_ref[...]
    s = c * jax.nn.sigmoid(c)
    o_ref[...] = _dot(s.astype(BF16), w_ref[...].astype(BF16)) + b_ref[...]


def _ada_mod(cvec, ada_w, ada_b):
    depth, d, d6 = ada_w.shape
    parts = d6 // d
    rows = cvec.shape[0]
    return pl.pallas_call(
        _ada_kernel,
        grid=(depth, parts),
        in_specs=[
            pl.BlockSpec((rows, d), lambda l, j: (0, 0)),
            pl.BlockSpec((None, d, d), lambda l, j: (l, 0, j)),
            pl.BlockSpec((None, 1, d), lambda l, j: (l, 0, j)),
        ],
        out_specs=pl.BlockSpec((None, None, rows, d), lambda l, j: (l, j, 0, 0)),
        out_shape=jax.ShapeDtypeStruct((depth, parts, rows, d), F32),
        compiler_params=_cparams(2),
        name="ada_mod",
    )(cvec, ada_w, ada_b.reshape(depth, 1, d6))


def _pre_attn_kernel(hc_ref, hl_ref, mod_ref, g_ref, w_ref, qg_ref, kg_ref, gs_ref, gb_ref, cos_ref, sin_ref,
                     q_ref, k_ref, v_ref, kf_ref, vf_ref, *, n_ctx_tiles, tiles_per_req):
    i = pl.program_id(0)
    seq = _seq_of_tile(i, n_ctx_tiles, tiles_per_req)
    d = hc_ref.shape[1]
    x = _pick_rows(i, n_ctx_tiles, hc_ref, hl_ref)
    a = _modulated(x, g_ref[...], _mod_row(mod_ref, 0, seq), _mod_row(mod_ref, 1, seq))
    qkv = _dot(a.astype(BF16), w_ref[...])
    q = qkv[:, :d]
    k = qkv[:, d:2 * d]
    v = qkv[:, 2 * d:]
    sq_hi, sq_lo = _split_bf16(jnp.concatenate([q * q, k * k], axis=1))
    gsum = _dot(sq_hi, gs_ref[...]) + _dot(sq_lo, gs_ref[...])
    r_hi, r_lo = _split_bf16(lax.rsqrt(gsum * (1.0 / HEAD_DIM) + EPS))
    rb = _dot(r_hi, gb_ref[...]) + _dot(r_lo, gb_ref[...])
    qn = (q * rb[:, :d]) * qg_ref[...]
    kn = (k * rb[:, d:]) * kg_ref[...]
    v_ref[...] = v.astype(BF16)
    q_scale = HEAD_DIM ** -0.5

    @pl.when(i < n_ctx_tiles)
    def _():
        q_ref[...] = (qn * q_scale).astype(BF16)
        k_ref[...] = kn.astype(BF16)
        kf_ref[...] = kn.T.reshape(kf_ref.shape)
        vf_ref[...] = v

    @pl.when(i >= n_ctx_tiles)
    def _():
        cos = cos_ref[...]
        sin = sin_ref[...]
        lane = lax.broadcasted_iota(jnp.int32, cos.shape, 1)
        first = (lane & 31) < 16

        def rope(x):
            outs = []
            for hh in range(d // LANES):
                xs = x[:, hh * LANES:(hh + 1) * LANES]
                rot = jnp.where(first, pltpu.roll(xs, LANES - 16, 1), pltpu.roll(xs, 16, 1))
                outs.append(xs * cos + rot * sin)
            return jnp.concatenate(outs, axis=1)

        q_ref[...] = (rope(qn) * q_scale).astype(BF16)
        k_ref[...] = rope(kn).astype(BF16)


def _rope_tables(n_pos):
    half = HEAD_DIM // 2
    pos = jnp.arange(n_pos, dtype=jnp.int32)
    row = (pos // GRID_W).astype(F32)
    col = (pos % GRID_W).astype(F32)
    inv = 1.0 / (ROPE_THETA ** (jnp.arange(0, half, 2, dtype=F32) / half))
    ang_r = row[:, None] * inv[None, :]
    ang_c = col[:, None] * inv[None, :]
    ang = jnp.concatenate([ang_r, ang_r, ang_c, ang_c], axis=-1)
    quarter = half // 2
    sign = jnp.tile(jnp.concatenate([-jnp.ones((quarter,), F32), jnp.ones((quarter,), F32)]), 2)
    cos = jnp.tile(jnp.cos(ang), (1, 2))
    sin = jnp.tile(jnp.sin(ang) * sign[None, :], (1, 2))
    return cos, sin


def _pre_attn(h_ctx, h_lat, mod, g, wqkv, qg, kg, rows_per_req):
    n_ctx, d = h_ctx.shape
    t = n_ctx + h_lat.shape[0]
    tm = ROW_TILE
    n_ctx_tiles = n_ctx // tm
    tiles_per_req = rows_per_req // tm
    groups = d // HEAD_DIM
    gid = jnp.arange(2 * d, dtype=jnp.int32) // HEAD_DIM
    gs = (gid[:, None] == jnp.arange(LANES, dtype=jnp.int32)[None, :]).astype(BF16)
    gb = gs.T
    assert 2 * groups <= LANES
    cos, sin = _rope_tables(rows_per_req)
    pos_map = lambda i: (jnp.where(i < n_ctx_tiles, 0, (i - n_ctx_tiles) % tiles_per_req), 0)
    ctx_map = lambda i: (jnp.minimum(i, n_ctx_tiles - 1), 0)
    full = lambda shape: pl.BlockSpec(shape, lambda i: (0,) * len(shape))
    row = pl.BlockSpec((tm, d), lambda i: (i, 0))
    kern = functools.partial(_pre_attn_kernel, n_ctx_tiles=n_ctx_tiles, tiles_per_req=tiles_per_req)
    return pl.pallas_call(
        kern,
        grid=(t // tm,),
        in_specs=_split_specs(tm, d, n_ctx_tiles) + [
                  full(mod.shape), full((1, d)), full(wqkv.shape), full((1, d)), full((1, d)),
                  full(gs.shape), full(gb.shape),
                  pl.BlockSpec((tm, LANES), pos_map), pl.BlockSpec((tm, LANES), pos_map)],
        out_specs=[row, row, row,
                   pl.BlockSpec((None, d // HEAD_DIM, HEAD_DIM, tm), lambda i: (ctx_map(i)[0], 0, 0, 0)),
                   pl.BlockSpec((tm, d), ctx_map)],
        out_shape=[jax.ShapeDtypeStruct((t, d), BF16)] * 3 + [
            jax.ShapeDtypeStruct((n_ctx_tiles, d // HEAD_DIM, HEAD_DIM, tm), F32), jax.ShapeDtypeStruct((n_ctx, d), F32)],
        compiler_params=_cparams(1),
        name="pre_attn",
    )(h_ctx, h_lat, mod, g, wqkv, qg, kg, gs, gb, cos, sin)


def _attn_kernel(*refs, n_heads, has_cache, layer):
    if has_cache:
        lam_ref, sub_ref, q_ref, kn_ref, vn_ref, kc_ref, vc_ref, o_ref = refs
    else:
        lam_ref, sub_ref, q_ref, kn_ref, vn_ref, o_ref = refs
    lf = lam_ref[...]
    lam = (jnp.exp(jnp.sum(lf[0:1] * lf[1:2], axis=-1, keepdims=True))
           - jnp.exp(jnp.sum(lf[2:3] * lf[3:4], axis=-1, keepdims=True)) + _lambda_init(layer))
    nt = (((1,), (1,)), ((), ()))
    tq = q_ref.shape[0]
    lane = lax.broadcasted_iota(jnp.int32, (tq, LANES), 1)
    for hh in range(n_heads):
        sl = slice(hh * LANES, (hh + 1) * LANES)
        qh = q_ref[:, sl]
        zero = jnp.zeros_like(qh)
        qs = (jnp.where(lane < HEAD_DIM, qh, zero), jnp.where(lane >= HEAD_DIM, qh, zero))
        keys = [kn_ref[:, sl]]
        vals = [vn_ref[:, sl]]
        if has_cache:
            keys.append(kc_ref[:, sl].astype(BF16))
            vals.append(vc_ref[:, sl].astype(BF16))
        probs = []
        for qm in qs:
            s = [lax.dot_general(qm, kk, nt, preferred_element_type=F32) for kk in keys]
            m = functools.reduce(jnp.maximum, [jnp.max(x, axis=-1, keepdims=True) for x in s])
            e = [jnp.exp(x - m) for x in s]
            den = functools.reduce(lambda u, w: u + w, [jnp.sum(x, axis=-1, keepdims=True) for x in e])
            probs.append((e, 1.0 / den))
        (e0, r0), (e1, r1) = probs
        o = None
        for j, vv in enumerate(vals):
            aj = e0[j] * r0 - lam * (e1[j] * r1)
            oj = _dot(aj.astype(BF16), vv)
            o = oj if o is None else o + oj
        o = o * lax.rsqrt(jnp.mean(o * o, axis=-1, keepdims=True) + EPS)
        o = (o * sub_ref[...]) * (1.0 - _lambda_init(layer))
        o_ref[:, sl] = o.astype(BF16)


def _attention(q, k, v, lam_p, subln, cache_k, cache_v, n_ctx, ctx_len, n_req, req_len, layer):
    t, d = q.shape
    small = lambda shape: pl.BlockSpec(shape, lambda *_: (0,) * len(shape))
    n_ctx_req = n_ctx // ctx_len
    blk = pl.BlockSpec((ctx_len, d), lambda b: (b, 0))
    o_ctx = pl.pallas_call(
        functools.partial(_attn_kernel, n_heads=N_HEADS, has_cache=False, layer=layer),
        grid=(n_ctx_req,),
        in_specs=[small(lam_p.shape), small(subln.shape), blk, blk, blk],
        out_specs=blk,
        out_shape=jax.ShapeDtypeStruct((n_ctx, d), BF16),
        compiler_params=_cparams(1),
        name="attn_ctx",
    )(lam_p, subln, q, k, v)

    tq = ATTN_Q_TILE
    nq = req_len // tq
    q0 = n_ctx // tq
    r0 = n_ctx // req_len
    past = cache_k.shape[1]
    hp = ATTN_HEADS_PER_STEP
    small3 = lambda shape: pl.BlockSpec(shape, lambda b, h, i: (0,) * len(shape))
    qspec = pl.BlockSpec((tq, hp * LANES), lambda b, h, i: (q0 + b * nq + i, h))
    kvspec = pl.BlockSpec((req_len, hp * LANES), lambda b, h, i: (r0 + b, h))
    cspec = pl.BlockSpec((None, past, hp * LANES), lambda b, h, i: (b, 0, h))
    o_lat = pl.pallas_call(
        functools.partial(_attn_kernel, n_heads=hp, has_cache=True, layer=layer),
        grid=(n_req, N_HEADS // hp, nq),
        in_specs=[small3(lam_p.shape), pl.BlockSpec((1, LANES), lambda b, h, i: (0, 0)),
                  qspec, kvspec, kvspec, cspec, cspec],
        out_specs=pl.BlockSpec((tq, hp * LANES), lambda b, h, i: (b * nq + i, h)),
        out_shape=jax.ShapeDtypeStruct((n_req * req_len, d), BF16),
        compiler_params=_cparams(3),
        name="attn_latent",
    )(lam_p, subln, q, k, v, cache_k, cache_v)
    return o_ctx, o_lat


def _route(a2, wr, br, ltri, carry_ref):
    tm = a2.shape[0]
    logits = _dot(a2.astype(BF16), wr) + br
    lane_i = lax.broadcasted_iota(jnp.int32, (tm, LANES), 1)
    lane = lane_i.astype(F32)
    neg = jnp.full((tm, LANES), -jnp.inf, F32)
    big = jnp.full((tm, LANES), float(LANES), F32)
    first_lane = lambda mask: jnp.min(jnp.where(mask, lane, big), axis=-1, keepdims=True)

    lc = jnp.where(lane_i < N_EXPERT_GROUPS, logits, neg)
    mc = jnp.max(lc, axis=-1, keepdims=True)
    pg = 1.0 / jnp.sum(jnp.exp(lc - mc), axis=-1, keepdims=True)
    gi = first_lane(lc == mc)
    assert EXPERTS_PER_GROUP == 8
    grp = lax.shift_right_arithmetic(lane_i - L_EXPERT0, 3).astype(F32)
    in_group = (lane_i >= L_EXPERT0) & (lane_i < L_EXPERT0 + N_EXPERTS) & (grp == gi)
    ls = jnp.where(in_group, logits, neg)
    t1 = jnp.max(ls, axis=-1, keepdims=True)
    i1 = first_lane(ls == t1)
    ls2 = jnp.where(lane == i1, neg, ls)
    t2 = jnp.max(ls2, axis=-1, keepdims=True)
    i2 = first_lane(ls2 == t2)
    ex = jnp.exp(t2 - t1)
    w1 = pg * (1.0 / (1.0 + ex))
    w2 = pg * (ex / (1.0 + ex))
    e1 = i1 - float(L_EXPERT0)
    e2 = i2 - float(L_EXPERT0)
    oh1 = lane == e1
    oh2 = lane == e2
    onehot = oh1.astype(F32) + oh2.astype(F32)
    before = _dot(ltri, onehot.astype(BF16)) + carry_ref[...]
    zero = jnp.zeros_like(before)
    rank1 = jnp.sum(jnp.where(oh1, before, zero), axis=-1, keepdims=True)
    rank2 = jnp.sum(jnp.where(oh2, before, zero), axis=-1, keepdims=True)
    carry_ref[...] = carry_ref[...] + jnp.sum(onehot, axis=0, keepdims=True)
    slab = jnp.zeros((tm, LANES), F32)
    for ln, val in ((R_E1, e1), (R_E2, e2), (R_W1, w1), (R_W2, w2), (R_RANK1, rank1), (R_RANK2, rank2)):
        slab = jnp.where(lane_i == ln, val, slab)
    return slab


def _router_weights(wc, bc, wf, bf_):
    d = wc.shape[0]
    pad = LANES - N_EXPERT_GROUPS - N_EXPERTS
    w = jnp.concatenate([wc, wf, jnp.zeros((d, pad), F32)], axis=1)
    b = jnp.concatenate([bc, bf_, jnp.zeros((pad,), F32)])[None, :]
    return w.astype(BF16), b


def _lower_tri(n):
    r = jnp.arange(n, dtype=jnp.int32)
    return (r[None, :] < r[:, None]).astype(BF16)


def _post_attn_kernel(hc_ref, hl_ref, oc_ref, ol_ref, mod_ref, wo_ref, g2_ref, wr_ref, br_ref, ltri_ref,
                      h1_ref, a2_ref, route_ref, cnt_ref, carry_ref, *, n_ctx_tiles, tiles_per_req):
    i = pl.program_id(0)
    seq = _seq_of_tile(i, n_ctx_tiles, tiles_per_req)

    @pl.when(i == 0)
    def _():
        carry_ref[...] = jnp.zeros_like(carry_ref)

    for s in range(h1_ref.shape[0] // ROW_TILE):
        rows = pl.ds(s * ROW_TILE, ROW_TILE)
        o = jnp.where(i < n_ctx_tiles, oc_ref[rows, :], ol_ref[rows, :])
        h = jnp.where(i < n_ctx_tiles, hc_ref[rows, :], hl_ref[rows, :])
        h1 = h + _mod_row(mod_ref, 2, seq) * _dot(o, wo_ref[...])
        h1_ref[rows, :] = h1
        a2 = _modulated(h1, g2_ref[...], _mod_row(mod_ref, 3, seq), _mod_row(mod_ref, 4, seq))
        a2_ref[rows, :] = _pack_bf16_pairs(a2)
        route_ref[rows, :] = _route(a2, wr_ref[...], br_ref[...], ltri_ref[...], carry_ref)
    cnt_ref[...] = carry_ref[...]


def _post_attn(h_ctx, h_lat, o_ctx, o_lat, mod, wo, g2, router, rows_per_req):
    n_ctx, d = h_ctx.shape
    t = n_ctx + h_lat.shape[0]
    tm = ROW_TILE * SUBTILES
    wr, br = router
    ltri = _lower_tri(ROW_TILE)
    full = lambda shape: pl.BlockSpec(shape, lambda i: (0,) * len(shape))
    row = pl.BlockSpec((tm, d), lambda i: (i, 0))
    kern = functools.partial(_post_attn_kernel, n_ctx_tiles=n_ctx // tm, tiles_per_req=rows_per_req // tm)
    return pl.pallas_call(
        kern,
        grid=(t // tm,),
        in_specs=_split_specs(tm, d, n_ctx // tm) + _split_specs(tm, d, n_ctx // tm) + [
                  full(mod.shape), full(wo.shape), full((1, d)), full(wr.shape),
                  full(br.shape), full(ltri.shape)],
        out_specs=[row, pl.BlockSpec((tm, d // 2), lambda i: (i, 0)), pl.BlockSpec((tm, LANES), lambda i: (i, 0)),
                   full((1, LANES))],
        out_shape=[jax.ShapeDtypeStruct((t, d), F32), jax.ShapeDtypeStruct((t, d // 2), jnp.int32),
                   jax.ShapeDtypeStruct((t, LANES), F32), jax.ShapeDtypeStruct((1, LANES), F32)],
        scratch_shapes=[pltpu.VMEM((1, LANES), F32)],
        compiler_params=_cparams(1),
        name="post_attn_router",
    )(h_ctx, h_lat, o_ctx, o_lat, mod, wo, g2, wr, br, ltri)


def _gather_rows(src, idx):
    n = idx.shape[0]
    d = src.shape[1]
    w = _sc_window_rows(src)
    mesh = plsc.VectorSubcoreMesh(core_axis_name="core", subcore_axis_name="subcore")

    @pl.kernel(out_type=jax.ShapeDtypeStruct((n, d), src.dtype), mesh=mesh)
    def gather(src_hbm, idx_hbm, out_hbm):
        def body(idx_vmem, out_vmem):
            pltpu.sync_copy(src_hbm.at[idx_vmem.at[0, pl.ds(0, w)]], out_vmem)

        pltpu.emit_pipeline(
            body,
            grid=(n // w,),
            in_specs=[pl.BlockSpec((1, LANES), lambda i: (i, 0))],
            out_specs=[pl.BlockSpec((w, d), lambda i: (i, 0))],
            core_axis_name=("core", "subcore"),
            dimension_semantics=(pltpu.PARALLEL,),
        )(idx_hbm, out_hbm)

    return gather(src, _index_windows(idx, w))


def _sc_window_rows(src):
    return min(LANES, SC_WINDOW_BYTES // (src.shape[1] * src.dtype.itemsize))


def _index_windows(idx, w):
    return jnp.pad(idx.reshape(idx.shape[0] // w, w), ((0, 0), (0, LANES - w)))


def _scatter_rows_twice(src, idx_a, idx_b, n_out):
    t, d = src.shape
    w = _sc_window_rows(src)
    mesh = plsc.VectorSubcoreMesh(core_axis_name="core", subcore_axis_name="subcore")

    @pl.kernel(out_type=jax.ShapeDtypeStruct((n_out, d), src.dtype), mesh=mesh)
    def scatter(src_hbm, ia_hbm, ib_hbm, out_hbm):
        def body(src_vmem, ia_vmem, ib_vmem):
            pltpu.sync_copy(src_vmem, out_hbm.at[ia_vmem.at[0, pl.ds(0, w)]])
            pltpu.sync_copy(src_vmem, out_hbm.at[ib_vmem.at[0, pl.ds(0, w)]])

        pltpu.emit_pipeline(
            body,
            grid=(t // w,),
            in_specs=[pl.BlockSpec((w, d), lambda i: (i, 0)),
                      pl.BlockSpec((1, LANES), lambda i: (i, 0)),
                      pl.BlockSpec((1, LANES), lambda i: (i, 0))],
            out_specs=[],
            core_axis_name=("core", "subcore"),
            dimension_semantics=(pltpu.PARALLEL,),
        )(src_hbm, ia_hbm, ib_hbm)

    return scatter(src, _index_windows(idx_a, w), _index_windows(idx_b, w))


def _expert_kernel(be_ref, nb_ref, ne_ref, x_ref, w1_hbm, w3_hbm, w2_hbm, y_ref,
                   w1s, w3s, w2s, w1b, w3b, w2b, sems, *, layer):
    b = pl.program_id(0)
    prev = be_ref[jnp.maximum(b - 1, 0)]
    used = b < nb_ref[0]

    def stage(e):
        return [pltpu.make_async_copy(src.at[layer, e], dst, sems.at[j])
                for j, (src, dst) in enumerate(((w1_hbm, w1s), (w3_hbm, w3s), (w2_hbm, w2s)))]

    @pl.when(used & ((b == 0) | (be_ref[b] != prev)))
    def _():
        @pl.when(b == 0)
        def _():
            for c in stage(be_ref[0]):
                c.start()

        for c in stage(be_ref[b]):
            c.wait()
        w1b[...] = w1s[...].astype(BF16)
        w3b[...] = w3s[...].astype(BF16)
        w2b[...] = w2s[...].astype(BF16)

        @pl.when(ne_ref[b] >= 0)
        def _():
            for c in stage(ne_ref[b]):
                c.start()

    @pl.when(used)
    def _():
        x_lo, x_hi = _unpack_bf16_pairs(x_ref[...])
        half = x_lo.shape[1]
        h1 = _dot(x_lo, w1b[:half, :]) + _dot(x_hi, w1b[half:, :])
        h3 = _dot(x_lo, w3b[:half, :]) + _dot(x_hi, w3b[half:, :])
        hb = jax.nn.silu(h1) * h3
        y_ref[...] = _dot(hb.astype(BF16), w2b[...])


def _expert_mlps(xs, block_e, n_used, next_e, w1, w3, w2, layer):
    npad, half = xs.shape
    d, ff = w1.shape[2], w1.shape[3]
    assert d == 2 * half
    bm = MOE_BLOCK
    rows = lambda b, be, nb, ne: (jnp.minimum(b, nb[0] - 1), 0)
    hbm = pl.BlockSpec(memory_space=pl.ANY)
    grid_spec = pltpu.PrefetchScalarGridSpec(
        num_scalar_prefetch=3,
        grid=(npad // bm,),
        in_specs=[pl.BlockSpec((bm, half), rows), hbm, hbm, hbm],
        out_specs=pl.BlockSpec((bm, d), rows),
        scratch_shapes=[pltpu.VMEM((d, ff), F32), pltpu.VMEM((d, ff), F32), pltpu.VMEM((ff, d), F32),
                        pltpu.VMEM((d, ff), BF16), pltpu.VMEM((d, ff), BF16), pltpu.VMEM((ff, d), BF16),
                        pltpu.SemaphoreType.DMA((3,))],
    )
    return pl.pallas_call(
        functools.partial(_expert_kernel, layer=layer),
        grid_spec=grid_spec,
        out_shape=jax.ShapeDtypeStruct((npad, d), F32),
        compiler_params=_cparams(1),
        name="expert_mlps",
    )(block_e, n_used, next_e, xs, w1, w3, w2)


def _moe(a2p, route, counts, w1, w3, w2, layer, row_ranges):
    t = a2p.shape[0]
    bm = MOE_BLOCK
    n = 2 * t
    e = route[:, R_E1:R_E2 + 1].astype(jnp.int32).reshape(n)
    rank = route[:, R_RANK1:R_RANK2 + 1].astype(jnp.int32).reshape(n)
    cnt = counts[0, :N_EXPERTS].astype(jnp.int32)
    padded = ((cnt + bm - 1) // bm) * bm
    pend = jnp.cumsum(padded)
    pstart = pend - padded
    pos = jnp.sum(jnp.where(e[:, None] == jnp.arange(N_EXPERTS, dtype=jnp.int32)[None, :], pstart[None, :], 0),
                  axis=1) + rank
    n_blocks = n // bm + N_EXPERTS
    npad = n_blocks * bm
    starts = jnp.arange(n_blocks, dtype=jnp.int32) * bm
    n_used = pend[-1:] // bm
    last_start = (n_used[0] - 1) * bm
    block_e = jnp.sum((pend[None, :] <= jnp.minimum(starts, last_start)[:, None]).astype(jnp.int32), axis=1)
    blk = jnp.arange(n_blocks, dtype=jnp.int32)
    later_other = (blk[None, :] > blk[:, None]) & (block_e[None, :] != block_e[:, None]) & (blk[None, :] < n_used[0])
    first_later = jnp.min(jnp.where(later_other, blk[None, :], n_blocks), axis=1)
    next_e = jnp.where(first_later < n_blocks, block_e[jnp.minimum(first_later, n_blocks - 1)], -1).astype(jnp.int32)
    pos_first, pos_second = pos[0::2], pos[1::2]
    xs = _scatter_rows_twice(a2p, pos_first, pos_second, npad)
    ys = _expert_mlps(xs, block_e, n_used, next_e, w1, w3, w2, layer)
    return [_gather_rows(ys, jnp.concatenate([pos_first[r0:r1], pos_second[r0:r1]])) for r0, r1 in row_ranges]


def _moe_combine(h, y_first, y_second, route, g2):
    w1 = route[:, R_W1:R_W1 + 1]
    w2 = route[:, R_W2:R_W2 + 1]
    return h + g2 * (w1 * y_first + w2 * y_second)


def _gmlp_kernel(h_ref, ya_ref, yb_ref, rt_ref, modp_ref, mod_ref, g1_ref, win_ref, bin_ref, vg_ref, ws_ref, bsb_ref,
                 wout_ref, g2_ref, wr_ref, br_ref, ltri_ref,
                 h1_ref, a2_ref, route_ref, cnt_ref, carry_ref, *, n_ctx_tiles, tiles_per_req):
    i = pl.program_id(0)
    seq = _seq_of_tile(i, n_ctx_tiles, tiles_per_req)

    @pl.when(i == 0)
    def _():
        carry_ref[...] = jnp.zeros_like(carry_ref)

    gw = vg_ref.shape[1]
    cg = gw // GM_GROUPS
    for s in range(h_ref.shape[0] // GMLP_SUB):
        rs = pl.ds(s * GMLP_SUB, GMLP_SUB)
        h = _moe_combine(h_ref[rs, :], ya_ref[rs, :], yb_ref[rs, :], rt_ref[rs, :], _mod_row(modp_ref, 5, seq))
        a = _modulated(h, g1_ref[...], _mod_row(mod_ref, 0, seq), _mod_row(mod_ref, 1, seq))
        z = _gelu_tanh(_dot(a.astype(BF16), win_ref[...]) + bin_ref[...])
        u = z[:, :gw]
        v = z[:, gw:]
        v = (v * lax.rsqrt(jnp.mean(v * v, axis=-1, keepdims=True) + EPS)) * vg_ref[...]
        vb = v.astype(BF16)
        rows = []
        for c in range(GMLP_SUB // CHUNK):
            cols = []
            for g in range(GM_GROUPS):
                cols.append(_dot(ws_ref[g], vb[c * CHUNK:(c + 1) * CHUNK, g * cg:(g + 1) * cg]))
            rows.append(jnp.concatenate(cols, axis=1) + bsb_ref[...])
        vm = jnp.concatenate(rows, axis=0)
        mix = _dot((u * vm).astype(BF16), wout_ref[...])
        h1 = h + _mod_row(mod_ref, 2, seq) * mix
        h1_ref[rs, :] = h1
        a2 = _modulated(h1, g2_ref[...], _mod_row(mod_ref, 3, seq), _mod_row(mod_ref, 4, seq))
        a2_ref[rs, :] = _pack_bf16_pairs(a2)
        route_ref[rs, :] = _route(a2, wr_ref[...], br_ref[...], ltri_ref[...], carry_ref)
    cnt_ref[...] = carry_ref[...]


def _gmlp_layer(h, y2, route_prev, mod_prev, mod, g1, win, bin_, vg, ws, bsb, wout, g2, router,
                n_ctx, rows_per_req):
    t, d = h.shape
    tm = ROW_TILE * SUBTILES
    wr, br = router
    ltri = _lower_tri(GMLP_SUB)
    full = lambda shape: pl.BlockSpec(shape, lambda i: (0,) * len(shape), pipeline_mode=pl.Buffered(1))
    row = lambda w: pl.BlockSpec((tm, w), lambda i: (i, 0))
    kern = functools.partial(_gmlp_kernel, n_ctx_tiles=n_ctx // tm, tiles_per_req=rows_per_req // tm)
    args = (h, y2, y2, route_prev, mod_prev, mod, g1, win, bin_, vg, ws, bsb, wout, g2, wr, br, ltri)
    second = pl.BlockSpec((tm, d), lambda i: (i + t // tm, 0))
    in_specs = [row(d), row(d), second, row(LANES)] + [full(a.shape) for a in args[4:]]
    return pl.pallas_call(
        kern,
        grid=(t // tm,),
        in_specs=in_specs,
        out_specs=[row(d), row(d // 2), row(LANES), pl.BlockSpec((1, LANES), lambda i: (0, 0))],
        out_shape=[jax.ShapeDtypeStruct((t, d), F32), jax.ShapeDtypeStruct((t, d // 2), jnp.int32),
                   jax.ShapeDtypeStruct((t, LANES), F32), jax.ShapeDtypeStruct((1, LANES), F32)],
        scratch_shapes=[pltpu.VMEM((1, LANES), F32)],
        compiler_params=_cparams(1),
        name="gmlp_router",
    )(*args)


def _final_kernel(h_ref, ya_ref, yb_ref, rt_ref, mod_ref, o_ref, *, first_tile, n_ctx_tiles, tiles_per_req):
    seq = _seq_of_tile(first_tile + pl.program_id(0), n_ctx_tiles, tiles_per_req)
    o_ref[...] = _moe_combine(h_ref[...], ya_ref[...], yb_ref[...], rt_ref[...], _mod_row(mod_ref, 5, seq))


def _final_combine(h, y2, route, mod, first_row, n_ctx, rows_per_req):
    d = h.shape[1]
    n = y2.shape[0] // 2
    tm = ROW_TILE
    first_tile = first_row // tm
    row = lambda w: pl.BlockSpec((tm, w), lambda i: (first_tile + i, 0))
    kern = functools.partial(_final_kernel, first_tile=first_tile, n_ctx_tiles=n_ctx // tm,
                             tiles_per_req=rows_per_req // tm)
    return pl.pallas_call(
        kern,
        grid=(n // tm,),
        in_specs=[row(d), pl.BlockSpec((tm, d), lambda i: (i, 0)), pl.BlockSpec((tm, d), lambda i: (i + n // tm, 0)),
                  row(LANES), pl.BlockSpec(mod.shape, lambda i: (0, 0, 0))],
        out_specs=pl.BlockSpec((tm, d), lambda i: (i, 0)),
        out_shape=jax.ShapeDtypeStruct((n, d), F32),
        compiler_params=_cparams(1),
        name="final_combine",
    )(h, y2, y2, route, mod)


def kernel(x_prompt, x_sample, cache_k, cache_v, c, c_ctx, ada_w, ada_b, norm1_g, norm2_g, attn_wq, attn_wk,
           attn_wv, attn_wo, attn_qnorm, attn_knorm, attn_lam, attn_subln, gm_win, gm_bin, gm_vnorm, gm_ws,
           gm_bs, gm_wout, moe_wc, moe_bc, moe_wf, moe_bf, moe_w1, moe_w3, moe_w2):
    batch, seq_len, d = x_prompt.shape
    n_req, req_len, _ = x_sample.shape
    depth = ada_w.shape[0]
    assert depth == 2 and attn_wq.shape[0] == 1 and gm_win.shape[0] == 1
    n_ctx = batch * seq_len
    assert n_ctx % req_len == 0 and req_len % ROW_TILE == 0 and seq_len == ROW_TILE
    past = cache_k.shape[2]

    h_ctx = x_prompt.reshape(n_ctx, d)
    h_lat = x_sample.reshape(n_req * req_len, d)
    cvec = jnp.concatenate([c_ctx[None, :], c, jnp.zeros((8 - 1 - n_req, d), F32)], axis=0)
    mod = _ada_mod(cvec, ada_w, ada_b)
    routers = [_router_weights(moe_wc[i], moe_bc[i], moe_wf[i], moe_bf[i]) for i in range(depth)]
    row_vec = lambda v: v.reshape(1, -1)

    wqkv = jnp.concatenate([attn_wq[0], attn_wk[0], attn_wv[0]], axis=1).astype(BF16)
    reps = d // HEAD_DIM
    q, k, v, k_new, v_new = _pre_attn(h_ctx, h_lat, mod[0], row_vec(norm1_g[0]), wqkv,
                                      row_vec(jnp.tile(attn_qnorm[0], reps)), row_vec(jnp.tile(attn_knorm[0], reps)),
                                      req_len)
    o_ctx, o_lat = _attention(q, k, v, attn_lam[0], row_vec(attn_subln[0]),
                              cache_k[:, 0].reshape(n_req, past, d), cache_v[:, 0].reshape(n_req, past, d),
                              n_ctx, seq_len, n_req, req_len, layer=0)
    h1, a2, route0, cnt0 = _post_attn(h_ctx, h_lat, o_ctx, o_lat, mod[0], attn_wo[0].astype(BF16),
                                      row_vec(norm2_g[0]), routers[0], req_len)
    t = n_ctx + n_req * req_len
    (y2,) = _moe(a2, route0, cnt0, moe_w1, moe_w3, moe_w2, 0, [(0, t)])

    gw = gm_vnorm.shape[1]
    bsb = jnp.repeat(gm_bs[0].T, gw // GM_GROUPS, axis=1)
    h2, a2, route1, cnt1 = _gmlp_layer(h1, y2, route0, mod[0], mod[1], row_vec(norm1_g[1]), gm_win[0].astype(BF16),
                                       row_vec(gm_bin[0]), row_vec(gm_vnorm[0]), gm_ws[0].astype(BF16), bsb,
                                       gm_wout[0].astype(BF16), row_vec(norm2_g[1]), routers[1], n_ctx, req_len)
    y2_ctx, y2_lat = _moe(a2, route1, cnt1, moe_w1, moe_w3, moe_w2, 1, [(0, n_ctx), (n_ctx, t)])
    y_ctx = _final_combine(h2, y2_ctx, route1, mod[1], 0, n_ctx, req_len)
    y_lat = _final_combine(h2, y2_lat, route1, mod[1], n_ctx, n_ctx, req_len)

    y_prompt = y_ctx.reshape(batch, seq_len, d)
    y_sample = y_lat.reshape(n_req, req_len, d)
    new_cache_k = jnp.transpose(k_new.reshape(batch, 1, N_HEADS, 2, HEAD_DIM, seq_len), (0, 1, 5, 2, 3, 4))
    new_cache_v = v_new.reshape(batch, 1, seq_len, N_HEADS, V_DIM)
    return (y_prompt, y_sample, new_cache_k, new_cache_v)
```

```python
import functools
import math

import jax
import jax.numpy as jnp
from jax import lax
from jax.experimental import pallas as pl
from jax.experimental.pallas import tpu as pltpu
from jax.experimental.pallas import tpu_sc as plsc

F32 = jnp.float32
BF16 = jnp.bfloat16

N_HEADS = 8
HEAD_DIM = 64
V_DIM = 2 * HEAD_DIM
GRID_W = 64
ROPE_THETA = 10000.0
CHUNK = 128
GM_GROUPS = 8
N_EXPERT_GROUPS = 4
EXPERTS_PER_GROUP = 8
N_EXPERTS = N_EXPERT_GROUPS * EXPERTS_PER_GROUP
EPS = 1e-6

LANES = 128
ROW_TILE = 256
SUBTILES = 2
GMLP_SUB = 512
MOE_BLOCK = 256
SC_WINDOW_BYTES = 128 * 1024
ATTN_Q_TILE = 256
ATTN_HEADS_PER_STEP = 4
VMEM_LIMIT = 56 * 1024 * 1024

R_E1, R_E2, R_W1, R_W2, R_RANK1, R_RANK2 = 0, 1, 2, 3, 4, 5
L_EXPERT0 = N_EXPERT_GROUPS


def _lambda_init(layer):
    return 0.8 - 0.6 * math.exp(-0.3 * layer)


def _cparams(n_axes):
    return pltpu.CompilerParams(dimension_semantics=("arbitrary",) * n_axes, vmem_limit_bytes=VMEM_LIMIT)


def _seq_of_tile(i, n_ctx_tiles, tiles_per_req):
    return jnp.where(i < n_ctx_tiles, 0, 1 + (i - n_ctx_tiles) // tiles_per_req)


def _pick_rows(i, n_ctx_tiles, ctx_ref, lat_ref):
    return jnp.where(i < n_ctx_tiles, ctx_ref[...], lat_ref[...])


def _split_specs(tm, d, n_ctx_tiles):
    return [pl.BlockSpec((tm, d), lambda i: (jnp.minimum(i, n_ctx_tiles - 1), 0)),
            pl.BlockSpec((tm, d), lambda i: (jnp.maximum(i - n_ctx_tiles, 0), 0))]


def _mod_row(mod_ref, part, seq):
    return mod_ref[part, pl.ds(seq, 1), :]


def _modulated(x, g, shift, scale):
    y = x * lax.rsqrt(jnp.mean(x * x, axis=-1, keepdims=True) + EPS)
    return (y * g) * (1.0 + scale) + shift


def _split_bf16(x):
    hi = x.astype(BF16)
    lo = (x - hi.astype(F32)).astype(BF16)
    return hi, lo


def _gelu_tanh(x):
    c = math.sqrt(2.0 / math.pi)
    hx = 0.5 * x
    return hx + hx * jnp.tanh(x * (c + (c * 0.044715) * (x * x)))


def _dot(a, b):
    return jnp.dot(a, b, preferred_element_type=F32)


def _pack_bf16_pairs(x):
    half = x.shape[1] // 2
    bits = lax.bitcast_convert_type(x.astype(BF16).astype(F32), jnp.uint32)
    packed = (bits[:, :half] >> 16) | bits[:, half:]
    return lax.bitcast_convert_type(packed, jnp.int32)


def _unpack_pairs_f32(p):
    u = lax.bitcast_convert_type(p, jnp.uint32)
    return lax.bitcast_convert_type(u << 16, F32), lax.bitcast_convert_type(u & jnp.uint32(0xFFFF0000), F32)


def _unpack_bf16_pairs(p):
    lo, hi = _unpack_pairs_f32(p)
    return lo.astype(BF16), hi.astype(BF16)


def _ada_kernel(c_ref, w_ref, b_ref, o_ref):
    c = c_ref[...]
    s = c * jax.nn.sigmoid(c)
    o_ref[...] = _dot(s.astype(BF16), w_ref[...].astype(BF16)) + b_ref[...]


def _ada_mod(cvec, ada_w, ada_b):
    depth, d, d6 = ada_w.shape
    parts = d6 // d
    rows = cvec.shape[0]
    return pl.pallas_call(
        _ada_kernel,
        grid=(depth, parts),
        in_specs=[
            pl.BlockSpec((rows, d), lambda l, j: (0, 0)),
            pl.BlockSpec((None, d, d), lambda l, j: (l, 0, j)),
            pl.BlockSpec((None, 1, d), lambda l, j: (l, 0, j)),
        ],
        out_specs=pl.BlockSpec((None, None, rows, d), lambda l, j: (l, j, 0, 0)),
        out_shape=jax.ShapeDtypeStruct((depth, parts, rows, d), F32),
        compiler_params=_cparams(2),
        name="ada_mod",
    )(cvec, ada_w, ada_b.reshape(depth, 1, d6))


def _pre_attn_kernel(hc_ref, hl_ref, mod_ref, g_ref, w_ref, qg_ref, kg_ref, gs_ref, gb_ref, cos_ref, sin_ref,
                     q_ref, k_ref, v_ref, kf_ref, vf_ref, *, n_ctx_tiles, tiles_per_req):
    i = pl.program_id(0)
    seq = _seq_of_tile(i, n_ctx_tiles, tiles_per_req)
    d = hc_ref.shape[1]
    x = _pick_rows(i, n_ctx_tiles, hc_ref, hl_ref)
    a = _modulated(x, g_ref[...], _mod_row(mod_ref, 0, seq), _mod_row(mod_ref, 1, seq))
    qkv = _dot(a.astype(BF16), w_ref[...])
    q = qkv[:, :d]
    k = qkv[:, d:2 * d]
    v = qkv[:, 2 * d:]
    sq_hi, sq_lo = _split_bf16(jnp.concatenate([q * q, k * k], axis=1))
    gsum = _dot(sq_hi, gs_ref[...]) + _dot(sq_lo, gs_ref[...])
    r_hi, r_lo = _split_bf16(lax.rsqrt(gsum * (1.0 / HEAD_DIM) + EPS))
    rb = _dot(r_hi, gb_ref[...]) + _dot(r_lo, gb_ref[...])
    qn = (q * rb[:, :d]) * qg_ref[...]
    kn = (k * rb[:, d:]) * kg_ref[...]
    v_ref[...] = v.astype(BF16)
    q_scale = HEAD_DIM ** -0.5

    @pl.when(i < n_ctx_tiles)
    def _():
        q_ref[...] = (qn * q_scale).astype(BF16)
        k_ref[...] = kn.astype(BF16)
        kf_ref[...] = kn.T.reshape(kf_ref.shape)
        vf_ref[...] = v

    @pl.when(i >= n_ctx_tiles)
    def _():
        cos = cos_ref[...]
        sin = sin_ref[...]
        lane = lax.broadcasted_iota(jnp.int32, cos.shape, 1)
        first = (lane & 31) < 16

        def rope(x):
            outs = []
            for hh in range(d // LANES):
                xs = x[:, hh * LANES:(hh + 1) * LANES]
                rot = jnp.where(first, pltpu.roll(xs, LANES - 16, 1), pltpu.roll(xs, 16, 1))
                outs.append(xs * cos + rot * sin)
            return jnp.concatenate(outs, axis=1)

        q_ref[...] = (rope(qn) * q_scale).astype(BF16)
        k_ref[...] = rope(kn).astype(BF16)


def _rope_tables(n_pos):
    half = HEAD_DIM // 2
    pos = jnp.arange(n_pos, dtype=jnp.int32)
    row = (pos // GRID_W).astype(F32)
    col = (pos % GRID_W).astype(F32)
    inv = 1.0 / (ROPE_THETA ** (jnp.arange(0, half, 2, dtype=F32) / half))
    ang_r = row[:, None] * inv[None, :]
    ang_c = col[:, None] * inv[None, :]
    ang = jnp.concatenate([ang_r, ang_r, ang_c, ang_c], axis=-1)
    quarter = half // 2
    sign = jnp.tile(jnp.concatenate([-jnp.ones((quarter,), F32), jnp.ones((quarter,), F32)]), 2)
    cos = jnp.tile(jnp.cos(ang), (1, 2))
    sin = jnp.tile(jnp.sin(ang) * sign[None, :], (1, 2))
    return cos, sin


def _pre_attn(h_ctx, h_lat, mod, g, wqkv, qg, kg, rows_per_req):
    n_ctx, d = h_ctx.shape
    t = n_ctx + h_lat.shape[0]
    tm = ROW_TILE
    n_ctx_tiles = n_ctx // tm
    tiles_per_req = rows_per_req // tm
    groups = d // HEAD_DIM
    gid = jnp.arange(2 * d, dtype=jnp.int32) // HEAD_DIM
    gs = (gid[:, None] == jnp.arange(LANES, dtype=jnp.int32)[None, :]).astype(BF16)
    gb = gs.T
    assert 2 * groups <= LANES
    cos, sin = _rope_tables(rows_per_req)
    pos_map = lambda i: (jnp.where(i < n_ctx_tiles, 0, (i - n_ctx_tiles) % tiles_per_req), 0)
    ctx_map = lambda i: (jnp.minimum(i, n_ctx_tiles - 1), 0)
    full = lambda shape: pl.BlockSpec(shape, lambda i: (0,) * len(shape))
    row = pl.BlockSpec((tm, d), lambda i: (i, 0))
    kern = functools.partial(_pre_attn_kernel, n_ctx_tiles=n_ctx_tiles, tiles_per_req=tiles_per_req)
    return pl.pallas_call(
        kern,
        grid=(t // tm,),
        in_specs=_split_specs(tm, d, n_ctx_tiles) + [
                  full(mod.shape), full((1, d)), full(wqkv.shape), full((1, d)), full((1, d)),
                  full(gs.shape), full(gb.shape),
                  pl.BlockSpec((tm, LANES), pos_map), pl.BlockSpec((tm, LANES), pos_map)],
        out_specs=[row, row, row,
                   pl.BlockSpec((None, d // HEAD_DIM, HEAD_DIM, tm), lambda i: (ctx_map(i)[0], 0, 0, 0)),
                   pl.BlockSpec((tm, d), ctx_map)],
        out_shape=[jax.ShapeDtypeStruct((t, d), BF16)] * 3 + [
            jax.ShapeDtypeStruct((n_ctx_tiles, d // HEAD_DIM, HEAD_DIM, tm), F32), jax.ShapeDtypeStruct((n_ctx, d), F32)],
        compiler_params=_cparams(1),
        name="pre_attn",
    )(h_ctx, h_lat, mod, g, wqkv, qg, kg, gs, gb, cos, sin)


def _attn_kernel(*refs, n_heads, has_cache, layer):
    if has_cache:
        lam_ref, sub_ref, q_ref, kn_ref, vn_ref, kc_ref, vc_ref, o_ref = refs
    else:
        lam_ref, sub_ref, q_ref, kn_ref, vn_ref, o_ref = refs
    lf = lam_ref[...]
    lam = (jnp.exp(jnp.sum(lf[0:1] * lf[1:2], axis=-1, keepdims=True))
           - jnp.exp(jnp.sum(lf[2:3] * lf[3:4], axis=-1, keepdims=True)) + _lambda_init(layer))
    nt = (((1,), (1,)), ((), ()))
    tq = q_ref.shape[0]
    lane = lax.broadcasted_iota(jnp.int32, (tq, LANES), 1)
    for hh in range(n_heads):
        sl = slice(hh * LANES, (hh + 1) * LANES)
        qh = q_ref[:, sl]
        zero = jnp.zeros_like(qh)
        qs = (jnp.where(lane < HEAD_DIM, qh, zero), jnp.where(lane >= HEAD_DIM, qh, zero))
        keys = [kn_ref[:, sl]]
        vals = [vn_ref[:, sl]]
        if has_cache:
            keys.append(kc_ref[:, sl].astype(BF16))
            vals.append(vc_ref[:, sl].astype(BF16))
        probs = []
        for qm in qs:
            s = [lax.dot_general(qm, kk, nt, preferred_element_type=F32) for kk in keys]
            m = functools.reduce(jnp.maximum, [jnp.max(x, axis=-1, keepdims=True) for x in s])
            e = [jnp.exp(x - m) for x in s]
            den = functools.reduce(lambda u, w: u + w, [jnp.sum(x, axis=-1, keepdims=True) for x in e])
            probs.append((e, 1.0 / den))
        (e0, r0), (e1, r1) = probs
        o = None
        for j, vv in enumerate(vals):
            aj = e0[j] * r0 - lam * (e1[j] * r1)
            oj = _dot(aj.astype(BF16), vv)
            o = oj if o is None else o + oj
        o = o * lax.rsqrt(jnp.mean(o * o, axis=-1, keepdims=True) + EPS)
        o = (o * sub_ref[...]) * (1.0 - _lambda_init(layer))
        o_ref[:, sl] = o.astype(BF16)


def _attention(q, k, v, lam_p, subln, cache_k, cache_v, n_ctx, ctx_len, n_req, req_len, layer):
    t, d = q.shape
    small = lambda shape: pl.BlockSpec(shape, lambda *_: (0,) * len(shape))
    n_ctx_req = n_ctx // ctx_len
    blk = pl.BlockSpec((ctx_len, d), lambda b: (b, 0))
    o_ctx = pl.pallas_call(
        functools.partial(_attn_kernel, n_heads=N_HEADS, has_cache=False, layer=layer),
        grid=(n_ctx_req,),
        in_specs=[small(lam_p.shape), small(subln.shape), blk, blk, blk],
        out_specs=blk,
        out_shape=jax.ShapeDtypeStruct((n_ctx, d), BF16),
        compiler_params=_cparams(1),
        name="attn_ctx",
    )(lam_p, subln, q, k, v)

    tq = ATTN_Q_TILE
    nq = req_len // tq
    q0 = n_ctx // tq
    r0 = n_ctx // req_len
    past = cache_k.shape[1]
    hp = ATTN_HEADS_PER_STEP
    small3 = lambda shape: pl.BlockSpec(shape, lambda b, h, i: (0,) * len(shape))
    qspec = pl.BlockSpec((tq, hp * LANES), lambda b, h, i: (q0 + b * nq + i, h))
    kvspec = pl.BlockSpec((req_len, hp * LANES), lambda b, h, i: (r0 + b, h))
    cspec = pl.BlockSpec((None, past, hp * LANES), lambda b, h, i: (b, 0, h))
    o_lat = pl.pallas_call(
        functools.partial(_attn_kernel, n_heads=hp, has_cache=True, layer=layer),
        grid=(n_req, N_HEADS // hp, nq),
        in_specs=[small3(lam_p.shape), pl.BlockSpec((1, LANES), lambda b, h, i: (0, 0)),
                  qspec, kvspec, kvspec, cspec, cspec],
        out_specs=pl.BlockSpec((tq, hp * LANES), lambda b, h, i: (b * nq + i, h)),
        out_shape=jax.ShapeDtypeStruct((n_req * req_len, d), BF16),
        compiler_params=_cparams(3),
        name="attn_latent",
    )(lam_p, subln, q, k, v, cache_k, cache_v)
    return o_ctx, o_lat


def _route(a2, wr, br, ltri, carry_ref):
    tm = a2.shape[0]
    logits = _dot(a2.astype(BF16), wr) + br
    lane_i = lax.broadcasted_iota(jnp.int32, (tm, LANES), 1)
    lane = lane_i.astype(F32)
    neg = jnp.full((tm, LANES), -jnp.inf, F32)
    big = jnp.full((tm, LANES), float(LANES), F32)
    first_lane = lambda mask: jnp.min(jnp.where(mask, lane, big), axis=-1, keepdims=True)

    lc = jnp.where(lane_i < N_EXPERT_GROUPS, logits, neg)
    mc = jnp.max(lc, axis=-1, keepdims=True)
    pg = 1.0 / jnp.sum(jnp.exp(lc - mc), axis=-1, keepdims=True)
    gi = first_lane(lc == mc)
    assert EXPERTS_PER_GROUP == 8
    grp = lax.shift_right_arithmetic(lane_i - L_EXPERT0, 3).astype(F32)
    in_group = (lane_i >= L_EXPERT0) & (lane_i < L_EXPERT0 + N_EXPERTS) & (grp == gi)
    ls = jnp.where(in_group, logits, neg)
    t1 = jnp.max(ls, axis=-1, keepdims=True)
    i1 = first_lane(ls == t1)
    ls2 = jnp.where(lane == i1, neg, ls)
    t2 = jnp.max(ls2, axis=-1, keepdims=True)
    i2 = first_lane(ls2 == t2)
    ex = jnp.exp(t2 - t1)
    w1 = pg * (1.0 / (1.0 + ex))
    w2 = pg * (ex / (1.0 + ex))
    e1 = i1 - float(L_EXPERT0)
    e2 = i2 - float(L_EXPERT0)
    oh1 = lane == e1
    oh2 = lane == e2
    onehot = oh1.astype(F32) + oh2.astype(F32)
    before = _dot(ltri, onehot.astype(BF16)) + carry_ref[...]
    zero = jnp.zeros_like(before)
    rank1 = jnp.sum(jnp.where(oh1, before, zero), axis=-1, keepdims=True)
    rank2 = jnp.sum(jnp.where(oh2, before, zero), axis=-1, keepdims=True)
    carry_ref[...] = carry_ref[...] + jnp.sum(onehot, axis=0, keepdims=True)
    slab = jnp.zeros((tm, LANES), F32)
    for ln, val in ((R_E1, e1), (R_E2, e2), (R_W1, w1), (R_W2, w2), (R_RANK1, rank1), (R_RANK2, rank2)):
        slab = jnp.where(lane_i == ln, val, slab)
    return slab


def _router_weights(wc, bc, wf, bf_):
    d = wc.shape[0]
    pad = LANES - N_EXPERT_GROUPS - N_EXPERTS
    w = jnp.concatenate([wc, wf, jnp.zeros((d, pad), F32)], axis=1)
    b = jnp.concatenate([bc, bf_, jnp.zeros((pad,), F32)])[None, :]
    return w.astype(BF16), b


def _lower_tri(n):
    r = jnp.arange(n, dtype=jnp.int32)
    return (r[None, :] < r[:, None]).astype(BF16)


def _post_attn_kernel(hc_ref, hl_ref, oc_ref, ol_ref, mod_ref, wo_ref, g2_ref, wr_ref, br_ref, ltri_ref,
                      h1_ref, a2_ref, route_ref, cnt_ref, carry_ref, *, n_ctx_tiles, tiles_per_req):
    i = pl.program_id(0)
    seq = _seq_of_tile(i, n_ctx_tiles, tiles_per_req)

    @pl.when(i == 0)
    def _():
        carry_ref[...] = jnp.zeros_like(carry_ref)

    for s in range(h1_ref.shape[0] // ROW_TILE):
        rows = pl.ds(s * ROW_TILE, ROW_TILE)
        o = jnp.where(i < n_ctx_tiles, oc_ref[rows, :], ol_ref[rows, :])
        h = jnp.where(i < n_ctx_tiles, hc_ref[rows, :], hl_ref[rows, :])
        h1 = h + _mod_row(mod_ref, 2, seq) * _dot(o, wo_ref[...])
        h1_ref[rows, :] = h1
        a2 = _modulated(h1, g2_ref[...], _mod_row(mod_ref, 3, seq), _mod_row(mod_ref, 4, seq))
        a2_ref[rows, :] = _pack_bf16_pairs(a2)
        route_ref[rows, :] = _route(a2, wr_ref[...], br_ref[...], ltri_ref[...], carry_ref)
    cnt_ref[...] = carry_ref[...]


def _post_attn(h_ctx, h_lat, o_ctx, o_lat, mod, wo, g2, router, rows_per_req):
    n_ctx, d = h_ctx.shape
    t = n_ctx + h_lat.shape[0]
    tm = ROW_TILE * SUBTILES
    wr, br = router
    ltri = _lower_tri(ROW_TILE)
    full = lambda shape: pl.BlockSpec(shape, lambda i: (0,) * len(shape))
    row = pl.BlockSpec((tm, d), lambda i: (i, 0))
    kern = functools.partial(_post_attn_kernel, n_ctx_tiles=n_ctx // tm, tiles_per_req=rows_per_req // tm)
    return pl.pallas_call(
        kern,
        grid=(t // tm,),
        in_specs=_split_specs(tm, d, n_ctx // tm) + _split_specs(tm, d, n_ctx // tm) + [
                  full(mod.shape), full(wo.shape), full((1, d)), full(wr.shape),
                  full(br.shape), full(ltri.shape)],
        out_specs=[row, pl.BlockSpec((tm, d // 2), lambda i: (i, 0)), pl.BlockSpec((tm, LANES), lambda i: (i, 0)),
                   full((1, LANES))],
        out_shape=[jax.ShapeDtypeStruct((t, d), F32), jax.ShapeDtypeStruct((t, d // 2), jnp.int32),
                   jax.ShapeDtypeStruct((t, LANES), F32), jax.ShapeDtypeStruct((1, LANES), F32)],
        scratch_shapes=[pltpu.VMEM((1, LANES), F32)],
        compiler_params=_cparams(1),
        name="post_attn_router",
    )(h_ctx, h_lat, o_ctx, o_lat, mod, wo, g2, wr, br, ltri)


def _gather_rows(src, idx):
    n = idx.shape[0]
    d = src.shape[1]
    w = _sc_window_rows(src)
    mesh = plsc.VectorSubcoreMesh(core_axis_name="core", subcore_axis_name="subcore")

    @pl.kernel(out_type=jax.ShapeDtypeStruct((n, d), src.dtype), mesh=mesh)
    def gather(src_hbm, idx_hbm, out_hbm):
        def body(idx_vmem, out_vmem):
            pltpu.sync_copy(src_hbm.at[idx_vmem.at[0, pl.ds(0, w)]], out_vmem)

        pltpu.emit_pipeline(
            body,
            grid=(n // w,),
            in_specs=[pl.BlockSpec((1, LANES), lambda i: (i, 0))],
            out_specs=[pl.BlockSpec((w, d), lambda i: (i, 0))],
            core_axis_name=("core", "subcore"),
            dimension_semantics=(pltpu.PARALLEL,),
        )(idx_hbm, out_hbm)

    return gather(src, _index_windows(idx, w))


def _sc_window_rows(src):
    return min(LANES, SC_WINDOW_BYTES // (src.shape[1] * src.dtype.itemsize))


def _index_windows(idx, w):
    return jnp.pad(idx.reshape(idx.shape[0] // w, w), ((0, 0), (0, LANES - w)))


def _scatter_rows_twice(src, idx_a, idx_b, n_out):
    t, d = src.shape
    w = _sc_window_rows(src)
    mesh = plsc.VectorSubcoreMesh(core_axis_name="core", subcore_axis_name="subcore")

    @pl.kernel(out_type=jax.ShapeDtypeStruct((n_out, d), src.dtype), mesh=mesh)
    def scatter(src_hbm, ia_hbm, ib_hbm, out_hbm):
        def body(src_vmem, ia_vmem, ib_vmem):
            pltpu.sync_copy(src_vmem, out_hbm.at[ia_vmem.at[0, pl.ds(0, w)]])
            pltpu.sync_copy(src_vmem, out_hbm.at[ib_vmem.at[0, pl.ds(0, w)]])

        pltpu.emit_pipeline(
            body,
            grid=(t // w,),
            in_specs=[pl.BlockSpec((w, d), lambda i: (i, 0)),
                      pl.BlockSpec((1, LANES), lambda i: (i, 0)),
                      pl.BlockSpec((1, LANES), lambda i: (i, 0))],
            out_specs=[],
            core_axis_name=("core", "subcore"),
            dimension_semantics=(pltpu.PARALLEL,),
        )(src_hbm, ia_hbm, ib_hbm)

    return scatter(src, _index_windows(idx_a, w), _index_windows(idx_b, w))


def _expert_kernel(be_ref, nb_ref, ne_ref, x_ref, w1_hbm, w3_hbm, w2_hbm, y_ref,
                   w1s, w3s, w2s, w1b, w3b, w2b, sems, *, layer):
    b = pl.program_id(0)
    prev = be_ref[jnp.maximum(b - 1, 0)]
    used = b < nb_ref[0]

    def stage(e):
        return [pltpu.make_async_copy(src.at[layer, e], dst, sems.at[j])
                for j, (src, dst) in enumerate(((w1_hbm, w1s), (w3_hbm, w3s), (w2_hbm, w2s)))]

    @pl.when(used & ((b == 0) | (be_ref[b] != prev)))
    def _():
        @pl.when(b == 0)
        def _():
            for c in stage(be_ref[0]):
                c.start()

        for c in stage(be_ref[b]):
            c.wait()
        w1b[...] = w1s[...].astype(BF16)
        w3b[...] = w3s[...].astype(BF16)
        w2b[...] = w2s[...].astype(BF16)

        @pl.when(ne_ref[b] >= 0)
        def _():
            for c in stage(ne_ref[b]):
                c.start()

    @pl.when(used)
    def _():
        x_lo, x_hi = _unpack_bf16_pairs(x_ref[...])
        half = x_lo.shape[1]
        h1 = _dot(x_lo, w1b[:half, :]) + _dot(x_hi, w1b[half:, :])
        h3 = _dot(x_lo, w3b[:half, :]) + _dot(x_hi, w3b[half:, :])
        hb = jax.nn.silu(h1) * h3
        y_ref[...] = _pack_bf16_pairs(_dot(hb.astype(BF16), w2b[...]))


def _expert_mlps(xs, block_e, n_used, next_e, w1, w3, w2, layer):
    npad, half = xs.shape
    d, ff = w1.shape[2], w1.shape[3]
    assert d == 2 * half
    bm = MOE_BLOCK
    rows = lambda b, be, nb, ne: (jnp.minimum(b, nb[0] - 1), 0)
    hbm = pl.BlockSpec(memory_space=pl.ANY)
    grid_spec = pltpu.PrefetchScalarGridSpec(
        num_scalar_prefetch=3,
        grid=(npad // bm,),
        in_specs=[pl.BlockSpec((bm, half), rows), hbm, hbm, hbm],
        out_specs=pl.BlockSpec((bm, half), rows),
        scratch_shapes=[pltpu.VMEM((d, ff), F32), pltpu.VMEM((d, ff), F32), pltpu.VMEM((ff, d), F32),
                        pltpu.VMEM((d, ff), BF16), pltpu.VMEM((d, ff), BF16), pltpu.VMEM((ff, d), BF16),
                        pltpu.SemaphoreType.DMA((3,))],
    )
    return pl.pallas_call(
        functools.partial(_expert_kernel, layer=layer),
        grid_spec=grid_spec,
        out_shape=jax.ShapeDtypeStruct((npad, half), jnp.int32),
        compiler_params=_cparams(1),
        name="expert_mlps",
    )(block_e, n_used, next_e, xs, w1, w3, w2)


def _moe(a2p, route, counts, w1, w3, w2, layer, row_ranges):
    t = a2p.shape[0]
    bm = MOE_BLOCK
    n = 2 * t
    e = route[:, R_E1:R_E2 + 1].astype(jnp.int32).reshape(n)
    rank = route[:, R_RANK1:R_RANK2 + 1].astype(jnp.int32).reshape(n)
    cnt = counts[0, :N_EXPERTS].astype(jnp.int32)
    padded = ((cnt + bm - 1) // bm) * bm
    pend = jnp.cumsum(padded)
    pstart = pend - padded
    pos = jnp.sum(jnp.where(e[:, None] == jnp.arange(N_EXPERTS, dtype=jnp.int32)[None, :], pstart[None, :], 0),
                  axis=1) + rank
    n_blocks = n // bm + N_EXPERTS
    npad = n_blocks * bm
    starts = jnp.arange(n_blocks, dtype=jnp.int32) * bm
    n_used = pend[-1:] // bm
    last_start = (n_used[0] - 1) * bm
    block_e = jnp.sum((pend[None, :] <= jnp.minimum(starts, last_start)[:, None]).astype(jnp.int32), axis=1)
    blk = jnp.arange(n_blocks, dtype=jnp.int32)
    later_other = (blk[None, :] > blk[:, None]) & (block_e[None, :] != block_e[:, None]) & (blk[None, :] < n_used[0])
    first_later = jnp.min(jnp.where(later_other, blk[None, :], n_blocks), axis=1)
    next_e = jnp.where(first_later < n_blocks, block_e[jnp.minimum(first_later, n_blocks - 1)], -1).astype(jnp.int32)
    pos_first, pos_second = pos[0::2], pos[1::2]
    xs = _scatter_rows_twice(a2p, pos_first, pos_second, npad)
    ys = _expert_mlps(xs, block_e, n_used, next_e, w1, w3, w2, layer)
    return [_gather_rows(ys, jnp.concatenate([pos_first[r0:r1], pos_second[r0:r1]])) for r0, r1 in row_ranges]


def _moe_combine(h, yp_first, yp_second, route, g2):
    w1 = route[:, R_W1:R_W1 + 1]
    w2 = route[:, R_W2:R_W2 + 1]
    wide = lambda p: jnp.concatenate(_unpack_pairs_f32(p), axis=1)
    return h + g2 * (w1 * wide(yp_first) + w2 * wide(yp_second))


def _gmlp_kernel(h_ref, ya_ref, yb_ref, rt_ref, modp_ref, mod_ref, g1_ref, win_ref, bin_ref, vg_ref, ws_ref, bsb_ref,
                 wout_ref, g2_ref, wr_ref, br_ref, ltri_ref,
                 h1_ref, a2_ref, route_ref, cnt_ref, carry_ref, *, n_ctx_tiles, tiles_per_req):
    i = pl.program_id(0)
    seq = _seq_of_tile(i, n_ctx_tiles, tiles_per_req)

    @pl.when(i == 0)
    def _():
        carry_ref[...] = jnp.zeros_like(carry_ref)

    gw = vg_ref.shape[1]
    cg = gw // GM_GROUPS
    for s in range(h_ref.shape[0] // GMLP_SUB):
        rs = pl.ds(s * GMLP_SUB, GMLP_SUB)
        h = _moe_combine(h_ref[rs, :], ya_ref[rs, :], yb_ref[rs, :], rt_ref[rs, :], _mod_row(modp_ref, 5, seq))
        a = _modulated(h, g1_ref[...], _mod_row(mod_ref, 0, seq), _mod_row(mod_ref, 1, seq))
        z = _gelu_tanh(_dot(a.astype(BF16), win_ref[...]) + bin_ref[...])
        u = z[:, :gw]
        v = z[:, gw:]
        v = (v * lax.rsqrt(jnp.mean(v * v, axis=-1, keepdims=True) + EPS)) * vg_ref[...]
        vb = v.astype(BF16)
        rows = []
        for c in range(GMLP_SUB // CHUNK):
            cols = []
            for g in range(GM_GROUPS):
                cols.append(_dot(ws_ref[g], vb[c * CHUNK:(c + 1) * CHUNK, g * cg:(g + 1) * cg]))
            rows.append(jnp.concatenate(cols, axis=1) + bsb_ref[...])
        vm = jnp.concatenate(rows, axis=0)
        mix = _dot((u * vm).astype(BF16), wout_ref[...])
        h1 = h + _mod_row(mod_ref, 2, seq) * mix
        h1_ref[rs, :] = h1
        a2 = _modulated(h1, g2_ref[...], _mod_row(mod_ref, 3, seq), _mod_row(mod_ref, 4, seq))
        a2_ref[rs, :] = _pack_bf16_pairs(a2)
        route_ref[rs, :] = _route(a2, wr_ref[...], br_ref[...], ltri_ref[...], carry_ref)
    cnt_ref[...] = carry_ref[...]


def _gmlp_layer(h, y2, route_prev, mod_prev, mod, g1, win, bin_, vg, ws, bsb, wout, g2, router,
                n_ctx, rows_per_req):
    t, d = h.shape
    tm = ROW_TILE * SUBTILES
    wr, br = router
    ltri = _lower_tri(GMLP_SUB)
    full = lambda shape: pl.BlockSpec(shape, lambda i: (0,) * len(shape), pipeline_mode=pl.Buffered(1))
    row = lambda w: pl.BlockSpec((tm, w), lambda i: (i, 0))
    kern = functools.partial(_gmlp_kernel, n_ctx_tiles=n_ctx // tm, tiles_per_req=rows_per_req // tm)
    args = (h, y2, y2, route_prev, mod_prev, mod, g1, win, bin_, vg, ws, bsb, wout, g2, wr, br, ltri)
    second = pl.BlockSpec((tm, d // 2), lambda i: (i + t // tm, 0))
    in_specs = [row(d), row(d // 2), second, row(LANES)] + [full(a.shape) for a in args[4:]]
    return pl.pallas_call(
        kern,
        grid=(t // tm,),
        in_specs=in_specs,
        out_specs=[row(d), row(d // 2), row(LANES), pl.BlockSpec((1, LANES), lambda i: (0, 0))],
        out_shape=[jax.ShapeDtypeStruct((t, d), F32), jax.ShapeDtypeStruct((t, d // 2), jnp.int32),
                   jax.ShapeDtypeStruct((t, LANES), F32), jax.ShapeDtypeStruct((1, LANES), F32)],
        scratch_shapes=[pltpu.VMEM((1, LANES), F32)],
        compiler_params=_cparams(1),
        name="gmlp_router",
    )(*args)


def _final_kernel(h_ref, ya_ref, yb_ref, rt_ref, mod_ref, o_ref, *, first_tile, n_ctx_tiles, tiles_per_req):
    seq = _seq_of_tile(first_tile + pl.program_id(0), n_ctx_tiles, tiles_per_req)
    o_ref[...] = _moe_combine(h_ref[...], ya_ref[...], yb_ref[...], rt_ref[...], _mod_row(mod_ref, 5, seq))


def _final_combine(h, y2, route, mod, first_row, n_ctx, rows_per_req):
    d = h.shape[1]
    n = y2.shape[0] // 2
    tm = ROW_TILE
    first_tile = first_row // tm
    row = lambda w: pl.BlockSpec((tm, w), lambda i: (first_tile + i, 0))
    kern = functools.partial(_final_kernel, first_tile=first_tile, n_ctx_tiles=n_ctx // tm,
                             tiles_per_req=rows_per_req // tm)
    return pl.pallas_call(
        kern,
        grid=(n // tm,),
        in_specs=[row(d), pl.BlockSpec((tm, d // 2), lambda i: (i, 0)),
                  pl.BlockSpec((tm, d // 2), lambda i: (i + n // tm, 0)),
                  row(LANES), pl.BlockSpec(mod.shape, lambda i: (0, 0, 0))],
        out_specs=pl.BlockSpec((tm, d), lambda i: (i, 0)),
        out_shape=jax.ShapeDtypeStruct((n, d), F32),
        compiler_params=_cparams(1),
        name="final_combine",
    )(h, y2, y2, route, mod)


def kernel(x_prompt, x_sample, cache_k, cache_v, c, c_ctx, ada_w, ada_b, norm1_g, norm2_g, attn_wq, attn_wk,
           attn_wv, attn_wo, attn_qnorm, attn_knorm, attn_lam, attn_subln, gm_win, gm_bin, gm_vnorm, gm_ws,
           gm_bs, gm_wout, moe_wc, moe_bc, moe_wf, moe_bf, moe_w1, moe_w3, moe_w2):
    batch, seq_len, d = x_prompt.shape
    n_req, req_len, _ = x_sample.shape
    depth = ada_w.shape[0]
    assert depth == 2 and attn_wq.shape[0] == 1 and gm_win.shape[0] == 1
    n_ctx = batch * seq_len
    assert n_ctx % req_len == 0 and req_len % ROW_TILE == 0 and seq_len == ROW_TILE
    past = cache_k.shape[2]

    h_ctx = x_prompt.reshape(n_ctx, d)
    h_lat = x_sample.reshape(n_req * req_len, d)
    cvec = jnp.concatenate([c_ctx[None, :], c, jnp.zeros((8 - 1 - n_req, d), F32)], axis=0)
    mod = _ada_mod(cvec, ada_w, ada_b)
    routers = [_router_weights(moe_wc[i], moe_bc[i], moe_wf[i], moe_bf[i]) for i in range(depth)]
    row_vec = lambda v: v.reshape(1, -1)

    wqkv = jnp.concatenate([attn_wq[0], attn_wk[0], attn_wv[0]], axis=1).astype(BF16)
    reps = d // HEAD_DIM
    q, k, v, k_new, v_new = _pre_attn(h_ctx, h_lat, mod[0], row_vec(norm1_g[0]), wqkv,
                                      row_vec(jnp.tile(attn_qnorm[0], reps)), row_vec(jnp.tile(attn_knorm[0], reps)),
                                      req_len)
    o_ctx, o_lat = _attention(q, k, v, attn_lam[0], row_vec(attn_subln[0]),
                              cache_k[:, 0].reshape(n_req, past, d), cache_v[:, 0].reshape(n_req, past, d),
                              n_ctx, seq_len, n_req, req_len, layer=0)
    h1, a2, route0, cnt0 = _post_attn(h_ctx, h_lat, o_ctx, o_lat, mod[0], attn_wo[0].astype(BF16),
                                      row_vec(norm2_g[0]), routers[0], req_len)
    t = n_ctx + n_req * req_len
    (y2,) = _moe(a2, route0, cnt0, moe_w1, moe_w3, moe_w2, 0, [(0, t)])

    gw = gm_vnorm.shape[1]
    bsb = jnp.repeat(gm_bs[0].T, gw // GM_GROUPS, axis=1)
    h2, a2, route1, cnt1 = _gmlp_layer(h1, y2, route0, mod[0], mod[1], row_vec(norm1_g[1]), gm_win[0].astype(BF16),
                                       row_vec(gm_bin[0]), row_vec(gm_vnorm[0]), gm_ws[0].astype(BF16), bsb,
                                       gm_wout[0].astype(BF16), row_vec(norm2_g[1]), routers[1], n_ctx, req_len)
    y2_ctx, y2_lat = _moe(a2, route1, cnt1, moe_w1, moe_w3, moe_w2, 1, [(0, n_ctx), (n_ctx, t)])
    y_ctx = _final_combine(h2, y2_ctx, route1, mod[1], 0, n_ctx, req_len)
    y_lat = _final_combine(h2, y2_lat, route1, mod[1], n_ctx, n_ctx, req_len)

    y_prompt = y_ctx.reshape(batch, seq_len, d)
    y_sample = y_lat.reshape(n_req, req_len, d)
    new_cache_k = jnp.transpose(k_new.reshape(batch, 1, N_HEADS, 2, HEAD_DIM, seq_len), (0, 1, 5, 2, 3, 4))
    new_cache_v = v_new.reshape(batch, 1, seq_len, N_HEADS, V_DIM)
    return (y_prompt, y_sample, new_cache_k, new_cache_v)
```

```python
import functools
import math

import jax
import jax.numpy as jnp
from jax import lax
from jax.experimental import pallas as pl
from jax.experimental.pallas import tpu as pltpu
from jax.experimental.pallas import tpu_sc as plsc

F32 = jnp.float32
BF16 = jnp.bfloat16

N_HEADS = 8
HEAD_DIM = 64
V_DIM = 2 * HEAD_DIM
GRID_W = 64
ROPE_THETA = 10000.0
CHUNK = 128
GM_GROUPS = 8
N_EXPERT_GROUPS = 4
EXPERTS_PER_GROUP = 8
N_EXPERTS = N_EXPERT_GROUPS * EXPERTS_PER_GROUP
EPS = 1e-6

LANES = 128
MXU_TILE = 256
ROW_TILE = 256
SUBTILES = 2
GMLP_SUB = 512
PRE_ATTN_TILE = 256
MOE_BLOCK = 256
SC_WINDOW_BYTES = 128 * 1024
ATTN_Q_TILE = 256
ATTN_HEADS_PER_STEP = 4
VMEM_LIMIT = 56 * 1024 * 1024

R_E1, R_E2, R_W1, R_W2, R_RANK1, R_RANK2 = 0, 1, 2, 3, 4, 5
L_EXPERT0 = N_EXPERT_GROUPS


def _lambda_init(layer):
    return 0.8 - 0.6 * math.exp(-0.3 * layer)


def _cparams(n_axes):
    return pltpu.CompilerParams(dimension_semantics=("arbitrary",) * n_axes, vmem_limit_bytes=VMEM_LIMIT)


def _seq_of_tile(i, n_ctx_tiles, tiles_per_req):
    return jnp.where(i < n_ctx_tiles, 0, 1 + (i - n_ctx_tiles) // tiles_per_req)


def _pick_rows(i, n_ctx_tiles, ctx_ref, lat_ref):
    return jnp.where(i < n_ctx_tiles, ctx_ref[...], lat_ref[...])


def _split_specs(tm, d, n_ctx_tiles):
    return [pl.BlockSpec((tm, d), lambda i: (jnp.minimum(i, n_ctx_tiles - 1), 0)),
            pl.BlockSpec((tm, d), lambda i: (jnp.maximum(i - n_ctx_tiles, 0), 0))]


def _mod_row(mod_ref, part, seq):
    return mod_ref[part, pl.ds(seq, 1), :]


def _modulated(x, g, shift, scale):
    y = x * lax.rsqrt(jnp.mean(x * x, axis=-1, keepdims=True) + EPS)
    return (y * g) * (1.0 + scale) + shift


def _split_bf16(x):
    hi = x.astype(BF16)
    lo = (x - hi.astype(F32)).astype(BF16)
    return hi, lo


def _gelu_tanh(x):
    c = math.sqrt(2.0 / math.pi)
    hx = 0.5 * x
    return hx + hx * jnp.tanh(x * (c + (c * 0.044715) * (x * x)))


def _dot(a, b):
    return jnp.dot(a, b, preferred_element_type=F32)


def _pack_bf16_pairs(x):
    half = x.shape[1] // 2
    bits = lax.bitcast_convert_type(x.astype(BF16).astype(F32), jnp.uint32)
    packed = (bits[:, :half] >> 16) | bits[:, half:]
    return lax.bitcast_convert_type(packed, jnp.int32)


def _unpack_pairs_f32(p):
    u = lax.bitcast_convert_type(p, jnp.uint32)
    return lax.bitcast_convert_type(u << 16, F32), lax.bitcast_convert_type(u & jnp.uint32(0xFFFF0000), F32)


def _unpack_bf16_pairs(p):
    lo, hi = _unpack_pairs_f32(p)
    return lo.astype(BF16), hi.astype(BF16)


def _ada_kernel(c_ref, w_ref, b_ref, o_ref):
    c = c_ref[...]
    s = c * jax.nn.sigmoid(c)
    o_ref[...] = _dot(s.astype(BF16), w_ref[...].astype(BF16)) + b_ref[...]


def _ada_mod(cvec, ada_w, ada_b):
    depth, d, d6 = ada_w.shape
    parts = d6 // d
    rows = cvec.shape[0]
    return pl.pallas_call(
        _ada_kernel,
        grid=(depth, parts),
        in_specs=[
            pl.BlockSpec((rows, d), lambda l, j: (0, 0)),
            pl.BlockSpec((None, d, d), lambda l, j: (l, 0, j)),
            pl.BlockSpec((None, 1, d), lambda l, j: (l, 0, j)),
        ],
        out_specs=pl.BlockSpec((None, None, rows, d), lambda l, j: (l, j, 0, 0)),
        out_shape=jax.ShapeDtypeStruct((depth, parts, rows, d), F32),
        compiler_params=_cparams(2),
        name="ada_mod",
    )(cvec, ada_w, ada_b.reshape(depth, 1, d6))


def _pre_attn_kernel(hc_ref, hl_ref, mod_ref, g_ref, w_ref, qg_ref, kg_ref, gs_ref, cos_ref, sin_ref,
                     q_ref, k_ref, v_ref, kf_ref, vf_ref, *, n_ctx_tiles, tiles_per_req):
    i = pl.program_id(0)
    seq = _seq_of_tile(i, n_ctx_tiles, tiles_per_req)
    d = hc_ref.shape[1]
    x = _pick_rows(i, n_ctx_tiles, hc_ref, hl_ref)
    a = _modulated(x, g_ref[...], _mod_row(mod_ref, 0, seq), _mod_row(mod_ref, 1, seq))
    qkv = _dot(a.astype(BF16), w_ref[...])
    q = qkv[:, :d]
    k = qkv[:, d:2 * d]
    v = qkv[:, 2 * d:]
    def group_rms_inv(x):
        sq_hi, sq_lo = _split_bf16(x * x)
        w = gs_ref.shape[0]
        sums = [_dot(sq_hi[:, c:c + w], gs_ref[...]) + _dot(sq_lo[:, c:c + w], gs_ref[...]) for c in range(0, d, w)]
        return lax.rsqrt(jnp.concatenate(sums, axis=1) * (1.0 / HEAD_DIM) + EPS)

    qn = (q * group_rms_inv(q)) * qg_ref[...]
    kn = (k * group_rms_inv(k)) * kg_ref[...]
    v_ref[...] = v.astype(BF16)
    q_scale = HEAD_DIM ** -0.5

    @pl.when(i < n_ctx_tiles)
    def _():
        q_ref[...] = (qn * q_scale).astype(BF16)
        k_ref[...] = kn.astype(BF16)
        seq = kf_ref.shape[3]
        for r in range(kf_ref.shape[0]):
            kf_ref[r] = kn[r * seq:(r + 1) * seq, :].T.reshape(kf_ref.shape[1:])
        vf_ref[...] = v

    @pl.when(i >= n_ctx_tiles)
    def _():
        cos = cos_ref[...]
        sin = sin_ref[...]
        lane = lax.broadcasted_iota(jnp.int32, cos.shape, 1)
        first = (lane & 31) < 16

        def rope(x):
            outs = []
            for hh in range(d // LANES):
                xs = x[:, hh * LANES:(hh + 1) * LANES]
                rot = jnp.where(first, pltpu.roll(xs, LANES - 16, 1), pltpu.roll(xs, 16, 1))
                outs.append(xs * cos + rot * sin)
            return jnp.concatenate(outs, axis=1)

        q_ref[...] = (rope(qn) * q_scale).astype(BF16)
        k_ref[...] = rope(kn).astype(BF16)


def _rope_tables(n_pos):
    half = HEAD_DIM // 2
    pos = jnp.arange(n_pos, dtype=jnp.int32)
    row = (pos // GRID_W).astype(F32)
    col = (pos % GRID_W).astype(F32)
    inv = 1.0 / (ROPE_THETA ** (jnp.arange(0, half, 2, dtype=F32) / half))
    ang_r = row[:, None] * inv[None, :]
    ang_c = col[:, None] * inv[None, :]
    ang = jnp.concatenate([ang_r, ang_r, ang_c, ang_c], axis=-1)
    quarter = half // 2
    sign = jnp.tile(jnp.concatenate([-jnp.ones((quarter,), F32), jnp.ones((quarter,), F32)]), 2)
    cos = jnp.tile(jnp.cos(ang), (1, 2))
    sin = jnp.tile(jnp.sin(ang) * sign[None, :], (1, 2))
    return cos, sin


def _pre_attn(h_ctx, h_lat, mod, g, wqkv, qg, kg, ctx_len, rows_per_req):
    n_ctx, d = h_ctx.shape
    t = n_ctx + h_lat.shape[0]
    tm = PRE_ATTN_TILE
    assert tm % ctx_len == 0
    n_ctx_tiles = n_ctx // tm
    tiles_per_req = rows_per_req // tm
    gid = jnp.arange(MXU_TILE, dtype=jnp.int32) // HEAD_DIM
    gs = (gid[:, None] == gid[None, :]).astype(BF16)
    cos, sin = _rope_tables(rows_per_req)
    pos_map = lambda i: (jnp.where(i < n_ctx_tiles, 0, (i - n_ctx_tiles) % tiles_per_req), 0)
    ctx_map = lambda i: (jnp.minimum(i, n_ctx_tiles - 1), 0)
    full = lambda shape: pl.BlockSpec(shape, lambda i: (0,) * len(shape), pipeline_mode=pl.Buffered(1))
    row = pl.BlockSpec((tm, d), lambda i: (i, 0))
    kern = functools.partial(_pre_attn_kernel, n_ctx_tiles=n_ctx_tiles, tiles_per_req=tiles_per_req)
    return pl.pallas_call(
        kern,
        grid=(t // tm,),
        in_specs=_split_specs(tm, d, n_ctx_tiles) + [
                  full(mod.shape), full((1, d)), full(wqkv.shape), full((1, d)), full((1, d)),
                  full(gs.shape),
                  pl.BlockSpec((tm, LANES), pos_map), pl.BlockSpec((tm, LANES), pos_map)],
        out_specs=[row, row, row,
                   pl.BlockSpec((tm // ctx_len, d // HEAD_DIM, HEAD_DIM, ctx_len), lambda i: (ctx_map(i)[0], 0, 0, 0)),
                   pl.BlockSpec((tm, d), ctx_map)],
        out_shape=[jax.ShapeDtypeStruct((t, d), BF16)] * 3 + [
            jax.ShapeDtypeStruct((n_ctx // ctx_len, d // HEAD_DIM, HEAD_DIM, ctx_len), F32),
            jax.ShapeDtypeStruct((n_ctx, d), F32)],
        compiler_params=_cparams(1),
        name="pre_attn",
    )(h_ctx, h_lat, mod, g, wqkv, qg, kg, gs, cos, sin)


def _attn_kernel(*refs, n_heads, has_cache, layer):
    if has_cache:
        lam_ref, sub_ref, q_ref, kn_ref, vn_ref, kc_ref, vc_ref, o_ref = refs
    else:
        lam_ref, sub_ref, q_ref, kn_ref, vn_ref, o_ref = refs
    lf = lam_ref[...]
    lam = (jnp.exp(jnp.sum(lf[0:1] * lf[1:2], axis=-1, keepdims=True))
           - jnp.exp(jnp.sum(lf[2:3] * lf[3:4], axis=-1, keepdims=True)) + _lambda_init(layer))
    nt = (((1,), (1,)), ((), ()))
    tq = q_ref.shape[0]
    lane = lax.broadcasted_iota(jnp.int32, (tq, LANES), 1)
    heads = [slice(hh * LANES, (hh + 1) * LANES) for hh in range(n_heads)]
    scores = []
    for sl in heads:
        qh = q_ref[:, sl]
        zero = jnp.zeros_like(qh)
        keys = [kn_ref[:, sl]] + ([kc_ref[:, sl].astype(BF16)] if has_cache else [])
        for qm in (jnp.where(lane < HEAD_DIM, qh, zero), jnp.where(lane >= HEAD_DIM, qh, zero)):
            scores.append([lax.dot_general(qm, kk, nt, preferred_element_type=F32) for kk in keys])
    maxes = [functools.reduce(jnp.maximum, [jnp.max(x, axis=-1, keepdims=True) for x in s]) for s in scores]
    exps = [[jnp.exp(x - m) for x in s] for s, m in zip(scores, maxes)]
    rinv = [1.0 / functools.reduce(lambda u, w: u + w, [jnp.sum(x, axis=-1, keepdims=True) for x in e]) for e in exps]
    mixed = []
    for hh in range(n_heads):
        e0, e1, r0, r1 = exps[2 * hh], exps[2 * hh + 1], rinv[2 * hh], rinv[2 * hh + 1]
        mixed.append([(x0 * r0 - lam * (x1 * r1)).astype(BF16) for x0, x1 in zip(e0, e1)])
    outs = []
    for sl, a in zip(heads, mixed):
        vals = [vn_ref[:, sl]] + ([vc_ref[:, sl].astype(BF16)] if has_cache else [])
        outs.append(functools.reduce(lambda u, w: u + w, [_dot(aj, vv) for aj, vv in zip(a, vals)]))
    for sl, o in zip(heads, outs):
        o = o * lax.rsqrt(jnp.mean(o * o, axis=-1, keepdims=True) + EPS)
        o = (o * sub_ref[...]) * (1.0 - _lambda_init(layer))
        o_ref[:, sl] = o.astype(BF16)


def _attention(q, k, v, lam_p, subln, cache_k, cache_v, n_ctx, ctx_len, n_req, req_len, layer):
    t, d = q.shape
    small = lambda shape: pl.BlockSpec(shape, lambda *_: (0,) * len(shape))
    n_ctx_req = n_ctx // ctx_len
    blk = pl.BlockSpec((ctx_len, d), lambda b: (b, 0))
    o_ctx = pl.pallas_call(
        functools.partial(_attn_kernel, n_heads=N_HEADS, has_cache=False, layer=layer),
        grid=(n_ctx_req,),
        in_specs=[small(lam_p.shape), small(subln.shape), blk, blk, blk],
        out_specs=blk,
        out_shape=jax.ShapeDtypeStruct((n_ctx, d), BF16),
        compiler_params=_cparams(1),
        name="attn_ctx",
    )(lam_p, subln, q, k, v)

    tq = ATTN_Q_TILE
    nq = req_len // tq
    q0 = n_ctx // tq
    r0 = n_ctx // req_len
    past = cache_k.shape[1]
    hp = ATTN_HEADS_PER_STEP
    small3 = lambda shape: pl.BlockSpec(shape, lambda b, h, i: (0,) * len(shape))
    qspec = pl.BlockSpec((tq, hp * LANES), lambda b, h, i: (q0 + b * nq + i, h))
    kvspec = pl.BlockSpec((req_len, hp * LANES), lambda b, h, i: (r0 + b, h))
    cspec = pl.BlockSpec((None, past, hp * LANES), lambda b, h, i: (b, 0, h))
    o_lat = pl.pallas_call(
        functools.partial(_attn_kernel, n_heads=hp, has_cache=True, layer=layer),
        grid=(n_req, N_HEADS // hp, nq),
        in_specs=[small3(lam_p.shape), pl.BlockSpec((1, LANES), lambda b, h, i: (0, 0)),
                  qspec, kvspec, kvspec, cspec, cspec],
        out_specs=pl.BlockSpec((tq, hp * LANES), lambda b, h, i: (b * nq + i, h)),
        out_shape=jax.ShapeDtypeStruct((n_req * req_len, d), BF16),
        compiler_params=_cparams(3),
        name="attn_latent",
    )(lam_p, subln, q, k, v, cache_k, cache_v)
    return o_ctx, o_lat


def _route(a2, wr, br, ltri, carry):
    tm = a2.shape[0]
    logits = _dot(a2.astype(BF16), wr) + br
    lane_i = lax.broadcasted_iota(jnp.int32, (tm, LANES), 1)
    lane = lane_i.astype(F32)
    neg = jnp.full((tm, LANES), -jnp.inf, F32)
    big = jnp.full((tm, LANES), float(LANES), F32)
    first_lane = lambda mask: jnp.min(jnp.where(mask, lane, big), axis=-1, keepdims=True)

    lc = jnp.where(lane_i < N_EXPERT_GROUPS, logits, neg)
    mc = jnp.max(lc, axis=-1, keepdims=True)
    pg = 1.0 / jnp.sum(jnp.exp(lc - mc), axis=-1, keepdims=True)
    gi = first_lane(lc == mc)
    assert EXPERTS_PER_GROUP == 8
    grp = lax.shift_right_arithmetic(lane_i - L_EXPERT0, 3).astype(F32)
    in_group = (lane_i >= L_EXPERT0) & (lane_i < L_EXPERT0 + N_EXPERTS) & (grp == gi)
    ls = jnp.where(in_group, logits, neg)
    t1 = jnp.max(ls, axis=-1, keepdims=True)
    i1 = first_lane(ls == t1)
    ls2 = jnp.where(lane == i1, neg, ls)
    t2 = jnp.max(ls2, axis=-1, keepdims=True)
    i2 = first_lane(ls2 == t2)
    ex = jnp.exp(t2 - t1)
    w1 = pg * (1.0 / (1.0 + ex))
    w2 = pg * (ex / (1.0 + ex))
    e1 = i1 - float(L_EXPERT0)
    e2 = i2 - float(L_EXPERT0)
    oh1 = lane == e1
    oh2 = lane == e2
    onehot = oh1.astype(F32) + oh2.astype(F32)
    before = _dot(ltri, onehot.astype(BF16)) + carry
    zero = jnp.zeros_like(before)
    rank1 = jnp.sum(jnp.where(oh1, before, zero), axis=-1, keepdims=True)
    rank2 = jnp.sum(jnp.where(oh2, before, zero), axis=-1, keepdims=True)
    slab = jnp.zeros((tm, LANES), F32)
    for ln, val in ((R_E1, e1), (R_E2, e2), (R_W1, w1), (R_W2, w2), (R_RANK1, rank1), (R_RANK2, rank2)):
        slab = jnp.where(lane_i == ln, val, slab)
    return slab, carry + jnp.sum(onehot, axis=0, keepdims=True)


def _router_weights(wc, bc, wf, bf_):
    d = wc.shape[0]
    pad = LANES - N_EXPERT_GROUPS - N_EXPERTS
    w = jnp.concatenate([wc, wf, jnp.zeros((d, pad), F32)], axis=1)
    b = jnp.concatenate([bc, bf_, jnp.zeros((pad,), F32)])[None, :]
    return w.astype(BF16), b


def _lower_tri(n):
    r = jnp.arange(n, dtype=jnp.int32)
    return (r[None, :] < r[:, None]).astype(BF16)


def _post_attn_kernel(hc_ref, hl_ref, oc_ref, ol_ref, mod_ref, wo_ref, g2_ref, wr_ref, br_ref, ltri_ref,
                      h1_ref, a2_ref, route_ref, cnt_ref, carry_ref, *, n_ctx_tiles, tiles_per_req):
    i = pl.program_id(0)
    seq = _seq_of_tile(i, n_ctx_tiles, tiles_per_req)

    @pl.when(i == 0)
    def _():
        carry_ref[...] = jnp.zeros_like(carry_ref)

    subs = [pl.ds(s * ROW_TILE, ROW_TILE) for s in range(h1_ref.shape[0] // ROW_TILE)]
    mixes = [_dot(jnp.where(i < n_ctx_tiles, oc_ref[rows, :], ol_ref[rows, :]), wo_ref[...]) for rows in subs]
    h1s = [jnp.where(i < n_ctx_tiles, hc_ref[rows, :], hl_ref[rows, :]) + _mod_row(mod_ref, 2, seq) * mix
           for rows, mix in zip(subs, mixes)]
    a2s = [_modulated(h1, g2_ref[...], _mod_row(mod_ref, 3, seq), _mod_row(mod_ref, 4, seq)) for h1 in h1s]
    for rows, h1, a2 in zip(subs, h1s, a2s):
        h1_ref[rows, :] = h1
        a2_ref[rows, :] = _pack_bf16_pairs(a2)
    carry = carry_ref[...]
    for rows, a2 in zip(subs, a2s):
        route_ref[rows, :], carry = _route(a2, wr_ref[...], br_ref[...], ltri_ref[...], carry)
    carry_ref[...] = carry
    cnt_ref[...] = carry


def _post_attn(h_ctx, h_lat, o_ctx, o_lat, mod, wo, g2, router, rows_per_req):
    n_ctx, d = h_ctx.shape
    t = n_ctx + h_lat.shape[0]
    tm = ROW_TILE * SUBTILES
    wr, br = router
    ltri = _lower_tri(ROW_TILE)
    full = lambda shape: pl.BlockSpec(shape, lambda i: (0,) * len(shape))
    row = pl.BlockSpec((tm, d), lambda i: (i, 0))
    kern = functools.partial(_post_attn_kernel, n_ctx_tiles=n_ctx // tm, tiles_per_req=rows_per_req // tm)
    return pl.pallas_call(
        kern,
        grid=(t // tm,),
        in_specs=_split_specs(tm, d, n_ctx // tm) + _split_specs(tm, d, n_ctx // tm) + [
                  full(mod.shape), full(wo.shape), full((1, d)), full(wr.shape),
                  full(br.shape), full(ltri.shape)],
        out_specs=[row, pl.BlockSpec((tm, d // 2), lambda i: (i, 0)), pl.BlockSpec((tm, LANES), lambda i: (i, 0)),
                   full((1, LANES))],
        out_shape=[jax.ShapeDtypeStruct((t, d), F32), jax.ShapeDtypeStruct((t, d // 2), jnp.int32),
                   jax.ShapeDtypeStruct((t, LANES), F32), jax.ShapeDtypeStruct((1, LANES), F32)],
        scratch_shapes=[pltpu.VMEM((1, LANES), F32)],
        compiler_params=_cparams(1),
        name="post_attn_router",
    )(h_ctx, h_lat, o_ctx, o_lat, mod, wo, g2, wr, br, ltri)


def _gather_rows(src, idx):
    n = idx.shape[0]
    d = src.shape[1]
    w = _sc_window_rows(src)
    mesh = plsc.VectorSubcoreMesh(core_axis_name="core", subcore_axis_name="subcore")

    @pl.kernel(out_type=jax.ShapeDtypeStruct((n, d), src.dtype), mesh=mesh)
    def gather(src_hbm, idx_hbm, out_hbm):
        def body(idx_vmem, out_vmem):
            pltpu.sync_copy(src_hbm.at[idx_vmem.at[0, pl.ds(0, w)]], out_vmem)

        pltpu.emit_pipeline(
            body,
            grid=(n // w,),
            in_specs=[pl.BlockSpec((1, LANES), lambda i: (i, 0))],
            out_specs=[pl.BlockSpec((w, d), lambda i: (i, 0))],
            core_axis_name=("core", "subcore"),
            dimension_semantics=(pltpu.PARALLEL,),
        )(idx_hbm, out_hbm)

    return gather(src, _index_windows(idx, w))


def _sc_window_rows(src):
    return min(LANES, SC_WINDOW_BYTES // (src.shape[1] * src.dtype.itemsize))


def _index_windows(idx, w):
    return jnp.pad(idx.reshape(idx.shape[0] // w, w), ((0, 0), (0, LANES - w)))


def _scatter_rows_twice(src, idx_a, idx_b, n_out):
    t, d = src.shape
    w = _sc_window_rows(src)
    mesh = plsc.VectorSubcoreMesh(core_axis_name="core", subcore_axis_name="subcore")

    @pl.kernel(out_type=jax.ShapeDtypeStruct((n_out, d), src.dtype), mesh=mesh)
    def scatter(src_hbm, ia_hbm, ib_hbm, out_hbm):
        def body(src_vmem, ia_vmem, ib_vmem):
            pltpu.sync_copy(src_vmem, out_hbm.at[ia_vmem.at[0, pl.ds(0, w)]])
            pltpu.sync_copy(src_vmem, out_hbm.at[ib_vmem.at[0, pl.ds(0, w)]])

        pltpu.emit_pipeline(
            body,
            grid=(t // w,),
            in_specs=[pl.BlockSpec((w, d), lambda i: (i, 0)),
                      pl.BlockSpec((1, LANES), lambda i: (i, 0)),
                      pl.BlockSpec((1, LANES), lambda i: (i, 0))],
            out_specs=[],
            core_axis_name=("core", "subcore"),
            dimension_semantics=(pltpu.PARALLEL,),
        )(src_hbm, ia_hbm, ib_hbm)

    return scatter(src, _index_windows(idx_a, w), _index_windows(idx_b, w))


def _expert_kernel(be_ref, nb_ref, ne_ref, x_ref, w1_hbm, w3_hbm, w2_hbm, y_ref,
                   w1s, w3s, w2s, w1b, w3b, w2b, sems, *, layer):
    b = pl.program_id(0)
    prev = be_ref[jnp.maximum(b - 1, 0)]
    used = b < nb_ref[0]

    def stage(e):
        return [pltpu.make_async_copy(src.at[layer, e], dst, sems.at[j])
                for j, (src, dst) in enumerate(((w1_hbm, w1s), (w3_hbm, w3s), (w2_hbm, w2s)))]

    @pl.when(used & ((b == 0) | (be_ref[b] != prev)))
    def _():
        @pl.when(b == 0)
        def _():
            for c in stage(be_ref[0]):
                c.start()

        for c in stage(be_ref[b]):
            c.wait()
        w1b[...] = w1s[...].astype(BF16)
        w3b[...] = w3s[...].astype(BF16)
        w2b[...] = w2s[...].astype(BF16)

        @pl.when(ne_ref[b] >= 0)
        def _():
            for c in stage(ne_ref[b]):
                c.start()

    @pl.when(used)
    def _():
        x_lo, x_hi = _unpack_bf16_pairs(x_ref[...])
        half = x_lo.shape[1]
        h1 = _dot(x_lo, w1b[:half, :]) + _dot(x_hi, w1b[half:, :])
        h3 = _dot(x_lo, w3b[:half, :]) + _dot(x_hi, w3b[half:, :])
        hb = jax.nn.silu(h1) * h3
        y_ref[...] = _pack_bf16_pairs(_dot(hb.astype(BF16), w2b[...]))


def _expert_mlps(xs, block_e, n_used, next_e, w1, w3, w2, layer):
    npad, half = xs.shape
    d, ff = w1.shape[2], w1.shape[3]
    assert d == 2 * half
    bm = MOE_BLOCK
    rows = lambda b, be, nb, ne: (jnp.minimum(b, nb[0] - 1), 0)
    hbm = pl.BlockSpec(memory_space=pl.ANY)
    grid_spec = pltpu.PrefetchScalarGridSpec(
        num_scalar_prefetch=3,
        grid=(npad // bm,),
        in_specs=[pl.BlockSpec((bm, half), rows), hbm, hbm, hbm],
        out_specs=pl.BlockSpec((bm, half), rows),
        scratch_shapes=[pltpu.VMEM((d, ff), F32), pltpu.VMEM((d, ff), F32), pltpu.VMEM((ff, d), F32),
                        pltpu.VMEM((d, ff), BF16), pltpu.VMEM((d, ff), BF16), pltpu.VMEM((ff, d), BF16),
                        pltpu.SemaphoreType.DMA((3,))],
    )
    return pl.pallas_call(
        functools.partial(_expert_kernel, layer=layer),
        grid_spec=grid_spec,
        out_shape=jax.ShapeDtypeStruct((npad, half), jnp.int32),
        compiler_params=_cparams(1),
        name="expert_mlps",
    )(block_e, n_used, next_e, xs, w1, w3, w2)


def _moe(a2p, route, counts, w1, w3, w2, layer, row_ranges):
    t = a2p.shape[0]
    bm = MOE_BLOCK
    n = 2 * t
    e = route[:, R_E1:R_E2 + 1].astype(jnp.int32).reshape(n)
    rank = route[:, R_RANK1:R_RANK2 + 1].astype(jnp.int32).reshape(n)
    cnt = counts[0, :N_EXPERTS].astype(jnp.int32)
    padded = ((cnt + bm - 1) // bm) * bm
    pend = jnp.cumsum(padded)
    pstart = pend - padded
    pos = jnp.sum(jnp.where(e[:, None] == jnp.arange(N_EXPERTS, dtype=jnp.int32)[None, :], pstart[None, :], 0),
                  axis=1) + rank
    n_blocks = n // bm + N_EXPERTS
    npad = n_blocks * bm
    starts = jnp.arange(n_blocks, dtype=jnp.int32) * bm
    n_used = pend[-1:] // bm
    last_start = (n_used[0] - 1) * bm
    block_e = jnp.sum((pend[None, :] <= jnp.minimum(starts, last_start)[:, None]).astype(jnp.int32), axis=1)
    blk = jnp.arange(n_blocks, dtype=jnp.int32)
    later_other = (blk[None, :] > blk[:, None]) & (block_e[None, :] != block_e[:, None]) & (blk[None, :] < n_used[0])
    first_later = jnp.min(jnp.where(later_other, blk[None, :], n_blocks), axis=1)
    next_e = jnp.where(first_later < n_blocks, block_e[jnp.minimum(first_later, n_blocks - 1)], -1).astype(jnp.int32)
    pos_first, pos_second = pos[0::2], pos[1::2]
    xs = _scatter_rows_twice(a2p, pos_first, pos_second, npad)
    ys = _expert_mlps(xs, block_e, n_used, next_e, w1, w3, w2, layer)
    return [_gather_rows(ys, jnp.concatenate([pos_first[r0:r1], pos_second[r0:r1]])) for r0, r1 in row_ranges]


def _moe_combine(h, yp_first, yp_second, route, g2):
    w1 = route[:, R_W1:R_W1 + 1]
    w2 = route[:, R_W2:R_W2 + 1]
    wide = lambda p: jnp.concatenate(_unpack_pairs_f32(p), axis=1)
    return h + g2 * (w1 * wide(yp_first) + w2 * wide(yp_second))


def _gmlp_kernel(h_ref, ya_ref, yb_ref, rt_ref, modp_ref, mod_ref, g1_ref, win_ref, bin_ref, vg_ref, ws_ref, bsb_ref,
                 wout_ref, g2_ref, wr_ref, br_ref, ltri_ref,
                 h1_ref, a2_ref, route_ref, cnt_ref, carry_ref, *, n_ctx_tiles, tiles_per_req):
    i = pl.program_id(0)
    seq = _seq_of_tile(i, n_ctx_tiles, tiles_per_req)

    @pl.when(i == 0)
    def _():
        carry_ref[...] = jnp.zeros_like(carry_ref)

    gw = vg_ref.shape[1]
    cg = gw // GM_GROUPS
    carry = carry_ref[...]
    for s in range(h_ref.shape[0] // GMLP_SUB):
        rs = pl.ds(s * GMLP_SUB, GMLP_SUB)
        h = _moe_combine(h_ref[rs, :], ya_ref[rs, :], yb_ref[rs, :], rt_ref[rs, :], _mod_row(modp_ref, 5, seq))
        a = _modulated(h, g1_ref[...], _mod_row(mod_ref, 0, seq), _mod_row(mod_ref, 1, seq))
        z = _gelu_tanh(_dot(a.astype(BF16), win_ref[...]) + bin_ref[...])
        u = z[:, :gw]
        v = z[:, gw:]
        v = (v * lax.rsqrt(jnp.mean(v * v, axis=-1, keepdims=True) + EPS)) * vg_ref[...]
        vb = v.astype(BF16)
        rows = []
        for c in range(GMLP_SUB // CHUNK):
            cols = []
            for g in range(GM_GROUPS):
                cols.append(_dot(ws_ref[g], vb[c * CHUNK:(c + 1) * CHUNK, g * cg:(g + 1) * cg]))
            rows.append(jnp.concatenate(cols, axis=1) + bsb_ref[...])
        vm = jnp.concatenate(rows, axis=0)
        mix = _dot((u * vm).astype(BF16), wout_ref[...])
        h1 = h + _mod_row(mod_ref, 2, seq) * mix
        h1_ref[rs, :] = h1
        a2 = _modulated(h1, g2_ref[...], _mod_row(mod_ref, 3, seq), _mod_row(mod_ref, 4, seq))
        a2_ref[rs, :] = _pack_bf16_pairs(a2)
        route_ref[rs, :], carry = _route(a2, wr_ref[...], br_ref[...], ltri_ref[...], carry)
    carry_ref[...] = carry
    cnt_ref[...] = carry


def _gmlp_layer(h, y2, route_prev, mod_prev, mod, g1, win, bin_, vg, ws, bsb, wout, g2, router,
                n_ctx, rows_per_req):
    t, d = h.shape
    tm = ROW_TILE * SUBTILES
    wr, br = router
    ltri = _lower_tri(GMLP_SUB)
    full = lambda shape: pl.BlockSpec(shape, lambda i: (0,) * len(shape), pipeline_mode=pl.Buffered(1))
    row = lambda w: pl.BlockSpec((tm, w), lambda i: (i, 0))
    kern = functools.partial(_gmlp_kernel, n_ctx_tiles=n_ctx // tm, tiles_per_req=rows_per_req // tm)
    args = (h, y2, y2, route_prev, mod_prev, mod, g1, win, bin_, vg, ws, bsb, wout, g2, wr, br, ltri)
    second = pl.BlockSpec((tm, d // 2), lambda i: (i + t // tm, 0))
    in_specs = [row(d), row(d // 2), second, row(LANES)] + [full(a.shape) for a in args[4:]]
    return pl.pallas_call(
        kern,
        grid=(t // tm,),
        in_specs=in_specs,
        out_specs=[row(d), row(d // 2), row(LANES), pl.BlockSpec((1, LANES), lambda i: (0, 0))],
        out_shape=[jax.ShapeDtypeStruct((t, d), F32), jax.ShapeDtypeStruct((t, d // 2), jnp.int32),
                   jax.ShapeDtypeStruct((t, LANES), F32), jax.ShapeDtypeStruct((1, LANES), F32)],
        scratch_shapes=[pltpu.VMEM((1, LANES), F32)],
        compiler_params=_cparams(1),
        name="gmlp_router",
    )(*args)


def _final_kernel(h_ref, ya_ref, yb_ref, rt_ref, mod_ref, o_ref, *, first_tile, n_ctx_tiles, tiles_per_req):
    seq = _seq_of_tile(first_tile + pl.program_id(0), n_ctx_tiles, tiles_per_req)
    o_ref[...] = _moe_combine(h_ref[...], ya_ref[...], yb_ref[...], rt_ref[...], _mod_row(mod_ref, 5, seq))


def _final_combine(h, y2, route, mod, first_row, n_ctx, rows_per_req):
    d = h.shape[1]
    n = y2.shape[0] // 2
    tm = ROW_TILE
    first_tile = first_row // tm
    row = lambda w: pl.BlockSpec((tm, w), lambda i: (first_tile + i, 0))
    kern = functools.partial(_final_kernel, first_tile=first_tile, n_ctx_tiles=n_ctx // tm,
                             tiles_per_req=rows_per_req // tm)
    return pl.pallas_call(
        kern,
        grid=(n // tm,),
        in_specs=[row(d), pl.BlockSpec((tm, d // 2), lambda i: (i, 0)),
                  pl.BlockSpec((tm, d // 2), lambda i: (i + n // tm, 0)),
                  row(LANES), pl.BlockSpec(mod.shape, lambda i: (0, 0, 0))],
        out_specs=pl.BlockSpec((tm, d), lambda i: (i, 0)),
        out_shape=jax.ShapeDtypeStruct((n, d), F32),
        compiler_params=_cparams(1),
        name="final_combine",
    )(h, y2, y2, route, mod)


def kernel(x_prompt, x_sample, cache_k, cache_v, c, c_ctx, ada_w, ada_b, norm1_g, norm2_g, attn_wq, attn_wk,
           attn_wv, attn_wo, attn_qnorm, attn_knorm, attn_lam, attn_subln, gm_win, gm_bin, gm_vnorm, gm_ws,
           gm_bs, gm_wout, moe_wc, moe_bc, moe_wf, moe_bf, moe_w1, moe_w3, moe_w2):
    batch, seq_len, d = x_prompt.shape
    n_req, req_len, _ = x_sample.shape
    depth = ada_w.shape[0]
    assert depth == 2 and attn_wq.shape[0] == 1 and gm_win.shape[0] == 1
    n_ctx = batch * seq_len
    assert n_ctx % req_len == 0 and req_len % ROW_TILE == 0 and seq_len == ROW_TILE
    past = cache_k.shape[2]

    h_ctx = x_prompt.reshape(n_ctx, d)
    h_lat = x_sample.reshape(n_req * req_len, d)
    cvec = jnp.concatenate([c_ctx[None, :], c, jnp.zeros((8 - 1 - n_req, d), F32)], axis=0)
    mod = _ada_mod(cvec, ada_w, ada_b)
    routers = [_router_weights(moe_wc[i], moe_bc[i], moe_wf[i], moe_bf[i]) for i in range(depth)]
    row_vec = lambda v: v.reshape(1, -1)

    wqkv = jnp.concatenate([attn_wq[0], attn_wk[0], attn_wv[0]], axis=1).astype(BF16)
    reps = d // HEAD_DIM
    q, k, v, k_new, v_new = _pre_attn(h_ctx, h_lat, mod[0], row_vec(norm1_g[0]), wqkv,
                                      row_vec(jnp.tile(attn_qnorm[0], reps)), row_vec(jnp.tile(attn_knorm[0], reps)),
                                      seq_len, req_len)
    o_ctx, o_lat = _attention(q, k, v, attn_lam[0], row_vec(attn_subln[0]),
                              cache_k[:, 0].reshape(n_req, past, d), cache_v[:, 0].reshape(n_req, past, d),
                              n_ctx, seq_len, n_req, req_len, layer=0)
    h1, a2, route0, cnt0 = _post_attn(h_ctx, h_lat, o_ctx, o_lat, mod[0], attn_wo[0].astype(BF16),
                                      row_vec(norm2_g[0]), routers[0], req_len)
    t = n_ctx + n_req * req_len
    (y2,) = _moe(a2, route0, cnt0, moe_w1, moe_w3, moe_w2, 0, [(0, t)])

    gw = gm_vnorm.shape[1]
    bsb = jnp.repeat(gm_bs[0].T, gw // GM_GROUPS, axis=1)
    h2, a2, route1, cnt1 = _gmlp_layer(h1, y2, route0, mod[0], mod[1], row_vec(norm1_g[1]), gm_win[0].astype(BF16),
                                       row_vec(gm_bin[0]), row_vec(gm_vnorm[0]), gm_ws[0].astype(BF16), bsb,
                                       gm_wout[0].astype(BF16), row_vec(norm2_g[1]), routers[1], n_ctx, req_len)
    y2_ctx, y2_lat = _moe(a2, route1, cnt1, moe_w1, moe_w3, moe_w2, 1, [(0, n_ctx), (n_ctx, t)])
    y_ctx = _final_combine(h2, y2_ctx, route1, mod[1], 0, n_ctx, req_len)
    y_lat = _final_combine(h2, y2_lat, route1, mod[1], n_ctx, n_ctx, req_len)

    y_prompt = y_ctx.reshape(batch, seq_len, d)
    y_sample = y_lat.reshape(n_req, req_len, d)
    new_cache_k = jnp.transpose(k_new.reshape(batch, 1, N_HEADS, 2, HEAD_DIM, seq_len), (0, 1, 5, 2, 3, 4))
    new_cache_v = v_new.reshape(batch, 1, seq_len, N_HEADS, V_DIM)
    return (y_prompt, y_sample, new_cache_k, new_cache_v)
```

```python
import functools
import math

import jax
import jax.numpy as jnp
from jax import lax
from jax.experimental import pallas as pl
from jax.experimental.pallas import tpu as pltpu
from jax.experimental.pallas import tpu_sc as plsc

F32 = jnp.float32
BF16 = jnp.bfloat16

N_HEADS = 8
HEAD_DIM = 64
V_DIM = 2 * HEAD_DIM
GRID_W = 64
ROPE_THETA = 10000.0
CHUNK = 128
GM_GROUPS = 8
N_EXPERT_GROUPS = 4
EXPERTS_PER_GROUP = 8
N_EXPERTS = N_EXPERT_GROUPS * EXPERTS_PER_GROUP
EPS = 1e-6

LANES = 128
MXU_TILE = 256
ROW_TILE = 256
SUBTILES = 2
GMLP_SUB = 512
PRE_ATTN_TILE = 512
MOE_BLOCK = 256
SC_WINDOW_BYTES = 128 * 1024
ATTN_Q_TILE = 512
ATTN_HEADS_PER_STEP = 2
VMEM_LIMIT = 56 * 1024 * 1024

R_E1, R_E2, R_W1, R_W2, R_RANK1, R_RANK2 = 0, 1, 2, 3, 4, 5
L_EXPERT0 = N_EXPERT_GROUPS


def _lambda_init(layer):
    return 0.8 - 0.6 * math.exp(-0.3 * layer)


def _cparams(n_axes):
    return pltpu.CompilerParams(dimension_semantics=("arbitrary",) * n_axes, vmem_limit_bytes=VMEM_LIMIT)


def _seq_of_tile(i, n_ctx_tiles, tiles_per_req):
    return jnp.where(i < n_ctx_tiles, 0, 1 + (i - n_ctx_tiles) // tiles_per_req)


def _pick_rows(i, n_ctx_tiles, ctx_ref, lat_ref):
    return jnp.where(i < n_ctx_tiles, ctx_ref[...], lat_ref[...])


def _split_specs(tm, d, n_ctx_tiles):
    return [pl.BlockSpec((tm, d), lambda i: (jnp.minimum(i, n_ctx_tiles - 1), 0)),
            pl.BlockSpec((tm, d), lambda i: (jnp.maximum(i - n_ctx_tiles, 0), 0))]


def _mod_row(mod_ref, part, seq):
    return mod_ref[part, pl.ds(seq, 1), :]


def _modulated(x, g, shift, scale):
    y = x * lax.rsqrt(jnp.mean(x * x, axis=-1, keepdims=True) + EPS)
    return (y * g) * (1.0 + scale) + shift


def _split_bf16(x):
    hi = x.astype(BF16)
    lo = (x - hi.astype(F32)).astype(BF16)
    return hi, lo


def _gelu_tanh(x):
    c = math.sqrt(2.0 / math.pi)
    hx = 0.5 * x
    return hx + hx * jnp.tanh(x * (c + (c * 0.044715) * (x * x)))


def _dot(a, b):
    return jnp.dot(a, b, preferred_element_type=F32)


def _pack_bf16_pairs(x):
    half = x.shape[1] // 2
    bits = lax.bitcast_convert_type(x.astype(BF16).astype(F32), jnp.uint32)
    packed = (bits[:, :half] >> 16) | bits[:, half:]
    return lax.bitcast_convert_type(packed, jnp.int32)


def _unpack_pairs_f32(p):
    u = lax.bitcast_convert_type(p, jnp.uint32)
    return lax.bitcast_convert_type(u << 16, F32), lax.bitcast_convert_type(u & jnp.uint32(0xFFFF0000), F32)


def _unpack_bf16_pairs(p):
    lo, hi = _unpack_pairs_f32(p)
    return lo.astype(BF16), hi.astype(BF16)


def _ada_kernel(c_ref, w_ref, b_ref, o_ref):
    c = c_ref[...]
    s = c * jax.nn.sigmoid(c)
    o_ref[...] = _dot(s.astype(BF16), w_ref[...].astype(BF16)) + b_ref[...]


def _ada_mod(cvec, ada_w, ada_b):
    depth, d, d6 = ada_w.shape
    parts = d6 // d
    rows = cvec.shape[0]
    return pl.pallas_call(
        _ada_kernel,
        grid=(depth, parts),
        in_specs=[
            pl.BlockSpec((rows, d), lambda l, j: (0, 0)),
            pl.BlockSpec((None, d, d), lambda l, j: (l, 0, j)),
            pl.BlockSpec((None, 1, d), lambda l, j: (l, 0, j)),
        ],
        out_specs=pl.BlockSpec((None, None, rows, d), lambda l, j: (l, j, 0, 0)),
        out_shape=jax.ShapeDtypeStruct((depth, parts, rows, d), F32),
        compiler_params=_cparams(2),
        name="ada_mod",
    )(cvec, ada_w, ada_b.reshape(depth, 1, d6))


def _pre_attn_kernel(*refs, latent, tiles_per_req):
    if latent:
        h_ref, mod_ref, g_ref, w_ref, qg_ref, kg_ref, gs_ref, cos_ref, sin_ref, q_ref, k_ref, v_ref = refs
        seq = 1 + pl.program_id(0) // tiles_per_req
    else:
        h_ref, mod_ref, g_ref, w_ref, qg_ref, kg_ref, gs_ref, q_ref, k_ref, v_ref, kf_ref, vf_ref = refs
        seq = 0
    d = h_ref.shape[1]
    q_scale = HEAD_DIM ** -0.5
    subs = [pl.ds(s * ROW_TILE, ROW_TILE) for s in range(q_ref.shape[0] // ROW_TILE)]

    def group_rms_inv(x):
        sq_hi, sq_lo = _split_bf16(x * x)
        w = gs_ref.shape[0]
        sums = [_dot(sq_hi[:, c:c + w], gs_ref[...]) + _dot(sq_lo[:, c:c + w], gs_ref[...]) for c in range(0, d, w)]
        return lax.rsqrt(jnp.concatenate(sums, axis=1) * (1.0 / HEAD_DIM) + EPS)

    def rope(x, rows):
        cos = cos_ref[rows, :]
        sin = sin_ref[rows, :]
        lane = lax.broadcasted_iota(jnp.int32, cos.shape, 1)
        first = (lane & 31) < 16
        outs = []
        for hh in range(d // LANES):
            xs = x[:, hh * LANES:(hh + 1) * LANES]
            rot = jnp.where(first, pltpu.roll(xs, LANES - 16, 1), pltpu.roll(xs, 16, 1))
            outs.append(xs * cos + rot * sin)
        return jnp.concatenate(outs, axis=1)

    acts = [_modulated(h_ref[rows, :], g_ref[...], _mod_row(mod_ref, 0, seq), _mod_row(mod_ref, 1, seq)).astype(BF16)
            for rows in subs]
    qkvs = [_dot(a, w_ref[...]) for a in acts]
    qns = [(qkv[:, :d] * group_rms_inv(qkv[:, :d])) * qg_ref[...] for qkv in qkvs]
    kns = [(qkv[:, d:2 * d] * group_rms_inv(qkv[:, d:2 * d])) * kg_ref[...] for qkv in qkvs]
    if latent:
        qrs = [rope(qn, rows) for qn, rows in zip(qns, subs)]
        krs = [rope(kn, rows) for kn, rows in zip(kns, subs)]
    else:
        qrs, krs = qns, kns
    for rows, qkv, qr, kr in zip(subs, qkvs, qrs, krs):
        q_ref[rows, :] = (qr * q_scale).astype(BF16)
        k_ref[rows, :] = kr.astype(BF16)
        v_ref[rows, :] = qkv[:, 2 * d:].astype(BF16)
    if not latent:
        kn_all = jnp.concatenate(kns, axis=0)
        n_pos = kf_ref.shape[3]
        for r in range(kf_ref.shape[0]):
            kf_ref[r] = kn_all[r * n_pos:(r + 1) * n_pos, :].T.reshape(kf_ref.shape[1:])
        for rows, qkv in zip(subs, qkvs):
            vf_ref[rows, :] = qkv[:, 2 * d:]


def _rope_tables(n_pos):
    half = HEAD_DIM // 2
    pos = jnp.arange(n_pos, dtype=jnp.int32)
    row = (pos // GRID_W).astype(F32)
    col = (pos % GRID_W).astype(F32)
    inv = 1.0 / (ROPE_THETA ** (jnp.arange(0, half, 2, dtype=F32) / half))
    ang_r = row[:, None] * inv[None, :]
    ang_c = col[:, None] * inv[None, :]
    ang = jnp.concatenate([ang_r, ang_r, ang_c, ang_c], axis=-1)
    quarter = half // 2
    sign = jnp.tile(jnp.concatenate([-jnp.ones((quarter,), F32), jnp.ones((quarter,), F32)]), 2)
    cos = jnp.tile(jnp.cos(ang), (1, 2))
    sin = jnp.tile(jnp.sin(ang) * sign[None, :], (1, 2))
    return cos, sin


def _pre_attn(h, mod, g, wqkv, qg, kg, *, ctx_len=None, rows_per_req=None):
    n, d = h.shape
    latent = rows_per_req is not None
    tm = PRE_ATTN_TILE
    gid = jnp.arange(MXU_TILE, dtype=jnp.int32) // HEAD_DIM
    gs = (gid[:, None] == gid[None, :]).astype(BF16)
    full = lambda shape: pl.BlockSpec(shape, lambda i: (0,) * len(shape), pipeline_mode=pl.Buffered(1))
    row = pl.BlockSpec((tm, d), lambda i: (i, 0))
    args = [h, mod, g, wqkv, qg, kg, gs]
    in_specs = [row] + [full(a.shape) for a in args[1:]]
    out_specs = [row, row, row]
    out_shape = [jax.ShapeDtypeStruct((n, d), BF16)] * 3
    if latent:
        tiles_per_req = rows_per_req // tm
        pos = pl.BlockSpec((tm, LANES), lambda i: (i % tiles_per_req, 0))
        args += list(_rope_tables(rows_per_req))
        in_specs += [pos, pos]
    else:
        assert tm % ctx_len == 0
        tiles_per_req = None
        out_specs += [pl.BlockSpec((tm // ctx_len, d // HEAD_DIM, HEAD_DIM, ctx_len), lambda i: (i, 0, 0, 0)), row]
        out_shape += [jax.ShapeDtypeStruct((n // ctx_len, d // HEAD_DIM, HEAD_DIM, ctx_len), F32),
                      jax.ShapeDtypeStruct((n, d), F32)]
    return pl.pallas_call(
        functools.partial(_pre_attn_kernel, latent=latent, tiles_per_req=tiles_per_req),
        grid=(n // tm,),
        in_specs=in_specs,
        out_specs=out_specs,
        out_shape=out_shape,
        compiler_params=_cparams(1),
        name="pre_attn_latent" if latent else "pre_attn_ctx",
    )(*args)


def _attn_kernel(*refs, n_heads, has_cache, layer):
    if has_cache:
        lam_ref, sub_ref, q_ref, kn_ref, vn_ref, kc_ref, vc_ref, o_ref = refs
    else:
        lam_ref, sub_ref, q_ref, kn_ref, vn_ref, o_ref = refs
    lf = lam_ref[...]
    lam = (jnp.exp(jnp.sum(lf[0:1] * lf[1:2], axis=-1, keepdims=True))
           - jnp.exp(jnp.sum(lf[2:3] * lf[3:4], axis=-1, keepdims=True)) + _lambda_init(layer))
    nt = (((1,), (1,)), ((), ()))
    tq = q_ref.shape[0]
    lane = lax.broadcasted_iota(jnp.int32, (tq, LANES), 1)
    heads = [slice(hh * LANES, (hh + 1) * LANES) for hh in range(n_heads)]

    def qk(sl):
        qh = q_ref[:, sl]
        zero = jnp.zeros_like(qh)
        keys = [kn_ref[:, sl]] + ([kc_ref[:, sl].astype(BF16)] if has_cache else [])
        return [[lax.dot_general(qm, kk, nt, preferred_element_type=F32) for kk in keys]
                for qm in (jnp.where(lane < HEAD_DIM, qh, zero), jnp.where(lane >= HEAD_DIM, qh, zero))]

    def soft(scores):
        maxes = [functools.reduce(jnp.maximum, [jnp.max(x, axis=-1, keepdims=True) for x in s]) for s in scores]
        exps = [[jnp.exp(x - m) for x in s] for s, m in zip(scores, maxes)]
        r0, r1 = [1.0 / functools.reduce(lambda u, w: u + w, [jnp.sum(x, axis=-1, keepdims=True) for x in e])
                  for e in exps]
        return [(x0 * r0 - lam * (x1 * r1)).astype(BF16) for x0, x1 in zip(*exps)]

    def pv(sl, a):
        vals = [vn_ref[:, sl]] + ([vc_ref[:, sl].astype(BF16)] if has_cache else [])
        o = functools.reduce(lambda u, w: u + w, [_dot(aj, vv) for aj, vv in zip(a, vals)])
        o = o * lax.rsqrt(jnp.mean(o * o, axis=-1, keepdims=True) + EPS)
        o_ref[:, sl] = ((o * sub_ref[...]) * (1.0 - _lambda_init(layer))).astype(BF16)

    scores = [qk(sl) for sl in heads]
    mixed = [soft(sc) for sc in scores]
    for sl, a in zip(heads, mixed):
        pv(sl, a)


def _attention(qkv_ctx, qkv_lat, lam_p, subln, cache_k, cache_v, ctx_len, n_req, req_len, layer):
    n_ctx, d = qkv_ctx[0].shape
    small = lambda shape: pl.BlockSpec(shape, lambda *_: (0,) * len(shape))
    n_ctx_req = n_ctx // ctx_len
    blk = pl.BlockSpec((ctx_len, d), lambda b: (b, 0))
    o_ctx = pl.pallas_call(
        functools.partial(_attn_kernel, n_heads=N_HEADS, has_cache=False, layer=layer),
        grid=(n_ctx_req,),
        in_specs=[small(lam_p.shape), small(subln.shape), blk, blk, blk],
        out_specs=blk,
        out_shape=jax.ShapeDtypeStruct((n_ctx, d), BF16),
        compiler_params=_cparams(1),
        name="attn_ctx",
    )(lam_p, subln, *qkv_ctx)

    tq = ATTN_Q_TILE
    nq = req_len // tq
    past = cache_k.shape[1]
    hp = ATTN_HEADS_PER_STEP
    small3 = lambda shape: pl.BlockSpec(shape, lambda b, h, i: (0,) * len(shape))
    qspec = pl.BlockSpec((tq, hp * LANES), lambda b, h, i: (b * nq + i, h))
    kvspec = pl.BlockSpec((req_len, hp * LANES), lambda b, h, i: (b, h))
    cspec = pl.BlockSpec((None, past, hp * LANES), lambda b, h, i: (b, 0, h))
    o_lat = pl.pallas_call(
        functools.partial(_attn_kernel, n_heads=hp, has_cache=True, layer=layer),
        grid=(n_req, N_HEADS // hp, nq),
        in_specs=[small3(lam_p.shape), pl.BlockSpec((1, LANES), lambda b, h, i: (0, 0)),
                  qspec, kvspec, kvspec, cspec, cspec],
        out_specs=pl.BlockSpec((tq, hp * LANES), lambda b, h, i: (b * nq + i, h)),
        out_shape=jax.ShapeDtypeStruct((n_req * req_len, d), BF16),
        compiler_params=_cparams(3),
        name="attn_latent",
    )(lam_p, subln, *qkv_lat, cache_k, cache_v)
    return o_ctx, o_lat


def _route(a2, wr, br, ltri, carry):
    tm = a2.shape[0]
    logits = _dot(a2.astype(BF16), wr) + br
    lane_i = lax.broadcasted_iota(jnp.int32, (tm, LANES), 1)
    lane = lane_i.astype(F32)
    neg = jnp.full((tm, LANES), -jnp.inf, F32)
    big = jnp.full((tm, LANES), float(LANES), F32)
    first_lane = lambda mask: jnp.min(jnp.where(mask, lane, big), axis=-1, keepdims=True)

    lc = jnp.where(lane_i < N_EXPERT_GROUPS, logits, neg)
    mc = jnp.max(lc, axis=-1, keepdims=True)
    pg = 1.0 / jnp.sum(jnp.exp(lc - mc), axis=-1, keepdims=True)
    gi = first_lane(lc == mc)
    assert EXPERTS_PER_GROUP == 8
    grp = lax.shift_right_arithmetic(lane_i - L_EXPERT0, 3).astype(F32)
    in_group = (lane_i >= L_EXPERT0) & (lane_i < L_EXPERT0 + N_EXPERTS) & (grp == gi)
    ls = jnp.where(in_group, logits, neg)
    t1 = jnp.max(ls, axis=-1, keepdims=True)
    i1 = first_lane(ls == t1)
    ls2 = jnp.where(lane == i1, neg, ls)
    t2 = jnp.max(ls2, axis=-1, keepdims=True)
    i2 = first_lane(ls2 == t2)
    ex = jnp.exp(t2 - t1)
    w1 = pg * (1.0 / (1.0 + ex))
    w2 = pg * (ex / (1.0 + ex))
    e1 = i1 - float(L_EXPERT0)
    e2 = i2 - float(L_EXPERT0)
    oh1 = lane == e1
    oh2 = lane == e2
    onehot = oh1.astype(F32) + oh2.astype(F32)
    before = _dot(ltri, onehot.astype(BF16)) + carry
    zero = jnp.zeros_like(before)
    rank1 = jnp.sum(jnp.where(oh1, before, zero), axis=-1, keepdims=True)
    rank2 = jnp.sum(jnp.where(oh2, before, zero), axis=-1, keepdims=True)
    slab = jnp.zeros((tm, LANES), F32)
    for ln, val in ((R_E1, e1), (R_E2, e2), (R_W1, w1), (R_W2, w2), (R_RANK1, rank1), (R_RANK2, rank2)):
        slab = jnp.where(lane_i == ln, val, slab)
    return slab, carry + jnp.sum(onehot, axis=0, keepdims=True)


def _router_weights(wc, bc, wf, bf_):
    d = wc.shape[0]
    pad = LANES - N_EXPERT_GROUPS - N_EXPERTS
    w = jnp.concatenate([wc, wf, jnp.zeros((d, pad), F32)], axis=1)
    b = jnp.concatenate([bc, bf_, jnp.zeros((pad,), F32)])[None, :]
    return w.astype(BF16), b


def _lower_tri(n):
    r = jnp.arange(n, dtype=jnp.int32)
    return (r[None, :] < r[:, None]).astype(BF16)


def _post_attn_kernel(hc_ref, hl_ref, oc_ref, ol_ref, mod_ref, wo_ref, g2_ref, wr_ref, br_ref, ltri_ref,
                      h1_ref, a2_ref, route_ref, cnt_ref, carry_ref, *, n_ctx_tiles, tiles_per_req):
    i = pl.program_id(0)
    seq = _seq_of_tile(i, n_ctx_tiles, tiles_per_req)

    @pl.when(i == 0)
    def _():
        carry_ref[...] = jnp.zeros_like(carry_ref)

    subs = [pl.ds(s * ROW_TILE, ROW_TILE) for s in range(h1_ref.shape[0] // ROW_TILE)]
    mixes = [_dot(jnp.where(i < n_ctx_tiles, oc_ref[rows, :], ol_ref[rows, :]), wo_ref[...]) for rows in subs]
    h1s = [jnp.where(i < n_ctx_tiles, hc_ref[rows, :], hl_ref[rows, :]) + _mod_row(mod_ref, 2, seq) * mix
           for rows, mix in zip(subs, mixes)]
    a2s = [_modulated(h1, g2_ref[...], _mod_row(mod_ref, 3, seq), _mod_row(mod_ref, 4, seq)) for h1 in h1s]
    for rows, h1, a2 in zip(subs, h1s, a2s):
        h1_ref[rows, :] = h1
        a2_ref[rows, :] = _pack_bf16_pairs(a2)
    carry = carry_ref[...]
    for rows, a2 in zip(subs, a2s):
        route_ref[rows, :], carry = _route(a2, wr_ref[...], br_ref[...], ltri_ref[...], carry)
    carry_ref[...] = carry
    cnt_ref[...] = carry


def _post_attn(h_ctx, h_lat, o_ctx, o_lat, mod, wo, g2, router, rows_per_req):
    n_ctx, d = h_ctx.shape
    t = n_ctx + h_lat.shape[0]
    tm = ROW_TILE * SUBTILES
    wr, br = router
    ltri = _lower_tri(ROW_TILE)
    full = lambda shape: pl.BlockSpec(shape, lambda i: (0,) * len(shape))
    row = pl.BlockSpec((tm, d), lambda i: (i, 0))
    kern = functools.partial(_post_attn_kernel, n_ctx_tiles=n_ctx // tm, tiles_per_req=rows_per_req // tm)
    return pl.pallas_call(
        kern,
        grid=(t // tm,),
        in_specs=_split_specs(tm, d, n_ctx // tm) + _split_specs(tm, d, n_ctx // tm) + [
                  full(mod.shape), full(wo.shape), full((1, d)), full(wr.shape),
                  full(br.shape), full(ltri.shape)],
        out_specs=[row, pl.BlockSpec((tm, d // 2), lambda i: (i, 0)), pl.BlockSpec((tm, LANES), lambda i: (i, 0)),
                   full((1, LANES))],
        out_shape=[jax.ShapeDtypeStruct((t, d), F32), jax.ShapeDtypeStruct((t, d // 2), jnp.int32),
                   jax.ShapeDtypeStruct((t, LANES), F32), jax.ShapeDtypeStruct((1, LANES), F32)],
        scratch_shapes=[pltpu.VMEM((1, LANES), F32)],
        compiler_params=_cparams(1),
        name="post_attn_router",
    )(h_ctx, h_lat, o_ctx, o_lat, mod, wo, g2, wr, br, ltri)


def _gather_rows(src, idx):
    n = idx.shape[0]
    d = src.shape[1]
    w = _sc_window_rows(src)
    mesh = plsc.VectorSubcoreMesh(core_axis_name="core", subcore_axis_name="subcore")

    @pl.kernel(out_type=jax.ShapeDtypeStruct((n, d), src.dtype), mesh=mesh)
    def gather(src_hbm, idx_hbm, out_hbm):
        def body(idx_vmem, out_vmem):
            pltpu.sync_copy(src_hbm.at[idx_vmem.at[0, pl.ds(0, w)]], out_vmem)

        pltpu.emit_pipeline(
            body,
            grid=(n // w,),
            in_specs=[pl.BlockSpec((1, LANES), lambda i: (i, 0))],
            out_specs=[pl.BlockSpec((w, d), lambda i: (i, 0))],
            core_axis_name=("core", "subcore"),
            dimension_semantics=(pltpu.PARALLEL,),
        )(idx_hbm, out_hbm)

    return gather(src, _index_windows(idx, w))


def _sc_window_rows(src):
    return min(LANES, SC_WINDOW_BYTES // (src.shape[1] * src.dtype.itemsize))


def _index_windows(idx, w):
    return jnp.pad(idx.reshape(idx.shape[0] // w, w), ((0, 0), (0, LANES - w)))


def _scatter_rows_twice(src, idx_a, idx_b, n_out):
    t, d = src.shape
    w = _sc_window_rows(src)
    mesh = plsc.VectorSubcoreMesh(core_axis_name="core", subcore_axis_name="subcore")

    @pl.kernel(out_type=jax.ShapeDtypeStruct((n_out, d), src.dtype), mesh=mesh)
    def scatter(src_hbm, ia_hbm, ib_hbm, out_hbm):
        def body(src_vmem, ia_vmem, ib_vmem):
            pltpu.sync_copy(src_vmem, out_hbm.at[ia_vmem.at[0, pl.ds(0, w)]])
            pltpu.sync_copy(src_vmem, out_hbm.at[ib_vmem.at[0, pl.ds(0, w)]])

        pltpu.emit_pipeline(
            body,
            grid=(t // w,),
            in_specs=[pl.BlockSpec((w, d), lambda i: (i, 0)),
                      pl.BlockSpec((1, LANES), lambda i: (i, 0)),
                      pl.BlockSpec((1, LANES), lambda i: (i, 0))],
            out_specs=[],
            core_axis_name=("core", "subcore"),
            dimension_semantics=(pltpu.PARALLEL,),
        )(src_hbm, ia_hbm, ib_hbm)

    return scatter(src, _index_windows(idx_a, w), _index_windows(idx_b, w))


def _expert_kernel(be_ref, nb_ref, ne_ref, x_ref, w1_hbm, w3_hbm, w2_hbm, y_ref,
                   w1s, w3s, w2s, w1b, w3b, w2b, sems, *, layer):
    b = pl.program_id(0)
    prev = be_ref[jnp.maximum(b - 1, 0)]
    used = b < nb_ref[0]

    def stage(e):
        return [pltpu.make_async_copy(src.at[layer, e], dst, sems.at[j])
                for j, (src, dst) in enumerate(((w1_hbm, w1s), (w3_hbm, w3s), (w2_hbm, w2s)))]

    @pl.when(used & ((b == 0) | (be_ref[b] != prev)))
    def _():
        @pl.when(b == 0)
        def _():
            for c in stage(be_ref[0]):
                c.start()

        for c in stage(be_ref[b]):
            c.wait()
        w1b[...] = w1s[...].astype(BF16)
        w3b[...] = w3s[...].astype(BF16)
        w2b[...] = w2s[...].astype(BF16)

        @pl.when(ne_ref[b] >= 0)
        def _():
            for c in stage(ne_ref[b]):
                c.start()

    @pl.when(used)
    def _():
        x_lo, x_hi = _unpack_bf16_pairs(x_ref[...])
        half = x_lo.shape[1]
        h1 = _dot(x_lo, w1b[:half, :]) + _dot(x_hi, w1b[half:, :])
        h3 = _dot(x_lo, w3b[:half, :]) + _dot(x_hi, w3b[half:, :])
        hb = jax.nn.silu(h1) * h3
        y_ref[...] = _pack_bf16_pairs(_dot(hb.astype(BF16), w2b[...]))


def _expert_mlps(xs, block_e, n_used, next_e, w1, w3, w2, layer):
    npad, half = xs.shape
    d, ff = w1.shape[2], w1.shape[3]
    assert d == 2 * half
    bm = MOE_BLOCK
    rows = lambda b, be, nb, ne: (jnp.minimum(b, nb[0] - 1), 0)
    hbm = pl.BlockSpec(memory_space=pl.ANY)
    grid_spec = pltpu.PrefetchScalarGridSpec(
        num_scalar_prefetch=3,
        grid=(npad // bm,),
        in_specs=[pl.BlockSpec((bm, half), rows), hbm, hbm, hbm],
        out_specs=pl.BlockSpec((bm, half), rows),
        scratch_shapes=[pltpu.VMEM((d, ff), F32), pltpu.VMEM((d, ff), F32), pltpu.VMEM((ff, d), F32),
                        pltpu.VMEM((d, ff), BF16), pltpu.VMEM((d, ff), BF16), pltpu.VMEM((ff, d), BF16),
                        pltpu.SemaphoreType.DMA((3,))],
    )
    return pl.pallas_call(
        functools.partial(_expert_kernel, layer=layer),
        grid_spec=grid_spec,
        out_shape=jax.ShapeDtypeStruct((npad, half), jnp.int32),
        compiler_params=_cparams(1),
        name="expert_mlps",
    )(block_e, n_used, next_e, xs, w1, w3, w2)


def _moe(a2p, route, counts, w1, w3, w2, layer, row_ranges):
    t = a2p.shape[0]
    bm = MOE_BLOCK
    n = 2 * t
    e = route[:, R_E1:R_E2 + 1].astype(jnp.int32).reshape(n)
    rank = route[:, R_RANK1:R_RANK2 + 1].astype(jnp.int32).reshape(n)
    cnt = counts[0, :N_EXPERTS].astype(jnp.int32)
    padded = ((cnt + bm - 1) // bm) * bm
    pend = jnp.cumsum(padded)
    pstart = pend - padded
    pos = jnp.sum(jnp.where(e[:, None] == jnp.arange(N_EXPERTS, dtype=jnp.int32)[None, :], pstart[None, :], 0),
                  axis=1) + rank
    n_blocks = n // bm + N_EXPERTS
    npad = n_blocks * bm
    starts = jnp.arange(n_blocks, dtype=jnp.int32) * bm
    n_used = pend[-1:] // bm
    last_start = (n_used[0] - 1) * bm
    block_e = jnp.sum((pend[None, :] <= jnp.minimum(starts, last_start)[:, None]).astype(jnp.int32), axis=1)
    blk = jnp.arange(n_blocks, dtype=jnp.int32)
    later_other = (blk[None, :] > blk[:, None]) & (block_e[None, :] != block_e[:, None]) & (blk[None, :] < n_used[0])
    first_later = jnp.min(jnp.where(later_other, blk[None, :], n_blocks), axis=1)
    next_e = jnp.where(first_later < n_blocks, block_e[jnp.minimum(first_later, n_blocks - 1)], -1).astype(jnp.int32)
    pos_first, pos_second = pos[0::2], pos[1::2]
    xs = _scatter_rows_twice(a2p, pos_first, pos_second, npad)
    ys = _expert_mlps(xs, block_e, n_used, next_e, w1, w3, w2, layer)
    return [_gather_rows(ys, jnp.concatenate([pos_first[r0:r1], pos_second[r0:r1]])) for r0, r1 in row_ranges]


def _moe_combine(h, yp_first, yp_second, route, g2):
    w1 = route[:, R_W1:R_W1 + 1]
    w2 = route[:, R_W2:R_W2 + 1]
    wide = lambda p: jnp.concatenate(_unpack_pairs_f32(p), axis=1)
    return h + g2 * (w1 * wide(yp_first) + w2 * wide(yp_second))


def _gmlp_kernel(h_ref, ya_ref, yb_ref, rt_ref, modp_ref, mod_ref, g1_ref, win_ref, bin_ref, vg_ref, ws_ref, bsb_ref,
                 wout_ref, g2_ref, wr_ref, br_ref, ltri_ref,
                 h1_ref, a2_ref, route_ref, cnt_ref, carry_ref, *, n_ctx_tiles, tiles_per_req):
    i = pl.program_id(0)
    seq = _seq_of_tile(i, n_ctx_tiles, tiles_per_req)

    @pl.when(i == 0)
    def _():
        carry_ref[...] = jnp.zeros_like(carry_ref)

    gw = vg_ref.shape[1]
    cg = gw // GM_GROUPS
    carry = carry_ref[...]
    for s in range(h_ref.shape[0] // GMLP_SUB):
        rs = pl.ds(s * GMLP_SUB, GMLP_SUB)
        h = _moe_combine(h_ref[rs, :], ya_ref[rs, :], yb_ref[rs, :], rt_ref[rs, :], _mod_row(modp_ref, 5, seq))
        a = _modulated(h, g1_ref[...], _mod_row(mod_ref, 0, seq), _mod_row(mod_ref, 1, seq))
        z = _gelu_tanh(_dot(a.astype(BF16), win_ref[...]) + bin_ref[...])
        u = z[:, :gw]
        v = z[:, gw:]
        v = (v * lax.rsqrt(jnp.mean(v * v, axis=-1, keepdims=True) + EPS)) * vg_ref[...]
        vb = v.astype(BF16)
        rows = []
        for c in range(GMLP_SUB // CHUNK):
            cols = []
            for g in range(GM_GROUPS):
                cols.append(_dot(ws_ref[g], vb[c * CHUNK:(c + 1) * CHUNK, g * cg:(g + 1) * cg]))
            rows.append(jnp.concatenate(cols, axis=1) + bsb_ref[...])
        vm = jnp.concatenate(rows, axis=0)
        mix = _dot((u * vm).astype(BF16), wout_ref[...])
        h1 = h + _mod_row(mod_ref, 2, seq) * mix
        h1_ref[rs, :] = h1
        a2 = _modulated(h1, g2_ref[...], _mod_row(mod_ref, 3, seq), _mod_row(mod_ref, 4, seq))
        a2_ref[rs, :] = _pack_bf16_pairs(a2)
        route_ref[rs, :], carry = _route(a2, wr_ref[...], br_ref[...], ltri_ref[...], carry)
    carry_ref[...] = carry
    cnt_ref[...] = carry


def _gmlp_layer(h, y2, route_prev, mod_prev, mod, g1, win, bin_, vg, ws, bsb, wout, g2, router,
                n_ctx, rows_per_req):
    t, d = h.shape
    tm = ROW_TILE * SUBTILES
    wr, br = router
    ltri = _lower_tri(GMLP_SUB)
    full = lambda shape: pl.BlockSpec(shape, lambda i: (0,) * len(shape), pipeline_mode=pl.Buffered(1))
    row = lambda w: pl.BlockSpec((tm, w), lambda i: (i, 0))
    kern = functools.partial(_gmlp_kernel, n_ctx_tiles=n_ctx // tm, tiles_per_req=rows_per_req // tm)
    args = (h, y2, y2, route_prev, mod_prev, mod, g1, win, bin_, vg, ws, bsb, wout, g2, wr, br, ltri)
    second = pl.BlockSpec((tm, d // 2), lambda i: (i + t // tm, 0))
    in_specs = [row(d), row(d // 2), second, row(LANES)] + [full(a.shape) for a in args[4:]]
    return pl.pallas_call(
        kern,
        grid=(t // tm,),
        in_specs=in_specs,
        out_specs=[row(d), row(d // 2), row(LANES), pl.BlockSpec((1, LANES), lambda i: (0, 0))],
        out_shape=[jax.ShapeDtypeStruct((t, d), F32), jax.ShapeDtypeStruct((t, d // 2), jnp.int32),
                   jax.ShapeDtypeStruct((t, LANES), F32), jax.ShapeDtypeStruct((1, LANES), F32)],
        scratch_shapes=[pltpu.VMEM((1, LANES), F32)],
        compiler_params=_cparams(1),
        name="gmlp_router",
    )(*args)


def _final_kernel(h_ref, ya_ref, yb_ref, rt_ref, mod_ref, o_ref, *, first_tile, n_ctx_tiles, tiles_per_req):
    seq = _seq_of_tile(first_tile + pl.program_id(0), n_ctx_tiles, tiles_per_req)
    o_ref[...] = _moe_combine(h_ref[...], ya_ref[...], yb_ref[...], rt_ref[...], _mod_row(mod_ref, 5, seq))


def _final_combine(h, y2, route, mod, first_row, n_ctx, rows_per_req):
    d = h.shape[1]
    n = y2.shape[0] // 2
    tm = ROW_TILE
    first_tile = first_row // tm
    row = lambda w: pl.BlockSpec((tm, w), lambda i: (first_tile + i, 0))
    kern = functools.partial(_final_kernel, first_tile=first_tile, n_ctx_tiles=n_ctx // tm,
                             tiles_per_req=rows_per_req // tm)
    return pl.pallas_call(
        kern,
        grid=(n // tm,),
        in_specs=[row(d), pl.BlockSpec((tm, d // 2), lambda i: (i, 0)),
                  pl.BlockSpec((tm, d // 2), lambda i: (i + n // tm, 0)),
                  row(LANES), pl.BlockSpec(mod.shape, lambda i: (0, 0, 0))],
        out_specs=pl.BlockSpec((tm, d), lambda i: (i, 0)),
        out_shape=jax.ShapeDtypeStruct((n, d), F32),
        compiler_params=_cparams(1),
        name="final_combine",
    )(h, y2, y2, route, mod)


def kernel(x_prompt, x_sample, cache_k, cache_v, c, c_ctx, ada_w, ada_b, norm1_g, norm2_g, attn_wq, attn_wk,
           attn_wv, attn_wo, attn_qnorm, attn_knorm, attn_lam, attn_subln, gm_win, gm_bin, gm_vnorm, gm_ws,
           gm_bs, gm_wout, moe_wc, moe_bc, moe_wf, moe_bf, moe_w1, moe_w3, moe_w2):
    batch, seq_len, d = x_prompt.shape
    n_req, req_len, _ = x_sample.shape
    depth = ada_w.shape[0]
    assert depth == 2 and attn_wq.shape[0] == 1 and gm_win.shape[0] == 1
    n_ctx = batch * seq_len
    assert n_ctx % req_len == 0 and req_len % ROW_TILE == 0 and seq_len == ROW_TILE
    past = cache_k.shape[2]

    h_ctx = x_prompt.reshape(n_ctx, d)
    h_lat = x_sample.reshape(n_req * req_len, d)
    cvec = jnp.concatenate([c_ctx[None, :], c, jnp.zeros((8 - 1 - n_req, d), F32)], axis=0)
    mod = _ada_mod(cvec, ada_w, ada_b)
    routers = [_router_weights(moe_wc[i], moe_bc[i], moe_wf[i], moe_bf[i]) for i in range(depth)]
    row_vec = lambda v: v.reshape(1, -1)

    wqkv = jnp.concatenate([attn_wq[0], attn_wk[0], attn_wv[0]], axis=1).astype(BF16)
    reps = d // HEAD_DIM
    qk_gains = (row_vec(jnp.tile(attn_qnorm[0], reps)), row_vec(jnp.tile(attn_knorm[0], reps)))
    *qkv_ctx, k_new, v_new = _pre_attn(h_ctx, mod[0], row_vec(norm1_g[0]), wqkv, *qk_gains, ctx_len=seq_len)
    qkv_lat = _pre_attn(h_lat, mod[0], row_vec(norm1_g[0]), wqkv, *qk_gains, rows_per_req=req_len)
    o_ctx, o_lat = _attention(qkv_ctx, qkv_lat, attn_lam[0], row_vec(attn_subln[0]),
                              cache_k[:, 0].reshape(n_req, past, d), cache_v[:, 0].reshape(n_req, past, d),
                              seq_len, n_req, req_len, layer=0)
    h1, a2, route0, cnt0 = _post_attn(h_ctx, h_lat, o_ctx, o_lat, mod[0], attn_wo[0].astype(BF16),
                                      row_vec(norm2_g[0]), routers[0], req_len)
    t = n_ctx + n_req * req_len
    (y2,) = _moe(a2, route0, cnt0, moe_w1, moe_w3, moe_w2, 0, [(0, t)])

    gw = gm_vnorm.shape[1]
    bsb = jnp.repeat(gm_bs[0].T, gw // GM_GROUPS, axis=1)
    h2, a2, route1, cnt1 = _gmlp_layer(h1, y2, route0, mod[0], mod[1], row_vec(norm1_g[1]), gm_win[0].astype(BF16),
                                       row_vec(gm_bin[0]), row_vec(gm_vnorm[0]), gm_ws[0].astype(BF16), bsb,
                                       gm_wout[0].astype(BF16), row_vec(norm2_g[1]), routers[1], n_ctx, req_len)
    y2_ctx, y2_lat = _moe(a2, route1, cnt1, moe_w1, moe_w3, moe_w2, 1, [(0, n_ctx), (n_ctx, t)])
    y_ctx = _final_combine(h2, y2_ctx, route1, mod[1], 0, n_ctx, req_len)
    y_lat = _final_combine(h2, y2_lat, route1, mod[1], n_ctx, n_ctx, req_len)

    y_prompt = y_ctx.reshape(batch, seq_len, d)
    y_sample = y_lat.reshape(n_req, req_len, d)
    new_cache_k = jnp.transpose(k_new.reshape(batch, 1, N_HEADS, 2, HEAD_DIM, seq_len), (0, 1, 5, 2, 3, 4))
    new_cache_v = v_new.reshape(batch, 1, seq_len, N_HEADS, V_DIM)
    return (y_prompt, y_sample, new_cache_k, new_cache_v)
```

```python
import functools
import math

import jax
import jax.numpy as jnp
from jax import lax
from jax.experimental import pallas as pl
from jax.experimental.pallas import tpu as pltpu
from jax.experimental.pallas import tpu_sc as plsc

F32 = jnp.float32
BF16 = jnp.bfloat16

N_HEADS = 8
HEAD_DIM = 64
V_DIM = 2 * HEAD_DIM
GRID_W = 64
ROPE_THETA = 10000.0
CHUNK = 128
GM_GROUPS = 8
N_EXPERT_GROUPS = 4
EXPERTS_PER_GROUP = 8
N_EXPERTS = N_EXPERT_GROUPS * EXPERTS_PER_GROUP
EPS = 1e-6

LANES = 128
MXU_TILE = 256
ROW_TILE = 256
SUBTILES = 2
POST_ATTN_SUBTILES = 4
GMLP_SUB = 512
PRE_ATTN_TILE = 512
MOE_BLOCK = 256
SC_WINDOW_BYTES = 128 * 1024
ATTN_Q_TILE = 512
ATTN_HEADS_PER_STEP = 2
VMEM_LIMIT = 56 * 1024 * 1024

R_E1, R_E2, R_W1, R_W2, R_RANK1, R_RANK2 = 0, 1, 2, 3, 4, 5
R_ROWS = 8
L_EXPERT0 = N_EXPERT_GROUPS


def _lambda_init(layer):
    return 0.8 - 0.6 * math.exp(-0.3 * layer)


def _cparams(n_axes):
    return pltpu.CompilerParams(dimension_semantics=("arbitrary",) * n_axes, vmem_limit_bytes=VMEM_LIMIT)


def _seq_of_tile(i, n_ctx_tiles, tiles_per_req):
    return jnp.where(i < n_ctx_tiles, 0, 1 + (i - n_ctx_tiles) // tiles_per_req)


def _pick_rows(i, n_ctx_tiles, ctx_ref, lat_ref):
    return jnp.where(i < n_ctx_tiles, ctx_ref[...], lat_ref[...])


def _split_specs(tm, d, n_ctx_tiles):
    return [pl.BlockSpec((tm, d), lambda i: (jnp.minimum(i, n_ctx_tiles - 1), 0)),
            pl.BlockSpec((tm, d), lambda i: (jnp.maximum(i - n_ctx_tiles, 0), 0))]


def _mod_row(mod_ref, part, seq):
    return mod_ref[part, pl.ds(seq, 1), :]


def _modulated(x, g, shift, scale):
    y = x * lax.rsqrt(jnp.mean(x * x, axis=-1, keepdims=True) + EPS)
    return (y * g) * (1.0 + scale) + shift


def _split_bf16(x):
    hi = x.astype(BF16)
    lo = (x - hi.astype(F32)).astype(BF16)
    return hi, lo


def _gelu_tanh(x):
    c = math.sqrt(2.0 / math.pi)
    hx = 0.5 * x
    return hx + hx * jnp.tanh(x * (c + (c * 0.044715) * (x * x)))


def _dot(a, b):
    return jnp.dot(a, b, preferred_element_type=F32)


def _pack_bf16_pairs(x):
    half = x.shape[1] // 2
    bits = lax.bitcast_convert_type(x.astype(BF16).astype(F32), jnp.uint32)
    packed = (bits[:, :half] >> 16) | bits[:, half:]
    return lax.bitcast_convert_type(packed, jnp.int32)


def _unpack_pairs_f32(p):
    u = lax.bitcast_convert_type(p, jnp.uint32)
    return lax.bitcast_convert_type(u << 16, F32), lax.bitcast_convert_type(u & jnp.uint32(0xFFFF0000), F32)


def _unpack_bf16_pairs(p):
    lo, hi = _unpack_pairs_f32(p)
    return lo.astype(BF16), hi.astype(BF16)


def _ada_kernel(c_ref, w_ref, b_ref, o_ref):
    c = c_ref[...]
    s = c * jax.nn.sigmoid(c)
    o_ref[...] = _dot(s.astype(BF16), w_ref[...].astype(BF16)) + b_ref[...]


def _ada_mod(cvec, ada_w, ada_b):
    depth, d, d6 = ada_w.shape
    parts = d6 // d
    rows = cvec.shape[0]
    return pl.pallas_call(
        _ada_kernel,
        grid=(depth, parts),
        in_specs=[
            pl.BlockSpec((rows, d), lambda l, j: (0, 0)),
            pl.BlockSpec((None, d, d), lambda l, j: (l, 0, j)),
            pl.BlockSpec((None, 1, d), lambda l, j: (l, 0, j)),
        ],
        out_specs=pl.BlockSpec((None, None, rows, d), lambda l, j: (l, j, 0, 0)),
        out_shape=jax.ShapeDtypeStruct((depth, parts, rows, d), F32),
        compiler_params=_cparams(2),
        name="ada_mod",
    )(cvec, ada_w, ada_b.reshape(depth, 1, d6))


def _pre_attn_kernel(*refs, latent, tiles_per_req):
    if latent:
        h_ref, mod_ref, g_ref, w_ref, qg_ref, kg_ref, gs_ref, cos_ref, sin_ref, q_ref, k_ref, v_ref = refs
        seq = 1 + pl.program_id(0) // tiles_per_req
    else:
        h_ref, mod_ref, g_ref, w_ref, qg_ref, kg_ref, gs_ref, q_ref, k_ref, v_ref, kf_ref, vf_ref = refs
        seq = 0
    d = h_ref.shape[1]
    q_scale = HEAD_DIM ** -0.5 * math.log2(math.e)
    subs = [pl.ds(s * ROW_TILE, ROW_TILE) for s in range(q_ref.shape[0] // ROW_TILE)]

    def group_rms_inv(x):
        sq_hi, sq_lo = _split_bf16(x * x)
        w = gs_ref.shape[0]
        sums = [_dot(sq_hi[:, c:c + w], gs_ref[...]) + _dot(sq_lo[:, c:c + w], gs_ref[...]) for c in range(0, d, w)]
        return lax.rsqrt(jnp.concatenate(sums, axis=1) * (1.0 / HEAD_DIM) + EPS)

    def rope(x, rows):
        cos = cos_ref[rows, :]
        sin = sin_ref[rows, :]
        lane = lax.broadcasted_iota(jnp.int32, cos.shape, 1)
        first = (lane & 31) < 16
        outs = []
        for hh in range(d // LANES):
            xs = x[:, hh * LANES:(hh + 1) * LANES]
            rot = jnp.where(first, pltpu.roll(xs, LANES - 16, 1), pltpu.roll(xs, 16, 1))
            outs.append(xs * cos + rot * sin)
        return jnp.concatenate(outs, axis=1)

    acts = [_modulated(h_ref[rows, :], g_ref[...], _mod_row(mod_ref, 0, seq), _mod_row(mod_ref, 1, seq)).astype(BF16)
            for rows in subs]
    qkvs = [_dot(a, w_ref[...]) for a in acts]
    qns = [(qkv[:, :d] * group_rms_inv(qkv[:, :d])) * qg_ref[...] for qkv in qkvs]
    kns = [(qkv[:, d:2 * d] * group_rms_inv(qkv[:, d:2 * d])) * kg_ref[...] for qkv in qkvs]
    if latent:
        qrs = [rope(qn, rows) for qn, rows in zip(qns, subs)]
        krs = [rope(kn, rows) for kn, rows in zip(kns, subs)]
    else:
        qrs, krs = qns, kns
    for rows, qkv, qr, kr in zip(subs, qkvs, qrs, krs):
        q_ref[rows, :] = (qr * q_scale).astype(BF16)
        k_ref[rows, :] = kr.astype(BF16)
        v_ref[rows, :] = qkv[:, 2 * d:].astype(BF16)
    if not latent:
        kn_all = jnp.concatenate(kns, axis=0)
        n_pos = kf_ref.shape[3]
        for r in range(kf_ref.shape[0]):
            kf_ref[r] = kn_all[r * n_pos:(r + 1) * n_pos, :].T.reshape(kf_ref.shape[1:])
        for rows, qkv in zip(subs, qkvs):
            vf_ref[rows, :] = qkv[:, 2 * d:]


def _rope_tables(n_pos):
    half = HEAD_DIM // 2
    pos = jnp.arange(n_pos, dtype=jnp.int32)
    row = (pos // GRID_W).astype(F32)
    col = (pos % GRID_W).astype(F32)
    inv = 1.0 / (ROPE_THETA ** (jnp.arange(0, half, 2, dtype=F32) / half))
    ang_r = row[:, None] * inv[None, :]
    ang_c = col[:, None] * inv[None, :]
    ang = jnp.concatenate([ang_r, ang_r, ang_c, ang_c], axis=-1)
    quarter = half // 2
    sign = jnp.tile(jnp.concatenate([-jnp.ones((quarter,), F32), jnp.ones((quarter,), F32)]), 2)
    cos = jnp.tile(jnp.cos(ang), (1, 2))
    sin = jnp.tile(jnp.sin(ang) * sign[None, :], (1, 2))
    return cos, sin


def _pre_attn(h, mod, g, wqkv, qg, kg, *, ctx_len=None, rows_per_req=None):
    n, d = h.shape
    latent = rows_per_req is not None
    tm = PRE_ATTN_TILE
    gid = jnp.arange(MXU_TILE, dtype=jnp.int32) // HEAD_DIM
    gs = (gid[:, None] == gid[None, :]).astype(BF16)
    full = lambda shape: pl.BlockSpec(shape, lambda i: (0,) * len(shape), pipeline_mode=pl.Buffered(1))
    row = pl.BlockSpec((tm, d), lambda i: (i, 0))
    args = [h, mod, g, wqkv, qg, kg, gs]
    in_specs = [row] + [full(a.shape) for a in args[1:]]
    out_specs = [row, row, row]
    out_shape = [jax.ShapeDtypeStruct((n, d), BF16)] * 3
    if latent:
        tiles_per_req = rows_per_req // tm
        pos = pl.BlockSpec((tm, LANES), lambda i: (i % tiles_per_req, 0))
        args += list(_rope_tables(rows_per_req))
        in_specs += [pos, pos]
    else:
        assert tm % ctx_len == 0
        tiles_per_req = None
        out_specs += [pl.BlockSpec((tm // ctx_len, d // HEAD_DIM, HEAD_DIM, ctx_len), lambda i: (i, 0, 0, 0)), row]
        out_shape += [jax.ShapeDtypeStruct((n // ctx_len, d // HEAD_DIM, HEAD_DIM, ctx_len), F32),
                      jax.ShapeDtypeStruct((n, d), F32)]
    return pl.pallas_call(
        functools.partial(_pre_attn_kernel, latent=latent, tiles_per_req=tiles_per_req),
        grid=(n // tm,),
        in_specs=in_specs,
        out_specs=out_specs,
        out_shape=out_shape,
        compiler_params=_cparams(1),
        name="pre_attn_latent" if latent else "pre_attn_ctx",
    )(*args)


def _attn_kernel(*refs, n_heads, has_cache, layer):
    if has_cache:
        lam_ref, sub_ref, q_ref, kn_ref, vn_ref, kc_ref, vc_ref, o_ref = refs
    else:
        lam_ref, sub_ref, q_ref, kn_ref, vn_ref, o_ref = refs
    lf = lam_ref[...]
    lam = (jnp.exp(jnp.sum(lf[0:1] * lf[1:2], axis=-1, keepdims=True))
           - jnp.exp(jnp.sum(lf[2:3] * lf[3:4], axis=-1, keepdims=True)) + _lambda_init(layer))
    nt = (((1,), (1,)), ((), ()))
    tq = q_ref.shape[0]
    lane = lax.broadcasted_iota(jnp.int32, (tq, LANES), 1)
    heads = [slice(hh * LANES, (hh + 1) * LANES) for hh in range(n_heads)]

    def qk(sl):
        qh = q_ref[:, sl]
        zero = jnp.zeros_like(qh)
        keys = [kn_ref[:, sl]] + ([kc_ref[:, sl].astype(BF16)] if has_cache else [])
        return [[lax.dot_general(qm, kk, nt, preferred_element_type=F32) for kk in keys]
                for qm in (jnp.where(lane < HEAD_DIM, qh, zero), jnp.where(lane >= HEAD_DIM, qh, zero))]

    def soft(scores):
        maxes = [functools.reduce(jnp.maximum, [jnp.max(x, axis=-1, keepdims=True) for x in s]) for s in scores]
        exps = [[jnp.exp2(x - m) for x in s] for s, m in zip(scores, maxes)]
        r0, r1 = [1.0 / functools.reduce(lambda u, w: u + w, [jnp.sum(x, axis=-1, keepdims=True) for x in e])
                  for e in exps]
        return [(x0 * r0 - lam * (x1 * r1)).astype(BF16) for x0, x1 in zip(*exps)]

    def pv(sl, a):
        vals = [vn_ref[:, sl]] + ([vc_ref[:, sl].astype(BF16)] if has_cache else [])
        o = functools.reduce(lambda u, w: u + w, [_dot(aj, vv) for aj, vv in zip(a, vals)])
        o = o * lax.rsqrt(jnp.mean(o * o, axis=-1, keepdims=True) + EPS)
        o_ref[:, sl] = ((o * sub_ref[...]) * (1.0 - _lambda_init(layer))).astype(BF16)

    scores = [qk(sl) for sl in heads]
    mixed = [soft(sc) for sc in scores]
    for sl, a in zip(heads, mixed):
        pv(sl, a)


def _attention(qkv_ctx, qkv_lat, lam_p, subln, cache_k, cache_v, ctx_len, n_req, req_len, layer):
    n_ctx, d = qkv_ctx[0].shape
    small = lambda shape: pl.BlockSpec(shape, lambda *_: (0,) * len(shape))
    n_ctx_req = n_ctx // ctx_len
    blk = pl.BlockSpec((ctx_len, d), lambda b: (b, 0))
    o_ctx = pl.pallas_call(
        functools.partial(_attn_kernel, n_heads=N_HEADS, has_cache=False, layer=layer),
        grid=(n_ctx_req,),
        in_specs=[small(lam_p.shape), small(subln.shape), blk, blk, blk],
        out_specs=blk,
        out_shape=jax.ShapeDtypeStruct((n_ctx, d), BF16),
        compiler_params=_cparams(1),
        name="attn_ctx",
    )(lam_p, subln, *qkv_ctx)

    tq = ATTN_Q_TILE
    nq = req_len // tq
    past = cache_k.shape[1]
    hp = ATTN_HEADS_PER_STEP
    small3 = lambda shape: pl.BlockSpec(shape, lambda b, h, i: (0,) * len(shape))
    qspec = pl.BlockSpec((tq, hp * LANES), lambda b, h, i: (b * nq + i, h))
    kvspec = pl.BlockSpec((req_len, hp * LANES), lambda b, h, i: (b, h))
    cspec = pl.BlockSpec((None, past, hp * LANES), lambda b, h, i: (b, 0, h))
    o_lat = pl.pallas_call(
        functools.partial(_attn_kernel, n_heads=hp, has_cache=True, layer=layer),
        grid=(n_req, N_HEADS // hp, nq),
        in_specs=[small3(lam_p.shape), pl.BlockSpec((1, LANES), lambda b, h, i: (0, 0)),
                  qspec, kvspec, kvspec, cspec, cspec],
        out_specs=pl.BlockSpec((tq, hp * LANES), lambda b, h, i: (b * nq + i, h)),
        out_shape=jax.ShapeDtypeStruct((n_req * req_len, d), BF16),
        compiler_params=_cparams(3),
        name="attn_latent",
    )(lam_p, subln, *qkv_lat, cache_k, cache_v)
    return o_ctx, o_lat


def _route(a2, wr, br, ltri, carry):
    tm = a2.shape[0]
    logits = _dot(a2.astype(BF16), wr) + br
    lane_i = lax.broadcasted_iota(jnp.int32, (tm, LANES), 1)
    lane = lane_i.astype(F32)
    neg = jnp.full((tm, LANES), -jnp.inf, F32)
    big = jnp.full((tm, LANES), float(LANES), F32)
    first_lane = lambda mask: jnp.min(jnp.where(mask, lane, big), axis=-1, keepdims=True)

    lc = jnp.where(lane_i < N_EXPERT_GROUPS, logits, neg)
    mc = jnp.max(lc, axis=-1, keepdims=True)
    pg = 1.0 / jnp.sum(jnp.exp(lc - mc), axis=-1, keepdims=True)
    gi = first_lane(lc == mc)
    assert EXPERTS_PER_GROUP == 8
    grp = lax.shift_right_arithmetic(lane_i - L_EXPERT0, 3).astype(F32)
    in_group = (lane_i >= L_EXPERT0) & (lane_i < L_EXPERT0 + N_EXPERTS) & (grp == gi)
    ls = jnp.where(in_group, logits, neg)
    t1 = jnp.max(ls, axis=-1, keepdims=True)
    i1 = first_lane(ls == t1)
    ls2 = jnp.where(lane == i1, neg, ls)
    t2 = jnp.max(ls2, axis=-1, keepdims=True)
    i2 = first_lane(ls2 == t2)
    ex = jnp.exp(t2 - t1)
    w1 = pg * (1.0 / (1.0 + ex))
    w2 = pg * (ex / (1.0 + ex))
    e1 = i1 - float(L_EXPERT0)
    e2 = i2 - float(L_EXPERT0)
    oh1 = lane == e1
    oh2 = lane == e2
    onehot = oh1.astype(F32) + oh2.astype(F32)
    before = _dot(ltri, onehot.astype(BF16)) + carry
    zero = jnp.zeros_like(before)
    rank1 = jnp.sum(jnp.where(oh1, before, zero), axis=-1, keepdims=True)
    rank2 = jnp.sum(jnp.where(oh2, before, zero), axis=-1, keepdims=True)
    slab = jnp.zeros((tm, LANES), F32)
    for ln, val in ((R_E1, e1), (R_E2, e2), (R_W1, w1), (R_W2, w2), (R_RANK1, rank1), (R_RANK2, rank2)):
        slab = jnp.where(lane_i == ln, val, slab)
    return slab, carry + jnp.sum(onehot, axis=0, keepdims=True)


def _router_weights(wc, bc, wf, bf_):
    d = wc.shape[0]
    pad = LANES - N_EXPERT_GROUPS - N_EXPERTS
    w = jnp.concatenate([wc, wf, jnp.zeros((d, pad), F32)], axis=1)
    b = jnp.concatenate([bc, bf_, jnp.zeros((pad,), F32)])[None, :]
    return w.astype(BF16), b


def _lower_tri(n):
    r = jnp.arange(n, dtype=jnp.int32)
    return (r[None, :] < r[:, None]).astype(BF16)


def _post_attn_kernel(hc_ref, hl_ref, oc_ref, ol_ref, mod_ref, wo_ref, g2_ref, wr_ref, br_ref, ltri_ref,
                      h1_ref, a2_ref, route_ref, rtt_ref, cnt_ref, carry_ref, *, n_ctx_tiles, tiles_per_req):
    i = pl.program_id(0)
    seq = _seq_of_tile(i, n_ctx_tiles, tiles_per_req)

    @pl.when(i == 0)
    def _():
        carry_ref[...] = jnp.zeros_like(carry_ref)

    subs = [pl.ds(s * ROW_TILE, ROW_TILE) for s in range(h1_ref.shape[0] // ROW_TILE)]
    mixes = [_dot(jnp.where(i < n_ctx_tiles, oc_ref[rows, :], ol_ref[rows, :]), wo_ref[...]) for rows in subs]
    h1s = [jnp.where(i < n_ctx_tiles, hc_ref[rows, :], hl_ref[rows, :]) + _mod_row(mod_ref, 2, seq) * mix
           for rows, mix in zip(subs, mixes)]
    a2s = [_modulated(h1, g2_ref[...], _mod_row(mod_ref, 3, seq), _mod_row(mod_ref, 4, seq)) for h1 in h1s]
    for rows, h1, a2 in zip(subs, h1s, a2s):
        h1_ref[rows, :] = h1
        a2_ref[rows, :] = _pack_bf16_pairs(a2)
    carry = carry_ref[...]
    for rows, a2 in zip(subs, a2s):
        slab, carry = _route(a2, wr_ref[...], br_ref[...], ltri_ref[...], carry)
        route_ref[rows, :] = slab
        rtt_ref[:, rows] = slab.T[:R_ROWS, :]
    carry_ref[...] = carry
    cnt_ref[...] = carry


def _post_attn(h_ctx, h_lat, o_ctx, o_lat, mod, wo, g2, router, rows_per_req):
    n_ctx, d = h_ctx.shape
    t = n_ctx + h_lat.shape[0]
    tm = ROW_TILE * POST_ATTN_SUBTILES
    wr, br = router
    ltri = _lower_tri(ROW_TILE)
    full = lambda shape: pl.BlockSpec(shape, lambda i: (0,) * len(shape))
    row = pl.BlockSpec((tm, d), lambda i: (i, 0))
    kern = functools.partial(_post_attn_kernel, n_ctx_tiles=n_ctx // tm, tiles_per_req=rows_per_req // tm)
    return pl.pallas_call(
        kern,
        grid=(t // tm,),
        in_specs=_split_specs(tm, d, n_ctx // tm) + _split_specs(tm, d, n_ctx // tm) + [
                  full(mod.shape), full(wo.shape), full((1, d)), full(wr.shape),
                  full(br.shape), full(ltri.shape)],
        out_specs=[row, pl.BlockSpec((tm, d // 2), lambda i: (i, 0)), pl.BlockSpec((tm, LANES), lambda i: (i, 0)),
                   pl.BlockSpec((R_ROWS, tm), lambda i: (0, i)), full((1, LANES))],
        out_shape=[jax.ShapeDtypeStruct((t, d), F32), jax.ShapeDtypeStruct((t, d // 2), jnp.int32),
                   jax.ShapeDtypeStruct((t, LANES), F32), jax.ShapeDtypeStruct((R_ROWS, t), F32),
                   jax.ShapeDtypeStruct((1, LANES), F32)],
        scratch_shapes=[pltpu.VMEM((1, LANES), F32)],
        compiler_params=_cparams(1),
        name="post_attn_router",
    )(h_ctx, h_lat, o_ctx, o_lat, mod, wo, g2, wr, br, ltri)


def _gather_rows(src, idx):
    n = idx.shape[0]
    d = src.shape[1]
    w = _sc_window_rows(src)
    mesh = plsc.VectorSubcoreMesh(core_axis_name="core", subcore_axis_name="subcore")

    @pl.kernel(out_type=jax.ShapeDtypeStruct((n, d), src.dtype), mesh=mesh)
    def gather(src_hbm, idx_hbm, out_hbm):
        def body(idx_vmem, out_vmem):
            pltpu.sync_copy(src_hbm.at[idx_vmem.at[0, pl.ds(0, w)]], out_vmem)

        pltpu.emit_pipeline(
            body,
            grid=(n // w,),
            in_specs=[pl.BlockSpec((1, LANES), lambda i: (i, 0))],
            out_specs=[pl.BlockSpec((w, d), lambda i: (i, 0))],
            core_axis_name=("core", "subcore"),
            dimension_semantics=(pltpu.PARALLEL,),
        )(idx_hbm, out_hbm)

    return gather(src, _index_windows(idx, w))


def _sc_window_rows(src):
    return min(LANES, SC_WINDOW_BYTES // (src.shape[1] * src.dtype.itemsize))


def _index_windows(idx, w):
    return jnp.pad(idx.reshape(idx.shape[0] // w, w), ((0, 0), (0, LANES - w)))


def _scatter_rows_twice(src, idx_a, idx_b, n_out):
    t, d = src.shape
    w = _sc_window_rows(src)
    mesh = plsc.VectorSubcoreMesh(core_axis_name="core", subcore_axis_name="subcore")

    @pl.kernel(out_type=jax.ShapeDtypeStruct((n_out, d), src.dtype), mesh=mesh)
    def scatter(src_hbm, ia_hbm, ib_hbm, out_hbm):
        def body(src_vmem, ia_vmem, ib_vmem):
            pltpu.sync_copy(src_vmem, out_hbm.at[ia_vmem.at[0, pl.ds(0, w)]])
            pltpu.sync_copy(src_vmem, out_hbm.at[ib_vmem.at[0, pl.ds(0, w)]])

        pltpu.emit_pipeline(
            body,
            grid=(t // w,),
            in_specs=[pl.BlockSpec((w, d), lambda i: (i, 0)),
                      pl.BlockSpec((1, LANES), lambda i: (i, 0)),
                      pl.BlockSpec((1, LANES), lambda i: (i, 0))],
            out_specs=[],
            core_axis_name=("core", "subcore"),
            dimension_semantics=(pltpu.PARALLEL,),
        )(src_hbm, ia_hbm, ib_hbm)

    return scatter(src, _index_windows(idx_a, w), _index_windows(idx_b, w))


def _expert_kernel(be_ref, nb_ref, ne_ref, x_ref, w1_hbm, w3_hbm, w2_hbm, y_ref,
                   w1s, w3s, w2s, w1b, w3b, w2b, sems, *, layer):
    b = pl.program_id(0)
    prev = be_ref[jnp.maximum(b - 1, 0)]
    used = b < nb_ref[0]

    def stage(e):
        return [pltpu.make_async_copy(src.at[layer, e], dst, sems.at[j])
                for j, (src, dst) in enumerate(((w1_hbm, w1s), (w3_hbm, w3s), (w2_hbm, w2s)))]

    @pl.when(used & ((b == 0) | (be_ref[b] != prev)))
    def _():
        @pl.when(b == 0)
        def _():
            for c in stage(be_ref[0]):
                c.start()

        for c in stage(be_ref[b]):
            c.wait()
        w1b[...] = w1s[...].astype(BF16)
        w3b[...] = w3s[...].astype(BF16)
        w2b[...] = w2s[...].astype(BF16)

        @pl.when(ne_ref[b] >= 0)
        def _():
            for c in stage(ne_ref[b]):
                c.start()

    @pl.when(used)
    def _():
        x_lo, x_hi = _unpack_bf16_pairs(x_ref[...])
        half = x_lo.shape[1]
        h1 = _dot(x_lo, w1b[:half, :]) + _dot(x_hi, w1b[half:, :])
        h3 = _dot(x_lo, w3b[:half, :]) + _dot(x_hi, w3b[half:, :])
        hb = jax.nn.silu(h1) * h3
        y_ref[...] = _pack_bf16_pairs(_dot(hb.astype(BF16), w2b[...]))


def _expert_mlps(xs, block_e, n_used, next_e, w1, w3, w2, layer):
    npad, half = xs.shape
    d, ff = w1.shape[2], w1.shape[3]
    assert d == 2 * half
    bm = MOE_BLOCK
    rows = lambda b, be, nb, ne: (jnp.minimum(b, nb[0] - 1), 0)
    hbm = pl.BlockSpec(memory_space=pl.ANY)
    grid_spec = pltpu.PrefetchScalarGridSpec(
        num_scalar_prefetch=3,
        grid=(npad // bm,),
        in_specs=[pl.BlockSpec((bm, half), rows), hbm, hbm, hbm],
        out_specs=pl.BlockSpec((bm, half), rows),
        scratch_shapes=[pltpu.VMEM((d, ff), F32), pltpu.VMEM((d, ff), F32), pltpu.VMEM((ff, d), F32),
                        pltpu.VMEM((d, ff), BF16), pltpu.VMEM((d, ff), BF16), pltpu.VMEM((ff, d), BF16),
                        pltpu.SemaphoreType.DMA((3,))],
    )
    return pl.pallas_call(
        functools.partial(_expert_kernel, layer=layer),
        grid_spec=grid_spec,
        out_shape=jax.ShapeDtypeStruct((npad, half), jnp.int32),
        compiler_params=_cparams(1),
        name="expert_mlps",
    )(block_e, n_used, next_e, xs, w1, w3, w2)


def _moe(a2p, route_t, counts, w1, w3, w2, layer, row_ranges):
    t = a2p.shape[0]
    bm = MOE_BLOCK
    n = 2 * t
    ints = route_t.astype(jnp.int32)
    cnt = counts[0, :N_EXPERTS].astype(jnp.int32)
    padded = ((cnt + bm - 1) // bm) * bm
    pend = jnp.cumsum(padded)
    pstart = pend - padded
    experts = jnp.arange(N_EXPERTS, dtype=jnp.int32)[None, :]
    slot = lambda e, rank: jnp.sum(jnp.where(e[:, None] == experts, pstart[None, :], 0), axis=1) + rank
    pos_first = slot(ints[R_E1], ints[R_RANK1])
    pos_second = slot(ints[R_E2], ints[R_RANK2])
    n_blocks = n // bm + N_EXPERTS
    npad = n_blocks * bm
    starts = jnp.arange(n_blocks, dtype=jnp.int32) * bm
    n_used = pend[-1:] // bm
    last_start = (n_used[0] - 1) * bm
    block_e = jnp.sum((pend[None, :] <= jnp.minimum(starts, last_start)[:, None]).astype(jnp.int32), axis=1)
    blk = jnp.arange(n_blocks, dtype=jnp.int32)
    later_other = (blk[None, :] > blk[:, None]) & (block_e[None, :] != block_e[:, None]) & (blk[None, :] < n_used[0])
    first_later = jnp.min(jnp.where(later_other, blk[None, :], n_blocks), axis=1)
    next_e = jnp.where(first_later < n_blocks, block_e[jnp.minimum(first_later, n_blocks - 1)], -1).astype(jnp.int32)
    xs = _scatter_rows_twice(a2p, pos_first, pos_second, npad)
    ys = _expert_mlps(xs, block_e, n_used, next_e, w1, w3, w2, layer)
    return [_gather_rows(ys, jnp.concatenate([pos_first[r0:r1], pos_second[r0:r1]])) for r0, r1 in row_ranges]


def _moe_combine(h, yp_first, yp_second, route, g2):
    w1 = route[:, R_W1:R_W1 + 1]
    w2 = route[:, R_W2:R_W2 + 1]
    wide = lambda p: jnp.concatenate(_unpack_pairs_f32(p), axis=1)
    return h + g2 * (w1 * wide(yp_first) + w2 * wide(yp_second))


def _gmlp_kernel(h_ref, ya_ref, yb_ref, rt_ref, modp_ref, mod_ref, g1_ref, win_ref, bin_ref, vg_ref, ws_ref, bsb_ref,
                 wout_ref, g2_ref, wr_ref, br_ref, ltri_ref,
                 h1_ref, a2_ref, route_ref, rtt_ref, cnt_ref, carry_ref, *, n_ctx_tiles, tiles_per_req):
    i = pl.program_id(0)
    seq = _seq_of_tile(i, n_ctx_tiles, tiles_per_req)

    @pl.when(i == 0)
    def _():
        carry_ref[...] = jnp.zeros_like(carry_ref)

    gw = vg_ref.shape[1]
    cg = gw // GM_GROUPS
    subs = [pl.ds(s * GMLP_SUB, GMLP_SUB) for s in range(h_ref.shape[0] // GMLP_SUB)]
    hs = [_moe_combine(h_ref[rs, :], ya_ref[rs, :], yb_ref[rs, :], rt_ref[rs, :], _mod_row(modp_ref, 5, seq))
          for rs in subs]
    acts = [_modulated(h, g1_ref[...], _mod_row(mod_ref, 0, seq), _mod_row(mod_ref, 1, seq)).astype(BF16) for h in hs]
    vs = [_gelu_tanh(_dot(a, win_ref[:, gw:]) + bin_ref[:, gw:]) for a in acts]
    vbs = [((v * lax.rsqrt(jnp.mean(v * v, axis=-1, keepdims=True) + EPS)) * vg_ref[...]).astype(BF16) for v in vs]
    us = [_gelu_tanh(_dot(a, win_ref[:, :gw]) + bin_ref[:, :gw]) for a in acts]
    vms = []
    for vb in vbs:
        rows = []
        for c in range(GMLP_SUB // CHUNK):
            cols = [_dot(ws_ref[g], vb[c * CHUNK:(c + 1) * CHUNK, g * cg:(g + 1) * cg]) for g in range(GM_GROUPS)]
            rows.append(jnp.concatenate(cols, axis=1) + bsb_ref[...])
        vms.append(jnp.concatenate(rows, axis=0))
    mixes = [_dot((u * vm).astype(BF16), wout_ref[...]) for u, vm in zip(us, vms)]
    h1s = [h + _mod_row(mod_ref, 2, seq) * mix for h, mix in zip(hs, mixes)]
    a2s = [_modulated(h1, g2_ref[...], _mod_row(mod_ref, 3, seq), _mod_row(mod_ref, 4, seq)) for h1 in h1s]
    carry = carry_ref[...]
    for rs, h1, a2 in zip(subs, h1s, a2s):
        h1_ref[rs, :] = h1
        a2_ref[rs, :] = _pack_bf16_pairs(a2)
        slab, carry = _route(a2, wr_ref[...], br_ref[...], ltri_ref[...], carry)
        route_ref[rs, :] = slab
        rtt_ref[:, rs] = slab.T[:R_ROWS, :]
    carry_ref[...] = carry
    cnt_ref[...] = carry


def _gmlp_layer(h, y2, route_prev, mod_prev, mod, g1, win, bin_, vg, ws, bsb, wout, g2, router,
                n_ctx, rows_per_req):
    t, d = h.shape
    tm = ROW_TILE * SUBTILES
    wr, br = router
    ltri = _lower_tri(GMLP_SUB)
    full = lambda shape: pl.BlockSpec(shape, lambda i: (0,) * len(shape), pipeline_mode=pl.Buffered(1))
    row = lambda w: pl.BlockSpec((tm, w), lambda i: (i, 0))
    kern = functools.partial(_gmlp_kernel, n_ctx_tiles=n_ctx // tm, tiles_per_req=rows_per_req // tm)
    args = (h, y2, y2, route_prev, mod_prev, mod, g1, win, bin_, vg, ws, bsb, wout, g2, wr, br, ltri)
    second = pl.BlockSpec((tm, d // 2), lambda i: (i + t // tm, 0))
    in_specs = [row(d), row(d // 2), second, row(LANES)] + [full(a.shape) for a in args[4:]]
    return pl.pallas_call(
        kern,
        grid=(t // tm,),
        in_specs=in_specs,
        out_specs=[row(d), row(d // 2), row(LANES), pl.BlockSpec((R_ROWS, tm), lambda i: (0, i)),
                   pl.BlockSpec((1, LANES), lambda i: (0, 0))],
        out_shape=[jax.ShapeDtypeStruct((t, d), F32), jax.ShapeDtypeStruct((t, d // 2), jnp.int32),
                   jax.ShapeDtypeStruct((t, LANES), F32), jax.ShapeDtypeStruct((R_ROWS, t), F32),
                   jax.ShapeDtypeStruct((1, LANES), F32)],
        scratch_shapes=[pltpu.VMEM((1, LANES), F32)],
        compiler_params=_cparams(1),
        name="gmlp_router",
    )(*args)


def _final_kernel(h_ref, ya_ref, yb_ref, rt_ref, mod_ref, o_ref, *, first_tile, n_ctx_tiles, tiles_per_req):
    seq = _seq_of_tile(first_tile + pl.program_id(0), n_ctx_tiles, tiles_per_req)
    o_ref[...] = _moe_combine(h_ref[...], ya_ref[...], yb_ref[...], rt_ref[...], _mod_row(mod_ref, 5, seq))


def _final_combine(h, y2, route, mod, first_row, n_ctx, rows_per_req):
    d = h.shape[1]
    n = y2.shape[0] // 2
    tm = ROW_TILE
    first_tile = first_row // tm
    row = lambda w: pl.BlockSpec((tm, w), lambda i: (first_tile + i, 0))
    kern = functools.partial(_final_kernel, first_tile=first_tile, n_ctx_tiles=n_ctx // tm,
                             tiles_per_req=rows_per_req // tm)
    return pl.pallas_call(
        kern,
        grid=(n // tm,),
        in_specs=[row(d), pl.BlockSpec((tm, d // 2), lambda i: (i, 0)),
                  pl.BlockSpec((tm, d // 2), lambda i: (i + n // tm, 0)),
                  row(LANES), pl.BlockSpec(mod.shape, lambda i: (0, 0, 0))],
        out_specs=pl.BlockSpec((tm, d), lambda i: (i, 0)),
        out_shape=jax.ShapeDtypeStruct((n, d), F32),
        compiler_params=_cparams(1),
        name="final_combine",
    )(h, y2, y2, route, mod)


def kernel(x_prompt, x_sample, cache_k, cache_v, c, c_ctx, ada_w, ada_b, norm1_g, norm2_g, attn_wq, attn_wk,
           attn_wv, attn_wo, attn_qnorm, attn_knorm, attn_lam, attn_subln, gm_win, gm_bin, gm_vnorm, gm_ws,
           gm_bs, gm_wout, moe_wc, moe_bc, moe_wf, moe_bf, moe_w1, moe_w3, moe_w2):
    batch, seq_len, d = x_prompt.shape
    n_req, req_len, _ = x_sample.shape
    depth = ada_w.shape[0]
    assert depth == 2 and attn_wq.shape[0] == 1 and gm_win.shape[0] == 1
    n_ctx = batch * seq_len
    assert n_ctx % req_len == 0 and req_len % ROW_TILE == 0 and seq_len == ROW_TILE
    past = cache_k.shape[2]

    h_ctx = x_prompt.reshape(n_ctx, d)
    h_lat = x_sample.reshape(n_req * req_len, d)
    cvec = jnp.concatenate([c_ctx[None, :], c, jnp.zeros((8 - 1 - n_req, d), F32)], axis=0)
    mod = _ada_mod(cvec, ada_w, ada_b)
    routers = [_router_weights(moe_wc[i], moe_bc[i], moe_wf[i], moe_bf[i]) for i in range(depth)]
    row_vec = lambda v: v.reshape(1, -1)

    wqkv = jnp.concatenate([attn_wq[0], attn_wk[0], attn_wv[0]], axis=1).astype(BF16)
    reps = d // HEAD_DIM
    qk_gains = (row_vec(jnp.tile(attn_qnorm[0], reps)), row_vec(jnp.tile(attn_knorm[0], reps)))
    *qkv_ctx, k_new, v_new = _pre_attn(h_ctx, mod[0], row_vec(norm1_g[0]), wqkv, *qk_gains, ctx_len=seq_len)
    qkv_lat = _pre_attn(h_lat, mod[0], row_vec(norm1_g[0]), wqkv, *qk_gains, rows_per_req=req_len)
    o_ctx, o_lat = _attention(qkv_ctx, qkv_lat, attn_lam[0], row_vec(attn_subln[0]),
                              cache_k[:, 0].reshape(n_req, past, d), cache_v[:, 0].reshape(n_req, past, d),
                              seq_len, n_req, req_len, layer=0)
    h1, a2, route0, route0_t, cnt0 = _post_attn(h_ctx, h_lat, o_ctx, o_lat, mod[0], attn_wo[0].astype(BF16),
                                      row_vec(norm2_g[0]), routers[0], req_len)
    t = n_ctx + n_req * req_len
    (y2,) = _moe(a2, route0_t, cnt0, moe_w1, moe_w3, moe_w2, 0, [(0, t)])

    gw = gm_vnorm.shape[1]
    bsb = jnp.repeat(gm_bs[0].T, gw // GM_GROUPS, axis=1)
    h2, a2, route1, route1_t, cnt1 = _gmlp_layer(h1, y2, route0, mod[0], mod[1], row_vec(norm1_g[1]), gm_win[0].astype(BF16),
                                       row_vec(gm_bin[0]), row_vec(gm_vnorm[0]), gm_ws[0].astype(BF16), bsb,
                                       gm_wout[0].astype(BF16), row_vec(norm2_g[1]), routers[1], n_ctx, req_len)
    y2_ctx, y2_lat = _moe(a2, route1_t, cnt1, moe_w1, moe_w3, moe_w2, 1, [(0, n_ctx), (n_ctx, t)])
    y_ctx = _final_combine(h2, y2_ctx, route1, mod[1], 0, n_ctx, req_len)
    y_lat = _final_combine(h2, y2_lat, route1, mod[1], n_ctx, n_ctx, req_len)

    y_prompt = y_ctx.reshape(batch, seq_len, d)
    y_sample = y_lat.reshape(n_req, req_len, d)
    new_cache_k = jnp.transpose(k_new.reshape(batch, 1, N_HEADS, 2, HEAD_DIM, seq_len), (0, 1, 5, 2, 3, 4))
    new_cache_v = v_new.reshape(batch, 1, seq_len, N_HEADS, V_DIM)
    return (y_prompt, y_sample, new_cache_k, new_cache_v)
```

```python
import functools
import math

import jax
import jax.numpy as jnp
from jax import lax
from jax.experimental import pallas as pl
from jax.experimental.pallas import tpu as pltpu
from jax.experimental.pallas import tpu_sc as plsc

F32 = jnp.float32
BF16 = jnp.bfloat16

N_HEADS = 8
HEAD_DIM = 64
V_DIM = 2 * HEAD_DIM
GRID_W = 64
ROPE_THETA = 10000.0
CHUNK = 128
GM_GROUPS = 8
N_EXPERT_GROUPS = 4
EXPERTS_PER_GROUP = 8
N_EXPERTS = N_EXPERT_GROUPS * EXPERTS_PER_GROUP
EPS = 1e-6

LANES = 128
MXU_TILE = 256
ROW_TILE = 256
PRE_ATTN_TILE = 512
POST_ATTN_SUBTILES = 4
GMLP_TILE = 512
MOE_BLOCK = 256
ATTN_Q_TILE = 512
ATTN_HEADS_PER_STEP = 2
SC_WINDOW_BYTES = 128 * 1024
VMEM_LIMIT = 56 * 1024 * 1024

R_E1, R_E2, R_W1, R_W2, R_RANK1, R_RANK2 = 0, 1, 2, 3, 4, 5
R_ROWS = 8
L_EXPERT0 = N_EXPERT_GROUPS


def _lambda_init(layer):
    return 0.8 - 0.6 * math.exp(-0.3 * layer)


def _cparams(n_axes):
    return pltpu.CompilerParams(dimension_semantics=("arbitrary",) * n_axes, vmem_limit_bytes=VMEM_LIMIT)


def _seq_of_tile(i, n_ctx_tiles, tiles_per_req):
    return jnp.where(i < n_ctx_tiles, 0, 1 + (i - n_ctx_tiles) // tiles_per_req)


def _split_specs(tm, d, n_ctx_tiles):
    return [pl.BlockSpec((tm, d), lambda i: (jnp.minimum(i, n_ctx_tiles - 1), 0)),
            pl.BlockSpec((tm, d), lambda i: (jnp.maximum(i - n_ctx_tiles, 0), 0))]


def _mod_row(mod_ref, part, seq):
    return mod_ref[part, pl.ds(seq, 1), :]


def _modulated(x, g, shift, scale):
    y = x * lax.rsqrt(jnp.mean(x * x, axis=-1, keepdims=True) + EPS)
    return (y * g) * (1.0 + scale) + shift


def _split_bf16(x):
    hi = x.astype(BF16)
    lo = (x - hi.astype(F32)).astype(BF16)
    return hi, lo


def _gelu_tanh(x):
    c = math.sqrt(2.0 / math.pi)
    hx = 0.5 * x
    return hx + hx * jnp.tanh(x * (c + (c * 0.044715) * (x * x)))


def _dot(a, b):
    return jnp.dot(a, b, preferred_element_type=F32)


def _pack_bf16_pairs(x):
    half = x.shape[1] // 2
    bits = lax.bitcast_convert_type(x.astype(BF16).astype(F32), jnp.uint32)
    packed = (bits[:, :half] >> 16) | bits[:, half:]
    return lax.bitcast_convert_type(packed, jnp.int32)


def _unpack_pairs_f32(p):
    u = lax.bitcast_convert_type(p, jnp.uint32)
    return lax.bitcast_convert_type(u << 16, F32), lax.bitcast_convert_type(u & jnp.uint32(0xFFFF0000), F32)


def _unpack_bf16_pairs(p):
    lo, hi = _unpack_pairs_f32(p)
    return lo.astype(BF16), hi.astype(BF16)


def _ada_kernel(c_ref, w_ref, b_ref, o_ref):
    c = c_ref[...]
    s = c * jax.nn.sigmoid(c)
    o_ref[...] = _dot(s.astype(BF16), w_ref[...].astype(BF16)) + b_ref[...]


def _ada_mod(cvec, ada_w, ada_b):
    depth, d, d6 = ada_w.shape
    parts = d6 // d
    rows = cvec.shape[0]
    return pl.pallas_call(
        _ada_kernel,
        grid=(depth, parts),
        in_specs=[
            pl.BlockSpec((rows, d), lambda l, j: (0, 0)),
            pl.BlockSpec((None, d, d), lambda l, j: (l, 0, j)),
            pl.BlockSpec((None, 1, d), lambda l, j: (l, 0, j)),
        ],
        out_specs=pl.BlockSpec((None, None, rows, d), lambda l, j: (l, j, 0, 0)),
        out_shape=jax.ShapeDtypeStruct((depth, parts, rows, d), F32),
        compiler_params=_cparams(2),
        name="ada_mod",
    )(cvec, ada_w, ada_b.reshape(depth, 1, d6))


def _pre_attn_kernel(*refs, latent, tiles_per_req):
    if latent:
        h_ref, mod_ref, g_ref, w_ref, qg_ref, kg_ref, gs_ref, cos_ref, sin_ref, q_ref, k_ref, v_ref = refs
        seq = 1 + pl.program_id(0) // tiles_per_req
    else:
        h_ref, mod_ref, g_ref, w_ref, qg_ref, kg_ref, gs_ref, q_ref, k_ref, v_ref, kf_ref, vf_ref = refs
        seq = 0
    d = h_ref.shape[1]
    q_scale = HEAD_DIM ** -0.5 * math.log2(math.e)
    subs = [pl.ds(s * ROW_TILE, ROW_TILE) for s in range(q_ref.shape[0] // ROW_TILE)]

    def group_rms_inv(x):
        sq_hi, sq_lo = _split_bf16(x * x)
        w = gs_ref.shape[0]
        sums = [_dot(sq_hi[:, c:c + w], gs_ref[...]) + _dot(sq_lo[:, c:c + w], gs_ref[...]) for c in range(0, d, w)]
        return lax.rsqrt(jnp.concatenate(sums, axis=1) * (1.0 / HEAD_DIM) + EPS)

    def rope(x, rows):
        cos = cos_ref[rows, :]
        sin = sin_ref[rows, :]
        lane = lax.broadcasted_iota(jnp.int32, cos.shape, 1)
        first = (lane & 31) < 16
        outs = []
        for hh in range(d // LANES):
            xs = x[:, hh * LANES:(hh + 1) * LANES]
            rot = jnp.where(first, pltpu.roll(xs, LANES - 16, 1), pltpu.roll(xs, 16, 1))
            outs.append(xs * cos + rot * sin)
        return jnp.concatenate(outs, axis=1)

    acts = [_modulated(h_ref[rows, :], g_ref[...], _mod_row(mod_ref, 0, seq), _mod_row(mod_ref, 1, seq)).astype(BF16)
            for rows in subs]
    qkvs = [_dot(a, w_ref[...]) for a in acts]
    qns = [(qkv[:, :d] * group_rms_inv(qkv[:, :d])) * qg_ref[...] for qkv in qkvs]
    kns = [(qkv[:, d:2 * d] * group_rms_inv(qkv[:, d:2 * d])) * kg_ref[...] for qkv in qkvs]
    if latent:
        qrs = [rope(qn, rows) for qn, rows in zip(qns, subs)]
        krs = [rope(kn, rows) for kn, rows in zip(kns, subs)]
    else:
        qrs, krs = qns, kns
    for rows, qkv, qr, kr in zip(subs, qkvs, qrs, krs):
        q_ref[rows, :] = (qr * q_scale).astype(BF16)
        k_ref[rows, :] = kr.astype(BF16)
        v_ref[rows, :] = qkv[:, 2 * d:].astype(BF16)
    if not latent:
        kn_all = jnp.concatenate(kns, axis=0)
        n_pos = kf_ref.shape[3]
        for r in range(kf_ref.shape[0]):
            kf_ref[r] = kn_all[r * n_pos:(r + 1) * n_pos, :].T.reshape(kf_ref.shape[1:])
        for rows, qkv in zip(subs, qkvs):
            vf_ref[rows, :] = qkv[:, 2 * d:]


def _rope_tables(n_pos):
    half = HEAD_DIM // 2
    pos = jnp.arange(n_pos, dtype=jnp.int32)
    row = (pos // GRID_W).astype(F32)
    col = (pos % GRID_W).astype(F32)
    inv = 1.0 / (ROPE_THETA ** (jnp.arange(0, half, 2, dtype=F32) / half))
    ang_r = row[:, None] * inv[None, :]
    ang_c = col[:, None] * inv[None, :]
    ang = jnp.concatenate([ang_r, ang_r, ang_c, ang_c], axis=-1)
    quarter = half // 2
    sign = jnp.tile(jnp.concatenate([-jnp.ones((quarter,), F32), jnp.ones((quarter,), F32)]), 2)
    cos = jnp.tile(jnp.cos(ang), (1, 2))
    sin = jnp.tile(jnp.sin(ang) * sign[None, :], (1, 2))
    return cos, sin


def _pre_attn(h, mod, g, wqkv, qg, kg, *, ctx_len=None, rows_per_req=None):
    n, d = h.shape
    latent = rows_per_req is not None
    tm = PRE_ATTN_TILE
    gid = jnp.arange(MXU_TILE, dtype=jnp.int32) // HEAD_DIM
    gs = (gid[:, None] == gid[None, :]).astype(BF16)
    full = lambda shape: pl.BlockSpec(shape, lambda i: (0,) * len(shape), pipeline_mode=pl.Buffered(1))
    row = pl.BlockSpec((tm, d), lambda i: (i, 0))
    args = [h, mod, g, wqkv, qg, kg, gs]
    in_specs = [row] + [full(a.shape) for a in args[1:]]
    out_specs = [row, row, row]
    out_shape = [jax.ShapeDtypeStruct((n, d), BF16)] * 3
    if latent:
        tiles_per_req = rows_per_req // tm
        pos = pl.BlockSpec((tm, LANES), lambda i: (i % tiles_per_req, 0))
        args += list(_rope_tables(rows_per_req))
        in_specs += [pos, pos]
    else:
        assert tm % ctx_len == 0
        tiles_per_req = None
        out_specs += [pl.BlockSpec((tm // ctx_len, d // HEAD_DIM, HEAD_DIM, ctx_len), lambda i: (i, 0, 0, 0)), row]
        out_shape += [jax.ShapeDtypeStruct((n // ctx_len, d // HEAD_DIM, HEAD_DIM, ctx_len), F32),
                      jax.ShapeDtypeStruct((n, d), F32)]
    return pl.pallas_call(
        functools.partial(_pre_attn_kernel, latent=latent, tiles_per_req=tiles_per_req),
        grid=(n // tm,),
        in_specs=in_specs,
        out_specs=out_specs,
        out_shape=out_shape,
        compiler_params=_cparams(1),
        name="pre_attn_latent" if latent else "pre_attn_ctx",
    )(*args)


def _attn_kernel(*refs, n_heads, has_cache, layer):
    if has_cache:
        lam_ref, sub_ref, q_ref, kn_ref, vn_ref, kc_ref, vc_ref, o_ref = refs
    else:
        lam_ref, sub_ref, q_ref, kn_ref, vn_ref, o_ref = refs
    lf = lam_ref[...]
    lam = (jnp.exp(jnp.sum(lf[0:1] * lf[1:2], axis=-1, keepdims=True))
           - jnp.exp(jnp.sum(lf[2:3] * lf[3:4], axis=-1, keepdims=True)) + _lambda_init(layer))
    nt = (((1,), (1,)), ((), ()))
    tq = q_ref.shape[0]
    lane = lax.broadcasted_iota(jnp.int32, (tq, LANES), 1)
    heads = [slice(hh * LANES, (hh + 1) * LANES) for hh in range(n_heads)]

    def qk(hh, sl):
        qh = q_ref[:, sl]
        zero = jnp.zeros_like(qh)
        kn = kn_ref[:, sl]
        kc_t = kc_ref[hh].astype(BF16) if has_cache else None
        out = []
        for qm in (jnp.where(lane < HEAD_DIM, qh, zero), jnp.where(lane >= HEAD_DIM, qh, zero)):
            s = [lax.dot_general(qm, kn, nt, preferred_element_type=F32)]
            out.append(s + [_dot(qm, kc_t)] if has_cache else s)
        return out

    def soft(scores):
        maxes = [functools.reduce(jnp.maximum, [jnp.max(x, axis=-1, keepdims=True) for x in s]) for s in scores]
        exps = [[jnp.exp2(x - m) for x in s] for s, m in zip(scores, maxes)]
        r0, r1 = [1.0 / functools.reduce(lambda u, w: u + w, [jnp.sum(x, axis=-1, keepdims=True) for x in e])
                  for e in exps]
        return [(x0 * r0 - lam * (x1 * r1)).astype(BF16) for x0, x1 in zip(*exps)]

    def pv(sl, a):
        vals = [vn_ref[:, sl]] + ([vc_ref[:, sl].astype(BF16)] if has_cache else [])
        o = functools.reduce(lambda u, w: u + w, [_dot(aj, vv) for aj, vv in zip(a, vals)])
        o = o * lax.rsqrt(jnp.mean(o * o, axis=-1, keepdims=True) + EPS)
        o_ref[:, sl] = ((o * sub_ref[...]) * (1.0 - _lambda_init(layer))).astype(BF16)

    scores = [qk(hh, sl) for hh, sl in enumerate(heads)]
    mixed = [soft(sc) for sc in scores]
    for sl, a in zip(heads, mixed):
        pv(sl, a)


def _attention(qkv_ctx, qkv_lat, lam_p, subln, cache_k, cache_v, ctx_len, n_req, req_len, layer):
    n_ctx, d = qkv_ctx[0].shape
    small = lambda shape: pl.BlockSpec(shape, lambda *_: (0,) * len(shape))
    n_ctx_req = n_ctx // ctx_len
    blk = pl.BlockSpec((ctx_len, d), lambda b: (b, 0))
    o_ctx = pl.pallas_call(
        functools.partial(_attn_kernel, n_heads=N_HEADS, has_cache=False, layer=layer),
        grid=(n_ctx_req,),
        in_specs=[small(lam_p.shape), small(subln.shape), blk, blk, blk],
        out_specs=blk,
        out_shape=jax.ShapeDtypeStruct((n_ctx, d), BF16),
        compiler_params=_cparams(1),
        name="attn_ctx",
    )(lam_p, subln, *qkv_ctx)

    tq = ATTN_Q_TILE
    nq = req_len // tq
    past = cache_v.shape[1]
    hp = ATTN_HEADS_PER_STEP
    small3 = lambda shape: pl.BlockSpec(shape, lambda b, h, i: (0,) * len(shape))
    qspec = pl.BlockSpec((tq, hp * LANES), lambda b, h, i: (b * nq + i, h))
    kvspec = pl.BlockSpec((req_len, hp * LANES), lambda b, h, i: (b, h))
    cspec = pl.BlockSpec((None, past, hp * LANES), lambda b, h, i: (b, 0, h))
    ckspec = pl.BlockSpec((None, hp, LANES, past), lambda b, h, i: (b, h, 0, 0))
    o_lat = pl.pallas_call(
        functools.partial(_attn_kernel, n_heads=hp, has_cache=True, layer=layer),
        grid=(n_req, N_HEADS // hp, nq),
        in_specs=[small3(lam_p.shape), pl.BlockSpec((1, LANES), lambda b, h, i: (0, 0)),
                  qspec, kvspec, kvspec, ckspec, cspec],
        out_specs=pl.BlockSpec((tq, hp * LANES), lambda b, h, i: (b * nq + i, h)),
        out_shape=jax.ShapeDtypeStruct((n_req * req_len, d), BF16),
        compiler_params=_cparams(3),
        name="attn_latent",
    )(lam_p, subln, *qkv_lat, cache_k, cache_v)
    return o_ctx, o_lat


def _route(a2, wr, br, ltri, carry):
    tm = a2.shape[0]
    logits = _dot(a2.astype(BF16), wr) + br
    lane_i = lax.broadcasted_iota(jnp.int32, (tm, LANES), 1)
    lane = lane_i.astype(F32)
    neg = jnp.full((tm, LANES), -jnp.inf, F32)
    big = jnp.full((tm, LANES), float(LANES), F32)
    first_lane = lambda mask: jnp.min(jnp.where(mask, lane, big), axis=-1, keepdims=True)

    lc = jnp.where(lane_i < N_EXPERT_GROUPS, logits, neg)
    mc = jnp.max(lc, axis=-1, keepdims=True)
    pg = 1.0 / jnp.sum(jnp.exp(lc - mc), axis=-1, keepdims=True)
    gi = first_lane(lc == mc)
    assert EXPERTS_PER_GROUP == 8
    grp = lax.shift_right_arithmetic(lane_i - L_EXPERT0, 3).astype(F32)
    in_group = (lane_i >= L_EXPERT0) & (lane_i < L_EXPERT0 + N_EXPERTS) & (grp == gi)
    ls = jnp.where(in_group, logits, neg)
    t1 = jnp.max(ls, axis=-1, keepdims=True)
    i1 = first_lane(ls == t1)
    ls2 = jnp.where(lane == i1, neg, ls)
    t2 = jnp.max(ls2, axis=-1, keepdims=True)
    i2 = first_lane(ls2 == t2)
    ex = jnp.exp(t2 - t1)
    w1 = pg * (1.0 / (1.0 + ex))
    w2 = pg * (ex / (1.0 + ex))
    e1 = i1 - float(L_EXPERT0)
    e2 = i2 - float(L_EXPERT0)
    oh1 = lane == e1
    oh2 = lane == e2
    onehot = oh1.astype(F32) + oh2.astype(F32)
    before = _dot(ltri, onehot.astype(BF16)) + carry
    zero = jnp.zeros_like(before)
    rank1 = jnp.sum(jnp.where(oh1, before, zero), axis=-1, keepdims=True)
    rank2 = jnp.sum(jnp.where(oh2, before, zero), axis=-1, keepdims=True)
    slab = jnp.zeros((tm, LANES), F32)
    for ln, val in ((R_E1, e1), (R_E2, e2), (R_W1, w1), (R_W2, w2), (R_RANK1, rank1), (R_RANK2, rank2)):
        slab = jnp.where(lane_i == ln, val, slab)
    return slab, carry + jnp.sum(onehot, axis=0, keepdims=True)


def _router_weights(wc, bc, wf, bf_):
    d = wc.shape[0]
    pad = LANES - N_EXPERT_GROUPS - N_EXPERTS
    w = jnp.concatenate([wc, wf, jnp.zeros((d, pad), F32)], axis=1)
    b = jnp.concatenate([bc, bf_, jnp.zeros((pad,), F32)])[None, :]
    return w.astype(BF16), b


def _lower_tri(n):
    r = jnp.arange(n, dtype=jnp.int32)
    return (r[None, :] < r[:, None]).astype(BF16)


def _post_attn_kernel(hc_ref, hl_ref, oc_ref, ol_ref, mod_ref, wo_ref, g2_ref, wr_ref, br_ref, ltri_ref,
                      h1_ref, a2_ref, route_ref, rtt_ref, cnt_ref, carry_ref, *, n_ctx_tiles, tiles_per_req):
    i = pl.program_id(0)
    seq = _seq_of_tile(i, n_ctx_tiles, tiles_per_req)

    @pl.when(i == 0)
    def _():
        carry_ref[...] = jnp.zeros_like(carry_ref)

    subs = [pl.ds(s * ROW_TILE, ROW_TILE) for s in range(h1_ref.shape[0] // ROW_TILE)]
    mixes = [_dot(jnp.where(i < n_ctx_tiles, oc_ref[rows, :], ol_ref[rows, :]), wo_ref[...]) for rows in subs]
    h1s = [jnp.where(i < n_ctx_tiles, hc_ref[rows, :], hl_ref[rows, :]) + _mod_row(mod_ref, 2, seq) * mix
           for rows, mix in zip(subs, mixes)]
    a2s = [_modulated(h1, g2_ref[...], _mod_row(mod_ref, 3, seq), _mod_row(mod_ref, 4, seq)) for h1 in h1s]
    for rows, h1, a2 in zip(subs, h1s, a2s):
        h1_ref[rows, :] = h1
        a2_ref[rows, :] = _pack_bf16_pairs(a2)
    carry = carry_ref[...]
    for rows, a2 in zip(subs, a2s):
        slab, carry = _route(a2, wr_ref[...], br_ref[...], ltri_ref[...], carry)
        route_ref[rows, :] = slab
        rtt_ref[:, rows] = slab.T[:R_ROWS, :]
    carry_ref[...] = carry
    cnt_ref[...] = carry


def _post_attn(h_ctx, h_lat, o_ctx, o_lat, mod, wo, g2, router, rows_per_req):
    n_ctx, d = h_ctx.shape
    t = n_ctx + h_lat.shape[0]
    tm = ROW_TILE * POST_ATTN_SUBTILES
    wr, br = router
    ltri = _lower_tri(ROW_TILE)
    full = lambda shape: pl.BlockSpec(shape, lambda i: (0,) * len(shape))
    row = pl.BlockSpec((tm, d), lambda i: (i, 0))
    kern = functools.partial(_post_attn_kernel, n_ctx_tiles=n_ctx // tm, tiles_per_req=rows_per_req // tm)
    return pl.pallas_call(
        kern,
        grid=(t // tm,),
        in_specs=_split_specs(tm, d, n_ctx // tm) + _split_specs(tm, d, n_ctx // tm) + [
                  full(mod.shape), full(wo.shape), full((1, d)), full(wr.shape),
                  full(br.shape), full(ltri.shape)],
        out_specs=[row, pl.BlockSpec((tm, d // 2), lambda i: (i, 0)), pl.BlockSpec((tm, LANES), lambda i: (i, 0)),
                   pl.BlockSpec((R_ROWS, tm), lambda i: (0, i)), full((1, LANES))],
        out_shape=[jax.ShapeDtypeStruct((t, d), F32), jax.ShapeDtypeStruct((t, d // 2), jnp.int32),
                   jax.ShapeDtypeStruct((t, LANES), F32), jax.ShapeDtypeStruct((R_ROWS, t), F32),
                   jax.ShapeDtypeStruct((1, LANES), F32)],
        scratch_shapes=[pltpu.VMEM((1, LANES), F32)],
        compiler_params=_cparams(1),
        name="post_attn_router",
    )(h_ctx, h_lat, o_ctx, o_lat, mod, wo, g2, wr, br, ltri)


def _gather_rows(src, idx):
    n = idx.shape[0]
    d = src.shape[1]
    w = _sc_window_rows(src)
    mesh = plsc.VectorSubcoreMesh(core_axis_name="core", subcore_axis_name="subcore")

    @pl.kernel(out_type=jax.ShapeDtypeStruct((n, d), src.dtype), mesh=mesh)
    def gather(src_hbm, idx_hbm, out_hbm):
        def body(idx_vmem, out_vmem):
            pltpu.sync_copy(src_hbm.at[idx_vmem.at[0, pl.ds(0, w)]], out_vmem)

        pltpu.emit_pipeline(
            body,
            grid=(n // w,),
            in_specs=[pl.BlockSpec((1, LANES), lambda i: (i, 0))],
            out_specs=[pl.BlockSpec((w, d), lambda i: (i, 0))],
            core_axis_name=("core", "subcore"),
            dimension_semantics=(pltpu.PARALLEL,),
        )(idx_hbm, out_hbm)

    return gather(src, _index_windows(idx, w))


def _sc_window_rows(src):
    return min(LANES, SC_WINDOW_BYTES // (src.shape[1] * src.dtype.itemsize))


def _index_windows(idx, w):
    return jnp.pad(idx.reshape(idx.shape[0] // w, w), ((0, 0), (0, LANES - w)))


def _scatter_rows_twice(src, idx_a, idx_b, n_out):
    t, d = src.shape
    w = _sc_window_rows(src)
    mesh = plsc.VectorSubcoreMesh(core_axis_name="core", subcore_axis_name="subcore")

    @pl.kernel(out_type=jax.ShapeDtypeStruct((n_out, d), src.dtype), mesh=mesh)
    def scatter(src_hbm, ia_hbm, ib_hbm, out_hbm):
        def body(src_vmem, ia_vmem, ib_vmem):
            pltpu.sync_copy(src_vmem, out_hbm.at[ia_vmem.at[0, pl.ds(0, w)]])
            pltpu.sync_copy(src_vmem, out_hbm.at[ib_vmem.at[0, pl.ds(0, w)]])

        pltpu.emit_pipeline(
            body,
            grid=(t // w,),
            in_specs=[pl.BlockSpec((w, d), lambda i: (i, 0)),
                      pl.BlockSpec((1, LANES), lambda i: (i, 0)),
                      pl.BlockSpec((1, LANES), lambda i: (i, 0))],
            out_specs=[],
            core_axis_name=("core", "subcore"),
            dimension_semantics=(pltpu.PARALLEL,),
        )(src_hbm, ia_hbm, ib_hbm)

    return scatter(src, _index_windows(idx_a, w), _index_windows(idx_b, w))


def _expert_kernel(be_ref, nb_ref, ne_ref, x_ref, w1_hbm, w3_hbm, w2_hbm, y_ref,
                   w1s, w3s, w2s, w1b, w3b, w2b, sems, *, layer):
    b = pl.program_id(0)
    prev = be_ref[jnp.maximum(b - 1, 0)]
    used = b < nb_ref[0]

    def stage(e):
        return [pltpu.make_async_copy(src.at[layer, e], dst, sems.at[j])
                for j, (src, dst) in enumerate(((w1_hbm, w1s), (w3_hbm, w3s), (w2_hbm, w2s)))]

    @pl.when(used & ((b == 0) | (be_ref[b] != prev)))
    def _():
        @pl.when(b == 0)
        def _():
            for c in stage(be_ref[0]):
                c.start()

        for c in stage(be_ref[b]):
            c.wait()
        w1b[...] = w1s[...].astype(BF16)
        w3b[...] = w3s[...].astype(BF16)
        w2b[...] = w2s[...].astype(BF16)

        @pl.when(ne_ref[b] >= 0)
        def _():
            for c in stage(ne_ref[b]):
                c.start()

    @pl.when(used)
    def _():
        x_lo, x_hi = _unpack_bf16_pairs(x_ref[...])
        half = x_lo.shape[1]
        h1 = _dot(x_lo, w1b[:half, :]) + _dot(x_hi, w1b[half:, :])
        h3 = _dot(x_lo, w3b[:half, :]) + _dot(x_hi, w3b[half:, :])
        hb = jax.nn.silu(h1) * h3
        y_ref[...] = _pack_bf16_pairs(_dot(hb.astype(BF16), w2b[...]))


def _expert_mlps(xs, block_e, n_used, next_e, w1, w3, w2, layer):
    npad, half = xs.shape
    d, ff = w1.shape[2], w1.shape[3]
    assert d == 2 * half
    bm = MOE_BLOCK
    rows = lambda b, be, nb, ne: (jnp.minimum(b, nb[0] - 1), 0)
    hbm = pl.BlockSpec(memory_space=pl.ANY)
    grid_spec = pltpu.PrefetchScalarGridSpec(
        num_scalar_prefetch=3,
        grid=(npad // bm,),
        in_specs=[pl.BlockSpec((bm, half), rows), hbm, hbm, hbm],
        out_specs=pl.BlockSpec((bm, half), rows),
        scratch_shapes=[pltpu.VMEM((d, ff), F32), pltpu.VMEM((d, ff), F32), pltpu.VMEM((ff, d), F32),
                        pltpu.VMEM((d, ff), BF16), pltpu.VMEM((d, ff), BF16), pltpu.VMEM((ff, d), BF16),
                        pltpu.SemaphoreType.DMA((3,))],
    )
    return pl.pallas_call(
        functools.partial(_expert_kernel, layer=layer),
        grid_spec=grid_spec,
        out_shape=jax.ShapeDtypeStruct((npad, half), jnp.int32),
        compiler_params=_cparams(1),
        name="expert_mlps",
    )(block_e, n_used, next_e, xs, w1, w3, w2)


def _moe(a2p, route_t, counts, w1, w3, w2, layer, row_ranges):
    t = a2p.shape[0]
    bm = MOE_BLOCK
    n = 2 * t
    ints = route_t.astype(jnp.int32)
    cnt = counts[0, :N_EXPERTS].astype(jnp.int32)
    padded = ((cnt + bm - 1) // bm) * bm
    pend = jnp.cumsum(padded)
    pstart = pend - padded
    experts = jnp.arange(N_EXPERTS, dtype=jnp.int32)[None, :]
    slot = lambda e, rank: jnp.sum(jnp.where(e[:, None] == experts, pstart[None, :], 0), axis=1) + rank
    pos_first = slot(ints[R_E1], ints[R_RANK1])
    pos_second = slot(ints[R_E2], ints[R_RANK2])
    n_blocks = n // bm + N_EXPERTS
    npad = n_blocks * bm
    starts = jnp.arange(n_blocks, dtype=jnp.int32) * bm
    n_used = pend[-1:] // bm
    last_start = (n_used[0] - 1) * bm
    block_e = jnp.sum((pend[None, :] <= jnp.minimum(starts, last_start)[:, None]).astype(jnp.int32), axis=1)
    blk = jnp.arange(n_blocks, dtype=jnp.int32)
    later_other = (blk[None, :] > blk[:, None]) & (block_e[None, :] != block_e[:, None]) & (blk[None, :] < n_used[0])
    first_later = jnp.min(jnp.where(later_other, blk[None, :], n_blocks), axis=1)
    next_e = jnp.where(first_later < n_blocks, block_e[jnp.minimum(first_later, n_blocks - 1)], -1).astype(jnp.int32)
    xs = _scatter_rows_twice(a2p, pos_first, pos_second, npad)
    ys = _expert_mlps(xs, block_e, n_used, next_e, w1, w3, w2, layer)
    return [_gather_rows(ys, jnp.concatenate([pos_first[r0:r1], pos_second[r0:r1]])) for r0, r1 in row_ranges]


def _moe_combine(h, yp_first, yp_second, route, g2):
    w1 = route[:, R_W1:R_W1 + 1]
    w2 = route[:, R_W2:R_W2 + 1]
    wide = lambda p: jnp.concatenate(_unpack_pairs_f32(p), axis=1)
    return h + g2 * (w1 * wide(yp_first) + w2 * wide(yp_second))


def _gmlp_kernel(h_ref, ya_ref, yb_ref, rt_ref, modp_ref, mod_ref, g1_ref, win_ref, bin_ref, vg_ref, ws_ref, bsb_ref,
                 wout_ref, g2_ref, wr_ref, br_ref, ltri_ref,
                 h1_ref, a2_ref, route_ref, rtt_ref, cnt_ref, carry_ref, *, n_ctx_tiles, tiles_per_req):
    i = pl.program_id(0)
    seq = _seq_of_tile(i, n_ctx_tiles, tiles_per_req)

    @pl.when(i == 0)
    def _():
        carry_ref[...] = jnp.zeros_like(carry_ref)

    gw = vg_ref.shape[1]
    cg = gw // GM_GROUPS
    subs = [pl.ds(s * GMLP_TILE, GMLP_TILE) for s in range(h_ref.shape[0] // GMLP_TILE)]
    hs = [_moe_combine(h_ref[rs, :], ya_ref[rs, :], yb_ref[rs, :], rt_ref[rs, :], _mod_row(modp_ref, 5, seq))
          for rs in subs]
    acts = [_modulated(h, g1_ref[...], _mod_row(mod_ref, 0, seq), _mod_row(mod_ref, 1, seq)).astype(BF16) for h in hs]
    vs = [_gelu_tanh(_dot(a, win_ref[:, gw:]) + bin_ref[:, gw:]) for a in acts]
    vbs = [((v * lax.rsqrt(jnp.mean(v * v, axis=-1, keepdims=True) + EPS)) * vg_ref[...]).astype(BF16) for v in vs]
    us = [_gelu_tanh(_dot(a, win_ref[:, :gw]) + bin_ref[:, :gw]) for a in acts]
    vms = []
    for vb in vbs:
        rows = []
        for c in range(GMLP_TILE // CHUNK):
            cols = [_dot(ws_ref[g], vb[c * CHUNK:(c + 1) * CHUNK, g * cg:(g + 1) * cg]) for g in range(GM_GROUPS)]
            rows.append(jnp.concatenate(cols, axis=1) + bsb_ref[...])
        vms.append(jnp.concatenate(rows, axis=0))
    mixes = [_dot((u * vm).astype(BF16), wout_ref[...]) for u, vm in zip(us, vms)]
    h1s = [h + _mod_row(mod_ref, 2, seq) * mix for h, mix in zip(hs, mixes)]
    a2s = [_modulated(h1, g2_ref[...], _mod_row(mod_ref, 3, seq), _mod_row(mod_ref, 4, seq)) for h1 in h1s]
    carry = carry_ref[...]
    for rs, h1, a2 in zip(subs, h1s, a2s):
        h1_ref[rs, :] = h1
        a2_ref[rs, :] = _pack_bf16_pairs(a2)
        slab, carry = _route(a2, wr_ref[...], br_ref[...], ltri_ref[...], carry)
        route_ref[rs, :] = slab
        rtt_ref[:, rs] = slab.T[:R_ROWS, :]
    carry_ref[...] = carry
    cnt_ref[...] = carry


def _gmlp_layer(h, y2, route_prev, mod_prev, mod, g1, win, bin_, vg, ws, bsb, wout, g2, router,
                n_ctx, rows_per_req):
    t, d = h.shape
    tm = GMLP_TILE
    wr, br = router
    ltri = _lower_tri(GMLP_TILE)
    full = lambda shape: pl.BlockSpec(shape, lambda i: (0,) * len(shape), pipeline_mode=pl.Buffered(1))
    row = lambda w: pl.BlockSpec((tm, w), lambda i: (i, 0))
    kern = functools.partial(_gmlp_kernel, n_ctx_tiles=n_ctx // tm, tiles_per_req=rows_per_req // tm)
    args = (h, y2, y2, route_prev, mod_prev, mod, g1, win, bin_, vg, ws, bsb, wout, g2, wr, br, ltri)
    second = pl.BlockSpec((tm, d // 2), lambda i: (i + t // tm, 0))
    in_specs = [row(d), row(d // 2), second, row(LANES)] + [full(a.shape) for a in args[4:]]
    return pl.pallas_call(
        kern,
        grid=(t // tm,),
        in_specs=in_specs,
        out_specs=[row(d), row(d // 2), row(LANES), pl.BlockSpec((R_ROWS, tm), lambda i: (0, i)),
                   pl.BlockSpec((1, LANES), lambda i: (0, 0))],
        out_shape=[jax.ShapeDtypeStruct((t, d), F32), jax.ShapeDtypeStruct((t, d // 2), jnp.int32),
                   jax.ShapeDtypeStruct((t, LANES), F32), jax.ShapeDtypeStruct((R_ROWS, t), F32),
                   jax.ShapeDtypeStruct((1, LANES), F32)],
        scratch_shapes=[pltpu.VMEM((1, LANES), F32)],
        compiler_params=_cparams(1),
        name="gmlp_router",
    )(*args)


def _final_kernel(h_ref, ya_ref, yb_ref, rt_ref, mod_ref, o_ref, *, first_tile, n_ctx_tiles, tiles_per_req):
    seq = _seq_of_tile(first_tile + pl.program_id(0), n_ctx_tiles, tiles_per_req)
    o_ref[...] = _moe_combine(h_ref[...], ya_ref[...], yb_ref[...], rt_ref[...], _mod_row(mod_ref, 5, seq))


def _final_combine(h, y2, route, mod, first_row, n_ctx, rows_per_req):
    d = h.shape[1]
    n = y2.shape[0] // 2
    tm = ROW_TILE
    first_tile = first_row // tm
    row = lambda w: pl.BlockSpec((tm, w), lambda i: (first_tile + i, 0))
    kern = functools.partial(_final_kernel, first_tile=first_tile, n_ctx_tiles=n_ctx // tm,
                             tiles_per_req=rows_per_req // tm)
    return pl.pallas_call(
        kern,
        grid=(n // tm,),
        in_specs=[row(d), pl.BlockSpec((tm, d // 2), lambda i: (i, 0)),
                  pl.BlockSpec((tm, d // 2), lambda i: (i + n // tm, 0)),
                  row(LANES), pl.BlockSpec(mod.shape, lambda i: (0, 0, 0))],
        out_specs=pl.BlockSpec((tm, d), lambda i: (i, 0)),
        out_shape=jax.ShapeDtypeStruct((n, d), F32),
        compiler_params=_cparams(1),
        name="final_combine",
    )(h, y2, y2, route, mod)


def kernel(x_prompt, x_sample, cache_k, cache_v, c, c_ctx, ada_w, ada_b, norm1_g, norm2_g, attn_wq, attn_wk,
           attn_wv, attn_wo, attn_qnorm, attn_knorm, attn_lam, attn_subln, gm_win, gm_bin, gm_vnorm, gm_ws,
           gm_bs, gm_wout, moe_wc, moe_bc, moe_wf, moe_bf, moe_w1, moe_w3, moe_w2):
    batch, seq_len, d = x_prompt.shape
    n_req, req_len, _ = x_sample.shape
    depth = ada_w.shape[0]
    assert depth == 2 and attn_wq.shape[0] == 1 and gm_win.shape[0] == 1
    n_ctx = batch * seq_len
    tiles = (PRE_ATTN_TILE, ROW_TILE * POST_ATTN_SUBTILES, GMLP_TILE, ATTN_Q_TILE)
    assert all(n_ctx % tm == 0 and req_len % tm == 0 for tm in tiles) and GMLP_TILE % CHUNK == 0
    past = cache_k.shape[2]

    h_ctx = x_prompt.reshape(n_ctx, d)
    h_lat = x_sample.reshape(n_req * req_len, d)
    cvec = jnp.concatenate([c_ctx[None, :], c, jnp.zeros((8 - 1 - n_req, d), F32)], axis=0)
    mod = _ada_mod(cvec, ada_w, ada_b)
    routers = [_router_weights(moe_wc[i], moe_bc[i], moe_wf[i], moe_bf[i]) for i in range(depth)]
    row_vec = lambda v: v.reshape(1, -1)

    wqkv = jnp.concatenate([attn_wq[0], attn_wk[0], attn_wv[0]], axis=1).astype(BF16)
    reps = d // HEAD_DIM
    cache_k_t = jnp.transpose(cache_k[:, 0], (0, 2, 3, 4, 1)).reshape(n_req, N_HEADS, 2 * HEAD_DIM, past)
    qk_gains = (row_vec(jnp.tile(attn_qnorm[0], reps)), row_vec(jnp.tile(attn_knorm[0], reps)))
    *qkv_ctx, k_new, v_new = _pre_attn(h_ctx, mod[0], row_vec(norm1_g[0]), wqkv, *qk_gains, ctx_len=seq_len)
    qkv_lat = _pre_attn(h_lat, mod[0], row_vec(norm1_g[0]), wqkv, *qk_gains, rows_per_req=req_len)
    o_ctx, o_lat = _attention(qkv_ctx, qkv_lat, attn_lam[0], row_vec(attn_subln[0]),
                              cache_k_t, cache_v[:, 0].reshape(n_req, past, d),
                              seq_len, n_req, req_len, layer=0)
    h1, a2, route0, route0_t, cnt0 = _post_attn(h_ctx, h_lat, o_ctx, o_lat, mod[0], attn_wo[0].astype(BF16),
                                      row_vec(norm2_g[0]), routers[0], req_len)
    t = n_ctx + n_req * req_len
    (y2,) = _moe(a2, route0_t, cnt0, moe_w1, moe_w3, moe_w2, 0, [(0, t)])

    gw = gm_vnorm.shape[1]
    bsb = jnp.repeat(gm_bs[0].T, gw // GM_GROUPS, axis=1)
    h2, a2, route1, route1_t, cnt1 = _gmlp_layer(h1, y2, route0, mod[0], mod[1], row_vec(norm1_g[1]), gm_win[0].astype(BF16),
                                       row_vec(gm_bin[0]), row_vec(gm_vnorm[0]), gm_ws[0].astype(BF16), bsb,
                                       gm_wout[0].astype(BF16), row_vec(norm2_g[1]), routers[1], n_ctx, req_len)
    y2_ctx, y2_lat = _moe(a2, route1_t, cnt1, moe_w1, moe_w3, moe_w2, 1, [(0, n_ctx), (n_ctx, t)])
    y_ctx = _final_combine(h2, y2_ctx, route1, mod[1], 0, n_ctx, req_len)
    y_lat = _final_combine(h2, y2_lat, route1, mod[1], n_ctx, n_ctx, req_len)

    y_prompt = y_ctx.reshape(batch, seq_len, d)
    y_sample = y_lat.reshape(n_req, req_len, d)
    new_cache_k = jnp.transpose(k_new.reshape(batch, 1, N_HEADS, 2, HEAD_DIM, seq_len), (0, 1, 5, 2, 3, 4))
    new_cache_v = v_new.reshape(batch, 1, seq_len, N_HEADS, V_DIM)
    return (y_prompt, y_sample, new_cache_k, new_cache_v)
```

```python
import functools
import math

import jax
import jax.numpy as jnp
from jax import lax
from jax.experimental import pallas as pl
from jax.experimental.pallas import tpu as pltpu
from jax.experimental.pallas import tpu_sc as plsc

F32 = jnp.float32
BF16 = jnp.bfloat16

N_HEADS = 8
HEAD_DIM = 64
V_DIM = 2 * HEAD_DIM
GRID_W = 64
ROPE_THETA = 10000.0
CHUNK = 128
GM_GROUPS = 8
N_EXPERT_GROUPS = 4
EXPERTS_PER_GROUP = 8
N_EXPERTS = N_EXPERT_GROUPS * EXPERTS_PER_GROUP
EPS = 1e-6

LANES = 128
MXU_TILE = 256
ROW_TILE = 256
PRE_ATTN_TILE = 512
POST_ATTN_SUBTILES = 4
GMLP_TILE = 512
MOE_BLOCK = 256
ATTN_Q_TILE = 512
ATTN_HEADS_PER_STEP = 2
SC_WINDOW_BYTES = 128 * 1024
VMEM_LIMIT = 56 * 1024 * 1024

R_E1, R_E2, R_W1, R_W2, R_RANK1, R_RANK2 = 0, 1, 2, 3, 4, 5
R_ROWS = 8
L_EXPERT0 = N_EXPERT_GROUPS


def _lambda_init(layer):
    return 0.8 - 0.6 * math.exp(-0.3 * layer)


def _cparams(n_axes):
    return pltpu.CompilerParams(dimension_semantics=("arbitrary",) * n_axes, vmem_limit_bytes=VMEM_LIMIT)


def _seq_of_tile(i, n_ctx_tiles, tiles_per_req):
    return jnp.where(i < n_ctx_tiles, 0, 1 + (i - n_ctx_tiles) // tiles_per_req)


def _split_specs(tm, d, n_ctx_tiles):
    return [pl.BlockSpec((tm, d), lambda i: (jnp.minimum(i, n_ctx_tiles - 1), 0)),
            pl.BlockSpec((tm, d), lambda i: (jnp.maximum(i - n_ctx_tiles, 0), 0))]


def _mod_row(mod_ref, part, seq):
    return mod_ref[part, pl.ds(seq, 1), :]


def _modulated(x, g, shift, scale):
    y = x * lax.rsqrt(jnp.mean(x * x, axis=-1, keepdims=True) + EPS)
    return (y * g) * (1.0 + scale) + shift


def _split_bf16(x):
    hi = x.astype(BF16)
    lo = (x - hi.astype(F32)).astype(BF16)
    return hi, lo


def _gelu_tanh(x):
    c = math.sqrt(2.0 / math.pi)
    hx = 0.5 * x
    return hx + hx * jnp.tanh(x * (c + (c * 0.044715) * (x * x)))


def _dot(a, b):
    return jnp.dot(a, b, preferred_element_type=F32)


def _pack_bf16_pairs(x):
    half = x.shape[1] // 2
    bits = lax.bitcast_convert_type(x.astype(BF16).astype(F32), jnp.uint32)
    packed = (bits[:, :half] >> 16) | bits[:, half:]
    return lax.bitcast_convert_type(packed, jnp.int32)


def _unpack_pairs_f32(p):
    u = lax.bitcast_convert_type(p, jnp.uint32)
    return lax.bitcast_convert_type(u << 16, F32), lax.bitcast_convert_type(u & jnp.uint32(0xFFFF0000), F32)


def _unpack_bf16_pairs(p):
    lo, hi = _unpack_pairs_f32(p)
    return lo.astype(BF16), hi.astype(BF16)


def _ada_kernel(c_ref, w_ref, b_ref, o_ref):
    c = c_ref[...]
    s = c * jax.nn.sigmoid(c)
    o_ref[...] = _dot(s.astype(BF16), w_ref[...].astype(BF16)) + b_ref[...]


def _ada_mod(cvec, ada_w, ada_b):
    depth, d, d6 = ada_w.shape
    parts = d6 // d
    rows = cvec.shape[0]
    return pl.pallas_call(
        _ada_kernel,
        grid=(depth, parts),
        in_specs=[
            pl.BlockSpec((rows, d), lambda l, j: (0, 0)),
            pl.BlockSpec((None, d, d), lambda l, j: (l, 0, j)),
            pl.BlockSpec((None, 1, d), lambda l, j: (l, 0, j)),
        ],
        out_specs=pl.BlockSpec((None, None, rows, d), lambda l, j: (l, j, 0, 0)),
        out_shape=jax.ShapeDtypeStruct((depth, parts, rows, d), F32),
        compiler_params=_cparams(2),
        name="ada_mod",
    )(cvec, ada_w, ada_b.reshape(depth, 1, d6))


def _pre_attn_kernel(*refs, latent, tiles_per_req):
    if latent:
        h_ref, mod_ref, g_ref, w_ref, qg_ref, kg_ref, gs_ref, cos_ref, sin_ref, q_ref, k_ref, v_ref = refs
        seq = 1 + pl.program_id(0) // tiles_per_req
    else:
        h_ref, mod_ref, g_ref, w_ref, qg_ref, kg_ref, gs_ref, q_ref, k_ref, v_ref, kf_ref, vf_ref = refs
        seq = 0
    d = h_ref.shape[1]
    q_scale = HEAD_DIM ** -0.5 * math.log2(math.e)
    subs = [pl.ds(s * ROW_TILE, ROW_TILE) for s in range(q_ref.shape[0] // ROW_TILE)]

    def group_rms_inv(x):
        sq_hi, sq_lo = _split_bf16(x * x)
        w = gs_ref.shape[0]
        sums = [_dot(sq_hi[:, c:c + w], gs_ref[...]) + _dot(sq_lo[:, c:c + w], gs_ref[...]) for c in range(0, d, w)]
        return lax.rsqrt(jnp.concatenate(sums, axis=1) * (1.0 / HEAD_DIM) + EPS)

    def rope(x, rows):
        cos = cos_ref[rows, :]
        sin = sin_ref[rows, :]
        lane = lax.broadcasted_iota(jnp.int32, cos.shape, 1)
        first = (lane & 31) < 16
        outs = []
        for hh in range(d // LANES):
            xs = x[:, hh * LANES:(hh + 1) * LANES]
            rot = jnp.where(first, pltpu.roll(xs, LANES - 16, 1), pltpu.roll(xs, 16, 1))
            outs.append(xs * cos + rot * sin)
        return jnp.concatenate(outs, axis=1)

    acts = [_modulated(h_ref[rows, :], g_ref[...], _mod_row(mod_ref, 0, seq), _mod_row(mod_ref, 1, seq)).astype(BF16)
            for rows in subs]
    qkvs = [_dot(a, w_ref[...]) for a in acts]
    qns = [(qkv[:, :d] * group_rms_inv(qkv[:, :d])) * qg_ref[...] for qkv in qkvs]
    kns = [(qkv[:, d:2 * d] * group_rms_inv(qkv[:, d:2 * d])) * kg_ref[...] for qkv in qkvs]
    if latent:
        qrs = [rope(qn, rows) for qn, rows in zip(qns, subs)]
        krs = [rope(kn, rows) for kn, rows in zip(kns, subs)]
    else:
        qrs, krs = qns, kns
    for rows, qkv, qr, kr in zip(subs, qkvs, qrs, krs):
        q_ref[rows, :] = (qr * q_scale).astype(BF16)
        k_ref[rows, :] = kr.astype(BF16)
        v_ref[rows, :] = qkv[:, 2 * d:].astype(BF16)
    if not latent:
        kn_all = jnp.concatenate(kns, axis=0)
        n_pos = kf_ref.shape[3]
        for r in range(kf_ref.shape[0]):
            kf_ref[r] = kn_all[r * n_pos:(r + 1) * n_pos, :].T.reshape(kf_ref.shape[1:])
        for rows, qkv in zip(subs, qkvs):
            vf_ref[rows, :] = qkv[:, 2 * d:]


def _rope_tables(n_pos):
    half = HEAD_DIM // 2
    pos = jnp.arange(n_pos, dtype=jnp.int32)
    row = (pos // GRID_W).astype(F32)
    col = (pos % GRID_W).astype(F32)
    inv = 1.0 / (ROPE_THETA ** (jnp.arange(0, half, 2, dtype=F32) / half))
    ang_r = row[:, None] * inv[None, :]
    ang_c = col[:, None] * inv[None, :]
    ang = jnp.concatenate([ang_r, ang_r, ang_c, ang_c], axis=-1)
    quarter = half // 2
    sign = jnp.tile(jnp.concatenate([-jnp.ones((quarter,), F32), jnp.ones((quarter,), F32)]), 2)
    cos = jnp.tile(jnp.cos(ang), (1, 2))
    sin = jnp.tile(jnp.sin(ang) * sign[None, :], (1, 2))
    return cos, sin


def _pre_attn(h, mod, g, wqkv, qg, kg, *, ctx_len=None, rows_per_req=None):
    n, d = h.shape
    latent = rows_per_req is not None
    tm = PRE_ATTN_TILE
    gid = jnp.arange(MXU_TILE, dtype=jnp.int32) // HEAD_DIM
    gs = (gid[:, None] == gid[None, :]).astype(BF16)
    full = lambda shape: pl.BlockSpec(shape, lambda i: (0,) * len(shape), pipeline_mode=pl.Buffered(1))
    row = pl.BlockSpec((tm, d), lambda i: (i, 0))
    args = [h, mod, g, wqkv, qg, kg, gs]
    in_specs = [row] + [full(a.shape) for a in args[1:]]
    out_specs = [row, row, row]
    out_shape = [jax.ShapeDtypeStruct((n, d), BF16)] * 3
    if latent:
        tiles_per_req = rows_per_req // tm
        pos = pl.BlockSpec((tm, LANES), lambda i: (i % tiles_per_req, 0))
        args += list(_rope_tables(rows_per_req))
        in_specs += [pos, pos]
    else:
        assert tm % ctx_len == 0
        tiles_per_req = None
        out_specs += [pl.BlockSpec((tm // ctx_len, d // HEAD_DIM, HEAD_DIM, ctx_len), lambda i: (i, 0, 0, 0)), row]
        out_shape += [jax.ShapeDtypeStruct((n // ctx_len, d // HEAD_DIM, HEAD_DIM, ctx_len), F32),
                      jax.ShapeDtypeStruct((n, d), F32)]
    return pl.pallas_call(
        functools.partial(_pre_attn_kernel, latent=latent, tiles_per_req=tiles_per_req),
        grid=(n // tm,),
        in_specs=in_specs,
        out_specs=out_specs,
        out_shape=out_shape,
        compiler_params=_cparams(1),
        name="pre_attn_latent" if latent else "pre_attn_ctx",
    )(*args)


def _attn_kernel(*refs, n_heads, has_cache, layer):
    if has_cache:
        lam_ref, sub_ref, q_ref, kn_ref, vn_ref, kc_ref, vc_ref, o_ref = refs
    else:
        lam_ref, sub_ref, q_ref, kn_ref, vn_ref, o_ref = refs
    lf = lam_ref[...]
    lam = (jnp.exp(jnp.sum(lf[0:1] * lf[1:2], axis=-1, keepdims=True))
           - jnp.exp(jnp.sum(lf[2:3] * lf[3:4], axis=-1, keepdims=True)) + _lambda_init(layer))
    nt = (((1,), (1,)), ((), ()))
    tq = q_ref.shape[0]
    lane = lax.broadcasted_iota(jnp.int32, (tq, LANES), 1)
    heads = [slice(hh * LANES, (hh + 1) * LANES) for hh in range(n_heads)]

    def qk(hh, sl):
        qh = q_ref[:, sl]
        zero = jnp.zeros_like(qh)
        kn = kn_ref[:, sl]
        kc_t = kc_ref[hh].astype(BF16) if has_cache else None
        out = []
        for qm in (jnp.where(lane < HEAD_DIM, qh, zero), jnp.where(lane >= HEAD_DIM, qh, zero)):
            s = [lax.dot_general(qm, kn, nt, preferred_element_type=F32)]
            out.append(s + [_dot(qm, kc_t)] if has_cache else s)
        return out

    def soft(scores):
        maxes = [functools.reduce(jnp.maximum, [jnp.max(x, axis=-1, keepdims=True) for x in s]) for s in scores]
        exps = [[jnp.exp2(x - m) for x in s] for s, m in zip(scores, maxes)]
        r0, r1 = [1.0 / functools.reduce(lambda u, w: u + w, [jnp.sum(x, axis=-1, keepdims=True) for x in e])
                  for e in exps]
        return [(x0 * r0 - lam * (x1 * r1)).astype(BF16) for x0, x1 in zip(*exps)]

    def pv(sl, a):
        vals = [vn_ref[:, sl]] + ([vc_ref[:, sl].astype(BF16)] if has_cache else [])
        o = functools.reduce(lambda u, w: u + w, [_dot(aj, vv) for aj, vv in zip(a, vals)])
        o = o * lax.rsqrt(jnp.mean(o * o, axis=-1, keepdims=True) + EPS)
        o_ref[:, sl] = ((o * sub_ref[...]) * (1.0 - _lambda_init(layer))).astype(BF16)

    scores = [qk(hh, sl) for hh, sl in enumerate(heads)]
    mixed = [soft(sc) for sc in scores]
    for sl, a in zip(heads, mixed):
        pv(sl, a)


def _attention(qkv_ctx, qkv_lat, lam_p, subln, cache_k, cache_v, ctx_len, n_req, req_len, layer):
    n_ctx, d = qkv_ctx[0].shape
    small = lambda shape: pl.BlockSpec(shape, lambda *_: (0,) * len(shape))
    n_ctx_req = n_ctx // ctx_len
    blk = pl.BlockSpec((ctx_len, d), lambda b: (b, 0))
    o_ctx = pl.pallas_call(
        functools.partial(_attn_kernel, n_heads=N_HEADS, has_cache=False, layer=layer),
        grid=(n_ctx_req,),
        in_specs=[small(lam_p.shape), small(subln.shape), blk, blk, blk],
        out_specs=blk,
        out_shape=jax.ShapeDtypeStruct((n_ctx, d), BF16),
        compiler_params=_cparams(1),
        name="attn_ctx",
    )(lam_p, subln, *qkv_ctx)

    tq = ATTN_Q_TILE
    nq = req_len // tq
    past = cache_v.shape[1]
    hp = ATTN_HEADS_PER_STEP
    small3 = lambda shape: pl.BlockSpec(shape, lambda b, h, i: (0,) * len(shape))
    qspec = pl.BlockSpec((tq, hp * LANES), lambda b, h, i: (b * nq + i, h))
    kvspec = pl.BlockSpec((req_len, hp * LANES), lambda b, h, i: (b, h))
    cspec = pl.BlockSpec((None, past, hp * LANES), lambda b, h, i: (b, 0, h))
    ckspec = pl.BlockSpec((None, hp, LANES, past), lambda b, h, i: (b, h, 0, 0))
    o_lat = pl.pallas_call(
        functools.partial(_attn_kernel, n_heads=hp, has_cache=True, layer=layer),
        grid=(n_req, N_HEADS // hp, nq),
        in_specs=[small3(lam_p.shape), pl.BlockSpec((1, LANES), lambda b, h, i: (0, 0)),
                  qspec, kvspec, kvspec, ckspec, cspec],
        out_specs=pl.BlockSpec((tq, hp * LANES), lambda b, h, i: (b * nq + i, h)),
        out_shape=jax.ShapeDtypeStruct((n_req * req_len, d), BF16),
        compiler_params=_cparams(3),
        name="attn_latent",
    )(lam_p, subln, *qkv_lat, cache_k, cache_v)
    return o_ctx, o_lat


def _route(a2, wr, br, ltri, carry):
    tm = a2.shape[0]
    logits = _dot(a2.astype(BF16), wr) + br
    lane_i = lax.broadcasted_iota(jnp.int32, (tm, LANES), 1)
    lane = lane_i.astype(F32)
    neg = jnp.full((tm, LANES), -jnp.inf, F32)
    big = jnp.full((tm, LANES), float(LANES), F32)
    first_lane = lambda mask: jnp.min(jnp.where(mask, lane, big), axis=-1, keepdims=True)

    lc = jnp.where(lane_i < N_EXPERT_GROUPS, logits, neg)
    mc = jnp.max(lc, axis=-1, keepdims=True)
    pg = 1.0 / jnp.sum(jnp.exp(lc - mc), axis=-1, keepdims=True)
    gi = first_lane(lc == mc)
    assert EXPERTS_PER_GROUP == 8
    grp = lax.shift_right_arithmetic(lane_i - L_EXPERT0, 3).astype(F32)
    in_group = (lane_i >= L_EXPERT0) & (lane_i < L_EXPERT0 + N_EXPERTS) & (grp == gi)
    ls = jnp.where(in_group, logits, neg)
    t1 = jnp.max(ls, axis=-1, keepdims=True)
    i1 = first_lane(ls == t1)
    ls2 = jnp.where(lane == i1, neg, ls)
    t2 = jnp.max(ls2, axis=-1, keepdims=True)
    i2 = first_lane(ls2 == t2)
    ex = jnp.exp(t2 - t1)
    w1 = pg * (1.0 / (1.0 + ex))
    w2 = pg * (ex / (1.0 + ex))
    e1 = i1 - float(L_EXPERT0)
    e2 = i2 - float(L_EXPERT0)
    oh1 = lane == e1
    oh2 = lane == e2
    onehot = oh1.astype(F32) + oh2.astype(F32)
    before = _dot(ltri, onehot.astype(BF16)) + carry
    zero = jnp.zeros_like(before)
    rank1 = jnp.sum(jnp.where(oh1, before, zero), axis=-1, keepdims=True)
    rank2 = jnp.sum(jnp.where(oh2, before, zero), axis=-1, keepdims=True)
    slab = jnp.zeros((tm, LANES), F32)
    for ln, val in ((R_E1, e1), (R_E2, e2), (R_W1, w1), (R_W2, w2), (R_RANK1, rank1), (R_RANK2, rank2)):
        slab = jnp.where(lane_i == ln, val, slab)
    return slab, carry + jnp.sum(onehot, axis=0, keepdims=True)


def _router_weights(wc, bc, wf, bf_):
    d = wc.shape[0]
    pad = LANES - N_EXPERT_GROUPS - N_EXPERTS
    w = jnp.concatenate([wc, wf, jnp.zeros((d, pad), F32)], axis=1)
    b = jnp.concatenate([bc, bf_, jnp.zeros((pad,), F32)])[None, :]
    return w.astype(BF16), b


def _lower_tri(n):
    r = jnp.arange(n, dtype=jnp.int32)
    return (r[None, :] < r[:, None]).astype(BF16)


def _post_attn_kernel(hc_ref, hl_ref, oc_ref, ol_ref, mod_ref, wo_ref, g2_ref, wr_ref, br_ref, ltri_ref,
                      h1_ref, a2_ref, route_ref, rtt_ref, cnt_ref, carry_ref, *, n_ctx_tiles, tiles_per_req):
    i = pl.program_id(0)
    seq = _seq_of_tile(i, n_ctx_tiles, tiles_per_req)

    @pl.when(i == 0)
    def _():
        carry_ref[...] = jnp.zeros_like(carry_ref)

    subs = [pl.ds(s * ROW_TILE, ROW_TILE) for s in range(h1_ref.shape[0] // ROW_TILE)]
    mixes = [_dot(jnp.where(i < n_ctx_tiles, oc_ref[rows, :], ol_ref[rows, :]), wo_ref[...]) for rows in subs]
    h1s = [jnp.where(i < n_ctx_tiles, hc_ref[rows, :], hl_ref[rows, :]) + _mod_row(mod_ref, 2, seq) * mix
           for rows, mix in zip(subs, mixes)]
    a2s = [_modulated(h1, g2_ref[...], _mod_row(mod_ref, 3, seq), _mod_row(mod_ref, 4, seq)) for h1 in h1s]
    for rows, h1, a2 in zip(subs, h1s, a2s):
        h1_ref[rows, :] = h1
        a2_ref[rows, :] = _pack_bf16_pairs(a2)
    carry = carry_ref[...]
    for rows, a2 in zip(subs, a2s):
        slab, carry = _route(a2, wr_ref[...], br_ref[...], ltri_ref[...], carry)
        route_ref[rows, :] = slab
        rtt_ref[:, rows] = slab.T[:R_ROWS, :]
    carry_ref[...] = carry
    cnt_ref[...] = carry


def _post_attn(h_ctx, h_lat, o_ctx, o_lat, mod, wo, g2, router, rows_per_req):
    n_ctx, d = h_ctx.shape
    t = n_ctx + h_lat.shape[0]
    tm = ROW_TILE * POST_ATTN_SUBTILES
    wr, br = router
    ltri = _lower_tri(ROW_TILE)
    full = lambda shape: pl.BlockSpec(shape, lambda i: (0,) * len(shape))
    row = pl.BlockSpec((tm, d), lambda i: (i, 0))
    kern = functools.partial(_post_attn_kernel, n_ctx_tiles=n_ctx // tm, tiles_per_req=rows_per_req // tm)
    return pl.pallas_call(
        kern,
        grid=(t // tm,),
        in_specs=_split_specs(tm, d, n_ctx // tm) + _split_specs(tm, d, n_ctx // tm) + [
                  full(mod.shape), full(wo.shape), full((1, d)), full(wr.shape),
                  full(br.shape), full(ltri.shape)],
        out_specs=[row, pl.BlockSpec((tm, d // 2), lambda i: (i, 0)), pl.BlockSpec((tm, LANES), lambda i: (i, 0)),
                   pl.BlockSpec((R_ROWS, tm), lambda i: (0, i)), full((1, LANES))],
        out_shape=[jax.ShapeDtypeStruct((t, d), F32), jax.ShapeDtypeStruct((t, d // 2), jnp.int32),
                   jax.ShapeDtypeStruct((t, LANES), F32), jax.ShapeDtypeStruct((R_ROWS, t), F32),
                   jax.ShapeDtypeStruct((1, LANES), F32)],
        scratch_shapes=[pltpu.VMEM((1, LANES), F32)],
        compiler_params=_cparams(1),
        name="post_attn_router",
    )(h_ctx, h_lat, o_ctx, o_lat, mod, wo, g2, wr, br, ltri)


def _gather_rows(src, idx):
    n = idx.shape[0]
    d = src.shape[1]
    w = _sc_window_rows(src)
    mesh = plsc.VectorSubcoreMesh(core_axis_name="core", subcore_axis_name="subcore")

    @pl.kernel(out_type=jax.ShapeDtypeStruct((n, d), src.dtype), mesh=mesh)
    def gather(src_hbm, idx_hbm, out_hbm):
        def body(idx_vmem, out_vmem):
            pltpu.sync_copy(src_hbm.at[idx_vmem.at[0, pl.ds(0, w)]], out_vmem)

        pltpu.emit_pipeline(
            body,
            grid=(n // w,),
            in_specs=[pl.BlockSpec((1, LANES), lambda i: (i, 0))],
            out_specs=[pl.BlockSpec((w, d), lambda i: (i, 0))],
            core_axis_name=("core", "subcore"),
            dimension_semantics=(pltpu.PARALLEL,),
        )(idx_hbm, out_hbm)

    return gather(src, _index_windows(idx, w))


def _sc_window_rows(src):
    return min(LANES, SC_WINDOW_BYTES // (src.shape[1] * src.dtype.itemsize))


def _index_windows(idx, w):
    return jnp.pad(idx.reshape(idx.shape[0] // w, w), ((0, 0), (0, LANES - w)))


def _scatter_rows_twice(src, idx_a, idx_b, n_out):
    t, d = src.shape
    w = _sc_window_rows(src)
    mesh = plsc.VectorSubcoreMesh(core_axis_name="core", subcore_axis_name="subcore")

    @pl.kernel(out_type=jax.ShapeDtypeStruct((n_out, d), src.dtype), mesh=mesh)
    def scatter(src_hbm, ia_hbm, ib_hbm, out_hbm):
        def body(src_vmem, ia_vmem, ib_vmem):
            pltpu.sync_copy(src_vmem, out_hbm.at[ia_vmem.at[0, pl.ds(0, w)]])
            pltpu.sync_copy(src_vmem, out_hbm.at[ib_vmem.at[0, pl.ds(0, w)]])

        pltpu.emit_pipeline(
            body,
            grid=(t // w,),
            in_specs=[pl.BlockSpec((w, d), lambda i: (i, 0)),
                      pl.BlockSpec((1, LANES), lambda i: (i, 0)),
                      pl.BlockSpec((1, LANES), lambda i: (i, 0))],
            out_specs=[],
            core_axis_name=("core", "subcore"),
            dimension_semantics=(pltpu.PARALLEL,),
        )(src_hbm, ia_hbm, ib_hbm)

    return scatter(src, _index_windows(idx_a, w), _index_windows(idx_b, w))


def _expert_kernel(be_ref, nb_ref, ne_ref, x_ref, w1_hbm, w3_hbm, w2_hbm, y_ref,
                   w1s, w3s, w2s, w1b, w3b, w2b, sems, *, layer):
    b = pl.program_id(0)
    prev = be_ref[jnp.maximum(b - 1, 0)]
    used = b < nb_ref[0]

    def stage(e):
        return [pltpu.make_async_copy(src.at[layer, e], dst, sems.at[j])
                for j, (src, dst) in enumerate(((w1_hbm, w1s), (w3_hbm, w3s), (w2_hbm, w2s)))]

    @pl.when(used & ((b == 0) | (be_ref[b] != prev)))
    def _():
        @pl.when(b == 0)
        def _():
            for c in stage(be_ref[0]):
                c.start()

        for c in stage(be_ref[b]):
            c.wait()
        w1b[...] = w1s[...].astype(BF16)
        w3b[...] = w3s[...].astype(BF16)
        w2b[...] = w2s[...].astype(BF16)

        @pl.when(ne_ref[b] >= 0)
        def _():
            for c in stage(ne_ref[b]):
                c.start()

    @pl.when(used)
    def _():
        x_lo, x_hi = _unpack_bf16_pairs(x_ref[...])
        half = x_lo.shape[1]
        h1 = _dot(x_lo, w1b[:half, :]) + _dot(x_hi, w1b[half:, :])
        h3 = _dot(x_lo, w3b[:half, :]) + _dot(x_hi, w3b[half:, :])
        hb = jax.nn.silu(h1) * h3
        y_ref[...] = _dot(hb.astype(BF16), w2b[...])


def _expert_mlps(xs, block_e, n_used, next_e, w1, w3, w2, layer):
    npad, half = xs.shape
    d, ff = w1.shape[2], w1.shape[3]
    assert d == 2 * half
    bm = MOE_BLOCK
    rows = lambda b, be, nb, ne: (jnp.minimum(b, nb[0] - 1), 0)
    hbm = pl.BlockSpec(memory_space=pl.ANY)
    grid_spec = pltpu.PrefetchScalarGridSpec(
        num_scalar_prefetch=3,
        grid=(npad // bm,),
        in_specs=[pl.BlockSpec((bm, half), rows), hbm, hbm, hbm],
        out_specs=pl.BlockSpec((bm, d), rows),
        scratch_shapes=[pltpu.VMEM((d, ff), F32), pltpu.VMEM((d, ff), F32), pltpu.VMEM((ff, d), F32),
                        pltpu.VMEM((d, ff), BF16), pltpu.VMEM((d, ff), BF16), pltpu.VMEM((ff, d), BF16),
                        pltpu.SemaphoreType.DMA((3,))],
    )
    return pl.pallas_call(
        functools.partial(_expert_kernel, layer=layer),
        grid_spec=grid_spec,
        out_shape=jax.ShapeDtypeStruct((npad, d), F32),
        compiler_params=_cparams(1),
        name="expert_mlps",
    )(block_e, n_used, next_e, xs, w1, w3, w2)


def _moe(a2p, route_t, counts, w1, w3, w2, layer, row_ranges):
    t = a2p.shape[0]
    bm = MOE_BLOCK
    n = 2 * t
    ints = route_t.astype(jnp.int32)
    cnt = counts[0, :N_EXPERTS].astype(jnp.int32)
    padded = ((cnt + bm - 1) // bm) * bm
    pend = jnp.cumsum(padded)
    pstart = pend - padded
    experts = jnp.arange(N_EXPERTS, dtype=jnp.int32)[None, :]
    slot = lambda e, rank: jnp.sum(jnp.where(e[:, None] == experts, pstart[None, :], 0), axis=1) + rank
    pos_first = slot(ints[R_E1], ints[R_RANK1])
    pos_second = slot(ints[R_E2], ints[R_RANK2])
    n_blocks = n // bm + N_EXPERTS
    npad = n_blocks * bm
    starts = jnp.arange(n_blocks, dtype=jnp.int32) * bm
    n_used = pend[-1:] // bm
    last_start = (n_used[0] - 1) * bm
    block_e = jnp.sum((pend[None, :] <= jnp.minimum(starts, last_start)[:, None]).astype(jnp.int32), axis=1)
    blk = jnp.arange(n_blocks, dtype=jnp.int32)
    later_other = (blk[None, :] > blk[:, None]) & (block_e[None, :] != block_e[:, None]) & (blk[None, :] < n_used[0])
    first_later = jnp.min(jnp.where(later_other, blk[None, :], n_blocks), axis=1)
    next_e = jnp.where(first_later < n_blocks, block_e[jnp.minimum(first_later, n_blocks - 1)], -1).astype(jnp.int32)
    xs = _scatter_rows_twice(a2p, pos_first, pos_second, npad)
    ys = _expert_mlps(xs, block_e, n_used, next_e, w1, w3, w2, layer)
    return [_gather_rows(ys, jnp.concatenate([pos_first[r0:r1], pos_second[r0:r1]])) for r0, r1 in row_ranges]


def _moe_combine(h, y_first, y_second, route, g2):
    w1 = route[:, R_W1:R_W1 + 1]
    w2 = route[:, R_W2:R_W2 + 1]
    return h + g2 * (w1 * y_first + w2 * y_second)


def _gmlp_kernel(h_ref, ya_ref, yb_ref, rt_ref, modp_ref, mod_ref, g1_ref, win_ref, bin_ref, vg_ref, ws_ref, bsb_ref,
                 wout_ref, g2_ref, wr_ref, br_ref, ltri_ref,
                 h1_ref, a2_ref, route_ref, rtt_ref, cnt_ref, carry_ref, *, n_ctx_tiles, tiles_per_req):
    i = pl.program_id(0)
    seq = _seq_of_tile(i, n_ctx_tiles, tiles_per_req)

    @pl.when(i == 0)
    def _():
        carry_ref[...] = jnp.zeros_like(carry_ref)

    gw = vg_ref.shape[1]
    cg = gw // GM_GROUPS
    subs = [pl.ds(s * GMLP_TILE, GMLP_TILE) for s in range(h_ref.shape[0] // GMLP_TILE)]
    hs = [_moe_combine(h_ref[rs, :], ya_ref[rs, :], yb_ref[rs, :], rt_ref[rs, :], _mod_row(modp_ref, 5, seq))
          for rs in subs]
    acts = [_modulated(h, g1_ref[...], _mod_row(mod_ref, 0, seq), _mod_row(mod_ref, 1, seq)).astype(BF16) for h in hs]
    vs = [_gelu_tanh(_dot(a, win_ref[:, gw:]) + bin_ref[:, gw:]) for a in acts]
    vbs = [((v * lax.rsqrt(jnp.mean(v * v, axis=-1, keepdims=True) + EPS)) * vg_ref[...]).astype(BF16) for v in vs]
    us = [_gelu_tanh(_dot(a, win_ref[:, :gw]) + bin_ref[:, :gw]) for a in acts]
    vms = []
    for vb in vbs:
        rows = []
        for c in range(GMLP_TILE // CHUNK):
            cols = [_dot(ws_ref[g], vb[c * CHUNK:(c + 1) * CHUNK, g * cg:(g + 1) * cg]) for g in range(GM_GROUPS)]
            rows.append(jnp.concatenate(cols, axis=1) + bsb_ref[...])
        vms.append(jnp.concatenate(rows, axis=0))
    mixes = [_dot((u * vm).astype(BF16), wout_ref[...]) for u, vm in zip(us, vms)]
    h1s = [h + _mod_row(mod_ref, 2, seq) * mix for h, mix in zip(hs, mixes)]
    a2s = [_modulated(h1, g2_ref[...], _mod_row(mod_ref, 3, seq), _mod_row(mod_ref, 4, seq)) for h1 in h1s]
    carry = carry_ref[...]
    for rs, h1, a2 in zip(subs, h1s, a2s):
        h1_ref[rs, :] = h1
        a2_ref[rs, :] = _pack_bf16_pairs(a2)
        slab, carry = _route(a2, wr_ref[...], br_ref[...], ltri_ref[...], carry)
        route_ref[rs, :] = slab
        rtt_ref[:, rs] = slab.T[:R_ROWS, :]
    carry_ref[...] = carry
    cnt_ref[...] = carry


def _gmlp_layer(h, y2, route_prev, mod_prev, mod, g1, win, bin_, vg, ws, bsb, wout, g2, router,
                n_ctx, rows_per_req):
    t, d = h.shape
    tm = GMLP_TILE
    wr, br = router
    ltri = _lower_tri(GMLP_TILE)
    full = lambda shape: pl.BlockSpec(shape, lambda i: (0,) * len(shape), pipeline_mode=pl.Buffered(1))
    row = lambda w: pl.BlockSpec((tm, w), lambda i: (i, 0))
    kern = functools.partial(_gmlp_kernel, n_ctx_tiles=n_ctx // tm, tiles_per_req=rows_per_req // tm)
    args = (h, y2, y2, route_prev, mod_prev, mod, g1, win, bin_, vg, ws, bsb, wout, g2, wr, br, ltri)
    second = pl.BlockSpec((tm, d), lambda i: (i + t // tm, 0))
    in_specs = [row(d), row(d), second, row(LANES)] + [full(a.shape) for a in args[4:]]
    return pl.pallas_call(
        kern,
        grid=(t // tm,),
        in_specs=in_specs,
        out_specs=[row(d), row(d // 2), row(LANES), pl.BlockSpec((R_ROWS, tm), lambda i: (0, i)),
                   pl.BlockSpec((1, LANES), lambda i: (0, 0))],
        out_shape=[jax.ShapeDtypeStruct((t, d), F32), jax.ShapeDtypeStruct((t, d // 2), jnp.int32),
                   jax.ShapeDtypeStruct((t, LANES), F32), jax.ShapeDtypeStruct((R_ROWS, t), F32),
                   jax.ShapeDtypeStruct((1, LANES), F32)],
        scratch_shapes=[pltpu.VMEM((1, LANES), F32)],
        compiler_params=_cparams(1),
        name="gmlp_router",
    )(*args)


def _final_kernel(h_ref, ya_ref, yb_ref, rt_ref, mod_ref, o_ref, *, first_tile, n_ctx_tiles, tiles_per_req):
    seq = _seq_of_tile(first_tile + pl.program_id(0), n_ctx_tiles, tiles_per_req)
    o_ref[...] = _moe_combine(h_ref[...], ya_ref[...], yb_ref[...], rt_ref[...], _mod_row(mod_ref, 5, seq))


def _final_combine(h, y2, route, mod, first_row, n_ctx, rows_per_req):
    d = h.shape[1]
    n = y2.shape[0] // 2
    tm = ROW_TILE
    first_tile = first_row // tm
    row = lambda w: pl.BlockSpec((tm, w), lambda i: (first_tile + i, 0))
    kern = functools.partial(_final_kernel, first_tile=first_tile, n_ctx_tiles=n_ctx // tm,
                             tiles_per_req=rows_per_req // tm)
    return pl.pallas_call(
        kern,
        grid=(n // tm,),
        in_specs=[row(d), pl.BlockSpec((tm, d), lambda i: (i, 0)),
                  pl.BlockSpec((tm, d), lambda i: (i + n // tm, 0)),
                  row(LANES), pl.BlockSpec(mod.shape, lambda i: (0, 0, 0))],
        out_specs=pl.BlockSpec((tm, d), lambda i: (i, 0)),
        out_shape=jax.ShapeDtypeStruct((n, d), F32),
        compiler_params=_cparams(1),
        name="final_combine",
    )(h, y2, y2, route, mod)


def kernel(x_prompt, x_sample, cache_k, cache_v, c, c_ctx, ada_w, ada_b, norm1_g, norm2_g, attn_wq, attn_wk,
           attn_wv, attn_wo, attn_qnorm, attn_knorm, attn_lam, attn_subln, gm_win, gm_bin, gm_vnorm, gm_ws,
           gm_bs, gm_wout, moe_wc, moe_bc, moe_wf, moe_bf, moe_w1, moe_w3, moe_w2):
    batch, seq_len, d = x_prompt.shape
    n_req, req_len, _ = x_sample.shape
    depth = ada_w.shape[0]
    assert depth == 2 and attn_wq.shape[0] == 1 and gm_win.shape[0] == 1
    n_ctx = batch * seq_len
    tiles = (PRE_ATTN_TILE, ROW_TILE * POST_ATTN_SUBTILES, GMLP_TILE, ATTN_Q_TILE)
    assert all(n_ctx % tm == 0 and req_len % tm == 0 for tm in tiles) and GMLP_TILE % CHUNK == 0
    past = cache_k.shape[2]

    h_ctx = x_prompt.reshape(n_ctx, d)
    h_lat = x_sample.reshape(n_req * req_len, d)
    cvec = jnp.concatenate([c_ctx[None, :], c, jnp.zeros((8 - 1 - n_req, d), F32)], axis=0)
    mod = _ada_mod(cvec, ada_w, ada_b)
    routers = [_router_weights(moe_wc[i], moe_bc[i], moe_wf[i], moe_bf[i]) for i in range(depth)]
    row_vec = lambda v: v.reshape(1, -1)

    wqkv = jnp.concatenate([attn_wq[0], attn_wk[0], attn_wv[0]], axis=1).astype(BF16)
    reps = d // HEAD_DIM
    cache_k_t = jnp.transpose(cache_k[:, 0], (0, 2, 3, 4, 1)).reshape(n_req, N_HEADS, 2 * HEAD_DIM, past)
    qk_gains = (row_vec(jnp.tile(attn_qnorm[0], reps)), row_vec(jnp.tile(attn_knorm[0], reps)))
    *qkv_ctx, k_new, v_new = _pre_attn(h_ctx, mod[0], row_vec(norm1_g[0]), wqkv, *qk_gains, ctx_len=seq_len)
    qkv_lat = _pre_attn(h_lat, mod[0], row_vec(norm1_g[0]), wqkv, *qk_gains, rows_per_req=req_len)
    o_ctx, o_lat = _attention(qkv_ctx, qkv_lat, attn_lam[0], row_vec(attn_subln[0]),
                              cache_k_t, cache_v[:, 0].reshape(n_req, past, d),
                              seq_len, n_req, req_len, layer=0)
    h1, a2, route0, route0_t, cnt0 = _post_attn(h_ctx, h_lat, o_ctx, o_lat, mod[0], attn_wo[0].astype(BF16),
                                      row_vec(norm2_g[0]), routers[0], req_len)
    t = n_ctx + n_req * req_len
    (y2,) = _moe(a2, route0_t, cnt0, moe_w1, moe_w3, moe_w2, 0, [(0, t)])

    gw = gm_vnorm.shape[1]
    bsb = jnp.repeat(gm_bs[0].T, gw // GM_GROUPS, axis=1)
    h2, a2, route1, route1_t, cnt1 = _gmlp_layer(h1, y2, route0, mod[0], mod[1], row_vec(norm1_g[1]), gm_win[0].astype(BF16),
                                       row_vec(gm_bin[0]), row_vec(gm_vnorm[0]), gm_ws[0].astype(BF16), bsb,
                                       gm_wout[0].astype(BF16), row_vec(norm2_g[1]), routers[1], n_ctx, req_len)
    y2_ctx, y2_lat = _moe(a2, route1_t, cnt1, moe_w1, moe_w3, moe_w2, 1, [(0, n_ctx), (n_ctx, t)])
    y_ctx = _final_combine(h2, y2_ctx, route1, mod[1], 0, n_ctx, req_len)
    y_lat = _final_combine(h2, y2_lat, route1, mod[1], n_ctx, n_ctx, req_len)

    y_prompt = y_ctx.reshape(batch, seq_len, d)
    y_sample = y_lat.reshape(n_req, req_len, d)
    new_cache_k = jnp.transpose(k_new.reshape(batch, 1, N_HEADS, 2, HEAD_DIM, seq_len), (0, 1, 5, 2, 3, 4))
    new_cache_v = v_new.reshape(batch, 1, seq_len, N_HEADS, V_DIM)
    return (y_prompt, y_sample, new_cache_k, new_cache_v)
```

```python
import functools
import math

import jax
import jax.numpy as jnp
from jax import lax
from jax.experimental import pallas as pl
from jax.experimental.pallas import tpu as pltpu
from jax.experimental.pallas import tpu_sc as plsc

F32 = jnp.float32
BF16 = jnp.bfloat16

N_HEADS = 8
HEAD_DIM = 64
V_DIM = 2 * HEAD_DIM
GRID_W = 64
ROPE_THETA = 10000.0
CHUNK = 128
GM_GROUPS = 8
N_EXPERT_GROUPS = 4
EXPERTS_PER_GROUP = 8
N_EXPERTS = N_EXPERT_GROUPS * EXPERTS_PER_GROUP
EPS = 1e-6

LANES = 128
MXU_TILE = 256
ROW_TILE = 256
PRE_ATTN_TILE = 512
POST_ATTN_SUBTILES = 4
GMLP_TILE = 512
FINAL_TILE = 512
MOE_BLOCK = 256
ATTN_Q_TILE = 512
ATTN_HEADS_PER_STEP = 2
SC_WINDOW_BYTES = 128 * 1024
VMEM_LIMIT = 56 * 1024 * 1024

R_E1, R_E2, R_W1, R_W2, R_RANK1, R_RANK2 = 0, 1, 2, 3, 4, 5
R_ROWS = 8
L_EXPERT0 = N_EXPERT_GROUPS


def _lambda_init(layer):
    return 0.8 - 0.6 * math.exp(-0.3 * layer)


def _cparams(n_axes):
    return pltpu.CompilerParams(dimension_semantics=("arbitrary",) * n_axes, vmem_limit_bytes=VMEM_LIMIT)


def _seq_of_tile(i, n_ctx_tiles, tiles_per_req):
    return jnp.where(i < n_ctx_tiles, 0, 1 + (i - n_ctx_tiles) // tiles_per_req)


def _split_specs(tm, d, n_ctx_tiles):
    return [pl.BlockSpec((tm, d), lambda i: (jnp.minimum(i, n_ctx_tiles - 1), 0)),
            pl.BlockSpec((tm, d), lambda i: (jnp.maximum(i - n_ctx_tiles, 0), 0))]


def _mod_row(mod_ref, part, seq):
    return mod_ref[part, pl.ds(seq, 1), :]


def _modulated(x, g, shift, scale):
    y = x * lax.rsqrt(jnp.mean(x * x, axis=-1, keepdims=True) + EPS)
    return (y * g) * (1.0 + scale) + shift


def _split_bf16(x):
    hi = x.astype(BF16)
    lo = (x - hi.astype(F32)).astype(BF16)
    return hi, lo


def _gelu_tanh(x):
    c = math.sqrt(2.0 / math.pi)
    hx = 0.5 * x
    return hx + hx * jnp.tanh(x * (c + (c * 0.044715) * (x * x)))


def _dot(a, b):
    return jnp.dot(a, b, preferred_element_type=F32)


def _pack_bf16_pairs(x):
    half = x.shape[1] // 2
    bits = lax.bitcast_convert_type(x.astype(BF16).astype(F32), jnp.uint32)
    packed = (bits[:, :half] >> 16) | bits[:, half:]
    return lax.bitcast_convert_type(packed, jnp.int32)


def _unpack_pairs_f32(p):
    u = lax.bitcast_convert_type(p, jnp.uint32)
    return lax.bitcast_convert_type(u << 16, F32), lax.bitcast_convert_type(u & jnp.uint32(0xFFFF0000), F32)


def _unpack_bf16_pairs(p):
    lo, hi = _unpack_pairs_f32(p)
    return lo.astype(BF16), hi.astype(BF16)


def _ada_kernel(c_ref, w_ref, b_ref, o_ref):
    c = c_ref[...]
    s = c * jax.nn.sigmoid(c)
    o_ref[...] = _dot(s.astype(BF16), w_ref[...].astype(BF16)) + b_ref[...]


def _ada_mod(cvec, ada_w, ada_b):
    depth, d, d6 = ada_w.shape
    parts = d6 // d
    rows = cvec.shape[0]
    return pl.pallas_call(
        _ada_kernel,
        grid=(depth, parts),
        in_specs=[
            pl.BlockSpec((rows, d), lambda l, j: (0, 0)),
            pl.BlockSpec((None, d, d), lambda l, j: (l, 0, j)),
            pl.BlockSpec((None, 1, d), lambda l, j: (l, 0, j)),
        ],
        out_specs=pl.BlockSpec((None, None, rows, d), lambda l, j: (l, j, 0, 0)),
        out_shape=jax.ShapeDtypeStruct((depth, parts, rows, d), F32),
        compiler_params=_cparams(2),
        name="ada_mod",
    )(cvec, ada_w, ada_b.reshape(depth, 1, d6))


def _pre_attn_kernel(*refs, latent, tiles_per_req):
    if latent:
        h_ref, mod_ref, g_ref, w_ref, qg_ref, kg_ref, gs_ref, cos_ref, sin_ref, q_ref, k_ref, v_ref = refs
        seq = 1 + pl.program_id(0) // tiles_per_req
    else:
        h_ref, mod_ref, g_ref, w_ref, qg_ref, kg_ref, gs_ref, q_ref, k_ref, v_ref, kf_ref, vf_ref = refs
        seq = 0
    d = h_ref.shape[1]
    q_scale = HEAD_DIM ** -0.5 * math.log2(math.e)
    subs = [pl.ds(s * ROW_TILE, ROW_TILE) for s in range(q_ref.shape[0] // ROW_TILE)]

    def group_rms_inv(x):
        sq_hi, sq_lo = _split_bf16(x * x)
        w = gs_ref.shape[0]
        sums = [_dot(sq_hi[:, c:c + w], gs_ref[...]) + _dot(sq_lo[:, c:c + w], gs_ref[...]) for c in range(0, d, w)]
        return lax.rsqrt(jnp.concatenate(sums, axis=1) * (1.0 / HEAD_DIM) + EPS)

    def rope(x, rows):
        cos = cos_ref[rows, :]
        sin = sin_ref[rows, :]
        lane = lax.broadcasted_iota(jnp.int32, cos.shape, 1)
        first = (lane & 31) < 16
        outs = []
        for hh in range(d // LANES):
            xs = x[:, hh * LANES:(hh + 1) * LANES]
            rot = jnp.where(first, pltpu.roll(xs, LANES - 16, 1), pltpu.roll(xs, 16, 1))
            outs.append(xs * cos + rot * sin)
        return jnp.concatenate(outs, axis=1)

    acts = [_modulated(h_ref[rows, :], g_ref[...], _mod_row(mod_ref, 0, seq), _mod_row(mod_ref, 1, seq)).astype(BF16)
            for rows in subs]
    qkvs = [_dot(a, w_ref[...]) for a in acts]
    qns = [(qkv[:, :d] * group_rms_inv(qkv[:, :d])) * qg_ref[...] for qkv in qkvs]
    kns = [(qkv[:, d:2 * d] * group_rms_inv(qkv[:, d:2 * d])) * kg_ref[...] for qkv in qkvs]
    if latent:
        qrs = [rope(qn, rows) for qn, rows in zip(qns, subs)]
        krs = [rope(kn, rows) for kn, rows in zip(kns, subs)]
    else:
        qrs, krs = qns, kns
    for rows, qkv, qr, kr in zip(subs, qkvs, qrs, krs):
        q_ref[rows, :] = (qr * q_scale).astype(BF16)
        k_ref[rows, :] = kr.astype(BF16)
        v_ref[rows, :] = qkv[:, 2 * d:].astype(BF16)
    if not latent:
        kn_all = jnp.concatenate(kns, axis=0)
        n_pos = kf_ref.shape[3]
        for r in range(kf_ref.shape[0]):
            kf_ref[r] = kn_all[r * n_pos:(r + 1) * n_pos, :].T.reshape(kf_ref.shape[1:])
        for rows, qkv in zip(subs, qkvs):
            vf_ref[rows, :] = qkv[:, 2 * d:]


def _rope_tables(n_pos):
    half = HEAD_DIM // 2
    pos = jnp.arange(n_pos, dtype=jnp.int32)
    row = (pos // GRID_W).astype(F32)
    col = (pos % GRID_W).astype(F32)
    inv = 1.0 / (ROPE_THETA ** (jnp.arange(0, half, 2, dtype=F32) / half))
    ang_r = row[:, None] * inv[None, :]
    ang_c = col[:, None] * inv[None, :]
    ang = jnp.concatenate([ang_r, ang_r, ang_c, ang_c], axis=-1)
    quarter = half // 2
    sign = jnp.tile(jnp.concatenate([-jnp.ones((quarter,), F32), jnp.ones((quarter,), F32)]), 2)
    cos = jnp.tile(jnp.cos(ang), (1, 2))
    sin = jnp.tile(jnp.sin(ang) * sign[None, :], (1, 2))
    return cos, sin


def _pre_attn(h, mod, g, wqkv, qg, kg, *, ctx_len=None, rows_per_req=None):
    n, d = h.shape
    latent = rows_per_req is not None
    tm = PRE_ATTN_TILE
    gid = jnp.arange(MXU_TILE, dtype=jnp.int32) // HEAD_DIM
    gs = (gid[:, None] == gid[None, :]).astype(BF16)
    full = lambda shape: pl.BlockSpec(shape, lambda i: (0,) * len(shape), pipeline_mode=pl.Buffered(1))
    row = pl.BlockSpec((tm, d), lambda i: (i, 0))
    args = [h, mod, g, wqkv, qg, kg, gs]
    in_specs = [row] + [full(a.shape) for a in args[1:]]
    out_specs = [row, row, row]
    out_shape = [jax.ShapeDtypeStruct((n, d), BF16)] * 3
    if latent:
        tiles_per_req = rows_per_req // tm
        pos = pl.BlockSpec((tm, LANES), lambda i: (i % tiles_per_req, 0))
        args += list(_rope_tables(rows_per_req))
        in_specs += [pos, pos]
    else:
        assert tm % ctx_len == 0
        tiles_per_req = None
        out_specs += [pl.BlockSpec((tm // ctx_len, d // HEAD_DIM, HEAD_DIM, ctx_len), lambda i: (i, 0, 0, 0)), row]
        out_shape += [jax.ShapeDtypeStruct((n // ctx_len, d // HEAD_DIM, HEAD_DIM, ctx_len), F32),
                      jax.ShapeDtypeStruct((n, d), F32)]
    return pl.pallas_call(
        functools.partial(_pre_attn_kernel, latent=latent, tiles_per_req=tiles_per_req),
        grid=(n // tm,),
        in_specs=in_specs,
        out_specs=out_specs,
        out_shape=out_shape,
        compiler_params=_cparams(1),
        name="pre_attn_latent" if latent else "pre_attn_ctx",
    )(*args)


def _attn_kernel(*refs, n_heads, has_cache, layer):
    if has_cache:
        lam_ref, sub_ref, q_ref, kn_ref, vn_ref, kc_ref, vc_ref, o_ref = refs
    else:
        lam_ref, sub_ref, q_ref, kn_ref, vn_ref, o_ref = refs
    lf = lam_ref[...]
    lam = (jnp.exp(jnp.sum(lf[0:1] * lf[1:2], axis=-1, keepdims=True))
           - jnp.exp(jnp.sum(lf[2:3] * lf[3:4], axis=-1, keepdims=True)) + _lambda_init(layer))
    nt = (((1,), (1,)), ((), ()))
    tq = q_ref.shape[0]
    lane = lax.broadcasted_iota(jnp.int32, (tq, LANES), 1)
    heads = [slice(hh * LANES, (hh + 1) * LANES) for hh in range(n_heads)]

    def qk(hh, sl):
        qh = q_ref[:, sl]
        zero = jnp.zeros_like(qh)
        kn = kn_ref[:, sl]
        kc_t = kc_ref[hh].astype(BF16) if has_cache else None
        out = []
        for qm in (jnp.where(lane < HEAD_DIM, qh, zero), jnp.where(lane >= HEAD_DIM, qh, zero)):
            s = [lax.dot_general(qm, kn, nt, preferred_element_type=F32)]
            out.append(s + [_dot(qm, kc_t)] if has_cache else s)
        return out

    def soft(scores):
        maxes = [functools.reduce(jnp.maximum, [jnp.max(x, axis=-1, keepdims=True) for x in s]) for s in scores]
        exps = [[jnp.exp2(x - m) for x in s] for s, m in zip(scores, maxes)]
        r0, r1 = [1.0 / functools.reduce(lambda u, w: u + w, [jnp.sum(x, axis=-1, keepdims=True) for x in e])
                  for e in exps]
        return [(x0 * r0 - lam * (x1 * r1)).astype(BF16) for x0, x1 in zip(*exps)]

    def pv(sl, a):
        vals = [vn_ref[:, sl]] + ([vc_ref[:, sl].astype(BF16)] if has_cache else [])
        o = functools.reduce(lambda u, w: u + w, [_dot(aj, vv) for aj, vv in zip(a, vals)])
        o = o * lax.rsqrt(jnp.mean(o * o, axis=-1, keepdims=True) + EPS)
        o_ref[:, sl] = ((o * sub_ref[...]) * (1.0 - _lambda_init(layer))).astype(BF16)

    scores = [qk(hh, sl) for hh, sl in enumerate(heads)]
    mixed = [soft(sc) for sc in scores]
    for sl, a in zip(heads, mixed):
        pv(sl, a)


def _attention(qkv_ctx, qkv_lat, lam_p, subln, cache_k, cache_v, ctx_len, n_req, req_len, layer):
    n_ctx, d = qkv_ctx[0].shape
    small = lambda shape: pl.BlockSpec(shape, lambda *_: (0,) * len(shape))
    n_ctx_req = n_ctx // ctx_len
    blk = pl.BlockSpec((ctx_len, d), lambda b: (b, 0))
    o_ctx = pl.pallas_call(
        functools.partial(_attn_kernel, n_heads=N_HEADS, has_cache=False, layer=layer),
        grid=(n_ctx_req,),
        in_specs=[small(lam_p.shape), small(subln.shape), blk, blk, blk],
        out_specs=blk,
        out_shape=jax.ShapeDtypeStruct((n_ctx, d), BF16),
        compiler_params=_cparams(1),
        name="attn_ctx",
    )(lam_p, subln, *qkv_ctx)

    tq = ATTN_Q_TILE
    nq = req_len // tq
    past = cache_v.shape[1]
    hp = ATTN_HEADS_PER_STEP
    small3 = lambda shape: pl.BlockSpec(shape, lambda b, h, i: (0,) * len(shape))
    qspec = pl.BlockSpec((tq, hp * LANES), lambda b, h, i: (b * nq + i, h))
    kvspec = pl.BlockSpec((req_len, hp * LANES), lambda b, h, i: (b, h))
    cspec = pl.BlockSpec((None, past, hp * LANES), lambda b, h, i: (b, 0, h))
    ckspec = pl.BlockSpec((None, hp, LANES, past), lambda b, h, i: (b, h, 0, 0))
    o_lat = pl.pallas_call(
        functools.partial(_attn_kernel, n_heads=hp, has_cache=True, layer=layer),
        grid=(n_req, N_HEADS // hp, nq),
        in_specs=[small3(lam_p.shape), pl.BlockSpec((1, LANES), lambda b, h, i: (0, 0)),
                  qspec, kvspec, kvspec, ckspec, cspec],
        out_specs=pl.BlockSpec((tq, hp * LANES), lambda b, h, i: (b * nq + i, h)),
        out_shape=jax.ShapeDtypeStruct((n_req * req_len, d), BF16),
        compiler_params=_cparams(3),
        name="attn_latent",
    )(lam_p, subln, *qkv_lat, cache_k, cache_v)
    return o_ctx, o_lat


def _route(a2, wr, br, ltri, carry):
    tm = a2.shape[0]
    logits = _dot(a2.astype(BF16), wr) + br
    lane_i = lax.broadcasted_iota(jnp.int32, (tm, LANES), 1)
    lane = lane_i.astype(F32)
    neg = jnp.full((tm, LANES), -jnp.inf, F32)
    big = jnp.full((tm, LANES), float(LANES), F32)
    first_lane = lambda mask: jnp.min(jnp.where(mask, lane, big), axis=-1, keepdims=True)

    lc = jnp.where(lane_i < N_EXPERT_GROUPS, logits, neg)
    mc = jnp.max(lc, axis=-1, keepdims=True)
    pg = 1.0 / jnp.sum(jnp.exp(lc - mc), axis=-1, keepdims=True)
    gi = first_lane(lc == mc)
    assert EXPERTS_PER_GROUP == 8
    grp = lax.shift_right_arithmetic(lane_i - L_EXPERT0, 3).astype(F32)
    in_group = (lane_i >= L_EXPERT0) & (lane_i < L_EXPERT0 + N_EXPERTS) & (grp == gi)
    ls = jnp.where(in_group, logits, neg)
    t1 = jnp.max(ls, axis=-1, keepdims=True)
    i1 = first_lane(ls == t1)
    ls2 = jnp.where(lane == i1, neg, ls)
    t2 = jnp.max(ls2, axis=-1, keepdims=True)
    i2 = first_lane(ls2 == t2)
    ex = jnp.exp(t2 - t1)
    w1 = pg * (1.0 / (1.0 + ex))
    w2 = pg * (ex / (1.0 + ex))
    e1 = i1 - float(L_EXPERT0)
    e2 = i2 - float(L_EXPERT0)
    oh1 = lane == e1
    oh2 = lane == e2
    onehot = oh1.astype(F32) + oh2.astype(F32)
    before = _dot(ltri, onehot.astype(BF16)) + carry
    zero = jnp.zeros_like(before)
    rank1 = jnp.sum(jnp.where(oh1, before, zero), axis=-1, keepdims=True)
    rank2 = jnp.sum(jnp.where(oh2, before, zero), axis=-1, keepdims=True)
    slab = jnp.zeros((tm, LANES), F32)
    for ln, val in ((R_E1, e1), (R_E2, e2), (R_W1, w1), (R_W2, w2), (R_RANK1, rank1), (R_RANK2, rank2)):
        slab = jnp.where(lane_i == ln, val, slab)
    return slab, carry + jnp.sum(onehot, axis=0, keepdims=True)


def _router_weights(wc, bc, wf, bf_):
    d = wc.shape[0]
    pad = LANES - N_EXPERT_GROUPS - N_EXPERTS
    w = jnp.concatenate([wc, wf, jnp.zeros((d, pad), F32)], axis=1)
    b = jnp.concatenate([bc, bf_, jnp.zeros((pad,), F32)])[None, :]
    return w.astype(BF16), b


def _lower_tri(n):
    r = jnp.arange(n, dtype=jnp.int32)
    return (r[None, :] < r[:, None]).astype(BF16)


def _post_attn_kernel(hc_ref, hl_ref, oc_ref, ol_ref, mod_ref, wo_ref, g2_ref, wr_ref, br_ref, ltri_ref,
                      h1_ref, a2_ref, route_ref, rtt_ref, cnt_ref, carry_ref, *, n_ctx_tiles, tiles_per_req):
    i = pl.program_id(0)
    seq = _seq_of_tile(i, n_ctx_tiles, tiles_per_req)

    @pl.when(i == 0)
    def _():
        carry_ref[...] = jnp.zeros_like(carry_ref)

    subs = [pl.ds(s * ROW_TILE, ROW_TILE) for s in range(h1_ref.shape[0] // ROW_TILE)]
    mixes = [_dot(jnp.where(i < n_ctx_tiles, oc_ref[rows, :], ol_ref[rows, :]), wo_ref[...]) for rows in subs]
    h1s = [jnp.where(i < n_ctx_tiles, hc_ref[rows, :], hl_ref[rows, :]) + _mod_row(mod_ref, 2, seq) * mix
           for rows, mix in zip(subs, mixes)]
    a2s = [_modulated(h1, g2_ref[...], _mod_row(mod_ref, 3, seq), _mod_row(mod_ref, 4, seq)) for h1 in h1s]
    for rows, h1, a2 in zip(subs, h1s, a2s):
        h1_ref[rows, :] = h1
        a2_ref[rows, :] = _pack_bf16_pairs(a2)
    carry = carry_ref[...]
    for rows, a2 in zip(subs, a2s):
        slab, carry = _route(a2, wr_ref[...], br_ref[...], ltri_ref[...], carry)
        route_ref[rows, :] = slab
        rtt_ref[:, rows] = slab.T[:R_ROWS, :]
    carry_ref[...] = carry
    cnt_ref[...] = carry


def _post_attn(h_ctx, h_lat, o_ctx, o_lat, mod, wo, g2, router, rows_per_req):
    n_ctx, d = h_ctx.shape
    t = n_ctx + h_lat.shape[0]
    tm = ROW_TILE * POST_ATTN_SUBTILES
    wr, br = router
    ltri = _lower_tri(ROW_TILE)
    full = lambda shape: pl.BlockSpec(shape, lambda i: (0,) * len(shape))
    row = pl.BlockSpec((tm, d), lambda i: (i, 0))
    kern = functools.partial(_post_attn_kernel, n_ctx_tiles=n_ctx // tm, tiles_per_req=rows_per_req // tm)
    return pl.pallas_call(
        kern,
        grid=(t // tm,),
        in_specs=_split_specs(tm, d, n_ctx // tm) + _split_specs(tm, d, n_ctx // tm) + [
                  full(mod.shape), full(wo.shape), full((1, d)), full(wr.shape),
                  full(br.shape), full(ltri.shape)],
        out_specs=[row, pl.BlockSpec((tm, d // 2), lambda i: (i, 0)), pl.BlockSpec((tm, LANES), lambda i: (i, 0)),
                   pl.BlockSpec((R_ROWS, tm), lambda i: (0, i)), full((1, LANES))],
        out_shape=[jax.ShapeDtypeStruct((t, d), F32), jax.ShapeDtypeStruct((t, d // 2), jnp.int32),
                   jax.ShapeDtypeStruct((t, LANES), F32), jax.ShapeDtypeStruct((R_ROWS, t), F32),
                   jax.ShapeDtypeStruct((1, LANES), F32)],
        scratch_shapes=[pltpu.VMEM((1, LANES), F32)],
        compiler_params=_cparams(1),
        name="post_attn_router",
    )(h_ctx, h_lat, o_ctx, o_lat, mod, wo, g2, wr, br, ltri)


def _gather_rows(src, idx):
    n = idx.shape[0]
    d = src.shape[1]
    w = _sc_window_rows(src)
    mesh = plsc.VectorSubcoreMesh(core_axis_name="core", subcore_axis_name="subcore")

    @pl.kernel(out_type=jax.ShapeDtypeStruct((n, d), src.dtype), mesh=mesh)
    def gather(src_hbm, idx_hbm, out_hbm):
        def body(idx_vmem, out_vmem):
            pltpu.sync_copy(src_hbm.at[idx_vmem.at[0, pl.ds(0, w)]], out_vmem)

        pltpu.emit_pipeline(
            body,
            grid=(n // w,),
            in_specs=[pl.BlockSpec((1, LANES), lambda i: (i, 0))],
            out_specs=[pl.BlockSpec((w, d), lambda i: (i, 0))],
            core_axis_name=("core", "subcore"),
            dimension_semantics=(pltpu.PARALLEL,),
        )(idx_hbm, out_hbm)

    return gather(src, _index_windows(idx, w))


def _sc_window_rows(src):
    return min(LANES, SC_WINDOW_BYTES // (src.shape[1] * src.dtype.itemsize))


def _index_windows(idx, w):
    return jnp.pad(idx.reshape(idx.shape[0] // w, w), ((0, 0), (0, LANES - w)))


def _scatter_rows_twice(src, idx_a, idx_b, n_out):
    t, d = src.shape
    w = _sc_window_rows(src)
    mesh = plsc.VectorSubcoreMesh(core_axis_name="core", subcore_axis_name="subcore")

    @pl.kernel(out_type=jax.ShapeDtypeStruct((n_out, d), src.dtype), mesh=mesh)
    def scatter(src_hbm, ia_hbm, ib_hbm, out_hbm):
        def body(src_vmem, ia_vmem, ib_vmem):
            pltpu.sync_copy(src_vmem, out_hbm.at[ia_vmem.at[0, pl.ds(0, w)]])
            pltpu.sync_copy(src_vmem, out_hbm.at[ib_vmem.at[0, pl.ds(0, w)]])

        pltpu.emit_pipeline(
            body,
            grid=(t // w,),
            in_specs=[pl.BlockSpec((w, d), lambda i: (i, 0)),
                      pl.BlockSpec((1, LANES), lambda i: (i, 0)),
                      pl.BlockSpec((1, LANES), lambda i: (i, 0))],
            out_specs=[],
            core_axis_name=("core", "subcore"),
            dimension_semantics=(pltpu.PARALLEL,),
        )(src_hbm, ia_hbm, ib_hbm)

    return scatter(src, _index_windows(idx_a, w), _index_windows(idx_b, w))


def _expert_kernel(be_ref, nb_ref, ne_ref, x_ref, w1_hbm, w3_hbm, w2_hbm, y_ref,
                   w1s, w3s, w2s, w1b, w3b, w2b, sems, *, layer):
    b = pl.program_id(0)
    prev = be_ref[jnp.maximum(b - 1, 0)]
    used = b < nb_ref[0]

    def stage(e):
        return [pltpu.make_async_copy(src.at[layer, e], dst, sems.at[j])
                for j, (src, dst) in enumerate(((w1_hbm, w1s), (w3_hbm, w3s), (w2_hbm, w2s)))]

    @pl.when(used & ((b == 0) | (be_ref[b] != prev)))
    def _():
        @pl.when(b == 0)
        def _():
            for c in stage(be_ref[0]):
                c.start()

        for c in stage(be_ref[b]):
            c.wait()
        w1b[...] = w1s[...].astype(BF16)
        w3b[...] = w3s[...].astype(BF16)
        w2b[...] = w2s[...].astype(BF16)

        @pl.when(ne_ref[b] >= 0)
        def _():
            for c in stage(ne_ref[b]):
                c.start()

    @pl.when(used)
    def _():
        x_lo, x_hi = _unpack_bf16_pairs(x_ref[...])
        half = x_lo.shape[1]
        h1 = _dot(x_lo, w1b[:half, :]) + _dot(x_hi, w1b[half:, :])
        h3 = _dot(x_lo, w3b[:half, :]) + _dot(x_hi, w3b[half:, :])
        hb = jax.nn.silu(h1) * h3
        y_ref[...] = _pack_bf16_pairs(_dot(hb.astype(BF16), w2b[...]))


def _expert_mlps(xs, block_e, n_used, next_e, w1, w3, w2, layer):
    npad, half = xs.shape
    d, ff = w1.shape[2], w1.shape[3]
    assert d == 2 * half
    bm = MOE_BLOCK
    rows = lambda b, be, nb, ne: (jnp.minimum(b, nb[0] - 1), 0)
    hbm = pl.BlockSpec(memory_space=pl.ANY)
    grid_spec = pltpu.PrefetchScalarGridSpec(
        num_scalar_prefetch=3,
        grid=(npad // bm,),
        in_specs=[pl.BlockSpec((bm, half), rows), hbm, hbm, hbm],
        out_specs=pl.BlockSpec((bm, half), rows),
        scratch_shapes=[pltpu.VMEM((d, ff), F32), pltpu.VMEM((d, ff), F32), pltpu.VMEM((ff, d), F32),
                        pltpu.VMEM((d, ff), BF16), pltpu.VMEM((d, ff), BF16), pltpu.VMEM((ff, d), BF16),
                        pltpu.SemaphoreType.DMA((3,))],
    )
    return pl.pallas_call(
        functools.partial(_expert_kernel, layer=layer),
        grid_spec=grid_spec,
        out_shape=jax.ShapeDtypeStruct((npad, half), jnp.int32),
        compiler_params=_cparams(1),
        name="expert_mlps",
    )(block_e, n_used, next_e, xs, w1, w3, w2)


def _moe(a2p, route_t, counts, w1, w3, w2, layer, row_ranges):
    t = a2p.shape[0]
    bm = MOE_BLOCK
    n = 2 * t
    ints = route_t.astype(jnp.int32)
    cnt = counts[0, :N_EXPERTS].astype(jnp.int32)
    padded = ((cnt + bm - 1) // bm) * bm
    pend = jnp.cumsum(padded)
    pstart = pend - padded
    experts = jnp.arange(N_EXPERTS, dtype=jnp.int32)[None, :]
    slot = lambda e, rank: jnp.sum(jnp.where(e[:, None] == experts, pstart[None, :], 0), axis=1) + rank
    pos_first = slot(ints[R_E1], ints[R_RANK1])
    pos_second = slot(ints[R_E2], ints[R_RANK2])
    n_blocks = n // bm + N_EXPERTS
    npad = n_blocks * bm
    starts = jnp.arange(n_blocks, dtype=jnp.int32) * bm
    n_used = pend[-1:] // bm
    last_start = (n_used[0] - 1) * bm
    block_e = jnp.sum((pend[None, :] <= jnp.minimum(starts, last_start)[:, None]).astype(jnp.int32), axis=1)
    blk = jnp.arange(n_blocks, dtype=jnp.int32)
    later_other = (blk[None, :] > blk[:, None]) & (block_e[None, :] != block_e[:, None]) & (blk[None, :] < n_used[0])
    first_later = jnp.min(jnp.where(later_other, blk[None, :], n_blocks), axis=1)
    next_e = jnp.where(first_later < n_blocks, block_e[jnp.minimum(first_later, n_blocks - 1)], -1).astype(jnp.int32)
    xs = _scatter_rows_twice(a2p, pos_first, pos_second, npad)
    ys = _expert_mlps(xs, block_e, n_used, next_e, w1, w3, w2, layer)
    return [_gather_rows(ys, jnp.concatenate([pos_first[r0:r1], pos_second[r0:r1]])) for r0, r1 in row_ranges]


def _moe_combine(h, yp_first, yp_second, route, g2):
    w1 = route[:, R_W1:R_W1 + 1]
    w2 = route[:, R_W2:R_W2 + 1]
    wide = lambda p: jnp.concatenate(_unpack_pairs_f32(p), axis=1)
    return h + g2 * (w1 * wide(yp_first) + w2 * wide(yp_second))


def _gmlp_kernel(h_ref, ya_ref, yb_ref, rt_ref, modp_ref, mod_ref, g1_ref, win_ref, bin_ref, vg_ref, ws_ref, bsb_ref,
                 wout_ref, g2_ref, wr_ref, br_ref, ltri_ref,
                 h1_ref, a2_ref, route_ref, rtt_ref, cnt_ref, carry_ref, *, n_ctx_tiles, tiles_per_req):
    i = pl.program_id(0)
    seq = _seq_of_tile(i, n_ctx_tiles, tiles_per_req)

    @pl.when(i == 0)
    def _():
        carry_ref[...] = jnp.zeros_like(carry_ref)

    gw = vg_ref.shape[1]
    cg = gw // GM_GROUPS
    subs = [pl.ds(s * GMLP_TILE, GMLP_TILE) for s in range(h_ref.shape[0] // GMLP_TILE)]
    hs = [_moe_combine(h_ref[rs, :], ya_ref[rs, :], yb_ref[rs, :], rt_ref[rs, :], _mod_row(modp_ref, 5, seq))
          for rs in subs]
    acts = [_modulated(h, g1_ref[...], _mod_row(mod_ref, 0, seq), _mod_row(mod_ref, 1, seq)).astype(BF16) for h in hs]
    vs = [_gelu_tanh(_dot(a, win_ref[:, gw:]) + bin_ref[:, gw:]) for a in acts]
    vbs = [((v * lax.rsqrt(jnp.mean(v * v, axis=-1, keepdims=True) + EPS)) * vg_ref[...]).astype(BF16) for v in vs]
    us = [_gelu_tanh(_dot(a, win_ref[:, :gw]) + bin_ref[:, :gw]) for a in acts]
    vms = []
    for vb in vbs:
        rows = []
        for c in range(GMLP_TILE // CHUNK):
            cols = [_dot(ws_ref[g], vb[c * CHUNK:(c + 1) * CHUNK, g * cg:(g + 1) * cg]) for g in range(GM_GROUPS)]
            rows.append(jnp.concatenate(cols, axis=1) + bsb_ref[...])
        vms.append(jnp.concatenate(rows, axis=0))
    mixes = [_dot((u * vm).astype(BF16), wout_ref[...]) for u, vm in zip(us, vms)]
    h1s = [h + _mod_row(mod_ref, 2, seq) * mix for h, mix in zip(hs, mixes)]
    a2s = [_modulated(h1, g2_ref[...], _mod_row(mod_ref, 3, seq), _mod_row(mod_ref, 4, seq)) for h1 in h1s]
    carry = carry_ref[...]
    for rs, h1, a2 in zip(subs, h1s, a2s):
        h1_ref[rs, :] = h1
        a2_ref[rs, :] = _pack_bf16_pairs(a2)
        slab, carry = _route(a2, wr_ref[...], br_ref[...], ltri_ref[...], carry)
        route_ref[rs, :] = slab
        rtt_ref[:, rs] = slab.T[:R_ROWS, :]
    carry_ref[...] = carry
    cnt_ref[...] = carry


def _gmlp_layer(h, y2, route_prev, mod_prev, mod, g1, win, bin_, vg, ws, bsb, wout, g2, router,
                n_ctx, rows_per_req):
    t, d = h.shape
    tm = GMLP_TILE
    wr, br = router
    ltri = _lower_tri(GMLP_TILE)
    full = lambda shape: pl.BlockSpec(shape, lambda i: (0,) * len(shape), pipeline_mode=pl.Buffered(1))
    row = lambda w: pl.BlockSpec((tm, w), lambda i: (i, 0))
    kern = functools.partial(_gmlp_kernel, n_ctx_tiles=n_ctx // tm, tiles_per_req=rows_per_req // tm)
    args = (h, y2, y2, route_prev, mod_prev, mod, g1, win, bin_, vg, ws, bsb, wout, g2, wr, br, ltri)
    second = pl.BlockSpec((tm, d // 2), lambda i: (i + t // tm, 0))
    in_specs = [row(d), row(d // 2), second, row(LANES)] + [full(a.shape) for a in args[4:]]
    return pl.pallas_call(
        kern,
        grid=(t // tm,),
        in_specs=in_specs,
        out_specs=[row(d), row(d // 2), row(LANES), pl.BlockSpec((R_ROWS, tm), lambda i: (0, i)),
                   pl.BlockSpec((1, LANES), lambda i: (0, 0))],
        out_shape=[jax.ShapeDtypeStruct((t, d), F32), jax.ShapeDtypeStruct((t, d // 2), jnp.int32),
                   jax.ShapeDtypeStruct((t, LANES), F32), jax.ShapeDtypeStruct((R_ROWS, t), F32),
                   jax.ShapeDtypeStruct((1, LANES), F32)],
        scratch_shapes=[pltpu.VMEM((1, LANES), F32)],
        compiler_params=_cparams(1),
        name="gmlp_router",
    )(*args)


def _final_kernel(h_ref, ya_ref, yb_ref, rt_ref, mod_ref, o_ref, *, first_tile, n_ctx_tiles, tiles_per_req):
    seq = _seq_of_tile(first_tile + pl.program_id(0), n_ctx_tiles, tiles_per_req)
    o_ref[...] = _moe_combine(h_ref[...], ya_ref[...], yb_ref[...], rt_ref[...], _mod_row(mod_ref, 5, seq))


def _final_combine(h, y2, route, mod, first_row, n_ctx, rows_per_req):
    d = h.shape[1]
    n = y2.shape[0] // 2
    tm = FINAL_TILE
    first_tile = first_row // tm
    row = lambda w: pl.BlockSpec((tm, w), lambda i: (first_tile + i, 0))
    kern = functools.partial(_final_kernel, first_tile=first_tile, n_ctx_tiles=n_ctx // tm,
                             tiles_per_req=rows_per_req // tm)
    return pl.pallas_call(
        kern,
        grid=(n // tm,),
        in_specs=[row(d), pl.BlockSpec((tm, d // 2), lambda i: (i, 0)),
                  pl.BlockSpec((tm, d // 2), lambda i: (i + n // tm, 0)),
                  row(LANES), pl.BlockSpec(mod.shape, lambda i: (0, 0, 0))],
        out_specs=pl.BlockSpec((tm, d), lambda i: (i, 0)),
        out_shape=jax.ShapeDtypeStruct((n, d), F32),
        compiler_params=_cparams(1),
        name="final_combine",
    )(h, y2, y2, route, mod)


def kernel(x_prompt, x_sample, cache_k, cache_v, c, c_ctx, ada_w, ada_b, norm1_g, norm2_g, attn_wq, attn_wk,
           attn_wv, attn_wo, attn_qnorm, attn_knorm, attn_lam, attn_subln, gm_win, gm_bin, gm_vnorm, gm_ws,
           gm_bs, gm_wout, moe_wc, moe_bc, moe_wf, moe_bf, moe_w1, moe_w3, moe_w2):
    batch, seq_len, d = x_prompt.shape
    n_req, req_len, _ = x_sample.shape
    depth = ada_w.shape[0]
    assert depth == 2 and attn_wq.shape[0] == 1 and gm_win.shape[0] == 1
    n_ctx = batch * seq_len
    tiles = (PRE_ATTN_TILE, ROW_TILE * POST_ATTN_SUBTILES, GMLP_TILE, ATTN_Q_TILE, FINAL_TILE)
    assert all(n_ctx % tm == 0 and req_len % tm == 0 for tm in tiles) and GMLP_TILE % CHUNK == 0
    past = cache_k.shape[2]

    h_ctx = x_prompt.reshape(n_ctx, d)
    h_lat = x_sample.reshape(n_req * req_len, d)
    cvec = jnp.concatenate([c_ctx[None, :], c, jnp.zeros((8 - 1 - n_req, d), F32)], axis=0)
    mod = _ada_mod(cvec, ada_w, ada_b)
    routers = [_router_weights(moe_wc[i], moe_bc[i], moe_wf[i], moe_bf[i]) for i in range(depth)]
    row_vec = lambda v: v.reshape(1, -1)

    wqkv = jnp.concatenate([attn_wq[0], attn_wk[0], attn_wv[0]], axis=1).astype(BF16)
    reps = d // HEAD_DIM
    cache_k_t = jnp.transpose(cache_k[:, 0], (0, 2, 3, 4, 1)).reshape(n_req, N_HEADS, 2 * HEAD_DIM, past)
    qk_gains = (row_vec(jnp.tile(attn_qnorm[0], reps)), row_vec(jnp.tile(attn_knorm[0], reps)))
    *qkv_ctx, k_new, v_new = _pre_attn(h_ctx, mod[0], row_vec(norm1_g[0]), wqkv, *qk_gains, ctx_len=seq_len)
    qkv_lat = _pre_attn(h_lat, mod[0], row_vec(norm1_g[0]), wqkv, *qk_gains, rows_per_req=req_len)
    o_ctx, o_lat = _attention(qkv_ctx, qkv_lat, attn_lam[0], row_vec(attn_subln[0]),
                              cache_k_t, cache_v[:, 0].reshape(n_req, past, d),
                              seq_len, n_req, req_len, layer=0)
    h1, a2, route0, route0_t, cnt0 = _post_attn(h_ctx, h_lat, o_ctx, o_lat, mod[0], attn_wo[0].astype(BF16),
                                      row_vec(norm2_g[0]), routers[0], req_len)
    t = n_ctx + n_req * req_len
    (y2,) = _moe(a2, route0_t, cnt0, moe_w1, moe_w3, moe_w2, 0, [(0, t)])

    gw = gm_vnorm.shape[1]
    bsb = jnp.repeat(gm_bs[0].T, gw // GM_GROUPS, axis=1)
    h2, a2, route1, route1_t, cnt1 = _gmlp_layer(h1, y2, route0, mod[0], mod[1], row_vec(norm1_g[1]), gm_win[0].astype(BF16),
                                       row_vec(gm_bin[0]), row_vec(gm_vnorm[0]), gm_ws[0].astype(BF16), bsb,
                                       gm_wout[0].astype(BF16), row_vec(norm2_g[1]), routers[1], n_ctx, req_len)
    y2_ctx, y2_lat = _moe(a2, route1_t, cnt1, moe_w1, moe_w3, moe_w2, 1, [(0, n_ctx), (n_ctx, t)])
    y_ctx = _final_combine(h2, y2_ctx, route1, mod[1], 0, n_ctx, req_len)
    y_lat = _final_combine(h2, y2_lat, route1, mod[1], n_ctx, n_ctx, req_len)

    y_prompt = y_ctx.reshape(batch, seq_len, d)
    y_sample = y_lat.reshape(n_req, req_len, d)
    new_cache_k = jnp.transpose(k_new.reshape(batch, 1, N_HEADS, 2, HEAD_DIM, seq_len), (0, 1, 5, 2, 3, 4))
    new_cache_v = v_new.reshape(batch, 1, seq_len, N_HEADS, V_DIM)
    return (y_prompt, y_sample, new_cache_k, new_cache_v)
```

```python
import functools
import math

import jax
import jax.numpy as jnp
from jax import lax
from jax.experimental import pallas as pl
from jax.experimental.pallas import tpu as pltpu
from jax.experimental.pallas import tpu_sc as plsc

F32 = jnp.float32
BF16 = jnp.bfloat16

N_HEADS = 8
HEAD_DIM = 64
V_DIM = 2 * HEAD_DIM
GRID_W = 64
ROPE_THETA = 10000.0
CHUNK = 128
GM_GROUPS = 8
N_EXPERT_GROUPS = 4
EXPERTS_PER_GROUP = 8
N_EXPERTS = N_EXPERT_GROUPS * EXPERTS_PER_GROUP
EPS = 1e-6

LANES = 128
MXU_TILE = 256
ROW_TILE = 256
PRE_ATTN_TILE = 512
POST_ATTN_SUBTILES = 4
GMLP_TILE = 512
FINAL_TILE = 512
MOE_BLOCK = 256
ATTN_Q_TILE = 512
ATTN_HEADS_PER_STEP = 4
SC_WINDOW_BYTES = 128 * 1024
VMEM_LIMIT = 56 * 1024 * 1024

R_E1, R_E2, R_W1, R_W2, R_RANK1, R_RANK2 = 0, 1, 2, 3, 4, 5
R_ROWS = 8
L_EXPERT0 = N_EXPERT_GROUPS


def _lambda_init(layer):
    return 0.8 - 0.6 * math.exp(-0.3 * layer)


def _cparams(n_axes):
    return pltpu.CompilerParams(dimension_semantics=("arbitrary",) * n_axes, vmem_limit_bytes=VMEM_LIMIT)


def _seq_of_tile(i, n_ctx_tiles, tiles_per_req):
    return jnp.where(i < n_ctx_tiles, 0, 1 + (i - n_ctx_tiles) // tiles_per_req)


def _split_specs(tm, d, n_ctx_tiles):
    return [pl.BlockSpec((tm, d), lambda i: (jnp.minimum(i, n_ctx_tiles - 1), 0)),
            pl.BlockSpec((tm, d), lambda i: (jnp.maximum(i - n_ctx_tiles, 0), 0))]


def _mod_row(mod_ref, part, seq):
    return mod_ref[part, pl.ds(seq, 1), :]


def _modulated(x, g, shift, scale):
    y = x * lax.rsqrt(jnp.mean(x * x, axis=-1, keepdims=True) + EPS)
    return (y * g) * (1.0 + scale) + shift


def _split_bf16(x):
    hi = x.astype(BF16)
    lo = (x - hi.astype(F32)).astype(BF16)
    return hi, lo


def _gelu_tanh(x):
    c = math.sqrt(2.0 / math.pi)
    hx = 0.5 * x
    return hx + hx * jnp.tanh(x * (c + (c * 0.044715) * (x * x)))


def _dot(a, b):
    return jnp.dot(a, b, preferred_element_type=F32)


def _pack_bf16_pairs(x):
    half = x.shape[1] // 2
    bits = lax.bitcast_convert_type(x.astype(BF16).astype(F32), jnp.uint32)
    packed = (bits[:, :half] >> 16) | bits[:, half:]
    return lax.bitcast_convert_type(packed, jnp.int32)


def _unpack_pairs_f32(p):
    u = lax.bitcast_convert_type(p, jnp.uint32)
    return lax.bitcast_convert_type(u << 16, F32), lax.bitcast_convert_type(u & jnp.uint32(0xFFFF0000), F32)


def _unpack_bf16_pairs(p):
    lo, hi = _unpack_pairs_f32(p)
    return lo.astype(BF16), hi.astype(BF16)


def _ada_kernel(c_ref, w_ref, b_ref, o_ref):
    c = c_ref[...]
    s = c * jax.nn.sigmoid(c)
    o_ref[...] = _dot(s.astype(BF16), w_ref[...].astype(BF16)) + b_ref[...]


def _ada_mod(cvec, ada_w, ada_b):
    depth, d, d6 = ada_w.shape
    parts = d6 // d
    rows = cvec.shape[0]
    return pl.pallas_call(
        _ada_kernel,
        grid=(depth, parts),
        in_specs=[
            pl.BlockSpec((rows, d), lambda l, j: (0, 0)),
            pl.BlockSpec((None, d, d), lambda l, j: (l, 0, j)),
            pl.BlockSpec((None, 1, d), lambda l, j: (l, 0, j)),
        ],
        out_specs=pl.BlockSpec((None, None, rows, d), lambda l, j: (l, j, 0, 0)),
        out_shape=jax.ShapeDtypeStruct((depth, parts, rows, d), F32),
        compiler_params=_cparams(2),
        name="ada_mod",
    )(cvec, ada_w, ada_b.reshape(depth, 1, d6))


def _pre_attn_kernel(*refs, latent, tiles_per_req):
    if latent:
        h_ref, mod_ref, g_ref, w_ref, qg_ref, kg_ref, gs_ref, cos_ref, sin_ref, q_ref, k_ref, v_ref = refs
        seq = 1 + pl.program_id(0) // tiles_per_req
    else:
        h_ref, mod_ref, g_ref, w_ref, qg_ref, kg_ref, gs_ref, q_ref, k_ref, v_ref, kf_ref, vf_ref = refs
        seq = 0
    d = h_ref.shape[1]
    q_scale = HEAD_DIM ** -0.5 * math.log2(math.e)
    subs = [pl.ds(s * ROW_TILE, ROW_TILE) for s in range(q_ref.shape[0] // ROW_TILE)]

    def group_rms_inv(x):
        sq_hi, sq_lo = _split_bf16(x * x)
        w = gs_ref.shape[0]
        sums = [_dot(sq_hi[:, c:c + w], gs_ref[...]) + _dot(sq_lo[:, c:c + w], gs_ref[...]) for c in range(0, d, w)]
        return lax.rsqrt(jnp.concatenate(sums, axis=1) * (1.0 / HEAD_DIM) + EPS)

    def rope(x, rows):
        cos = cos_ref[rows, :]
        sin = sin_ref[rows, :]
        lane = lax.broadcasted_iota(jnp.int32, cos.shape, 1)
        first = (lane & 31) < 16
        outs = []
        for hh in range(d // LANES):
            xs = x[:, hh * LANES:(hh + 1) * LANES]
            rot = jnp.where(first, pltpu.roll(xs, LANES - 16, 1), pltpu.roll(xs, 16, 1))
            outs.append(xs * cos + rot * sin)
        return jnp.concatenate(outs, axis=1)

    acts = [_modulated(h_ref[rows, :], g_ref[...], _mod_row(mod_ref, 0, seq), _mod_row(mod_ref, 1, seq)).astype(BF16)
            for rows in subs]
    qkvs = [_dot(a, w_ref[...]) for a in acts]
    qns = [(qkv[:, :d] * group_rms_inv(qkv[:, :d])) * qg_ref[...] for qkv in qkvs]
    kns = [(qkv[:, d:2 * d] * group_rms_inv(qkv[:, d:2 * d])) * kg_ref[...] for qkv in qkvs]
    if latent:
        qrs = [rope(qn, rows) for qn, rows in zip(qns, subs)]
        krs = [rope(kn, rows) for kn, rows in zip(kns, subs)]
    else:
        qrs, krs = qns, kns
    for rows, qkv, qr, kr in zip(subs, qkvs, qrs, krs):
        q_ref[rows, :] = (qr * q_scale).astype(BF16)
        k_ref[rows, :] = kr.astype(BF16)
        v_ref[rows, :] = qkv[:, 2 * d:].astype(BF16)
    if not latent:
        kn_all = jnp.concatenate(kns, axis=0)
        n_pos = kf_ref.shape[3]
        for r in range(kf_ref.shape[0]):
            kf_ref[r] = kn_all[r * n_pos:(r + 1) * n_pos, :].T.reshape(kf_ref.shape[1:])
        for rows, qkv in zip(subs, qkvs):
            vf_ref[rows, :] = qkv[:, 2 * d:]


def _rope_tables(n_pos):
    half = HEAD_DIM // 2
    pos = jnp.arange(n_pos, dtype=jnp.int32)
    row = (pos // GRID_W).astype(F32)
    col = (pos % GRID_W).astype(F32)
    inv = 1.0 / (ROPE_THETA ** (jnp.arange(0, half, 2, dtype=F32) / half))
    ang_r = row[:, None] * inv[None, :]
    ang_c = col[:, None] * inv[None, :]
    ang = jnp.concatenate([ang_r, ang_r, ang_c, ang_c], axis=-1)
    quarter = half // 2
    sign = jnp.tile(jnp.concatenate([-jnp.ones((quarter,), F32), jnp.ones((quarter,), F32)]), 2)
    cos = jnp.tile(jnp.cos(ang), (1, 2))
    sin = jnp.tile(jnp.sin(ang) * sign[None, :], (1, 2))
    return cos, sin


def _pre_attn(h, mod, g, wqkv, qg, kg, *, ctx_len=None, rows_per_req=None):
    n, d = h.shape
    latent = rows_per_req is not None
    tm = PRE_ATTN_TILE
    gid = jnp.arange(MXU_TILE, dtype=jnp.int32) // HEAD_DIM
    gs = (gid[:, None] == gid[None, :]).astype(BF16)
    full = lambda shape: pl.BlockSpec(shape, lambda i: (0,) * len(shape), pipeline_mode=pl.Buffered(1))
    row = pl.BlockSpec((tm, d), lambda i: (i, 0))
    args = [h, mod, g, wqkv, qg, kg, gs]
    in_specs = [row] + [full(a.shape) for a in args[1:]]
    out_specs = [row, row, row]
    out_shape = [jax.ShapeDtypeStruct((n, d), BF16)] * 3
    if latent:
        tiles_per_req = rows_per_req // tm
        pos = pl.BlockSpec((tm, LANES), lambda i: (i % tiles_per_req, 0))
        args += list(_rope_tables(rows_per_req))
        in_specs += [pos, pos]
    else:
        assert tm % ctx_len == 0
        tiles_per_req = None
        out_specs += [pl.BlockSpec((tm // ctx_len, d // HEAD_DIM, HEAD_DIM, ctx_len), lambda i: (i, 0, 0, 0)), row]
        out_shape += [jax.ShapeDtypeStruct((n // ctx_len, d // HEAD_DIM, HEAD_DIM, ctx_len), F32),
                      jax.ShapeDtypeStruct((n, d), F32)]
    return pl.pallas_call(
        functools.partial(_pre_attn_kernel, latent=latent, tiles_per_req=tiles_per_req),
        grid=(n // tm,),
        in_specs=in_specs,
        out_specs=out_specs,
        out_shape=out_shape,
        compiler_params=_cparams(1),
        name="pre_attn_latent" if latent else "pre_attn_ctx",
    )(*args)


def _attn_kernel(*refs, n_heads, has_cache, layer):
    if has_cache:
        lam_ref, sub_ref, q_ref, kn_ref, vn_ref, kc_ref, vc_ref, o_ref = refs
    else:
        lam_ref, sub_ref, q_ref, kn_ref, vn_ref, o_ref = refs
    lf = lam_ref[...]
    lam = (jnp.exp(jnp.sum(lf[0:1] * lf[1:2], axis=-1, keepdims=True))
           - jnp.exp(jnp.sum(lf[2:3] * lf[3:4], axis=-1, keepdims=True)) + _lambda_init(layer))
    nt = (((1,), (1,)), ((), ()))
    tq = q_ref.shape[0]
    lane = lax.broadcasted_iota(jnp.int32, (tq, LANES), 1)
    heads = [slice(hh * LANES, (hh + 1) * LANES) for hh in range(n_heads)]

    def qk(hh, sl):
        qh = q_ref[:, sl]
        zero = jnp.zeros_like(qh)
        kn = kn_ref[:, sl]
        kc_t = kc_ref[hh].astype(BF16) if has_cache else None
        out = []
        for qm in (jnp.where(lane < HEAD_DIM, qh, zero), jnp.where(lane >= HEAD_DIM, qh, zero)):
            s = [lax.dot_general(qm, kn, nt, preferred_element_type=F32)]
            out.append(s + [_dot(qm, kc_t)] if has_cache else s)
        return out

    def soft(scores):
        maxes = [functools.reduce(jnp.maximum, [jnp.max(x, axis=-1, keepdims=True) for x in s]) for s in scores]
        exps = [[jnp.exp2(x - m) for x in s] for s, m in zip(scores, maxes)]
        r0, r1 = [1.0 / functools.reduce(lambda u, w: u + w, [jnp.sum(x, axis=-1, keepdims=True) for x in e])
                  for e in exps]
        return [(x0 * r0 - lam * (x1 * r1)).astype(BF16) for x0, x1 in zip(*exps)]

    def pv(sl, a):
        vals = [vn_ref[:, sl]] + ([vc_ref[:, sl].astype(BF16)] if has_cache else [])
        o = functools.reduce(lambda u, w: u + w, [_dot(aj, vv) for aj, vv in zip(a, vals)])
        o = o * lax.rsqrt(jnp.mean(o * o, axis=-1, keepdims=True) + EPS)
        o_ref[:, sl] = ((o * sub_ref[...]) * (1.0 - _lambda_init(layer))).astype(BF16)

    scores = [qk(hh, sl) for hh, sl in enumerate(heads)]
    mixed = [soft(sc) for sc in scores]
    for sl, a in zip(heads, mixed):
        pv(sl, a)


def _attention(qkv_ctx, qkv_lat, lam_p, subln, cache_k, cache_v, ctx_len, n_req, req_len, layer):
    n_ctx, d = qkv_ctx[0].shape
    small = lambda shape: pl.BlockSpec(shape, lambda *_: (0,) * len(shape))
    n_ctx_req = n_ctx // ctx_len
    blk = pl.BlockSpec((ctx_len, d), lambda b: (b, 0))
    o_ctx = pl.pallas_call(
        functools.partial(_attn_kernel, n_heads=N_HEADS, has_cache=False, layer=layer),
        grid=(n_ctx_req,),
        in_specs=[small(lam_p.shape), small(subln.shape), blk, blk, blk],
        out_specs=blk,
        out_shape=jax.ShapeDtypeStruct((n_ctx, d), BF16),
        compiler_params=_cparams(1),
        name="attn_ctx",
    )(lam_p, subln, *qkv_ctx)

    tq = ATTN_Q_TILE
    nq = req_len // tq
    past = cache_v.shape[1]
    hp = ATTN_HEADS_PER_STEP
    small3 = lambda shape: pl.BlockSpec(shape, lambda b, h, i: (0,) * len(shape))
    qspec = pl.BlockSpec((tq, hp * LANES), lambda b, h, i: (b * nq + i, h))
    kvspec = pl.BlockSpec((req_len, hp * LANES), lambda b, h, i: (b, h))
    cspec = pl.BlockSpec((None, past, hp * LANES), lambda b, h, i: (b, 0, h))
    ckspec = pl.BlockSpec((None, hp, LANES, past), lambda b, h, i: (b, h, 0, 0))
    o_lat = pl.pallas_call(
        functools.partial(_attn_kernel, n_heads=hp, has_cache=True, layer=layer),
        grid=(n_req, N_HEADS // hp, nq),
        in_specs=[small3(lam_p.shape), pl.BlockSpec((1, LANES), lambda b, h, i: (0, 0)),
                  qspec, kvspec, kvspec, ckspec, cspec],
        out_specs=pl.BlockSpec((tq, hp * LANES), lambda b, h, i: (b * nq + i, h)),
        out_shape=jax.ShapeDtypeStruct((n_req * req_len, d), BF16),
        compiler_params=_cparams(3),
        name="attn_latent",
    )(lam_p, subln, *qkv_lat, cache_k, cache_v)
    return o_ctx, o_lat


def _route(a2, wr, br, ltri, carry):
    tm = a2.shape[0]
    logits = _dot(a2.astype(BF16), wr) + br
    lane_i = lax.broadcasted_iota(jnp.int32, (tm, LANES), 1)
    lane = lane_i.astype(F32)
    neg = jnp.full((tm, LANES), -jnp.inf, F32)
    big = jnp.full((tm, LANES), float(LANES), F32)
    first_lane = lambda mask: jnp.min(jnp.where(mask, lane, big), axis=-1, keepdims=True)

    lc = jnp.where(lane_i < N_EXPERT_GROUPS, logits, neg)
    mc = jnp.max(lc, axis=-1, keepdims=True)
    pg = 1.0 / jnp.sum(jnp.exp(lc - mc), axis=-1, keepdims=True)
    gi = first_lane(lc == mc)
    assert EXPERTS_PER_GROUP == 8
    grp = lax.shift_right_arithmetic(lane_i - L_EXPERT0, 3).astype(F32)
    in_group = (lane_i >= L_EXPERT0) & (lane_i < L_EXPERT0 + N_EXPERTS) & (grp == gi)
    ls = jnp.where(in_group, logits, neg)
    t1 = jnp.max(ls, axis=-1, keepdims=True)
    i1 = first_lane(ls == t1)
    ls2 = jnp.where(lane == i1, neg, ls)
    t2 = jnp.max(ls2, axis=-1, keepdims=True)
    i2 = first_lane(ls2 == t2)
    ex = jnp.exp(t2 - t1)
    w1 = pg * (1.0 / (1.0 + ex))
    w2 = pg * (ex / (1.0 + ex))
    e1 = i1 - float(L_EXPERT0)
    e2 = i2 - float(L_EXPERT0)
    oh1 = lane == e1
    oh2 = lane == e2
    onehot = oh1.astype(F32) + oh2.astype(F32)
    before = _dot(ltri, onehot.astype(BF16)) + carry
    zero = jnp.zeros_like(before)
    rank1 = jnp.sum(jnp.where(oh1, before, zero), axis=-1, keepdims=True)
    rank2 = jnp.sum(jnp.where(oh2, before, zero), axis=-1, keepdims=True)
    slab = jnp.zeros((tm, LANES), F32)
    for ln, val in ((R_E1, e1), (R_E2, e2), (R_W1, w1), (R_W2, w2), (R_RANK1, rank1), (R_RANK2, rank2)):
        slab = jnp.where(lane_i == ln, val, slab)
    return slab, carry + jnp.sum(onehot, axis=0, keepdims=True)


def _router_weights(wc, bc, wf, bf_):
    d = wc.shape[0]
    pad = LANES - N_EXPERT_GROUPS - N_EXPERTS
    w = jnp.concatenate([wc, wf, jnp.zeros((d, pad), F32)], axis=1)
    b = jnp.concatenate([bc, bf_, jnp.zeros((pad,), F32)])[None, :]
    return w.astype(BF16), b


def _lower_tri(n):
    r = jnp.arange(n, dtype=jnp.int32)
    return (r[None, :] < r[:, None]).astype(BF16)


def _post_attn_kernel(hc_ref, hl_ref, oc_ref, ol_ref, mod_ref, wo_ref, g2_ref, wr_ref, br_ref, ltri_ref,
                      h1_ref, a2_ref, route_ref, rtt_ref, cnt_ref, carry_ref, *, n_ctx_tiles, tiles_per_req):
    i = pl.program_id(0)
    seq = _seq_of_tile(i, n_ctx_tiles, tiles_per_req)

    @pl.when(i == 0)
    def _():
        carry_ref[...] = jnp.zeros_like(carry_ref)

    subs = [pl.ds(s * ROW_TILE, ROW_TILE) for s in range(h1_ref.shape[0] // ROW_TILE)]
    mixes = [_dot(jnp.where(i < n_ctx_tiles, oc_ref[rows, :], ol_ref[rows, :]), wo_ref[...]) for rows in subs]
    h1s = [jnp.where(i < n_ctx_tiles, hc_ref[rows, :], hl_ref[rows, :]) + _mod_row(mod_ref, 2, seq) * mix
           for rows, mix in zip(subs, mixes)]
    a2s = [_modulated(h1, g2_ref[...], _mod_row(mod_ref, 3, seq), _mod_row(mod_ref, 4, seq)) for h1 in h1s]
    for rows, h1, a2 in zip(subs, h1s, a2s):
        h1_ref[rows, :] = h1
        a2_ref[rows, :] = _pack_bf16_pairs(a2)
    carry = carry_ref[...]
    for rows, a2 in zip(subs, a2s):
        slab, carry = _route(a2, wr_ref[...], br_ref[...], ltri_ref[...], carry)
        route_ref[rows, :] = slab
        rtt_ref[:, rows] = slab.T[:R_ROWS, :]
    carry_ref[...] = carry
    cnt_ref[...] = carry


def _post_attn(h_ctx, h_lat, o_ctx, o_lat, mod, wo, g2, router, rows_per_req):
    n_ctx, d = h_ctx.shape
    t = n_ctx + h_lat.shape[0]
    tm = ROW_TILE * POST_ATTN_SUBTILES
    wr, br = router
    ltri = _lower_tri(ROW_TILE)
    full = lambda shape: pl.BlockSpec(shape, lambda i: (0,) * len(shape))
    row = pl.BlockSpec((tm, d), lambda i: (i, 0))
    kern = functools.partial(_post_attn_kernel, n_ctx_tiles=n_ctx // tm, tiles_per_req=rows_per_req // tm)
    return pl.pallas_call(
        kern,
        grid=(t // tm,),
        in_specs=_split_specs(tm, d, n_ctx // tm) + _split_specs(tm, d, n_ctx // tm) + [
                  full(mod.shape), full(wo.shape), full((1, d)), full(wr.shape),
                  full(br.shape), full(ltri.shape)],
        out_specs=[row, pl.BlockSpec((tm, d // 2), lambda i: (i, 0)), pl.BlockSpec((tm, LANES), lambda i: (i, 0)),
                   pl.BlockSpec((R_ROWS, tm), lambda i: (0, i)), full((1, LANES))],
        out_shape=[jax.ShapeDtypeStruct((t, d), F32), jax.ShapeDtypeStruct((t, d // 2), jnp.int32),
                   jax.ShapeDtypeStruct((t, LANES), F32), jax.ShapeDtypeStruct((R_ROWS, t), F32),
                   jax.ShapeDtypeStruct((1, LANES), F32)],
        scratch_shapes=[pltpu.VMEM((1, LANES), F32)],
        compiler_params=_cparams(1),
        name="post_attn_router",
    )(h_ctx, h_lat, o_ctx, o_lat, mod, wo, g2, wr, br, ltri)


def _gather_rows(src, idx):
    n = idx.shape[0]
    d = src.shape[1]
    w = _sc_window_rows(src)
    mesh = plsc.VectorSubcoreMesh(core_axis_name="core", subcore_axis_name="subcore")

    @pl.kernel(out_type=jax.ShapeDtypeStruct((n, d), src.dtype), mesh=mesh)
    def gather(src_hbm, idx_hbm, out_hbm):
        def body(idx_vmem, out_vmem):
            pltpu.sync_copy(src_hbm.at[idx_vmem.at[0, pl.ds(0, w)]], out_vmem)

        pltpu.emit_pipeline(
            body,
            grid=(n // w,),
            in_specs=[pl.BlockSpec((1, LANES), lambda i: (i, 0))],
            out_specs=[pl.BlockSpec((w, d), lambda i: (i, 0))],
            core_axis_name=("core", "subcore"),
            dimension_semantics=(pltpu.PARALLEL,),
        )(idx_hbm, out_hbm)

    return gather(src, _index_windows(idx, w))


def _sc_window_rows(src):
    return min(LANES, SC_WINDOW_BYTES // (src.shape[1] * src.dtype.itemsize))


def _index_windows(idx, w):
    return jnp.pad(idx.reshape(idx.shape[0] // w, w), ((0, 0), (0, LANES - w)))


def _scatter_rows_twice(src, idx_a, idx_b, n_out):
    t, d = src.shape
    w = _sc_window_rows(src)
    mesh = plsc.VectorSubcoreMesh(core_axis_name="core", subcore_axis_name="subcore")

    @pl.kernel(out_type=jax.ShapeDtypeStruct((n_out, d), src.dtype), mesh=mesh)
    def scatter(src_hbm, ia_hbm, ib_hbm, out_hbm):
        def body(src_vmem, ia_vmem, ib_vmem):
            pltpu.sync_copy(src_vmem, out_hbm.at[ia_vmem.at[0, pl.ds(0, w)]])
            pltpu.sync_copy(src_vmem, out_hbm.at[ib_vmem.at[0, pl.ds(0, w)]])

        pltpu.emit_pipeline(
            body,
            grid=(t // w,),
            in_specs=[pl.BlockSpec((w, d), lambda i: (i, 0)),
                      pl.BlockSpec((1, LANES), lambda i: (i, 0)),
                      pl.BlockSpec((1, LANES), lambda i: (i, 0))],
            out_specs=[],
            core_axis_name=("core", "subcore"),
            dimension_semantics=(pltpu.PARALLEL,),
        )(src_hbm, ia_hbm, ib_hbm)

    return scatter(src, _index_windows(idx_a, w), _index_windows(idx_b, w))


def _expert_kernel(be_ref, nb_ref, ne_ref, x_ref, w1_hbm, w3_hbm, w2_hbm, y_ref,
                   w1s, w3s, w2s, w1b, w3b, w2b, sems, *, layer):
    b = pl.program_id(0)
    prev = be_ref[jnp.maximum(b - 1, 0)]
    used = b < nb_ref[0]

    def stage(e):
        return [pltpu.make_async_copy(src.at[layer, e], dst, sems.at[j])
                for j, (src, dst) in enumerate(((w1_hbm, w1s), (w3_hbm, w3s), (w2_hbm, w2s)))]

    @pl.when(used & ((b == 0) | (be_ref[b] != prev)))
    def _():
        @pl.when(b == 0)
        def _():
            for c in stage(be_ref[0]):
                c.start()

        for c in stage(be_ref[b]):
            c.wait()
        w1b[...] = w1s[...].astype(BF16)
        w3b[...] = w3s[...].astype(BF16)
        w2b[...] = w2s[...].astype(BF16)

        @pl.when(ne_ref[b] >= 0)
        def _():
            for c in stage(ne_ref[b]):
                c.start()

    @pl.when(used)
    def _():
        x_lo, x_hi = _unpack_bf16_pairs(x_ref[...])
        half = x_lo.shape[1]
        h1 = _dot(x_lo, w1b[:half, :]) + _dot(x_hi, w1b[half:, :])
        h3 = _dot(x_lo, w3b[:half, :]) + _dot(x_hi, w3b[half:, :])
        hb = jax.nn.silu(h1) * h3
        y_ref[...] = _pack_bf16_pairs(_dot(hb.astype(BF16), w2b[...]))


def _expert_mlps(xs, block_e, n_used, next_e, w1, w3, w2, layer):
    npad, half = xs.shape
    d, ff = w1.shape[2], w1.shape[3]
    assert d == 2 * half
    bm = MOE_BLOCK
    rows = lambda b, be, nb, ne: (jnp.minimum(b, nb[0] - 1), 0)
    hbm = pl.BlockSpec(memory_space=pl.ANY)
    grid_spec = pltpu.PrefetchScalarGridSpec(
        num_scalar_prefetch=3,
        grid=(npad // bm,),
        in_specs=[pl.BlockSpec((bm, half), rows), hbm, hbm, hbm],
        out_specs=pl.BlockSpec((bm, half), rows),
        scratch_shapes=[pltpu.VMEM((d, ff), F32), pltpu.VMEM((d, ff), F32), pltpu.VMEM((ff, d), F32),
                        pltpu.VMEM((d, ff), BF16), pltpu.VMEM((d, ff), BF16), pltpu.VMEM((ff, d), BF16),
                        pltpu.SemaphoreType.DMA((3,))],
    )
    return pl.pallas_call(
        functools.partial(_expert_kernel, layer=layer),
        grid_spec=grid_spec,
        out_shape=jax.ShapeDtypeStruct((npad, half), jnp.int32),
        compiler_params=_cparams(1),
        name="expert_mlps",
    )(block_e, n_used, next_e, xs, w1, w3, w2)


def _moe(a2p, route_t, counts, w1, w3, w2, layer, row_ranges):
    t = a2p.shape[0]
    bm = MOE_BLOCK
    n = 2 * t
    ints = route_t.astype(jnp.int32)
    cnt = counts[0, :N_EXPERTS].astype(jnp.int32)
    padded = ((cnt + bm - 1) // bm) * bm
    pend = jnp.cumsum(padded)
    pstart = pend - padded
    experts = jnp.arange(N_EXPERTS, dtype=jnp.int32)[None, :]
    slot = lambda e, rank: jnp.sum(jnp.where(e[:, None] == experts, pstart[None, :], 0), axis=1) + rank
    pos_first = slot(ints[R_E1], ints[R_RANK1])
    pos_second = slot(ints[R_E2], ints[R_RANK2])
    n_blocks = n // bm + N_EXPERTS
    npad = n_blocks * bm
    starts = jnp.arange(n_blocks, dtype=jnp.int32) * bm
    n_used = pend[-1:] // bm
    last_start = (n_used[0] - 1) * bm
    block_e = jnp.sum((pend[None, :] <= jnp.minimum(starts, last_start)[:, None]).astype(jnp.int32), axis=1)
    blk = jnp.arange(n_blocks, dtype=jnp.int32)
    later_other = (blk[None, :] > blk[:, None]) & (block_e[None, :] != block_e[:, None]) & (blk[None, :] < n_used[0])
    first_later = jnp.min(jnp.where(later_other, blk[None, :], n_blocks), axis=1)
    next_e = jnp.where(first_later < n_blocks, block_e[jnp.minimum(first_later, n_blocks - 1)], -1).astype(jnp.int32)
    xs = _scatter_rows_twice(a2p, pos_first, pos_second, npad)
    ys = _expert_mlps(xs, block_e, n_used, next_e, w1, w3, w2, layer)
    return [_gather_rows(ys, jnp.concatenate([pos_first[r0:r1], pos_second[r0:r1]])) for r0, r1 in row_ranges]


def _moe_combine(h, yp_first, yp_second, route, g2):
    w1 = route[:, R_W1:R_W1 + 1]
    w2 = route[:, R_W2:R_W2 + 1]
    wide = lambda p: jnp.concatenate(_unpack_pairs_f32(p), axis=1)
    return h + g2 * (w1 * wide(yp_first) + w2 * wide(yp_second))


def _gmlp_kernel(h_ref, ya_ref, yb_ref, rt_ref, modp_ref, mod_ref, g1_ref, win_ref, bin_ref, vg_ref, ws_ref, bsb_ref,
                 wout_ref, g2_ref, wr_ref, br_ref, ltri_ref,
                 h1_ref, a2_ref, route_ref, rtt_ref, cnt_ref, carry_ref, *, n_ctx_tiles, tiles_per_req):
    i = pl.program_id(0)
    seq = _seq_of_tile(i, n_ctx_tiles, tiles_per_req)

    @pl.when(i == 0)
    def _():
        carry_ref[...] = jnp.zeros_like(carry_ref)

    gw = vg_ref.shape[1]
    cg = gw // GM_GROUPS
    subs = [pl.ds(s * GMLP_TILE, GMLP_TILE) for s in range(h_ref.shape[0] // GMLP_TILE)]
    hs = [_moe_combine(h_ref[rs, :], ya_ref[rs, :], yb_ref[rs, :], rt_ref[rs, :], _mod_row(modp_ref, 5, seq))
          for rs in subs]
    acts = [_modulated(h, g1_ref[...], _mod_row(mod_ref, 0, seq), _mod_row(mod_ref, 1, seq)).astype(BF16) for h in hs]
    vs = [_gelu_tanh(_dot(a, win_ref[:, gw:]) + bin_ref[:, gw:]) for a in acts]
    vbs = [((v * lax.rsqrt(jnp.mean(v * v, axis=-1, keepdims=True) + EPS)) * vg_ref[...]).astype(BF16) for v in vs]
    us = [_gelu_tanh(_dot(a, win_ref[:, :gw]) + bin_ref[:, :gw]) for a in acts]
    vms = []
    for vb in vbs:
        rows = []
        for c in range(GMLP_TILE // CHUNK):
            cols = [_dot(ws_ref[g], vb[c * CHUNK:(c + 1) * CHUNK, g * cg:(g + 1) * cg]) for g in range(GM_GROUPS)]
            rows.append(jnp.concatenate(cols, axis=1) + bsb_ref[...])
        vms.append(jnp.concatenate(rows, axis=0))
    mixes = [_dot((u * vm).astype(BF16), wout_ref[...]) for u, vm in zip(us, vms)]
    h1s = [h + _mod_row(mod_ref, 2, seq) * mix for h, mix in zip(hs, mixes)]
    a2s = [_modulated(h1, g2_ref[...], _mod_row(mod_ref, 3, seq), _mod_row(mod_ref, 4, seq)) for h1 in h1s]
    carry = carry_ref[...]
    for rs, h1, a2 in zip(subs, h1s, a2s):
        h1_ref[rs, :] = h1
        a2_ref[rs, :] = _pack_bf16_pairs(a2)
        slab, carry = _route(a2, wr_ref[...], br_ref[...], ltri_ref[...], carry)
        route_ref[rs, :] = slab
        rtt_ref[:, rs] = slab.T[:R_ROWS, :]
    carry_ref[...] = carry
    cnt_ref[...] = carry


def _gmlp_layer(h, y2, route_prev, mod_prev, mod, g1, win, bin_, vg, ws, bsb, wout, g2, router,
                n_ctx, rows_per_req):
    t, d = h.shape
    tm = GMLP_TILE
    wr, br = router
    ltri = _lower_tri(GMLP_TILE)
    full = lambda shape: pl.BlockSpec(shape, lambda i: (0,) * len(shape), pipeline_mode=pl.Buffered(1))
    row = lambda w: pl.BlockSpec((tm, w), lambda i: (i, 0))
    kern = functools.partial(_gmlp_kernel, n_ctx_tiles=n_ctx // tm, tiles_per_req=rows_per_req // tm)
    args = (h, y2, y2, route_prev, mod_prev, mod, g1, win, bin_, vg, ws, bsb, wout, g2, wr, br, ltri)
    second = pl.BlockSpec((tm, d // 2), lambda i: (i + t // tm, 0))
    in_specs = [row(d), row(d // 2), second, row(LANES)] + [full(a.shape) for a in args[4:]]
    return pl.pallas_call(
        kern,
        grid=(t // tm,),
        in_specs=in_specs,
        out_specs=[row(d), row(d // 2), row(LANES), pl.BlockSpec((R_ROWS, tm), lambda i: (0, i)),
                   pl.BlockSpec((1, LANES), lambda i: (0, 0))],
        out_shape=[jax.ShapeDtypeStruct((t, d), F32), jax.ShapeDtypeStruct((t, d // 2), jnp.int32),
                   jax.ShapeDtypeStruct((t, LANES), F32), jax.ShapeDtypeStruct((R_ROWS, t), F32),
                   jax.ShapeDtypeStruct((1, LANES), F32)],
        scratch_shapes=[pltpu.VMEM((1, LANES), F32)],
        compiler_params=_cparams(1),
        name="gmlp_router",
    )(*args)


def _final_kernel(h_ref, ya_ref, yb_ref, rt_ref, mod_ref, o_ref, *, first_tile, n_ctx_tiles, tiles_per_req):
    seq = _seq_of_tile(first_tile + pl.program_id(0), n_ctx_tiles, tiles_per_req)
    o_ref[...] = _moe_combine(h_ref[...], ya_ref[...], yb_ref[...], rt_ref[...], _mod_row(mod_ref, 5, seq))


def _final_combine(h, y2, route, mod, first_row, n_ctx, rows_per_req):
    d = h.shape[1]
    n = y2.shape[0] // 2
    tm = FINAL_TILE
    first_tile = first_row // tm
    row = lambda w: pl.BlockSpec((tm, w), lambda i: (first_tile + i, 0))
    kern = functools.partial(_final_kernel, first_tile=first_tile, n_ctx_tiles=n_ctx // tm,
                             tiles_per_req=rows_per_req // tm)
    return pl.pallas_call(
        kern,
        grid=(n // tm,),
        in_specs=[row(d), pl.BlockSpec((tm, d // 2), lambda i: (i, 0)),
                  pl.BlockSpec((tm, d // 2), lambda i: (i + n // tm, 0)),
                  row(LANES), pl.BlockSpec(mod.shape, lambda i: (0, 0, 0))],
        out_specs=pl.BlockSpec((tm, d), lambda i: (i, 0)),
        out_shape=jax.ShapeDtypeStruct((n, d), F32),
        compiler_params=_cparams(1),
        name="final_combine",
    )(h, y2, y2, route, mod)


def kernel(x_prompt, x_sample, cache_k, cache_v, c, c_ctx, ada_w, ada_b, norm1_g, norm2_g, attn_wq, attn_wk,
           attn_wv, attn_wo, attn_qnorm, attn_knorm, attn_lam, attn_subln, gm_win, gm_bin, gm_vnorm, gm_ws,
           gm_bs, gm_wout, moe_wc, moe_bc, moe_wf, moe_bf, moe_w1, moe_w3, moe_w2):
    batch, seq_len, d = x_prompt.shape
    n_req, req_len, _ = x_sample.shape
    depth = ada_w.shape[0]
    assert depth == 2 and attn_wq.shape[0] == 1 and gm_win.shape[0] == 1
    n_ctx = batch * seq_len
    tiles = (PRE_ATTN_TILE, ROW_TILE * POST_ATTN_SUBTILES, GMLP_TILE, ATTN_Q_TILE, FINAL_TILE)
    assert all(n_ctx % tm == 0 and req_len % tm == 0 for tm in tiles) and GMLP_TILE % CHUNK == 0
    past = cache_k.shape[2]

    h_ctx = x_prompt.reshape(n_ctx, d)
    h_lat = x_sample.reshape(n_req * req_len, d)
    cvec = jnp.concatenate([c_ctx[None, :], c, jnp.zeros((8 - 1 - n_req, d), F32)], axis=0)
    mod = _ada_mod(cvec, ada_w, ada_b)
    routers = [_router_weights(moe_wc[i], moe_bc[i], moe_wf[i], moe_bf[i]) for i in range(depth)]
    row_vec = lambda v: v.reshape(1, -1)

    wqkv = jnp.concatenate([attn_wq[0], attn_wk[0], attn_wv[0]], axis=1).astype(BF16)
    reps = d // HEAD_DIM
    cache_k_t = jnp.transpose(cache_k[:, 0], (0, 2, 3, 4, 1)).reshape(n_req, N_HEADS, 2 * HEAD_DIM, past)
    qk_gains = (row_vec(jnp.tile(attn_qnorm[0], reps)), row_vec(jnp.tile(attn_knorm[0], reps)))
    *qkv_ctx, k_new, v_new = _pre_attn(h_ctx, mod[0], row_vec(norm1_g[0]), wqkv, *qk_gains, ctx_len=seq_len)
    qkv_lat = _pre_attn(h_lat, mod[0], row_vec(norm1_g[0]), wqkv, *qk_gains, rows_per_req=req_len)
    o_ctx, o_lat = _attention(qkv_ctx, qkv_lat, attn_lam[0], row_vec(attn_subln[0]),
                              cache_k_t, cache_v[:, 0].reshape(n_req, past, d),
                              seq_len, n_req, req_len, layer=0)
    h1, a2, route0, route0_t, cnt0 = _post_attn(h_ctx, h_lat, o_ctx, o_lat, mod[0], attn_wo[0].astype(BF16),
                                      row_vec(norm2_g[0]), routers[0], req_len)
    t = n_ctx + n_req * req_len
    (y2,) = _moe(a2, route0_t, cnt0, moe_w1, moe_w3, moe_w2, 0, [(0, t)])

    gw = gm_vnorm.shape[1]
    bsb = jnp.repeat(gm_bs[0].T, gw // GM_GROUPS, axis=1)
    h2, a2, route1, route1_t, cnt1 = _gmlp_layer(h1, y2, route0, mod[0], mod[1], row_vec(norm1_g[1]), gm_win[0].astype(BF16),
                                       row_vec(gm_bin[0]), row_vec(gm_vnorm[0]), gm_ws[0].astype(BF16), bsb,
                                       gm_wout[0].astype(BF16), row_vec(norm2_g[1]), routers[1], n_ctx, req_len)
    y2_ctx, y2_lat = _moe(a2, route1_t, cnt1, moe_w1, moe_w3, moe_w2, 1, [(0, n_ctx), (n_ctx, t)])
    y_ctx = _final_combine(h2, y2_ctx, route1, mod[1], 0, n_ctx, req_len)
    y_lat = _final_combine(h2, y2_lat, route1, mod[1], n_ctx, n_ctx, req_len)

    y_prompt = y_ctx.reshape(batch, seq_len, d)
    y_sample = y_lat.reshape(n_req, req_len, d)
    new_cache_k = jnp.transpose(k_new.reshape(batch, 1, N_HEADS, 2, HEAD_DIM, seq_len), (0, 1, 5, 2, 3, 4))
    new_cache_v = v_new.reshape(batch, 1, seq_len, N_HEADS, V_DIM)
    return (y_prompt, y_sample, new_cache_k, new_cache_v)
```

```python
import functools
import math

import jax
import jax.numpy as jnp
from jax import lax
from jax.experimental import pallas as pl
from jax.experimental.pallas import tpu as pltpu
from jax.experimental.pallas import tpu_sc as plsc

F32 = jnp.float32
BF16 = jnp.bfloat16

N_HEADS = 8
HEAD_DIM = 64
V_DIM = 2 * HEAD_DIM
GRID_W = 64
ROPE_THETA = 10000.0
CHUNK = 128
GM_GROUPS = 8
N_EXPERT_GROUPS = 4
EXPERTS_PER_GROUP = 8
N_EXPERTS = N_EXPERT_GROUPS * EXPERTS_PER_GROUP
EPS = 1e-6

LANES = 128
MXU_TILE = 256
ROW_TILE = 256
PRE_ATTN_TILE = 512
POST_ATTN_SUBTILES = 4
GMLP_TILE = 512
FINAL_TILE = 512
MOE_BLOCK = 256
ATTN_Q_TILE = 512
ATTN_HEADS_PER_STEP = 4
SC_WINDOW_BYTES = 128 * 1024
VMEM_LIMIT = 56 * 1024 * 1024

R_E1, R_E2, R_W1, R_W2, R_RANK1, R_RANK2 = 0, 1, 2, 3, 4, 5
R_ROWS = 8
L_EXPERT0 = N_EXPERT_GROUPS


def _lambda_init(layer):
    return 0.8 - 0.6 * math.exp(-0.3 * layer)


def _cparams(n_axes):
    return pltpu.CompilerParams(dimension_semantics=("arbitrary",) * n_axes, vmem_limit_bytes=VMEM_LIMIT)


def _seq_of_tile(i, n_ctx_tiles, tiles_per_req):
    return jnp.where(i < n_ctx_tiles, 0, 1 + (i - n_ctx_tiles) // tiles_per_req)


def _split_specs(tm, d, n_ctx_tiles):
    return [pl.BlockSpec((tm, d), lambda i: (jnp.minimum(i, n_ctx_tiles - 1), 0)),
            pl.BlockSpec((tm, d), lambda i: (jnp.maximum(i - n_ctx_tiles, 0), 0))]


def _mod_row(mod_ref, part, seq):
    return mod_ref[part, pl.ds(seq, 1), :]


def _modulated(x, g, shift, scale):
    y = x * lax.rsqrt(jnp.mean(x * x, axis=-1, keepdims=True) + EPS)
    return (y * g) * (1.0 + scale) + shift


def _split_bf16(x):
    hi = x.astype(BF16)
    lo = (x - hi.astype(F32)).astype(BF16)
    return hi, lo


def _gelu_tanh(x):
    c = math.sqrt(2.0 / math.pi)
    hx = 0.5 * x
    return hx + hx * jnp.tanh(x * (c + (c * 0.044715) * (x * x)))


def _dot(a, b):
    return jnp.dot(a, b, preferred_element_type=F32)


def _pack_bf16_pairs(x):
    half = x.shape[1] // 2
    bits = lax.bitcast_convert_type(x.astype(BF16).astype(F32), jnp.uint32)
    packed = (bits[:, :half] >> 16) | bits[:, half:]
    return lax.bitcast_convert_type(packed, jnp.int32)


def _unpack_pairs_f32(p):
    u = lax.bitcast_convert_type(p, jnp.uint32)
    return lax.bitcast_convert_type(u << 16, F32), lax.bitcast_convert_type(u & jnp.uint32(0xFFFF0000), F32)


def _unpack_bf16_pairs(p):
    lo, hi = _unpack_pairs_f32(p)
    return lo.astype(BF16), hi.astype(BF16)


def _ada_kernel(c_ref, w_ref, b_ref, o_ref):
    c = c_ref[...]
    s = c * jax.nn.sigmoid(c)
    o_ref[...] = _dot(s.astype(BF16), w_ref[...].astype(BF16)) + b_ref[...]


def _ada_mod(cvec, ada_w, ada_b):
    depth, d, d6 = ada_w.shape
    parts = d6 // d
    rows = cvec.shape[0]
    return pl.pallas_call(
        _ada_kernel,
        grid=(depth, parts),
        in_specs=[
            pl.BlockSpec((rows, d), lambda l, j: (0, 0)),
            pl.BlockSpec((None, d, d), lambda l, j: (l, 0, j)),
            pl.BlockSpec((None, 1, d), lambda l, j: (l, 0, j)),
        ],
        out_specs=pl.BlockSpec((None, None, rows, d), lambda l, j: (l, j, 0, 0)),
        out_shape=jax.ShapeDtypeStruct((depth, parts, rows, d), F32),
        compiler_params=_cparams(2),
        name="ada_mod",
    )(cvec, ada_w, ada_b.reshape(depth, 1, d6))


def _pre_attn_kernel(*refs, latent, tiles_per_req):
    if latent:
        h_ref, mod_ref, g_ref, w_ref, qg_ref, kg_ref, gs_ref, cos_ref, sin_ref, q_ref, k_ref, v_ref = refs
        seq = 1 + pl.program_id(0) // tiles_per_req
    else:
        h_ref, mod_ref, g_ref, w_ref, qg_ref, kg_ref, gs_ref, q_ref, k_ref, v_ref, kf_ref, vf_ref = refs
        seq = 0
    d = h_ref.shape[1]
    q_scale = HEAD_DIM ** -0.5 * math.log2(math.e)
    subs = [pl.ds(s * ROW_TILE, ROW_TILE) for s in range(q_ref.shape[0] // ROW_TILE)]

    def group_rms_inv(x):
        sq_hi, sq_lo = _split_bf16(x * x)
        w = gs_ref.shape[0]
        sums = [_dot(sq_hi[:, c:c + w], gs_ref[...]) + _dot(sq_lo[:, c:c + w], gs_ref[...]) for c in range(0, d, w)]
        return lax.rsqrt(jnp.concatenate(sums, axis=1) * (1.0 / HEAD_DIM) + EPS)

    def rope(x, rows):
        cos = cos_ref[rows, :]
        sin = sin_ref[rows, :]
        lane = lax.broadcasted_iota(jnp.int32, cos.shape, 1)
        first = (lane & 31) < 16
        outs = []
        for hh in range(d // LANES):
            xs = x[:, hh * LANES:(hh + 1) * LANES]
            rot = jnp.where(first, pltpu.roll(xs, LANES - 16, 1), pltpu.roll(xs, 16, 1))
            outs.append(xs * cos + rot * sin)
        return jnp.concatenate(outs, axis=1)

    acts = [_modulated(h_ref[rows, :], g_ref[...], _mod_row(mod_ref, 0, seq), _mod_row(mod_ref, 1, seq)).astype(BF16)
            for rows in subs]
    qkvs = [_dot(a, w_ref[...]) for a in acts]
    qns = [(qkv[:, :d] * group_rms_inv(qkv[:, :d])) * qg_ref[...] for qkv in qkvs]
    kns = [(qkv[:, d:2 * d] * group_rms_inv(qkv[:, d:2 * d])) * kg_ref[...] for qkv in qkvs]
    if latent:
        qrs = [rope(qn, rows) for qn, rows in zip(qns, subs)]
        krs = [rope(kn, rows) for kn, rows in zip(kns, subs)]
    else:
        qrs, krs = qns, kns
    for rows, qkv, qr, kr in zip(subs, qkvs, qrs, krs):
        q_ref[rows, :] = (qr * q_scale).astype(BF16)
        k_ref[rows, :] = kr.astype(BF16)
        v_ref[rows, :] = qkv[:, 2 * d:].astype(BF16)
    if not latent:
        kn_all = jnp.concatenate(kns, axis=0)
        n_pos = kf_ref.shape[3]
        for r in range(kf_ref.shape[0]):
            kf_ref[r] = kn_all[r * n_pos:(r + 1) * n_pos, :].T.reshape(kf_ref.shape[1:])
        for rows, qkv in zip(subs, qkvs):
            vf_ref[rows, :] = qkv[:, 2 * d:]


def _rope_tables(n_pos):
    half = HEAD_DIM // 2
    pos = jnp.arange(n_pos, dtype=jnp.int32)
    row = (pos // GRID_W).astype(F32)
    col = (pos % GRID_W).astype(F32)
    inv = 1.0 / (ROPE_THETA ** (jnp.arange(0, half, 2, dtype=F32) / half))
    ang_r = row[:, None] * inv[None, :]
    ang_c = col[:, None] * inv[None, :]
    ang = jnp.concatenate([ang_r, ang_r, ang_c, ang_c], axis=-1)
    quarter = half // 2
    sign = jnp.tile(jnp.concatenate([-jnp.ones((quarter,), F32), jnp.ones((quarter,), F32)]), 2)
    cos = jnp.tile(jnp.cos(ang), (1, 2))
    sin = jnp.tile(jnp.sin(ang) * sign[None, :], (1, 2))
    return cos, sin


def _pre_attn(h, mod, g, wqkv, qg, kg, *, ctx_len=None, rows_per_req=None):
    n, d = h.shape
    latent = rows_per_req is not None
    tm = PRE_ATTN_TILE
    gid = jnp.arange(MXU_TILE, dtype=jnp.int32) // HEAD_DIM
    gs = (gid[:, None] == gid[None, :]).astype(BF16)
    full = lambda shape: pl.BlockSpec(shape, lambda i: (0,) * len(shape), pipeline_mode=pl.Buffered(1))
    row = pl.BlockSpec((tm, d), lambda i: (i, 0))
    args = [h, mod, g, wqkv, qg, kg, gs]
    in_specs = [row] + [full(a.shape) for a in args[1:]]
    out_specs = [row, row, row]
    out_shape = [jax.ShapeDtypeStruct((n, d), BF16)] * 3
    if latent:
        tiles_per_req = rows_per_req // tm
        pos = pl.BlockSpec((tm, LANES), lambda i: (i % tiles_per_req, 0))
        args += list(_rope_tables(rows_per_req))
        in_specs += [pos, pos]
    else:
        assert tm % ctx_len == 0
        tiles_per_req = None
        out_specs += [pl.BlockSpec((tm // ctx_len, d // HEAD_DIM, HEAD_DIM, ctx_len), lambda i: (i, 0, 0, 0)), row]
        out_shape += [jax.ShapeDtypeStruct((n // ctx_len, d // HEAD_DIM, HEAD_DIM, ctx_len), F32),
                      jax.ShapeDtypeStruct((n, d), F32)]
    return pl.pallas_call(
        functools.partial(_pre_attn_kernel, latent=latent, tiles_per_req=tiles_per_req),
        grid=(n // tm,),
        in_specs=in_specs,
        out_specs=out_specs,
        out_shape=out_shape,
        compiler_params=_cparams(1),
        name="pre_attn_latent" if latent else "pre_attn_ctx",
    )(*args)


def _attn_kernel(*refs, n_heads, has_cache, layer):
    if has_cache:
        lam_ref, sub_ref, q_ref, kn_ref, vn_ref, kc_ref, vc_ref, o_ref = refs
    else:
        lam_ref, sub_ref, q_ref, kn_ref, vn_ref, o_ref = refs
    lf = lam_ref[...]
    lam = (jnp.exp(jnp.sum(lf[0:1] * lf[1:2], axis=-1, keepdims=True))
           - jnp.exp(jnp.sum(lf[2:3] * lf[3:4], axis=-1, keepdims=True)) + _lambda_init(layer))
    nt = (((1,), (1,)), ((), ()))
    tq = q_ref.shape[0]
    lane = lax.broadcasted_iota(jnp.int32, (tq, LANES), 1)
    heads = [slice(hh * LANES, (hh + 1) * LANES) for hh in range(n_heads)]

    def qk(hh, sl):
        qh = q_ref[:, sl]
        zero = jnp.zeros_like(qh)
        kn = kn_ref[:, sl]
        kc_t = kc_ref[hh].astype(BF16) if has_cache else None
        out = []
        for qm in (jnp.where(lane < HEAD_DIM, qh, zero), jnp.where(lane >= HEAD_DIM, qh, zero)):
            s = [lax.dot_general(qm, kn, nt, preferred_element_type=F32)]
            out.append(s + [_dot(qm, kc_t)] if has_cache else s)
        return out

    def soft(scores):
        maxes = [functools.reduce(jnp.maximum, [jnp.max(x, axis=-1, keepdims=True) for x in s]) for s in scores]
        exps = [[jnp.exp2(x - m) for x in s] for s, m in zip(scores, maxes)]
        r0, r1 = [1.0 / functools.reduce(lambda u, w: u + w, [jnp.sum(x, axis=-1, keepdims=True) for x in e])
                  for e in exps]
        return [(x0 * r0 - lam * (x1 * r1)).astype(BF16) for x0, x1 in zip(*exps)]

    def pv(sl, a):
        vals = [vn_ref[:, sl]] + ([vc_ref[:, sl].astype(BF16)] if has_cache else [])
        o = functools.reduce(lambda u, w: u + w, [_dot(aj, vv) for aj, vv in zip(a, vals)])
        o = o * lax.rsqrt(jnp.mean(o * o, axis=-1, keepdims=True) + EPS)
        o_ref[:, sl] = ((o * sub_ref[...]) * (1.0 - _lambda_init(layer))).astype(BF16)

    scores = [qk(hh, sl) for hh, sl in enumerate(heads)]
    mixed = [soft(sc) for sc in scores]
    for sl, a in zip(heads, mixed):
        pv(sl, a)


def _attention(qkv_ctx, qkv_lat, lam_p, subln, cache_k, cache_v, ctx_len, n_req, req_len, layer):
    n_ctx, d = qkv_ctx[0].shape
    small = lambda shape: pl.BlockSpec(shape, lambda *_: (0,) * len(shape))
    n_ctx_req = n_ctx // ctx_len
    blk = pl.BlockSpec((ctx_len, d), lambda b: (b, 0))
    o_ctx = pl.pallas_call(
        functools.partial(_attn_kernel, n_heads=N_HEADS, has_cache=False, layer=layer),
        grid=(n_ctx_req,),
        in_specs=[small(lam_p.shape), small(subln.shape), blk, blk, blk],
        out_specs=blk,
        out_shape=jax.ShapeDtypeStruct((n_ctx, d), BF16),
        compiler_params=_cparams(1),
        name="attn_ctx",
    )(lam_p, subln, *qkv_ctx)

    tq = ATTN_Q_TILE
    nq = req_len // tq
    past = cache_v.shape[1]
    hp = ATTN_HEADS_PER_STEP
    small3 = lambda shape: pl.BlockSpec(shape, lambda b, h, i: (0,) * len(shape))
    qspec = pl.BlockSpec((tq, hp * LANES), lambda b, h, i: (b * nq + i, h))
    kvspec = pl.BlockSpec((req_len, hp * LANES), lambda b, h, i: (b, h))
    cspec = pl.BlockSpec((None, past, hp * LANES), lambda b, h, i: (b, 0, h))
    ckspec = pl.BlockSpec((None, hp, LANES, past), lambda b, h, i: (b, h, 0, 0))
    o_lat = pl.pallas_call(
        functools.partial(_attn_kernel, n_heads=hp, has_cache=True, layer=layer),
        grid=(n_req, N_HEADS // hp, nq),
        in_specs=[small3(lam_p.shape), pl.BlockSpec((1, LANES), lambda b, h, i: (0, 0)),
                  qspec, kvspec, kvspec, ckspec, cspec],
        out_specs=pl.BlockSpec((tq, hp * LANES), lambda b, h, i: (b * nq + i, h)),
        out_shape=jax.ShapeDtypeStruct((n_req * req_len, d), BF16),
        compiler_params=_cparams(3),
        name="attn_latent",
    )(lam_p, subln, *qkv_lat, cache_k, cache_v)
    return o_ctx, o_lat


def _route(a2, wr, br, ltri, carry):
    tm = a2.shape[0]
    logits = _dot(a2.astype(BF16), wr) + br
    lane_i = lax.broadcasted_iota(jnp.int32, (tm, LANES), 1)
    lane = lane_i.astype(F32)
    neg = jnp.full((tm, LANES), -jnp.inf, F32)
    big = jnp.full((tm, LANES), float(LANES), F32)
    first_lane = lambda mask: jnp.min(jnp.where(mask, lane, big), axis=-1, keepdims=True)

    lc = jnp.where(lane_i < N_EXPERT_GROUPS, logits, neg)
    mc = jnp.max(lc, axis=-1, keepdims=True)
    pg = 1.0 / jnp.sum(jnp.exp(lc - mc), axis=-1, keepdims=True)
    gi = first_lane(lc == mc)
    assert EXPERTS_PER_GROUP == 8
    grp = lax.shift_right_arithmetic(lane_i - L_EXPERT0, 3).astype(F32)
    in_group = (lane_i >= L_EXPERT0) & (lane_i < L_EXPERT0 + N_EXPERTS) & (grp == gi)
    ls = jnp.where(in_group, logits, neg)
    t1 = jnp.max(ls, axis=-1, keepdims=True)
    i1 = first_lane(ls == t1)
    ls2 = jnp.where(lane == i1, neg, ls)
    t2 = jnp.max(ls2, axis=-1, keepdims=True)
    i2 = first_lane(ls2 == t2)
    ex = jnp.exp(t2 - t1)
    w1 = pg * (1.0 / (1.0 + ex))
    w2 = pg * (ex / (1.0 + ex))
    e1 = i1 - float(L_EXPERT0)
    e2 = i2 - float(L_EXPERT0)
    oh1 = lane == e1
    oh2 = lane == e2
    onehot = oh1.astype(F32) + oh2.astype(F32)
    before = _dot(ltri, onehot.astype(BF16)) + carry
    zero = jnp.zeros_like(before)
    rank1 = jnp.sum(jnp.where(oh1, before, zero), axis=-1, keepdims=True)
    rank2 = jnp.sum(jnp.where(oh2, before, zero), axis=-1, keepdims=True)
    slab = jnp.zeros((tm, LANES), F32)
    for ln, val in ((R_E1, e1), (R_E2, e2), (R_W1, w1), (R_W2, w2), (R_RANK1, rank1), (R_RANK2, rank2)):
        slab = jnp.where(lane_i == ln, val, slab)
    return slab, carry + jnp.sum(onehot, axis=0, keepdims=True)


def _router_weights(wc, bc, wf, bf_):
    d = wc.shape[0]
    pad = LANES - N_EXPERT_GROUPS - N_EXPERTS
    w = jnp.concatenate([wc, wf, jnp.zeros((d, pad), F32)], axis=1)
    b = jnp.concatenate([bc, bf_, jnp.zeros((pad,), F32)])[None, :]
    return w.astype(BF16), b


def _lower_tri(n):
    r = jnp.arange(n, dtype=jnp.int32)
    return (r[None, :] < r[:, None]).astype(BF16)


def _post_attn_kernel(hc_ref, hl_ref, oc_ref, ol_ref, mod_ref, wo_ref, g2_ref, wr_ref, br_ref, ltri_ref,
                      h1_ref, a2_ref, route_ref, rtt_ref, cnt_ref, carry_ref, *, n_ctx_tiles, tiles_per_req):
    i = pl.program_id(0)
    seq = _seq_of_tile(i, n_ctx_tiles, tiles_per_req)

    @pl.when(i == 0)
    def _():
        carry_ref[...] = jnp.zeros_like(carry_ref)

    subs = [pl.ds(s * ROW_TILE, ROW_TILE) for s in range(h1_ref.shape[0] // ROW_TILE)]
    mixes = [_dot(jnp.where(i < n_ctx_tiles, oc_ref[rows, :], ol_ref[rows, :]), wo_ref[...]) for rows in subs]
    h1s = [jnp.where(i < n_ctx_tiles, hc_ref[rows, :], hl_ref[rows, :]) + _mod_row(mod_ref, 2, seq) * mix
           for rows, mix in zip(subs, mixes)]
    a2s = [_modulated(h1, g2_ref[...], _mod_row(mod_ref, 3, seq), _mod_row(mod_ref, 4, seq)) for h1 in h1s]
    for rows, h1, a2 in zip(subs, h1s, a2s):
        h1_ref[rows, :] = h1
        a2_ref[rows, :] = _pack_bf16_pairs(a2)
    carry = carry_ref[...]
    for rows, a2 in zip(subs, a2s):
        slab, carry = _route(a2, wr_ref[...], br_ref[...], ltri_ref[...], carry)
        route_ref[rows, :] = slab
        rtt_ref[:, rows] = slab.T[:R_ROWS, :]
    carry_ref[...] = carry
    cnt_ref[...] = carry


def _post_attn(h_ctx, h_lat, o_ctx, o_lat, mod, wo, g2, router, rows_per_req):
    n_ctx, d = h_ctx.shape
    t = n_ctx + h_lat.shape[0]
    tm = ROW_TILE * POST_ATTN_SUBTILES
    wr, br = router
    ltri = _lower_tri(ROW_TILE)
    full = lambda shape: pl.BlockSpec(shape, lambda i: (0,) * len(shape))
    row = pl.BlockSpec((tm, d), lambda i: (i, 0))
    kern = functools.partial(_post_attn_kernel, n_ctx_tiles=n_ctx // tm, tiles_per_req=rows_per_req // tm)
    return pl.pallas_call(
        kern,
        grid=(t // tm,),
        in_specs=_split_specs(tm, d, n_ctx // tm) + _split_specs(tm, d, n_ctx // tm) + [
                  full(mod.shape), full(wo.shape), full((1, d)), full(wr.shape),
                  full(br.shape), full(ltri.shape)],
        out_specs=[row, pl.BlockSpec((tm, d // 2), lambda i: (i, 0)), pl.BlockSpec((tm, LANES), lambda i: (i, 0)),
                   pl.BlockSpec((R_ROWS, tm), lambda i: (0, i)), full((1, LANES))],
        out_shape=[jax.ShapeDtypeStruct((t, d), F32), jax.ShapeDtypeStruct((t, d // 2), jnp.int32),
                   jax.ShapeDtypeStruct((t, LANES), F32), jax.ShapeDtypeStruct((R_ROWS, t), F32),
                   jax.ShapeDtypeStruct((1, LANES), F32)],
        scratch_shapes=[pltpu.VMEM((1, LANES), F32)],
        compiler_params=_cparams(1),
        name="post_attn_router",
    )(h_ctx, h_lat, o_ctx, o_lat, mod, wo, g2, wr, br, ltri)


def _gather_rows(src, idx):
    n = idx.shape[0]
    d = src.shape[1]
    w = _sc_window_rows(src)
    mesh = plsc.VectorSubcoreMesh(core_axis_name="core", subcore_axis_name="subcore")

    @pl.kernel(out_type=jax.ShapeDtypeStruct((n, d), src.dtype), mesh=mesh)
    def gather(src_hbm, idx_hbm, out_hbm):
        def body(idx_vmem, out_vmem):
            pltpu.sync_copy(src_hbm.at[idx_vmem.at[0, pl.ds(0, w)]], out_vmem)

        pltpu.emit_pipeline(
            body,
            grid=(n // w,),
            in_specs=[pl.BlockSpec((1, LANES), lambda i: (i, 0))],
            out_specs=[pl.BlockSpec((w, d), lambda i: (i, 0))],
            core_axis_name=("core", "subcore"),
            dimension_semantics=(pltpu.PARALLEL,),
        )(idx_hbm, out_hbm)

    return gather(src, _index_windows(idx, w))


def _sc_window_rows(src):
    return min(LANES, SC_WINDOW_BYTES // (src.shape[1] * src.dtype.itemsize))


def _index_windows(idx, w):
    return jnp.pad(idx.reshape(idx.shape[0] // w, w), ((0, 0), (0, LANES - w)))


def _scatter_rows_twice(src, idx_a, idx_b, n_out):
    t, d = src.shape
    w = _sc_window_rows(src)
    mesh = plsc.VectorSubcoreMesh(core_axis_name="core", subcore_axis_name="subcore")

    @pl.kernel(out_type=jax.ShapeDtypeStruct((n_out, d), src.dtype), mesh=mesh)
    def scatter(src_hbm, ia_hbm, ib_hbm, out_hbm):
        def body(src_vmem, ia_vmem, ib_vmem):
            pltpu.sync_copy(src_vmem, out_hbm.at[ia_vmem.at[0, pl.ds(0, w)]])
            pltpu.sync_copy(src_vmem, out_hbm.at[ib_vmem.at[0, pl.ds(0, w)]])

        pltpu.emit_pipeline(
            body,
            grid=(t // w,),
            in_specs=[pl.BlockSpec((w, d), lambda i: (i, 0)),
                      pl.BlockSpec((1, LANES), lambda i: (i, 0)),
                      pl.BlockSpec((1, LANES), lambda i: (i, 0))],
            out_specs=[],
            core_axis_name=("core", "subcore"),
            dimension_semantics=(pltpu.PARALLEL,),
        )(src_hbm, ia_hbm, ib_hbm)

    return scatter(src, _index_windows(idx_a, w), _index_windows(idx_b, w))


def _expert_kernel(be_ref, nb_ref, ne_ref, x_ref, w1_hbm, w3_hbm, w2_hbm, y_ref,
                   w1s, w3s, w2s, w1b, w3b, w2b, sems, *, layer):
    b = pl.program_id(0)
    prev = be_ref[jnp.maximum(b - 1, 0)]
    used = b < nb_ref[0]

    def stage(e):
        return [pltpu.make_async_copy(src.at[layer, e], dst, sems.at[j])
                for j, (src, dst) in enumerate(((w1_hbm, w1s), (w3_hbm, w3s), (w2_hbm, w2s)))]

    @pl.when(used & ((b == 0) | (be_ref[b] != prev)))
    def _():
        @pl.when(b == 0)
        def _():
            for c in stage(be_ref[0]):
                c.start()

        for c in stage(be_ref[b]):
            c.wait()
        w1b[...] = w1s[...].astype(BF16)
        w3b[...] = w3s[...].astype(BF16)
        w2b[...] = w2s[...].astype(BF16)

        @pl.when(ne_ref[b] >= 0)
        def _():
            for c in stage(ne_ref[b]):
                c.start()

    @pl.when(used)
    def _():
        x_lo, x_hi = _unpack_bf16_pairs(x_ref[...])
        half = x_lo.shape[1]
        h1 = _dot(x_lo, w1b[:half, :]) + _dot(x_hi, w1b[half:, :])
        h3 = _dot(x_lo, w3b[:half, :]) + _dot(x_hi, w3b[half:, :])
        hb = jax.nn.silu(h1) * h3
        y_ref[...] = _pack_bf16_pairs(_dot(hb.astype(BF16), w2b[...]))


def _expert_mlps(xs, block_e, n_used, next_e, w1, w3, w2, layer):
    npad, half = xs.shape
    d, ff = w1.shape[2], w1.shape[3]
    assert d == 2 * half
    bm = MOE_BLOCK
    rows = lambda b, be, nb, ne: (jnp.minimum(b, nb[0] - 1), 0)
    hbm = pl.BlockSpec(memory_space=pl.ANY)
    grid_spec = pltpu.PrefetchScalarGridSpec(
        num_scalar_prefetch=3,
        grid=(npad // bm,),
        in_specs=[pl.BlockSpec((bm, half), rows), hbm, hbm, hbm],
        out_specs=pl.BlockSpec((bm, half), rows),
        scratch_shapes=[pltpu.VMEM((d, ff), F32), pltpu.VMEM((d, ff), F32), pltpu.VMEM((ff, d), F32),
                        pltpu.VMEM((d, ff), BF16), pltpu.VMEM((d, ff), BF16), pltpu.VMEM((ff, d), BF16),
                        pltpu.SemaphoreType.DMA((3,))],
    )
    return pl.pallas_call(
        functools.partial(_expert_kernel, layer=layer),
        grid_spec=grid_spec,
        out_shape=jax.ShapeDtypeStruct((npad, half), jnp.int32),
        compiler_params=_cparams(1),
        name="expert_mlps",
    )(block_e, n_used, next_e, xs, w1, w3, w2)


def _moe(a2p, route_t, counts, w1, w3, w2, layer, row_ranges):
    t = a2p.shape[0]
    bm = MOE_BLOCK
    n = 2 * t
    ints = route_t.astype(jnp.int32)
    cnt = counts[0, :N_EXPERTS].astype(jnp.int32)
    padded = ((cnt + bm - 1) // bm) * bm
    pend = jnp.cumsum(padded)
    pstart = pend - padded
    experts = jnp.arange(N_EXPERTS, dtype=jnp.int32)[None, :]
    slot = lambda e, rank: jnp.sum(jnp.where(e[:, None] == experts, pstart[None, :], 0), axis=1) + rank
    pos_first = slot(ints[R_E1], ints[R_RANK1])
    pos_second = slot(ints[R_E2], ints[R_RANK2])
    n_blocks = n // bm + N_EXPERTS
    npad = n_blocks * bm
    starts = jnp.arange(n_blocks, dtype=jnp.int32) * bm
    n_used = pend[-1:] // bm
    last_start = (n_used[0] - 1) * bm
    block_e = jnp.sum((pend[None, :] <= jnp.minimum(starts, last_start)[:, None]).astype(jnp.int32), axis=1)
    blk = jnp.arange(n_blocks, dtype=jnp.int32)
    later_other = (blk[None, :] > blk[:, None]) & (block_e[None, :] != block_e[:, None]) & (blk[None, :] < n_used[0])
    first_later = jnp.min(jnp.where(later_other, blk[None, :], n_blocks), axis=1)
    next_e = jnp.where(first_later < n_blocks, block_e[jnp.minimum(first_later, n_blocks - 1)], -1).astype(jnp.int32)
    xs = _scatter_rows_twice(a2p, pos_first, pos_second, npad)
    ys = _expert_mlps(xs, block_e, n_used, next_e, w1, w3, w2, layer)
    return [_gather_rows(ys, jnp.concatenate([pos_first[r0:r1], pos_second[r0:r1]])) for r0, r1 in row_ranges]


def _moe_combine(h, yp_first, yp_second, route, g2):
    w1 = route[:, R_W1:R_W1 + 1]
    w2 = route[:, R_W2:R_W2 + 1]
    wide = lambda p: jnp.concatenate(_unpack_pairs_f32(p), axis=1)
    return h + g2 * (w1 * wide(yp_first) + w2 * wide(yp_second))


def _gmlp_kernel(h_ref, ya_ref, yb_ref, rt_ref, modp_ref, mod_ref, g1_ref, win_ref, bin_ref, vg_ref, ws_ref, bsb_ref,
                 wout_ref, g2_ref, wr_ref, br_ref, ltri_ref,
                 h1_ref, a2_ref, route_ref, rtt_ref, cnt_ref, carry_ref, *, n_ctx_tiles, tiles_per_req):
    i = pl.program_id(0)
    seq = _seq_of_tile(i, n_ctx_tiles, tiles_per_req)

    @pl.when(i == 0)
    def _():
        carry_ref[...] = jnp.zeros_like(carry_ref)

    gw = vg_ref.shape[1]
    cg = gw // GM_GROUPS
    subs = [pl.ds(s * GMLP_TILE, GMLP_TILE) for s in range(h_ref.shape[0] // GMLP_TILE)]
    hs = [_moe_combine(h_ref[rs, :], ya_ref[rs, :], yb_ref[rs, :], rt_ref[rs, :], _mod_row(modp_ref, 5, seq))
          for rs in subs]
    acts = [_modulated(h, g1_ref[...], _mod_row(mod_ref, 0, seq), _mod_row(mod_ref, 1, seq)).astype(BF16) for h in hs]
    vs = [_gelu_tanh(_dot(a, win_ref[:, gw:]) + bin_ref[:, gw:]) for a in acts]
    vbs = [((v * lax.rsqrt(jnp.mean(v * v, axis=-1, keepdims=True) + EPS)) * vg_ref[...]).astype(BF16) for v in vs]
    us = [_gelu_tanh(_dot(a, win_ref[:, :gw]) + bin_ref[:, :gw]) for a in acts]
    vms = []
    for vb in vbs:
        rows = []
        for c in range(GMLP_TILE // CHUNK):
            cols = [_dot(ws_ref[g], vb[c * CHUNK:(c + 1) * CHUNK, g * cg:(g + 1) * cg]) for g in range(GM_GROUPS)]
            rows.append(jnp.concatenate(cols, axis=1) + bsb_ref[...])
        vms.append(jnp.concatenate(rows, axis=0))
    mixes = [_dot((u * vm).astype(BF16), wout_ref[...]) for u, vm in zip(us, vms)]
    h1s = [h + _mod_row(mod_ref, 2, seq) * mix for h, mix in zip(hs, mixes)]
    a2s = [_modulated(h1, g2_ref[...], _mod_row(mod_ref, 3, seq), _mod_row(mod_ref, 4, seq)) for h1 in h1s]
    carry = carry_ref[...]
    for rs, h1, a2 in zip(subs, h1s, a2s):
        h1_ref[rs, :] = h1
        a2_ref[rs, :] = _pack_bf16_pairs(a2)
        slab, carry = _route(a2, wr_ref[...], br_ref[...], ltri_ref[...], carry)
        route_ref[rs, :] = slab
        rtt_ref[:, rs] = slab.T[:R_ROWS, :]
    carry_ref[...] = carry
    cnt_ref[...] = carry


def _gmlp_layer(h, y2, route_prev, mod_prev, mod, g1, win, bin_, vg, ws, bsb, wout, g2, router,
                n_ctx, rows_per_req):
    t, d = h.shape
    tm = GMLP_TILE
    wr, br = router
    ltri = _lower_tri(GMLP_TILE)
    full = lambda shape: pl.BlockSpec(shape, lambda i: (0,) * len(shape), pipeline_mode=pl.Buffered(1))
    row = lambda w: pl.BlockSpec((tm, w), lambda i: (i, 0))
    kern = functools.partial(_gmlp_kernel, n_ctx_tiles=n_ctx // tm, tiles_per_req=rows_per_req // tm)
    args = (h, y2, y2, route_prev, mod_prev, mod, g1, win, bin_, vg, ws, bsb, wout, g2, wr, br, ltri)
    second = pl.BlockSpec((tm, d // 2), lambda i: (i + t // tm, 0))
    in_specs = [row(d), row(d // 2), second, row(LANES)] + [full(a.shape) for a in args[4:]]
    return pl.pallas_call(
        kern,
        grid=(t // tm,),
        in_specs=in_specs,
        out_specs=[row(d), row(d // 2), row(LANES), pl.BlockSpec((R_ROWS, tm), lambda i: (0, i)),
                   pl.BlockSpec((1, LANES), lambda i: (0, 0))],
        out_shape=[jax.ShapeDtypeStruct((t, d), F32), jax.ShapeDtypeStruct((t, d // 2), jnp.int32),
                   jax.ShapeDtypeStruct((t, LANES), F32), jax.ShapeDtypeStruct((R_ROWS, t), F32),
                   jax.ShapeDtypeStruct((1, LANES), F32)],
        scratch_shapes=[pltpu.VMEM((1, LANES), F32)],
        compiler_params=_cparams(1),
        name="gmlp_router",
    )(*args)


def _final_kernel(h_hbm, y_hbm, rt_hbm, g2_hbm, o_hbm, *, first_tile, n_tiles, tm, n_ctx_tiles, tiles_per_req):
    d = h_hbm.shape[1]

    def body(h_ref, ya_ref, yb_ref, rt_ref, g2_ref, o_ref):
        o_ref[...] = _moe_combine(h_ref[...], ya_ref[...], yb_ref[...], rt_ref[...], g2_ref[0])

    deep = pl.Buffered(3)
    seq = lambda i: _seq_of_tile(first_tile + i, n_ctx_tiles, tiles_per_req)
    pltpu.emit_pipeline(
        body,
        grid=(n_tiles,),
        in_specs=[pl.BlockSpec((tm, d), lambda i: (first_tile + i, 0), pipeline_mode=deep),
                  pl.BlockSpec((tm, d // 2), lambda i: (i, 0), pipeline_mode=deep),
                  pl.BlockSpec((tm, d // 2), lambda i: (i + n_tiles, 0), pipeline_mode=deep),
                  pl.BlockSpec((tm, LANES), lambda i: (first_tile + i, 0), pipeline_mode=deep),
                  pl.BlockSpec((1, 1, d), lambda i: (seq(i), 0, 0))],
        out_specs=[pl.BlockSpec((tm, d), lambda i: (i, 0))],
    )(h_hbm, y_hbm, y_hbm, rt_hbm, g2_hbm, o_hbm)


def _final_combine(h, y2, route, mod, first_row, n_ctx, rows_per_req):
    d = h.shape[1]
    n = y2.shape[0] // 2
    tm = FINAL_TILE
    kern = functools.partial(_final_kernel, first_tile=first_row // tm, n_tiles=n // tm, tm=tm,
                             n_ctx_tiles=n_ctx // tm, tiles_per_req=rows_per_req // tm)
    anywhere = pl.BlockSpec(memory_space=pl.ANY)
    return pl.pallas_call(
        kern,
        in_specs=[anywhere] * 4,
        out_specs=anywhere,
        out_shape=jax.ShapeDtypeStruct((n, d), F32),
        compiler_params=pltpu.CompilerParams(vmem_limit_bytes=VMEM_LIMIT),
        name="final_combine",
    )(h, y2, route, mod[5].reshape(mod.shape[1], 1, d))


def kernel(x_prompt, x_sample, cache_k, cache_v, c, c_ctx, ada_w, ada_b, norm1_g, norm2_g, attn_wq, attn_wk,
           attn_wv, attn_wo, attn_qnorm, attn_knorm, attn_lam, attn_subln, gm_win, gm_bin, gm_vnorm, gm_ws,
           gm_bs, gm_wout, moe_wc, moe_bc, moe_wf, moe_bf, moe_w1, moe_w3, moe_w2):
    batch, seq_len, d = x_prompt.shape
    n_req, req_len, _ = x_sample.shape
    depth = ada_w.shape[0]
    assert depth == 2 and attn_wq.shape[0] == 1 and gm_win.shape[0] == 1
    n_ctx = batch * seq_len
    tiles = (PRE_ATTN_TILE, ROW_TILE * POST_ATTN_SUBTILES, GMLP_TILE, ATTN_Q_TILE, FINAL_TILE)
    assert all(n_ctx % tm == 0 and req_len % tm == 0 for tm in tiles) and GMLP_TILE % CHUNK == 0
    past = cache_k.shape[2]

    h_ctx = x_prompt.reshape(n_ctx, d)
    h_lat = x_sample.reshape(n_req * req_len, d)
    cvec = jnp.concatenate([c_ctx[None, :], c, jnp.zeros((8 - 1 - n_req, d), F32)], axis=0)
    mod = _ada_mod(cvec, ada_w, ada_b)
    routers = [_router_weights(moe_wc[i], moe_bc[i], moe_wf[i], moe_bf[i]) for i in range(depth)]
    row_vec = lambda v: v.reshape(1, -1)

    wqkv = jnp.concatenate([attn_wq[0], attn_wk[0], attn_wv[0]], axis=1).astype(BF16)
    reps = d // HEAD_DIM
    cache_k_t = jnp.transpose(cache_k[:, 0], (0, 2, 3, 4, 1)).reshape(n_req, N_HEADS, 2 * HEAD_DIM, past)
    qk_gains = (row_vec(jnp.tile(attn_qnorm[0], reps)), row_vec(jnp.tile(attn_knorm[0], reps)))
    *qkv_ctx, k_new, v_new = _pre_attn(h_ctx, mod[0], row_vec(norm1_g[0]), wqkv, *qk_gains, ctx_len=seq_len)
    qkv_lat = _pre_attn(h_lat, mod[0], row_vec(norm1_g[0]), wqkv, *qk_gains, rows_per_req=req_len)
    o_ctx, o_lat = _attention(qkv_ctx, qkv_lat, attn_lam[0], row_vec(attn_subln[0]),
                              cache_k_t, cache_v[:, 0].reshape(n_req, past, d),
                              seq_len, n_req, req_len, layer=0)
    h1, a2, route0, route0_t, cnt0 = _post_attn(h_ctx, h_lat, o_ctx, o_lat, mod[0], attn_wo[0].astype(BF16),
                                      row_vec(norm2_g[0]), routers[0], req_len)
    t = n_ctx + n_req * req_len
    (y2,) = _moe(a2, route0_t, cnt0, moe_w1, moe_w3, moe_w2, 0, [(0, t)])

    gw = gm_vnorm.shape[1]
    bsb = jnp.repeat(gm_bs[0].T, gw // GM_GROUPS, axis=1)
    h2, a2, route1, route1_t, cnt1 = _gmlp_layer(h1, y2, route0, mod[0], mod[1], row_vec(norm1_g[1]), gm_win[0].astype(BF16),
                                       row_vec(gm_bin[0]), row_vec(gm_vnorm[0]), gm_ws[0].astype(BF16), bsb,
                                       gm_wout[0].astype(BF16), row_vec(norm2_g[1]), routers[1], n_ctx, req_len)
    y2_ctx, y2_lat = _moe(a2, route1_t, cnt1, moe_w1, moe_w3, moe_w2, 1, [(0, n_ctx), (n_ctx, t)])
    y_ctx = _final_combine(h2, y2_ctx, route1, mod[1], 0, n_ctx, req_len)
    y_lat = _final_combine(h2, y2_lat, route1, mod[1], n_ctx, n_ctx, req_len)

    y_prompt = y_ctx.reshape(batch, seq_len, d)
    y_sample = y_lat.reshape(n_req, req_len, d)
    new_cache_k = jnp.transpose(k_new.reshape(batch, 1, N_HEADS, 2, HEAD_DIM, seq_len), (0, 1, 5, 2, 3, 4))
    new_cache_v = v_new.reshape(batch, 1, seq_len, N_HEADS, V_DIM)
    return (y_prompt, y_sample, new_cache_k, new_cache_v)
```

```python
import functools
import math

import jax
import jax.numpy as jnp
from jax import lax
from jax.experimental import pallas as pl
from jax.experimental.pallas import tpu as pltpu
from jax.experimental.pallas import tpu_sc as plsc

F32 = jnp.float32
BF16 = jnp.bfloat16

N_HEADS = 8
HEAD_DIM = 64
V_DIM = 2 * HEAD_DIM
GRID_W = 64
ROPE_THETA = 10000.0
CHUNK = 128
GM_GROUPS = 8
N_EXPERT_GROUPS = 4
EXPERTS_PER_GROUP = 8
N_EXPERTS = N_EXPERT_GROUPS * EXPERTS_PER_GROUP
EPS = 1e-6

LANES = 128
MXU_TILE = 256
ROW_TILE = 256
PRE_ATTN_TILE = 512
POST_ATTN_SUBTILES = 4
GMLP_TILE = 512
FINAL_TILE = 512
MOE_BLOCK = 256
ATTN_Q_TILE = 512
ATTN_HEADS_PER_STEP = 4
SC_WINDOW_BYTES = 128 * 1024
VMEM_LIMIT = 56 * 1024 * 1024

R_E1, R_E2, R_W1, R_W2, R_RANK1, R_RANK2 = 0, 1, 2, 3, 4, 5
R_ROWS = 8
L_EXPERT0 = N_EXPERT_GROUPS


def _lambda_init(layer):
    return 0.8 - 0.6 * math.exp(-0.3 * layer)


def _cparams(n_axes):
    return pltpu.CompilerParams(dimension_semantics=("arbitrary",) * n_axes, vmem_limit_bytes=VMEM_LIMIT)


def _seq_of_tile(i, n_ctx_tiles, tiles_per_req):
    return jnp.where(i < n_ctx_tiles, 0, 1 + (i - n_ctx_tiles) // tiles_per_req)


def _split_specs(tm, d, n_ctx_tiles):
    return [pl.BlockSpec((tm, d), lambda i: (jnp.minimum(i, n_ctx_tiles - 1), 0)),
            pl.BlockSpec((tm, d), lambda i: (jnp.maximum(i - n_ctx_tiles, 0), 0))]


def _mod_row(mod_ref, part, seq):
    return mod_ref[part, pl.ds(seq, 1), :]


def _modulated(x, g, shift, scale):
    y = x * lax.rsqrt(jnp.mean(x * x, axis=-1, keepdims=True) + EPS)
    return (y * g) * (1.0 + scale) + shift


def _split_bf16(x):
    hi = x.astype(BF16)
    lo = (x - hi.astype(F32)).astype(BF16)
    return hi, lo


def _gelu_tanh(x):
    c = math.sqrt(2.0 / math.pi)
    hx = 0.5 * x
    return hx + hx * jnp.tanh(x * (c + (c * 0.044715) * (x * x)))


def _dot(a, b):
    return jnp.dot(a, b, preferred_element_type=F32)


def _pack_bf16_pairs(x):
    half = x.shape[1] // 2
    bits = lax.bitcast_convert_type(x.astype(BF16).astype(F32), jnp.uint32)
    packed = (bits[:, :half] >> 16) | bits[:, half:]
    return lax.bitcast_convert_type(packed, jnp.int32)


def _unpack_pairs_f32(p):
    u = lax.bitcast_convert_type(p, jnp.uint32)
    return lax.bitcast_convert_type(u << 16, F32), lax.bitcast_convert_type(u & jnp.uint32(0xFFFF0000), F32)


def _unpack_bf16_pairs(p):
    lo, hi = _unpack_pairs_f32(p)
    return lo.astype(BF16), hi.astype(BF16)


def _ada_kernel(c_ref, w_ref, b_ref, o_ref):
    c = c_ref[...]
    s = c * jax.nn.sigmoid(c)
    o_ref[...] = _dot(s.astype(BF16), w_ref[...].astype(BF16)) + b_ref[...]


def _ada_mod(cvec, ada_w, ada_b):
    depth, d, d6 = ada_w.shape
    parts = d6 // d
    rows = cvec.shape[0]
    return pl.pallas_call(
        _ada_kernel,
        grid=(depth, parts),
        in_specs=[
            pl.BlockSpec((rows, d), lambda l, j: (0, 0)),
            pl.BlockSpec((None, d, d), lambda l, j: (l, 0, j)),
            pl.BlockSpec((None, 1, d), lambda l, j: (l, 0, j)),
        ],
        out_specs=pl.BlockSpec((None, None, rows, d), lambda l, j: (l, j, 0, 0)),
        out_shape=jax.ShapeDtypeStruct((depth, parts, rows, d), F32),
        compiler_params=_cparams(2),
        name="ada_mod",
    )(cvec, ada_w, ada_b.reshape(depth, 1, d6))


def _pre_attn_kernel(*refs, latent, tiles_per_req):
    if latent:
        h_ref, mod_ref, g_ref, w_ref, qg_ref, kg_ref, gs_ref, cos_ref, sin_ref, q_ref, k_ref, v_ref = refs
        seq = 1 + pl.program_id(0) // tiles_per_req
    else:
        h_ref, mod_ref, g_ref, w_ref, qg_ref, kg_ref, gs_ref, q_ref, k_ref, v_ref, kf_ref, vf_ref = refs
        seq = 0
    d = h_ref.shape[1]
    q_scale = HEAD_DIM ** -0.5 * math.log2(math.e)
    subs = [pl.ds(s * ROW_TILE, ROW_TILE) for s in range(q_ref.shape[0] // ROW_TILE)]

    def group_rms_inv(x):
        sq_hi, sq_lo = _split_bf16(x * x)
        w = gs_ref.shape[0]
        sums = [_dot(sq_hi[:, c:c + w], gs_ref[...]) + _dot(sq_lo[:, c:c + w], gs_ref[...]) for c in range(0, d, w)]
        return lax.rsqrt(jnp.concatenate(sums, axis=1) * (1.0 / HEAD_DIM) + EPS)

    def rope(x, rows):
        cos = cos_ref[rows, :]
        sin = sin_ref[rows, :]
        lane = lax.broadcasted_iota(jnp.int32, cos.shape, 1)
        first = (lane & 31) < 16
        outs = []
        for hh in range(d // LANES):
            xs = x[:, hh * LANES:(hh + 1) * LANES]
            rot = jnp.where(first, pltpu.roll(xs, LANES - 16, 1), pltpu.roll(xs, 16, 1))
            outs.append(xs * cos + rot * sin)
        return jnp.concatenate(outs, axis=1)

    acts = [_modulated(h_ref[rows, :], g_ref[...], _mod_row(mod_ref, 0, seq), _mod_row(mod_ref, 1, seq)).astype(BF16)
            for rows in subs]
    qkvs = [_dot(a, w_ref[...]) for a in acts]
    qns = [(qkv[:, :d] * group_rms_inv(qkv[:, :d])) * qg_ref[...] for qkv in qkvs]
    kns = [(qkv[:, d:2 * d] * group_rms_inv(qkv[:, d:2 * d])) * kg_ref[...] for qkv in qkvs]
    if latent:
        qrs = [rope(qn, rows) for qn, rows in zip(qns, subs)]
        krs = [rope(kn, rows) for kn, rows in zip(kns, subs)]
    else:
        qrs, krs = qns, kns
    for rows, qkv, qr, kr in zip(subs, qkvs, qrs, krs):
        q_ref[rows, :] = (qr * q_scale).astype(BF16)
        k_ref[rows, :] = kr.astype(BF16)
        v_ref[rows, :] = qkv[:, 2 * d:].astype(BF16)
    if not latent:
        kn_all = jnp.concatenate(kns, axis=0)
        n_pos = kf_ref.shape[3]
        for r in range(kf_ref.shape[0]):
            kf_ref[r] = kn_all[r * n_pos:(r + 1) * n_pos, :].T.reshape(kf_ref.shape[1:])
        for rows, qkv in zip(subs, qkvs):
            vf_ref[rows, :] = qkv[:, 2 * d:]


def _rope_tables(n_pos):
    half = HEAD_DIM // 2
    pos = jnp.arange(n_pos, dtype=jnp.int32)
    row = (pos // GRID_W).astype(F32)
    col = (pos % GRID_W).astype(F32)
    inv = 1.0 / (ROPE_THETA ** (jnp.arange(0, half, 2, dtype=F32) / half))
    ang_r = row[:, None] * inv[None, :]
    ang_c = col[:, None] * inv[None, :]
    ang = jnp.concatenate([ang_r, ang_r, ang_c, ang_c], axis=-1)
    quarter = half // 2
    sign = jnp.tile(jnp.concatenate([-jnp.ones((quarter,), F32), jnp.ones((quarter,), F32)]), 2)
    cos = jnp.tile(jnp.cos(ang), (1, 2))
    sin = jnp.tile(jnp.sin(ang) * sign[None, :], (1, 2))
    return cos, sin


def _pre_attn(h, mod, g, wqkv, qg, kg, *, ctx_len=None, rows_per_req=None):
    n, d = h.shape
    latent = rows_per_req is not None
    tm = PRE_ATTN_TILE
    gid = jnp.arange(MXU_TILE, dtype=jnp.int32) // HEAD_DIM
    gs = (gid[:, None] == gid[None, :]).astype(BF16)
    full = lambda shape: pl.BlockSpec(shape, lambda i: (0,) * len(shape), pipeline_mode=pl.Buffered(1))
    row = pl.BlockSpec((tm, d), lambda i: (i, 0))
    args = [h, mod, g, wqkv, qg, kg, gs]
    in_specs = [row] + [full(a.shape) for a in args[1:]]
    out_specs = [row, row, row]
    out_shape = [jax.ShapeDtypeStruct((n, d), BF16)] * 3
    if latent:
        tiles_per_req = rows_per_req // tm
        pos = pl.BlockSpec((tm, LANES), lambda i: (i % tiles_per_req, 0))
        args += list(_rope_tables(rows_per_req))
        in_specs += [pos, pos]
    else:
        assert tm % ctx_len == 0
        tiles_per_req = None
        out_specs += [pl.BlockSpec((tm // ctx_len, d // HEAD_DIM, HEAD_DIM, ctx_len), lambda i: (i, 0, 0, 0)), row]
        out_shape += [jax.ShapeDtypeStruct((n // ctx_len, d // HEAD_DIM, HEAD_DIM, ctx_len), F32),
                      jax.ShapeDtypeStruct((n, d), F32)]
    return pl.pallas_call(
        functools.partial(_pre_attn_kernel, latent=latent, tiles_per_req=tiles_per_req),
        grid=(n // tm,),
        in_specs=in_specs,
        out_specs=out_specs,
        out_shape=out_shape,
        compiler_params=_cparams(1),
        name="pre_attn_latent" if latent else "pre_attn_ctx",
    )(*args)


def _attn_kernel(*refs, n_heads, has_cache, layer):
    if has_cache:
        lam_ref, sub_ref, q_ref, kn_ref, vn_ref, kc_ref, vc_ref, o_ref = refs
    else:
        lam_ref, sub_ref, q_ref, kn_ref, vn_ref, o_ref = refs
    lf = lam_ref[...]
    lam = (jnp.exp(jnp.sum(lf[0:1] * lf[1:2], axis=-1, keepdims=True))
           - jnp.exp(jnp.sum(lf[2:3] * lf[3:4], axis=-1, keepdims=True)) + _lambda_init(layer))
    nt = (((1,), (1,)), ((), ()))
    tq = q_ref.shape[0]
    lane = lax.broadcasted_iota(jnp.int32, (tq, LANES), 1)
    heads = [slice(hh * LANES, (hh + 1) * LANES) for hh in range(n_heads)]

    def qk(hh, sl):
        qh = q_ref[:, sl]
        zero = jnp.zeros_like(qh)
        kn = kn_ref[:, sl]
        kc_t = kc_ref[hh].astype(BF16) if has_cache else None
        out = []
        for qm in (jnp.where(lane < HEAD_DIM, qh, zero), jnp.where(lane >= HEAD_DIM, qh, zero)):
            s = [lax.dot_general(qm, kn, nt, preferred_element_type=F32)]
            out.append(s + [_dot(qm, kc_t)] if has_cache else s)
        return out

    def soft(scores):
        maxes = [functools.reduce(jnp.maximum, [jnp.max(x, axis=-1, keepdims=True) for x in s]) for s in scores]
        exps = [[jnp.exp2(x - m) for x in s] for s, m in zip(scores, maxes)]
        r0, r1 = [1.0 / functools.reduce(lambda u, w: u + w, [jnp.sum(x, axis=-1, keepdims=True) for x in e])
                  for e in exps]
        return [(x0 * r0 - lam * (x1 * r1)).astype(BF16) for x0, x1 in zip(*exps)]

    def pv(sl, a):
        vals = [vn_ref[:, sl]] + ([vc_ref[:, sl].astype(BF16)] if has_cache else [])
        o = functools.reduce(lambda u, w: u + w, [_dot(aj, vv) for aj, vv in zip(a, vals)])
        o = o * lax.rsqrt(jnp.mean(o * o, axis=-1, keepdims=True) + EPS)
        o_ref[:, sl] = ((o * sub_ref[...]) * (1.0 - _lambda_init(layer))).astype(BF16)

    scores = [qk(hh, sl) for hh, sl in enumerate(heads)]
    mixed = [soft(sc) for sc in scores]
    for sl, a in zip(heads, mixed):
        pv(sl, a)


def _attention(qkv_ctx, qkv_lat, lam_p, subln, cache_k, cache_v, ctx_len, n_req, req_len, layer):
    n_ctx, d = qkv_ctx[0].shape
    small = lambda shape: pl.BlockSpec(shape, lambda *_: (0,) * len(shape))
    n_ctx_req = n_ctx // ctx_len
    blk = pl.BlockSpec((ctx_len, d), lambda b: (b, 0))
    o_ctx = pl.pallas_call(
        functools.partial(_attn_kernel, n_heads=N_HEADS, has_cache=False, layer=layer),
        grid=(n_ctx_req,),
        in_specs=[small(lam_p.shape), small(subln.shape), blk, blk, blk],
        out_specs=blk,
        out_shape=jax.ShapeDtypeStruct((n_ctx, d), BF16),
        compiler_params=_cparams(1),
        name="attn_ctx",
    )(lam_p, subln, *qkv_ctx)

    tq = ATTN_Q_TILE
    nq = req_len // tq
    past = cache_v.shape[1]
    hp = ATTN_HEADS_PER_STEP
    small3 = lambda shape: pl.BlockSpec(shape, lambda b, h, i: (0,) * len(shape))
    qspec = pl.BlockSpec((tq, hp * LANES), lambda b, h, i: (b * nq + i, h))
    kvspec = pl.BlockSpec((req_len, hp * LANES), lambda b, h, i: (b, h))
    cspec = pl.BlockSpec((None, past, hp * LANES), lambda b, h, i: (b, 0, h))
    ckspec = pl.BlockSpec((None, hp, LANES, past), lambda b, h, i: (b, h, 0, 0))
    o_lat = pl.pallas_call(
        functools.partial(_attn_kernel, n_heads=hp, has_cache=True, layer=layer),
        grid=(n_req, N_HEADS // hp, nq),
        in_specs=[small3(lam_p.shape), pl.BlockSpec((1, LANES), lambda b, h, i: (0, 0)),
                  qspec, kvspec, kvspec, ckspec, cspec],
        out_specs=pl.BlockSpec((tq, hp * LANES), lambda b, h, i: (b * nq + i, h)),
        out_shape=jax.ShapeDtypeStruct((n_req * req_len, d), BF16),
        compiler_params=_cparams(3),
        name="attn_latent",
    )(lam_p, subln, *qkv_lat, cache_k, cache_v)
    return o_ctx, o_lat


def _route(a2, wr, br, ltri, carry):
    tm = a2.shape[0]
    logits = _dot(a2.astype(BF16), wr) + br
    lane_i = lax.broadcasted_iota(jnp.int32, (tm, LANES), 1)
    lane = lane_i.astype(F32)
    neg = jnp.full((tm, LANES), -jnp.inf, F32)
    big = jnp.full((tm, LANES), float(LANES), F32)
    first_lane = lambda mask: jnp.min(jnp.where(mask, lane, big), axis=-1, keepdims=True)

    lc = jnp.where(lane_i < N_EXPERT_GROUPS, logits, neg)
    mc = jnp.max(lc, axis=-1, keepdims=True)
    pg = 1.0 / jnp.sum(jnp.exp(lc - mc), axis=-1, keepdims=True)
    gi = first_lane(lc == mc)
    assert EXPERTS_PER_GROUP == 8
    grp = lax.shift_right_arithmetic(lane_i - L_EXPERT0, 3).astype(F32)
    in_group = (lane_i >= L_EXPERT0) & (lane_i < L_EXPERT0 + N_EXPERTS) & (grp == gi)
    ls = jnp.where(in_group, logits, neg)
    t1 = jnp.max(ls, axis=-1, keepdims=True)
    i1 = first_lane(ls == t1)
    ls2 = jnp.where(lane == i1, neg, ls)
    t2 = jnp.max(ls2, axis=-1, keepdims=True)
    i2 = first_lane(ls2 == t2)
    ex = jnp.exp(t2 - t1)
    w1 = pg * (1.0 / (1.0 + ex))
    w2 = pg * (ex / (1.0 + ex))
    e1 = i1 - float(L_EXPERT0)
    e2 = i2 - float(L_EXPERT0)
    oh1 = lane == e1
    oh2 = lane == e2
    onehot = oh1.astype(F32) + oh2.astype(F32)
    before = _dot(ltri, onehot.astype(BF16)) + carry
    zero = jnp.zeros_like(before)
    rank1 = jnp.sum(jnp.where(oh1, before, zero), axis=-1, keepdims=True)
    rank2 = jnp.sum(jnp.where(oh2, before, zero), axis=-1, keepdims=True)
    slab = jnp.zeros((tm, LANES), F32)
    for ln, val in ((R_E1, e1), (R_E2, e2), (R_W1, w1), (R_W2, w2), (R_RANK1, rank1), (R_RANK2, rank2)):
        slab = jnp.where(lane_i == ln, val, slab)
    return slab, carry + jnp.sum(onehot, axis=0, keepdims=True)


def _router_weights(wc, bc, wf, bf_):
    d = wc.shape[0]
    pad = LANES - N_EXPERT_GROUPS - N_EXPERTS
    w = jnp.concatenate([wc, wf, jnp.zeros((d, pad), F32)], axis=1)
    b = jnp.concatenate([bc, bf_, jnp.zeros((pad,), F32)])[None, :]
    return w.astype(BF16), b


def _lower_tri(n):
    r = jnp.arange(n, dtype=jnp.int32)
    return (r[None, :] < r[:, None]).astype(BF16)


def _post_attn_kernel(hc_ref, hl_ref, oc_ref, ol_ref, mod_ref, wo_ref, g2_ref, wr_ref, br_ref, ltri_ref,
                      h1_ref, a2_ref, route_ref, rtt_ref, cnt_ref, carry_ref, *, n_ctx_tiles, tiles_per_req):
    i = pl.program_id(0)
    seq = _seq_of_tile(i, n_ctx_tiles, tiles_per_req)

    @pl.when(i == 0)
    def _():
        carry_ref[...] = jnp.zeros_like(carry_ref)

    subs = [pl.ds(s * ROW_TILE, ROW_TILE) for s in range(h1_ref.shape[0] // ROW_TILE)]
    mixes = [_dot(jnp.where(i < n_ctx_tiles, oc_ref[rows, :], ol_ref[rows, :]), wo_ref[...]) for rows in subs]
    h1s = [jnp.where(i < n_ctx_tiles, hc_ref[rows, :], hl_ref[rows, :]) + _mod_row(mod_ref, 2, seq) * mix
           for rows, mix in zip(subs, mixes)]
    a2s = [_modulated(h1, g2_ref[...], _mod_row(mod_ref, 3, seq), _mod_row(mod_ref, 4, seq)) for h1 in h1s]
    for rows, h1, a2 in zip(subs, h1s, a2s):
        h1_ref[rows, :] = h1
        a2_ref[rows, :] = _pack_bf16_pairs(a2)
    slab, carry = _route(jnp.concatenate(a2s, axis=0), wr_ref[...], br_ref[...], ltri_ref[...], carry_ref[...])
    route_ref[...] = slab
    rtt_ref[...] = slab.T[:R_ROWS, :]
    carry_ref[...] = carry
    cnt_ref[...] = carry


def _post_attn(h_ctx, h_lat, o_ctx, o_lat, mod, wo, g2, router, rows_per_req):
    n_ctx, d = h_ctx.shape
    t = n_ctx + h_lat.shape[0]
    tm = ROW_TILE * POST_ATTN_SUBTILES
    wr, br = router
    ltri = _lower_tri(tm)
    full = lambda shape: pl.BlockSpec(shape, lambda i: (0,) * len(shape))
    row = pl.BlockSpec((tm, d), lambda i: (i, 0))
    kern = functools.partial(_post_attn_kernel, n_ctx_tiles=n_ctx // tm, tiles_per_req=rows_per_req // tm)
    return pl.pallas_call(
        kern,
        grid=(t // tm,),
        in_specs=_split_specs(tm, d, n_ctx // tm) + _split_specs(tm, d, n_ctx // tm) + [
                  full(mod.shape), full(wo.shape), full((1, d)), full(wr.shape),
                  full(br.shape), full(ltri.shape)],
        out_specs=[row, pl.BlockSpec((tm, d // 2), lambda i: (i, 0)), pl.BlockSpec((tm, LANES), lambda i: (i, 0)),
                   pl.BlockSpec((R_ROWS, tm), lambda i: (0, i)), full((1, LANES))],
        out_shape=[jax.ShapeDtypeStruct((t, d), F32), jax.ShapeDtypeStruct((t, d // 2), jnp.int32),
                   jax.ShapeDtypeStruct((t, LANES), F32), jax.ShapeDtypeStruct((R_ROWS, t), F32),
                   jax.ShapeDtypeStruct((1, LANES), F32)],
        scratch_shapes=[pltpu.VMEM((1, LANES), F32)],
        compiler_params=_cparams(1),
        name="post_attn_router",
    )(h_ctx, h_lat, o_ctx, o_lat, mod, wo, g2, wr, br, ltri)


def _gather_rows(src, idx):
    n = idx.shape[0]
    d = src.shape[1]
    w = _sc_window_rows(src)
    mesh = plsc.VectorSubcoreMesh(core_axis_name="core", subcore_axis_name="subcore")

    @pl.kernel(out_type=jax.ShapeDtypeStruct((n, d), src.dtype), mesh=mesh)
    def gather(src_hbm, idx_hbm, out_hbm):
        def body(idx_vmem, out_vmem):
            pltpu.sync_copy(src_hbm.at[idx_vmem.at[0, pl.ds(0, w)]], out_vmem)

        pltpu.emit_pipeline(
            body,
            grid=(n // w,),
            in_specs=[pl.BlockSpec((1, LANES), lambda i: (i, 0))],
            out_specs=[pl.BlockSpec((w, d), lambda i: (i, 0))],
            core_axis_name=("core", "subcore"),
            dimension_semantics=(pltpu.PARALLEL,),
        )(idx_hbm, out_hbm)

    return gather(src, _index_windows(idx, w))


def _sc_window_rows(src):
    return min(LANES, SC_WINDOW_BYTES // (src.shape[1] * src.dtype.itemsize))


def _index_windows(idx, w):
    return jnp.pad(idx.reshape(idx.shape[0] // w, w), ((0, 0), (0, LANES - w)))


def _scatter_rows_twice(src, idx_a, idx_b, n_out):
    t, d = src.shape
    w = _sc_window_rows(src)
    mesh = plsc.VectorSubcoreMesh(core_axis_name="core", subcore_axis_name="subcore")

    @pl.kernel(out_type=jax.ShapeDtypeStruct((n_out, d), src.dtype), mesh=mesh)
    def scatter(src_hbm, ia_hbm, ib_hbm, out_hbm):
        def body(src_vmem, ia_vmem, ib_vmem):
            pltpu.sync_copy(src_vmem, out_hbm.at[ia_vmem.at[0, pl.ds(0, w)]])
            pltpu.sync_copy(src_vmem, out_hbm.at[ib_vmem.at[0, pl.ds(0, w)]])

        pltpu.emit_pipeline(
            body,
            grid=(t // w,),
            in_specs=[pl.BlockSpec((w, d), lambda i: (i, 0)),
                      pl.BlockSpec((1, LANES), lambda i: (i, 0)),
                      pl.BlockSpec((1, LANES), lambda i: (i, 0))],
            out_specs=[],
            core_axis_name=("core", "subcore"),
            dimension_semantics=(pltpu.PARALLEL,),
        )(src_hbm, ia_hbm, ib_hbm)

    return scatter(src, _index_windows(idx_a, w), _index_windows(idx_b, w))


def _expert_kernel(be_ref, nb_ref, ne_ref, x_ref, w1_hbm, w3_hbm, w2_hbm, y_ref,
                   w1s, w3s, w2s, w1b, w3b, w2b, sems, *, layer):
    b = pl.program_id(0)
    prev = be_ref[jnp.maximum(b - 1, 0)]
    used = b < nb_ref[0]

    def stage(e):
        return [pltpu.make_async_copy(src.at[layer, e], dst, sems.at[j])
                for j, (src, dst) in enumerate(((w1_hbm, w1s), (w3_hbm, w3s), (w2_hbm, w2s)))]

    @pl.when(used & ((b == 0) | (be_ref[b] != prev)))
    def _():
        @pl.when(b == 0)
        def _():
            for c in stage(be_ref[0]):
                c.start()

        for c in stage(be_ref[b]):
            c.wait()
        w1b[...] = w1s[...].astype(BF16)
        w3b[...] = w3s[...].astype(BF16)
        w2b[...] = w2s[...].astype(BF16)

        @pl.when(ne_ref[b] >= 0)
        def _():
            for c in stage(ne_ref[b]):
                c.start()

    @pl.when(used)
    def _():
        x_lo, x_hi = _unpack_bf16_pairs(x_ref[...])
        half = x_lo.shape[1]
        h1 = _dot(x_lo, w1b[:half, :]) + _dot(x_hi, w1b[half:, :])
        h3 = _dot(x_lo, w3b[:half, :]) + _dot(x_hi, w3b[half:, :])
        hb = jax.nn.silu(h1) * h3
        y_ref[...] = _pack_bf16_pairs(_dot(hb.astype(BF16), w2b[...]))


def _expert_mlps(xs, block_e, n_used, next_e, w1, w3, w2, layer):
    npad, half = xs.shape
    d, ff = w1.shape[2], w1.shape[3]
    assert d == 2 * half
    bm = MOE_BLOCK
    rows = lambda b, be, nb, ne: (jnp.minimum(b, nb[0] - 1), 0)
    hbm = pl.BlockSpec(memory_space=pl.ANY)
    grid_spec = pltpu.PrefetchScalarGridSpec(
        num_scalar_prefetch=3,
        grid=(npad // bm,),
        in_specs=[pl.BlockSpec((bm, half), rows), hbm, hbm, hbm],
        out_specs=pl.BlockSpec((bm, half), rows),
        scratch_shapes=[pltpu.VMEM((d, ff), F32), pltpu.VMEM((d, ff), F32), pltpu.VMEM((ff, d), F32),
                        pltpu.VMEM((d, ff), BF16), pltpu.VMEM((d, ff), BF16), pltpu.VMEM((ff, d), BF16),
                        pltpu.SemaphoreType.DMA((3,))],
    )
    return pl.pallas_call(
        functools.partial(_expert_kernel, layer=layer),
        grid_spec=grid_spec,
        out_shape=jax.ShapeDtypeStruct((npad, half), jnp.int32),
        compiler_params=_cparams(1),
        name="expert_mlps",
    )(block_e, n_used, next_e, xs, w1, w3, w2)


def _moe(a2p, route_t, counts, w1, w3, w2, layer, row_ranges):
    t = a2p.shape[0]
    bm = MOE_BLOCK
    n = 2 * t
    ints = route_t.astype(jnp.int32)
    cnt = counts[0, :N_EXPERTS].astype(jnp.int32)
    padded = ((cnt + bm - 1) // bm) * bm
    pend = jnp.cumsum(padded)
    pstart = pend - padded
    experts = jnp.arange(N_EXPERTS, dtype=jnp.int32)[None, :]
    slot = lambda e, rank: jnp.sum(jnp.where(e[:, None] == experts, pstart[None, :], 0), axis=1) + rank
    pos_first = slot(ints[R_E1], ints[R_RANK1])
    pos_second = slot(ints[R_E2], ints[R_RANK2])
    n_blocks = n // bm + N_EXPERTS
    npad = n_blocks * bm
    starts = jnp.arange(n_blocks, dtype=jnp.int32) * bm
    n_used = pend[-1:] // bm
    last_start = (n_used[0] - 1) * bm
    block_e = jnp.sum((pend[None, :] <= jnp.minimum(starts, last_start)[:, None]).astype(jnp.int32), axis=1)
    blk = jnp.arange(n_blocks, dtype=jnp.int32)
    later_other = (blk[None, :] > blk[:, None]) & (block_e[None, :] != block_e[:, None]) & (blk[None, :] < n_used[0])
    first_later = jnp.min(jnp.where(later_other, blk[None, :], n_blocks), axis=1)
    next_e = jnp.where(first_later < n_blocks, block_e[jnp.minimum(first_later, n_blocks - 1)], -1).astype(jnp.int32)
    xs = _scatter_rows_twice(a2p, pos_first, pos_second, npad)
    ys = _expert_mlps(xs, block_e, n_used, next_e, w1, w3, w2, layer)
    return [_gather_rows(ys, jnp.concatenate([pos_first[r0:r1], pos_second[r0:r1]])) for r0, r1 in row_ranges]


def _moe_combine(h, yp_first, yp_second, route, g2):
    w1 = route[:, R_W1:R_W1 + 1]
    w2 = route[:, R_W2:R_W2 + 1]
    wide = lambda p: jnp.concatenate(_unpack_pairs_f32(p), axis=1)
    return h + g2 * (w1 * wide(yp_first) + w2 * wide(yp_second))


def _gmlp_kernel(h_ref, ya_ref, yb_ref, rt_ref, modp_ref, mod_ref, g1_ref, win_ref, bin_ref, vg_ref, ws_ref, bsb_ref,
                 wout_ref, g2_ref, wr_ref, br_ref, ltri_ref,
                 h1_ref, a2_ref, route_ref, rtt_ref, cnt_ref, carry_ref, *, n_ctx_tiles, tiles_per_req):
    i = pl.program_id(0)
    seq = _seq_of_tile(i, n_ctx_tiles, tiles_per_req)

    @pl.when(i == 0)
    def _():
        carry_ref[...] = jnp.zeros_like(carry_ref)

    gw = vg_ref.shape[1]
    cg = gw // GM_GROUPS
    subs = [pl.ds(s * GMLP_TILE, GMLP_TILE) for s in range(h_ref.shape[0] // GMLP_TILE)]
    hs = [_moe_combine(h_ref[rs, :], ya_ref[rs, :], yb_ref[rs, :], rt_ref[rs, :], _mod_row(modp_ref, 5, seq))
          for rs in subs]
    acts = [_modulated(h, g1_ref[...], _mod_row(mod_ref, 0, seq), _mod_row(mod_ref, 1, seq)).astype(BF16) for h in hs]
    vs = [_gelu_tanh(_dot(a, win_ref[:, gw:]) + bin_ref[:, gw:]) for a in acts]
    vbs = [((v * lax.rsqrt(jnp.mean(v * v, axis=-1, keepdims=True) + EPS)) * vg_ref[...]).astype(BF16) for v in vs]
    us = [_gelu_tanh(_dot(a, win_ref[:, :gw]) + bin_ref[:, :gw]) for a in acts]
    vms = []
    for vb in vbs:
        rows = []
        for c in range(GMLP_TILE // CHUNK):
            cols = [_dot(ws_ref[g], vb[c * CHUNK:(c + 1) * CHUNK, g * cg:(g + 1) * cg]) for g in range(GM_GROUPS)]
            rows.append(jnp.concatenate(cols, axis=1) + bsb_ref[...])
        vms.append(jnp.concatenate(rows, axis=0))
    mixes = [_dot((u * vm).astype(BF16), wout_ref[...]) for u, vm in zip(us, vms)]
    h1s = [h + _mod_row(mod_ref, 2, seq) * mix for h, mix in zip(hs, mixes)]
    a2s = [_modulated(h1, g2_ref[...], _mod_row(mod_ref, 3, seq), _mod_row(mod_ref, 4, seq)) for h1 in h1s]
    carry = carry_ref[...]
    for rs, h1, a2 in zip(subs, h1s, a2s):
        h1_ref[rs, :] = h1
        a2_ref[rs, :] = _pack_bf16_pairs(a2)
        slab, carry = _route(a2, wr_ref[...], br_ref[...], ltri_ref[...], carry)
        route_ref[rs, :] = slab
        rtt_ref[:, rs] = slab.T[:R_ROWS, :]
    carry_ref[...] = carry
    cnt_ref[...] = carry


def _gmlp_layer(h, y2, route_prev, mod_prev, mod, g1, win, bin_, vg, ws, bsb, wout, g2, router,
                n_ctx, rows_per_req):
    t, d = h.shape
    tm = GMLP_TILE
    wr, br = router
    ltri = _lower_tri(GMLP_TILE)
    full = lambda shape: pl.BlockSpec(shape, lambda i: (0,) * len(shape), pipeline_mode=pl.Buffered(1))
    row = lambda w: pl.BlockSpec((tm, w), lambda i: (i, 0))
    kern = functools.partial(_gmlp_kernel, n_ctx_tiles=n_ctx // tm, tiles_per_req=rows_per_req // tm)
    args = (h, y2, y2, route_prev, mod_prev, mod, g1, win, bin_, vg, ws, bsb, wout, g2, wr, br, ltri)
    second = pl.BlockSpec((tm, d // 2), lambda i: (i + t // tm, 0))
    in_specs = [row(d), row(d // 2), second, row(LANES)] + [full(a.shape) for a in args[4:]]
    return pl.pallas_call(
        kern,
        grid=(t // tm,),
        in_specs=in_specs,
        out_specs=[row(d), row(d // 2), row(LANES), pl.BlockSpec((R_ROWS, tm), lambda i: (0, i)),
                   pl.BlockSpec((1, LANES), lambda i: (0, 0))],
        out_shape=[jax.ShapeDtypeStruct((t, d), F32), jax.ShapeDtypeStruct((t, d // 2), jnp.int32),
                   jax.ShapeDtypeStruct((t, LANES), F32), jax.ShapeDtypeStruct((R_ROWS, t), F32),
                   jax.ShapeDtypeStruct((1, LANES), F32)],
        scratch_shapes=[pltpu.VMEM((1, LANES), F32)],
        compiler_params=_cparams(1),
        name="gmlp_router",
    )(*args)


def _final_kernel(h_hbm, y_hbm, rt_hbm, g2_hbm, o_hbm, *, first_tile, n_tiles, tm, n_ctx_tiles, tiles_per_req):
    d = h_hbm.shape[1]

    def body(h_ref, ya_ref, yb_ref, rt_ref, g2_ref, o_ref):
        o_ref[...] = _moe_combine(h_ref[...], ya_ref[...], yb_ref[...], rt_ref[...], g2_ref[0])

    deep = pl.Buffered(3)
    seq = lambda i: _seq_of_tile(first_tile + i, n_ctx_tiles, tiles_per_req)
    pltpu.emit_pipeline(
        body,
        grid=(n_tiles,),
        in_specs=[pl.BlockSpec((tm, d), lambda i: (first_tile + i, 0), pipeline_mode=deep),
                  pl.BlockSpec((tm, d // 2), lambda i: (i, 0), pipeline_mode=deep),
                  pl.BlockSpec((tm, d // 2), lambda i: (i + n_tiles, 0), pipeline_mode=deep),
                  pl.BlockSpec((tm, LANES), lambda i: (first_tile + i, 0), pipeline_mode=deep),
                  pl.BlockSpec((1, 1, d), lambda i: (seq(i), 0, 0))],
        out_specs=[pl.BlockSpec((tm, d), lambda i: (i, 0))],
    )(h_hbm, y_hbm, y_hbm, rt_hbm, g2_hbm, o_hbm)


def _final_combine(h, y2, route, mod, first_row, n_ctx, rows_per_req):
    d = h.shape[1]
    n = y2.shape[0] // 2
    tm = FINAL_TILE
    kern = functools.partial(_final_kernel, first_tile=first_row // tm, n_tiles=n // tm, tm=tm,
                             n_ctx_tiles=n_ctx // tm, tiles_per_req=rows_per_req // tm)
    anywhere = pl.BlockSpec(memory_space=pl.ANY)
    return pl.pallas_call(
        kern,
        in_specs=[anywhere] * 4,
        out_specs=anywhere,
        out_shape=jax.ShapeDtypeStruct((n, d), F32),
        compiler_params=pltpu.CompilerParams(vmem_limit_bytes=VMEM_LIMIT),
        name="final_combine",
    )(h, y2, route, mod[5].reshape(mod.shape[1], 1, d))


def kernel(x_prompt, x_sample, cache_k, cache_v, c, c_ctx, ada_w, ada_b, norm1_g, norm2_g, attn_wq, attn_wk,
           attn_wv, attn_wo, attn_qnorm, attn_knorm, attn_lam, attn_subln, gm_win, gm_bin, gm_vnorm, gm_ws,
           gm_bs, gm_wout, moe_wc, moe_bc, moe_wf, moe_bf, moe_w1, moe_w3, moe_w2):
    batch, seq_len, d = x_prompt.shape
    n_req, req_len, _ = x_sample.shape
    depth = ada_w.shape[0]
    assert depth == 2 and attn_wq.shape[0] == 1 and gm_win.shape[0] == 1
    n_ctx = batch * seq_len
    tiles = (PRE_ATTN_TILE, ROW_TILE * POST_ATTN_SUBTILES, GMLP_TILE, ATTN_Q_TILE, FINAL_TILE)
    assert all(n_ctx % tm == 0 and req_len % tm == 0 for tm in tiles) and GMLP_TILE % CHUNK == 0
    past = cache_k.shape[2]

    h_ctx = x_prompt.reshape(n_ctx, d)
    h_lat = x_sample.reshape(n_req * req_len, d)
    cvec = jnp.concatenate([c_ctx[None, :], c, jnp.zeros((8 - 1 - n_req, d), F32)], axis=0)
    mod = _ada_mod(cvec, ada_w, ada_b)
    routers = [_router_weights(moe_wc[i], moe_bc[i], moe_wf[i], moe_bf[i]) for i in range(depth)]
    row_vec = lambda v: v.reshape(1, -1)

    wqkv = jnp.concatenate([attn_wq[0], attn_wk[0], attn_wv[0]], axis=1).astype(BF16)
    reps = d // HEAD_DIM
    cache_k_t = jnp.transpose(cache_k[:, 0], (0, 2, 3, 4, 1)).reshape(n_req, N_HEADS, 2 * HEAD_DIM, past)
    qk_gains = (row_vec(jnp.tile(attn_qnorm[0], reps)), row_vec(jnp.tile(attn_knorm[0], reps)))
    *qkv_ctx, k_new, v_new = _pre_attn(h_ctx, mod[0], row_vec(norm1_g[0]), wqkv, *qk_gains, ctx_len=seq_len)
    qkv_lat = _pre_attn(h_lat, mod[0], row_vec(norm1_g[0]), wqkv, *qk_gains, rows_per_req=req_len)
    o_ctx, o_lat = _attention(qkv_ctx, qkv_lat, attn_lam[0], row_vec(attn_subln[0]),
                              cache_k_t, cache_v[:, 0].reshape(n_req, past, d),
                              seq_len, n_req, req_len, layer=0)
    h1, a2, route0, route0_t, cnt0 = _post_attn(h_ctx, h_lat, o_ctx, o_lat, mod[0], attn_wo[0].astype(BF16),
                                      row_vec(norm2_g[0]), routers[0], req_len)
    t = n_ctx + n_req * req_len
    (y2,) = _moe(a2, route0_t, cnt0, moe_w1, moe_w3, moe_w2, 0, [(0, t)])

    gw = gm_vnorm.shape[1]
    bsb = jnp.repeat(gm_bs[0].T, gw // GM_GROUPS, axis=1)
    h2, a2, route1, route1_t, cnt1 = _gmlp_layer(h1, y2, route0, mod[0], mod[1], row_vec(norm1_g[1]), gm_win[0].astype(BF16),
                                       row_vec(gm_bin[0]), row_vec(gm_vnorm[0]), gm_ws[0].astype(BF16), bsb,
                                       gm_wout[0].astype(BF16), row_vec(norm2_g[1]), routers[1], n_ctx, req_len)
    y2_ctx, y2_lat = _moe(a2, route1_t, cnt1, moe_w1, moe_w3, moe_w2, 1, [(0, n_ctx), (n_ctx, t)])
    y_ctx = _final_combine(h2, y2_ctx, route1, mod[1], 0, n_ctx, req_len)
    y_lat = _final_combine(h2, y2_lat, route1, mod[1], n_ctx, n_ctx, req_len)

    y_prompt = y_ctx.reshape(batch, seq_len, d)
    y_sample = y_lat.reshape(n_req, req_len, d)
    new_cache_k = jnp.transpose(k_new.reshape(batch, 1, N_HEADS, 2, HEAD_DIM, seq_len), (0, 1, 5, 2, 3, 4))
    new_cache_v = v_new.reshape(batch, 1, seq_len, N_HEADS, V_DIM)
    return (y_prompt, y_sample, new_cache_k, new_cache_v)
```
